```python
import math
import jax, jax.numpy as jnp
from jax import lax
import numpy as np

D_MODEL = 1024
BATCH = 1
SEQ = 16384
DEPTH = 1
DEC_BATCH = 128
DEC_SEQ = 1
PAST_LEN = 16384
PAGE_SIZE = 128

N_HEADS = 8
N_KV_HEADS = 2
HEAD_DIM = 64
GROUP = N_HEADS // N_KV_HEADS
ATTN_W = N_HEADS * HEAD_DIM
KV_W = N_KV_HEADS * HEAD_DIM
CONV_CH = D_MODEL - ATTN_W
MIX_W = ATTN_W + CONV_CH
IN_W = ATTN_W + 2 * KV_W + 2 * CONV_CH
CONV_WIDTH = 31
WINDOW = 128
BLOCK = 128
N_BUCKETS = 32
MAX_DISTANCE = WINDOW
N_META = 16
D_FF = 4 * D_MODEL
EPS = 1e-6

kernel_name = "hymba_swa_sink_conformer_conv_decoder_step"


def rmsnorm(x, g):
    xf = x.astype(jnp.float32)
    y = xf * lax.rsqrt(jnp.mean(xf * xf, axis=-1, keepdims=True) + EPS)
    return (y * g.astype(jnp.float32)).astype(x.dtype)


def layernorm(x, g, b):
    xf = x.astype(jnp.float32)
    mu = jnp.mean(xf, axis=-1, keepdims=True)
    xc = xf - mu
    y = xc * lax.rsqrt(jnp.mean(xc * xc, axis=-1, keepdims=True) + EPS)
    return (y * g.astype(jnp.float32) + b.astype(jnp.float32)).astype(x.dtype)


def t5_bucket(d):
    max_exact = N_BUCKETS // 2
    d_f = jnp.maximum(d, 1).astype(jnp.float32)
    large = max_exact + (jnp.log(d_f / max_exact) / math.log(MAX_DISTANCE / max_exact)
                         * (N_BUCKETS - max_exact)).astype(jnp.int32)
    large = jnp.minimum(large, N_BUCKETS - 1)
    return jnp.where(d < max_exact, d, large)


def sink_attention(q, k, v, dist, valid, rel_bias, sinks):
    q = q.reshape(q.shape[:-2] + (N_KV_HEADS, GROUP, HEAD_DIM))
    s = jnp.einsum('...qhgd,...khd->...hgqk', q, k,
                   preferred_element_type=jnp.float32) * (HEAD_DIM ** -0.5)
    bias = rel_bias.astype(jnp.float32)[t5_bucket(jnp.clip(dist, 0, WINDOW))]
    bias = jnp.moveaxis(bias, -1, 0).reshape((N_KV_HEADS, GROUP) + dist.shape)
    s = jnp.where(valid, s + bias, -jnp.inf)
    sk = sinks.astype(jnp.float32).reshape(N_KV_HEADS, GROUP, 1, 1)
    m = jnp.maximum(jnp.max(s, axis=-1, keepdims=True), sk)
    p = jnp.exp(s - m)
    p = p / (jnp.sum(p, axis=-1, keepdims=True) + jnp.exp(sk - m))
    o = jnp.einsum('...hgqk,...khd->...qhgd', p.astype(v.dtype), v)
    return o.reshape(o.shape[:-3] + (ATTN_W,))


def project_in(x, norm1_g, w_in):
    h = rmsnorm(x, norm1_g)
    p = h @ w_in
    lead = p.shape[:-1]
    o1, o2, o3, o4 = ATTN_W, ATTN_W + KV_W, ATTN_W + 2 * KV_W, ATTN_W + 2 * KV_W + CONV_CH
    q = p[..., :o1].reshape(lead + (N_HEADS, HEAD_DIM))
    k = p[..., o1:o2].reshape(lead + (N_KV_HEADS, HEAD_DIM))
    v = p[..., o2:o3].reshape(lead + (N_KV_HEADS, HEAD_DIM))
    u = p[..., o3:o4] * jax.nn.sigmoid(p[..., o4:])
    return q, k, v, u


def depthwise_conv(u_padded, conv_w, conv_b):
    out = lax.conv_general_dilated(u_padded, conv_w[:, None, :].astype(u_padded.dtype),
                                   window_strides=(1,), padding='VALID',
                                   dimension_numbers=('NWC', 'WIO', 'NWC'),
                                   feature_group_count=CONV_CH)
    return out + conv_b


def finish_layer(x, attn_o, conv_c, conv_ln_g, conv_ln_b, w_out, norm2_g, w_up, w_down):
    c = jax.nn.silu(layernorm(conv_c, conv_ln_g, conv_ln_b))
    x = x + jnp.concatenate([attn_o.astype(x.dtype), c.astype(x.dtype)], axis=-1) @ w_out
    hid = jnp.square(jax.nn.relu(rmsnorm(x, norm2_g) @ w_up))
    return x + hid @ w_down


def setup_inputs(seed: int = 0) -> dict:
    key = jax.random.key(seed)
    ks = jax.random.split(key, 20)
    f = jnp.float32
    nrm = lambda k, shape, sc: jax.random.normal(k, shape, f) * sc
    return {
        'x_prompt': nrm(ks[0], (BATCH, SEQ, D_MODEL), 1.0),
        'x_sample': nrm(ks[1], (DEC_BATCH, DEC_SEQ, D_MODEL), 1.0),
        'cache_k': nrm(ks[2], (DEPTH, DEC_BATCH, WINDOW, N_KV_HEADS, HEAD_DIM), 1.0),
        'cache_v': nrm(ks[3], (DEPTH, DEC_BATCH, WINDOW, N_KV_HEADS, HEAD_DIM), 1.0),
        'state_conv': nrm(ks[4], (DEPTH, DEC_BATCH, CONV_WIDTH - 1, CONV_CH), 0.5),
        'meta_tokens': nrm(ks[5], (N_META, D_MODEL), 1.0),
        'rel_bias': nrm(ks[6], (N_BUCKETS, N_HEADS), 0.5),
        'norm1_g': 1.0 + nrm(ks[7], (DEPTH, D_MODEL), 0.02),
        'w_in': nrm(ks[8], (DEPTH, D_MODEL, IN_W), D_MODEL ** -0.5),
        'attn_sinks': nrm(ks[9], (DEPTH, N_HEADS), 0.5),
        'conv_w': nrm(ks[10], (DEPTH, CONV_WIDTH, CONV_CH), CONV_WIDTH ** -0.5),
        'conv_b': nrm(ks[11], (DEPTH, CONV_CH), 0.02),
        'conv_ln_g': 1.0 + nrm(ks[12], (DEPTH, CONV_CH), 0.02),
        'conv_ln_b': nrm(ks[13], (DEPTH, CONV_CH), 0.02),
        'w_out': nrm(ks[14], (DEPTH, MIX_W, D_MODEL), MIX_W ** -0.5),
        'norm2_g': 1.0 + nrm(ks[15], (DEPTH, D_MODEL), 0.02),
        'w_up': nrm(ks[16], (DEPTH, D_MODEL, D_FF), D_MODEL ** -0.5),
        'w_down': nrm(ks[17], (DEPTH, D_FF, D_MODEL), D_FF ** -0.5),
        'norm_f_g': 1.0 + nrm(ks[18], (D_MODEL,), 0.02),
    }


def reference(x_prompt, x_sample, cache_k, cache_v, state_conv, meta_tokens, rel_bias,
              norm1_g, w_in, attn_sinks, conv_w, conv_b, conv_ln_g, conv_ln_b,
              w_out, norm2_g, w_up, w_down, norm_f_g):
    B, S_p, _ = x_prompt.shape
    DB, S_s, _ = x_sample.shape
    xp = jnp.concatenate([jnp.broadcast_to(meta_tokens[None].astype(x_prompt.dtype), (B, N_META, D_MODEL)),
                          x_prompt], axis=1)
    L = S_p + N_META
    pad = (-N_META) % BLOCK
    Lp = L + pad
    nblk = Lp // BLOCK
    bi = jnp.arange(BLOCK)[:, None]
    bj = jnp.arange(2 * BLOCK)[None, :]
    dist_p = bi + BLOCK - bj
    key_idx = (jnp.arange(nblk)[:, None] - 1) * BLOCK + jnp.arange(2 * BLOCK)[None, :]
    valid_p = ((dist_p >= 0) & (dist_p <= WINDOW))[None] & (key_idx >= pad)[:, None, :]
    valid_p = valid_p[:, None, None]
    qi = jnp.arange(S_s)[:, None]
    kj = jnp.arange(WINDOW + S_s)[None, :]
    dist_s = qi + WINDOW - kj
    valid_s = ((dist_s >= 0) & (dist_s <= WINDOW))[None, None, None]

    def blocks(t):
        t = jnp.pad(t, ((0, 0), (pad, 0)) + ((0, 0),) * (t.ndim - 2))
        return t.reshape((B, nblk, BLOCK) + t.shape[2:])

    def with_prev(tb):
        prev = jnp.pad(tb, ((0, 0), (1, 0)) + ((0, 0),) * (tb.ndim - 2))[:, :-1]
        return jnp.concatenate([prev, tb], axis=2)

    xs = x_sample
    nk_p, nv_p, nc_p, nk_s, nv_s, nc_s = [], [], [], [], [], []
    for l in range(DEPTH):
        q, k, v, u = project_in(xp, norm1_g[l], w_in[l])
        ao = sink_attention(blocks(q), with_prev(blocks(k)), with_prev(blocks(v)),
                            dist_p, valid_p, rel_bias, attn_sinks[l])
        ao = ao.reshape(B, Lp, ATTN_W)[:, pad:]
        cc = depthwise_conv(jnp.pad(u, ((0, 0), (CONV_WIDTH - 1, 0), (0, 0))), conv_w[l], conv_b[l])
        xp = finish_layer(xp, ao, cc, conv_ln_g[l], conv_ln_b[l], w_out[l], norm2_g[l], w_up[l], w_down[l])
        nk_p.append(k[:, -WINDOW:])
        nv_p.append(v[:, -WINDOW:])
        nc_p.append(u[:, -(CONV_WIDTH - 1):])
        q, k, v, u = project_in(xs, norm1_g[l], w_in[l])
        kk = jnp.concatenate([cache_k[l].astype(k.dtype), k], axis=1)
        vv = jnp.concatenate([cache_v[l].astype(v.dtype), v], axis=1)
        ao = sink_attention(q, kk, vv, dist_s, valid_s, rel_bias, attn_sinks[l])
        uu = jnp.concatenate([state_conv[l].astype(u.dtype), u], axis=1)
        cc = depthwise_conv(uu, conv_w[l], conv_b[l])
        xs = finish_layer(xs, ao, cc, conv_ln_g[l], conv_ln_b[l], w_out[l], norm2_g[l], w_up[l], w_down[l])
        nk_s.append(kk[:, -WINDOW:])
        nv_s.append(vv[:, -WINDOW:])
        nc_s.append(uu[:, -(CONV_WIDTH - 1):])

    y_prompt = rmsnorm(xp, norm_f_g)[:, N_META:]
    y_sample = rmsnorm(xs, norm_f_g)
    return (y_prompt, y_sample, jnp.stack(nk_p), jnp.stack(nv_p), jnp.stack(nc_p),
            jnp.stack(nk_s), jnp.stack(nv_s), jnp.stack(nc_s))
```

```python
import functools
import math

import jax
import jax.numpy as jnp
from jax import lax
from jax.experimental import pallas as pl
from jax.experimental.pallas import tpu as pltpu

D_MODEL = 1024
N_HEADS = 8
N_KV_HEADS = 2
HEAD_DIM = 64
GROUP = N_HEADS // N_KV_HEADS
ATTN_W = N_HEADS * HEAD_DIM
KV_W = N_KV_HEADS * HEAD_DIM
CONV_CH = D_MODEL - ATTN_W
IN_W = ATTN_W + 2 * KV_W + 2 * CONV_CH
CONV_WIDTH = 31
WINDOW = 128
BLOCK = 128
N_BUCKETS = 32
MAX_DISTANCE = WINDOW
N_META = 16
D_FF = 4 * D_MODEL
EPS = 1e-6
SCALE = HEAD_DIM ** -0.5

O_K = ATTN_W
O_V = ATTN_W + KV_W
O_A = ATTN_W + 2 * KV_W
O_B = O_A + CONV_CH

PAD = (-N_META) % BLOCK
U_CARRY = 32
U_SHIFT = U_CARRY - (CONV_WIDTH - 1)

V7X_VMEM_LIMIT_BYTES = 60 * 1024 * 1024

PROMPT_TILE = 512
FFN_CHUNK = 1024
CONV_ROWS = 32
SAMPLE_CHUNK = 16
HEAD_ROWS = 16

BF16 = jnp.bfloat16
F32 = jnp.float32
NT_DIMS = (((1,), (1,)), ((), ()))


def _t5_bucket(d):
    max_exact = N_BUCKETS // 2
    d_f = jnp.maximum(d, 1).astype(jnp.float32)
    large = max_exact + (jnp.log(d_f / max_exact) / math.log(MAX_DISTANCE / max_exact)
                         * (N_BUCKETS - max_exact)).astype(jnp.int32)
    large = jnp.minimum(large, N_BUCKETS - 1)
    return jnp.where(d < max_exact, d, large)


def _rms(x, g):
    y = x * lax.rsqrt(jnp.mean(x * x, axis=-1, keepdims=True) + EPS)
    return y * g


def _sigmoid(x):
    return 1.0 / (1.0 + jnp.exp(-x))


def _bias_from_buckets(bucket, relb_ref, h):
    b = jnp.zeros(bucket.shape, F32)
    for bk in range(N_BUCKETS):
        b = jnp.where(bucket == bk, relb_ref[bk, h], b)
    return b


def _conv_ln_silu(u_rows, cw_ref, cb, lng, lnb):
    acc = cb
    for w in range(CONV_WIDTH):
        acc = acc + cw_ref[w:w + 1, :] * u_rows(w)
    mu = jnp.mean(acc, axis=-1, keepdims=True)
    xc = acc - mu
    y = xc * lax.rsqrt(jnp.mean(xc * xc, axis=-1, keepdims=True) + EPS)
    y = y * lng + lnb
    return y * _sigmoid(y)


def _prompt_mixer_body(relb_ref, sink_ref, x_ref, meta_ref, g1_ref, win_ref, bucket_ref,
                       cw_ref, cb_ref, lng_ref, lnb_ref, wout_ref,
                       x2_ref, nk_ref, nv_ref, nc_ref,
                       kbuf, vbuf, ubuf, qbuf, mix, bias_s, *, tile):
    i = pl.program_id(0)
    last = pl.num_programs(0) - 1
    g1 = g1_ref[...]

    @pl.when(i == 0)
    def _init():
        bucket = bucket_ref[...]
        row = lax.broadcasted_iota(jnp.int32, (BLOCK, 2 * BLOCK), 0)
        col = lax.broadcasted_iota(jnp.int32, (BLOCK, 2 * BLOCK), 1)
        dist = row + BLOCK - col
        band = (dist >= 0) & (dist <= WINDOW)
        band_first = band & (col >= PAD)
        for h in range(N_HEADS):
            b = _bias_from_buckets(bucket, relb_ref, h)
            bias_s[0, h] = jnp.where(band, b, -jnp.inf)
            bias_s[1, h] = jnp.where(band_first, b, -jnp.inf)
        hm = _rms(meta_ref[...], g1).astype(BF16)
        pm = jnp.dot(hm, win_ref[:, O_K:], preferred_element_type=F32)
        kbuf[0:PAD, :] = jnp.zeros((PAD, KV_W), BF16)
        vbuf[0:PAD, :] = jnp.zeros((PAD, KV_W), BF16)
        kbuf[PAD:BLOCK, :] = pm[:, 0:KV_W].astype(BF16)
        vbuf[PAD:BLOCK, :] = pm[:, KV_W:2 * KV_W].astype(BF16)
        um = pm[:, 2 * KV_W:2 * KV_W + CONV_CH] * _sigmoid(pm[:, 2 * KV_W + CONV_CH:])
        ubuf[0:U_CARRY - N_META, :] = jnp.zeros((U_CARRY - N_META, CONV_CH), F32)
        ubuf[U_CARRY - N_META:U_CARRY, :] = um

    x = x_ref[...]
    h = _rms(x, g1).astype(BF16)
    q = jnp.dot(h, win_ref[:, 0:ATTN_W], preferred_element_type=F32) * SCALE
    qbuf[...] = q.astype(BF16)
    kv = jnp.dot(h, win_ref[:, O_K:O_A], preferred_element_type=F32)
    kbuf[BLOCK:BLOCK + tile, :] = kv[:, 0:KV_W].astype(BF16)
    vbuf[BLOCK:BLOCK + tile, :] = kv[:, KV_W:].astype(BF16)

    @pl.when(i == last)
    def _new_kv():
        nk_ref[...] = kv[tile - WINDOW:, 0:KV_W]
        nv_ref[...] = kv[tile - WINDOW:, KV_W:]

    a = jnp.dot(h, win_ref[:, O_A:O_B], preferred_element_type=F32)
    b = jnp.dot(h, win_ref[:, O_B:], preferred_element_type=F32)
    ubuf[U_CARRY:U_CARRY + tile, :] = a * _sigmoid(b)

    @pl.when(i == last)
    def _new_conv():
        nc_ref[...] = ubuf[U_CARRY + tile - (CONV_WIDTH - 1):U_CARRY + tile, :]

    cb, lng, lnb = cb_ref[...], lng_ref[...], lnb_ref[...]
    for r0 in range(0, tile, CONV_ROWS):
        c = _conv_ln_silu(lambda w: ubuf[r0 + w + U_SHIFT:r0 + w + U_SHIFT + CONV_ROWS, :],
                          cw_ref, cb, lng, lnb)
        mix[r0:r0 + CONV_ROWS, ATTN_W:] = c.astype(BF16)

    for blk in range(tile // BLOCK):
        r0 = blk * BLOCK
        sel = jnp.where(i == 0, 1, 0) if blk == 0 else 0
        for kvh in range(N_KV_HEADS):
            c0 = kvh * HEAD_DIM
            qg = jnp.concatenate(
                [qbuf[r0:r0 + BLOCK, (kvh * GROUP + g) * HEAD_DIM:(kvh * GROUP + g + 1) * HEAD_DIM]
                 for g in range(GROUP)], axis=0)
            kk = kbuf[r0:r0 + 2 * BLOCK, c0:c0 + HEAD_DIM]
            vv = vbuf[r0:r0 + 2 * BLOCK, c0:c0 + HEAD_DIM]
            s = lax.dot_general(qg, kk, NT_DIMS, preferred_element_type=F32)
            ps, ls = [], []
            for g in range(GROUP):
                hd = kvh * GROUP + g
                sg = s[g * BLOCK:(g + 1) * BLOCK] + bias_s[sel, hd]
                sk = sink_ref[hd]
                m = jnp.maximum(jnp.max(sg, axis=-1, keepdims=True), sk)
                p = jnp.exp(sg - m)
                ls.append(jnp.sum(p, axis=-1, keepdims=True) + jnp.exp(sk - m))
                ps.append(p.astype(BF16))
            o = jnp.dot(jnp.concatenate(ps, axis=0), vv, preferred_element_type=F32)
            for g in range(GROUP):
                hd = kvh * GROUP + g
                og = o[g * BLOCK:(g + 1) * BLOCK] / ls[g]
                mix[r0:r0 + BLOCK, hd * HEAD_DIM:(hd + 1) * HEAD_DIM] = og.astype(BF16)

    x2_ref[...] = x + jnp.dot(mix[...], wout_ref[...], preferred_element_type=F32)

    kbuf[0:BLOCK, :] = kbuf[tile:tile + BLOCK, :]
    vbuf[0:BLOCK, :] = vbuf[tile:tile + BLOCK, :]
    ubuf[0:U_CARRY, :] = ubuf[tile:tile + U_CARRY, :]


def _const_spec(shape):
    return pl.BlockSpec(shape, lambda i: (0,) * len(shape), pipeline_mode=pl.Buffered(1))


def _smem_spec():
    return pl.BlockSpec(memory_space=pltpu.SMEM)


def _prompt_mixer(x, meta, rel_bias, sinks, g1, w_in, bucket, cw, cb, lng, lnb, w_out):
    seq = x.shape[0]
    tile = PROMPT_TILE
    body = functools.partial(_prompt_mixer_body, tile=tile)
    return pl.pallas_call(
        body,
        grid=(seq // tile,),
        in_specs=[
            _smem_spec(), _smem_spec(),
            pl.BlockSpec((tile, D_MODEL), lambda i: (i, 0)),
            _const_spec((N_META, D_MODEL)),
            _const_spec((1, D_MODEL)),
            _const_spec((D_MODEL, IN_W)),
            _const_spec((BLOCK, 2 * BLOCK)),
            _const_spec((CONV_WIDTH, CONV_CH)),
            _const_spec((1, CONV_CH)),
            _const_spec((1, CONV_CH)),
            _const_spec((1, CONV_CH)),
            _const_spec((ATTN_W + CONV_CH, D_MODEL)),
        ],
        out_specs=[
            pl.BlockSpec((tile, D_MODEL), lambda i: (i, 0)),
            pl.BlockSpec((WINDOW, KV_W), lambda i: (0, 0)),
            pl.BlockSpec((WINDOW, KV_W), lambda i: (0, 0)),
            pl.BlockSpec((CONV_WIDTH - 1, CONV_CH), lambda i: (0, 0)),
        ],
        out_shape=[
            jax.ShapeDtypeStruct((seq, D_MODEL), F32),
            jax.ShapeDtypeStruct((WINDOW, KV_W), F32),
            jax.ShapeDtypeStruct((WINDOW, KV_W), F32),
            jax.ShapeDtypeStruct((CONV_WIDTH - 1, CONV_CH), F32),
        ],
        scratch_shapes=[
            pltpu.VMEM((BLOCK + tile, KV_W), BF16),
            pltpu.VMEM((BLOCK + tile, KV_W), BF16),
            pltpu.VMEM((U_CARRY + tile, CONV_CH), F32),
            pltpu.VMEM((tile, ATTN_W), BF16),
            pltpu.VMEM((tile, ATTN_W + CONV_CH), BF16),
            pltpu.VMEM((2, N_HEADS, BLOCK, 2 * BLOCK), F32),
        ],
        compiler_params=pltpu.CompilerParams(
            dimension_semantics=("arbitrary",), vmem_limit_bytes=V7X_VMEM_LIMIT_BYTES),
        name="prompt_mixer",
    )(rel_bias, sinks, x, meta, g1, w_in, bucket, cw, cb, lng, lnb, w_out)


def _ffn_body(x_ref, g2_ref, wup_ref, wdn_ref, gf_ref, y_ref):
    x = x_ref[...]
    h = _rms(x, g2_ref[...]).astype(BF16)
    acc = x
    for c0 in range(0, D_FF, FFN_CHUNK):
        hid = jnp.dot(h, wup_ref[:, c0:c0 + FFN_CHUNK], preferred_element_type=F32)
        hid = jnp.square(jnp.maximum(hid, 0.0)).astype(BF16)
        acc = acc + jnp.dot(hid, wdn_ref[c0:c0 + FFN_CHUNK, :], preferred_element_type=F32)
    y_ref[...] = _rms(acc, gf_ref[...])


def _ffn(x, g2, w_up, w_down, gf, tile):
    rows = x.shape[0]
    return pl.pallas_call(
        _ffn_body,
        grid=(rows // tile,),
        in_specs=[
            pl.BlockSpec((tile, D_MODEL), lambda i: (i, 0)),
            _const_spec((1, D_MODEL)),
            _const_spec((D_MODEL, D_FF)),
            _const_spec((D_FF, D_MODEL)),
            _const_spec((1, D_MODEL)),
        ],
        out_specs=pl.BlockSpec((tile, D_MODEL), lambda i: (i, 0)),
        out_shape=jax.ShapeDtypeStruct((rows, D_MODEL), F32),
        compiler_params=pltpu.CompilerParams(
            dimension_semantics=("arbitrary",), vmem_limit_bytes=V7X_VMEM_LIMIT_BYTES),
        name="ffn",
    )(x, g2, w_up, w_down, gf)


def _sample_mixer_body(relb_ref, sink_ref, x_ref, ck_ref, cv_ref, st_ref, g1_ref, win_ref,
                       bucket_ref, cw_ref, cb_ref, lng_ref, lnb_ref, wout_ref,
                       x2_ref, nk_ref, nv_ref, nc_ref,
                       pbuf, mix, convbuf, bias_c, sink_c, *, chunk):
    i = pl.program_id(0)
    last = pl.num_programs(0) - 1
    rows_h = chunk * HEAD_ROWS

    @pl.when(i == 0)
    def _init():
        h = _rms(x_ref[...], g1_ref[...]).astype(BF16)
        pbuf[...] = jnp.dot(h, win_ref[...], preferred_element_type=F32)
        bucket = bucket_ref[...]
        rid = lax.broadcasted_iota(jnp.int32, (HEAD_ROWS, 1), 0)
        bias = jnp.zeros((HEAD_ROWS, 2 * BLOCK), F32)
        sk = jnp.zeros((HEAD_ROWS, 1), F32)
        for hd in range(N_HEADS):
            bias = jnp.where(rid == hd, _bias_from_buckets(bucket, relb_ref, hd), bias)
            sk = jnp.where(rid == hd, sink_ref[hd], sk)
        bias_c[...] = bias
        sink_c[...] = sk

    r0 = pl.multiple_of(i * chunk, chunk)
    pr = pbuf[pl.ds(r0, chunk), :]
    q = pr[:, 0:ATTN_W] * SCALE
    knew = pr[:, O_K:O_V]
    vnew = pr[:, O_V:O_A]
    unew = pr[:, O_A:O_B] * _sigmoid(pr[:, O_B:])

    def per_head(t):
        n = t.shape[-1]
        return jnp.broadcast_to(t[:, None, :], (chunk, HEAD_ROWS, n)).reshape(rows_h, n)

    hid = lax.broadcasted_iota(jnp.int32, (rows_h, 1), 0) % HEAD_ROWS
    lane = lax.broadcasted_iota(jnp.int32, (1, BLOCK), 1)
    qrep = per_head(q)
    qsum = jnp.zeros((rows_h, BLOCK), F32)
    for c in range(ATTN_W // BLOCK):
        piece = qrep[:, c * BLOCK:(c + 1) * BLOCK]
        in_head = (lane // HEAD_DIM + 2 * c) == hid
        qsum = qsum + jnp.where(in_head, piece, 0.0)
    keep = (hid % 2) == (hid // GROUP)
    qf = jnp.where(keep, qsum, pltpu.roll(qsum, HEAD_DIM, axis=1))
    qf_b = qf.astype(BF16)

    bias = jnp.broadcast_to(bias_c[...][None], (chunk, HEAD_ROWS, 2 * BLOCK)).reshape(rows_h, 2 * BLOCK)
    sk = jnp.broadcast_to(sink_c[...][None], (chunk, HEAD_ROWS, 1)).reshape(rows_h, 1)

    s_rows = []
    for b in range(chunk):
        kb = ck_ref[b].astype(BF16)
        s_rows.append(lax.dot_general(qf_b[b * HEAD_ROWS:(b + 1) * HEAD_ROWS], kb, NT_DIMS,
                                      preferred_element_type=F32))
    s_c = jnp.concatenate(s_rows, axis=0) + bias[:, 0:BLOCK]
    s_n = jnp.sum(qf * per_head(knew), axis=-1, keepdims=True) + bias[:, BLOCK:BLOCK + 1]
    m = jnp.maximum(jnp.maximum(jnp.max(s_c, axis=-1, keepdims=True), s_n), sk)
    p_c = jnp.exp(s_c - m)
    p_n = jnp.exp(s_n - m)
    l = jnp.sum(p_c, axis=-1, keepdims=True) + p_n + jnp.exp(sk - m)
    p_cb = p_c.astype(BF16)
    o_rows = []
    for b in range(chunk):
        vb = cv_ref[b].astype(BF16)
        o_rows.append(jnp.dot(p_cb[b * HEAD_ROWS:(b + 1) * HEAD_ROWS], vb, preferred_element_type=F32))
    o = (jnp.concatenate(o_rows, axis=0) + p_n * per_head(vnew)) / l
    o = jnp.where(keep, o, pltpu.roll(o, HEAD_DIM, axis=1))
    o = jnp.where((lane // HEAD_DIM) == (hid % 2), o, 0.0)
    wide = jnp.concatenate([jnp.where(hid // 2 == c, o, 0.0) for c in range(ATTN_W // BLOCK)], axis=1)
    gi = lax.broadcasted_iota(jnp.int32, (chunk, rows_h), 0)
    gj = lax.broadcasted_iota(jnp.int32, (chunk, rows_h), 1)
    gather = jnp.where(gj // HEAD_ROWS == gi, 1.0, 0.0).astype(BF16)
    ao = jnp.dot(gather, wide.astype(BF16), preferred_element_type=F32)
    mix[pl.ds(r0, chunk), 0:ATTN_W] = ao

    cw_hist = cw_ref[0:CONV_WIDTH - 1, :]
    cw_new = cw_ref[CONV_WIDTH - 1:CONV_WIDTH, :]
    for b in range(chunk):
        convbuf[b:b + 1, :] = jnp.sum(st_ref[b] * cw_hist, axis=0, keepdims=True)
        nk_ref[b, 0:WINDOW - 1, :] = ck_ref[b, 1:WINDOW, :]
        nk_ref[b, WINDOW - 1:WINDOW, :] = knew[b:b + 1, :]
        nv_ref[b, 0:WINDOW - 1, :] = cv_ref[b, 1:WINDOW, :]
        nv_ref[b, WINDOW - 1:WINDOW, :] = vnew[b:b + 1, :]
        nc_ref[b, 0:CONV_WIDTH - 2, :] = st_ref[b, 1:CONV_WIDTH - 1, :]
        nc_ref[b, CONV_WIDTH - 2:CONV_WIDTH - 1, :] = unew[b:b + 1, :]
    acc = convbuf[...] + cw_new * unew + cb_ref[...]
    mu = jnp.mean(acc, axis=-1, keepdims=True)
    xc = acc - mu
    y = xc * lax.rsqrt(jnp.mean(xc * xc, axis=-1, keepdims=True) + EPS)
    y = y * lng_ref[...] + lnb_ref[...]
    mix[pl.ds(r0, chunk), ATTN_W:] = y * _sigmoid(y)

    @pl.when(i == last)
    def _out():
        x2_ref[...] = x_ref[...] + jnp.dot(mix[...].astype(BF16), wout_ref[...],
                                           preferred_element_type=F32)


def _sample_mixer(x, ck, cv, st, rel_bias, sinks, g1, w_in, bucket, cw, cb, lng, lnb, w_out):
    nb = x.shape[0]
    chunk = SAMPLE_CHUNK
    body = functools.partial(_sample_mixer_body, chunk=chunk)
    return pl.pallas_call(
        body,
        grid=(nb // chunk,),
        in_specs=[
            _smem_spec(), _smem_spec(),
            _const_spec((nb, D_MODEL)),
            pl.BlockSpec((chunk, WINDOW, KV_W), lambda i: (i, 0, 0)),
            pl.BlockSpec((chunk, WINDOW, KV_W), lambda i: (i, 0, 0)),
            pl.BlockSpec((chunk, CONV_WIDTH - 1, CONV_CH), lambda i: (i, 0, 0)),
            _const_spec((1, D_MODEL)),
            _const_spec((D_MODEL, IN_W)),
            _const_spec((1, 2 * BLOCK)),
            _const_spec((CONV_WIDTH, CONV_CH)),
            _const_spec((1, CONV_CH)),
            _const_spec((1, CONV_CH)),
            _const_spec((1, CONV_CH)),
            _const_spec((ATTN_W + CONV_CH, D_MODEL)),
        ],
        out_specs=[
            pl.BlockSpec((nb, D_MODEL), lambda i: (0, 0)),
            pl.BlockSpec((chunk, WINDOW, KV_W), lambda i: (i, 0, 0)),
            pl.BlockSpec((chunk, WINDOW, KV_W), lambda i: (i, 0, 0)),
            pl.BlockSpec((chunk, CONV_WIDTH - 1, CONV_CH), lambda i: (i, 0, 0)),
        ],
        out_shape=[
            jax.ShapeDtypeStruct((nb, D_MODEL), F32),
            jax.ShapeDtypeStruct((nb, WINDOW, KV_W), F32),
            jax.ShapeDtypeStruct((nb, WINDOW, KV_W), F32),
            jax.ShapeDtypeStruct((nb, CONV_WIDTH - 1, CONV_CH), F32),
        ],
        scratch_shapes=[
            pltpu.VMEM((nb, IN_W), F32),
            pltpu.VMEM((nb, ATTN_W + CONV_CH), F32),
            pltpu.VMEM((chunk, CONV_CH), F32),
            pltpu.VMEM((HEAD_ROWS, 2 * BLOCK), F32),
            pltpu.VMEM((HEAD_ROWS, 1), F32),
        ],
        compiler_params=pltpu.CompilerParams(
            dimension_semantics=("arbitrary",), vmem_limit_bytes=V7X_VMEM_LIMIT_BYTES),
        name="sample_mixer",
    )(rel_bias, sinks, x, ck, cv, st, g1, w_in, bucket, cw, cb, lng, lnb, w_out)


def kernel(x_prompt, x_sample, cache_k, cache_v, state_conv, meta_tokens, rel_bias, norm1_g, w_in,
           attn_sinks, conv_w, conv_b, conv_ln_g, conv_ln_b, w_out, norm2_g, w_up, w_down, norm_f_g):
    batch, seq, _ = x_prompt.shape
    nb, dec_seq, _ = x_sample.shape
    assert batch == 1 and dec_seq == 1 and w_in.shape[0] == 1
    assert seq % PROMPT_TILE == 0 and nb % SAMPLE_CHUNK == 0

    w_in_b = w_in[0].astype(BF16)
    w_out_b = w_out[0].astype(BF16)
    w_up_b = w_up[0].astype(BF16)
    w_down_b = w_down[0].astype(BF16)
    g1 = norm1_g[0][None]
    g2 = norm2_g[0][None]
    gf = norm_f_g[None]
    cw, cb = conv_w[0], conv_b[0][None]
    lng, lnb = conv_ln_g[0][None], conv_ln_b[0][None]
    sinks = attn_sinks[0]

    dist_p = jnp.arange(BLOCK)[:, None] + BLOCK - jnp.arange(2 * BLOCK)[None, :]
    bucket_p = _t5_bucket(jnp.clip(dist_p, 0, WINDOW)).astype(jnp.int32)
    lane = jnp.arange(2 * BLOCK)
    dist_s = jnp.where(lane < WINDOW, WINDOW - lane, 0)
    bucket_s = jnp.where(lane <= WINDOW, _t5_bucket(jnp.clip(dist_s, 0, WINDOW)), -1)
    bucket_s = bucket_s.astype(jnp.int32)[None]

    x2_p, nk_p, nv_p, nc_p = _prompt_mixer(x_prompt[0], meta_tokens, rel_bias, sinks, g1, w_in_b,
                                           bucket_p, cw, cb, lng, lnb, w_out_b)
    y_p = _ffn(x2_p, g2, w_up_b, w_down_b, gf, PROMPT_TILE)

    ck = cache_k[0].reshape(nb, WINDOW, KV_W)
    cv = cache_v[0].reshape(nb, WINDOW, KV_W)
    x2_s, nk_s, nv_s, nc_s = _sample_mixer(x_sample[:, 0], ck, cv, state_conv[0], rel_bias, sinks, g1,
                                           w_in_b, bucket_s, cw, cb, lng, lnb, w_out_b)
    y_s = _ffn(x2_s, g2, w_up_b, w_down_b, gf, nb)

    kv_shape = (1, WINDOW, N_KV_HEADS, HEAD_DIM)
    return (y_p[None], y_s[:, None],
            nk_p.reshape((1,) + kv_shape), nv_p.reshape((1,) + kv_shape), nc_p[None, None],
            nk_s.reshape((1, nb) + kv_shape[1:]), nv_s.reshape((1, nb) + kv_shape[1:]), nc_s[None])
```

```python
import functools
import math

import jax
import jax.numpy as jnp
from jax import lax
from jax.experimental import pallas as pl
from jax.experimental.pallas import tpu as pltpu

D_MODEL = 1024
N_HEADS = 8
N_KV_HEADS = 2
HEAD_DIM = 64
GROUP = N_HEADS // N_KV_HEADS
ATTN_W = N_HEADS * HEAD_DIM
KV_W = N_KV_HEADS * HEAD_DIM
CONV_CH = D_MODEL - ATTN_W
IN_W = ATTN_W + 2 * KV_W + 2 * CONV_CH
CONV_WIDTH = 31
WINDOW = 128
BLOCK = 128
N_BUCKETS = 32
MAX_DISTANCE = WINDOW
N_META = 16
D_FF = 4 * D_MODEL
EPS = 1e-6
SCALE = HEAD_DIM ** -0.5

O_K = ATTN_W
O_V = ATTN_W + KV_W
O_A = ATTN_W + 2 * KV_W
O_B = O_A + CONV_CH

PAD = (-N_META) % BLOCK
U_CARRY = 32
U_SHIFT = U_CARRY - (CONV_WIDTH - 1)

V7X_VMEM_LIMIT_BYTES = 60 * 1024 * 1024

PROMPT_TILE = 512
FFN_CHUNK = 1024
CONV_ROWS = 64
SAMPLE_CHUNK = 16
HEAD_ROWS = 16

BF16 = jnp.bfloat16
F32 = jnp.float32
NT_DIMS = (((1,), (1,)), ((), ()))


def _t5_bucket(d):
    max_exact = N_BUCKETS // 2
    d_f = jnp.maximum(d, 1).astype(jnp.float32)
    large = max_exact + (jnp.log(d_f / max_exact) / math.log(MAX_DISTANCE / max_exact)
                         * (N_BUCKETS - max_exact)).astype(jnp.int32)
    large = jnp.minimum(large, N_BUCKETS - 1)
    return jnp.where(d < max_exact, d, large)


def _rms(x, g):
    y = x * lax.rsqrt(jnp.mean(x * x, axis=-1, keepdims=True) + EPS)
    return y * g


def _sigmoid(x):
    return 1.0 / (1.0 + jnp.exp(-x))


def _bias_from_buckets(bucket, relb_ref, h):
    b = jnp.zeros(bucket.shape, F32)
    for bk in range(N_BUCKETS):
        b = jnp.where(bucket == bk, relb_ref[bk, h], b)
    return b


def _conv_rows(ubuf, cw_ref, r0, rows):
    n = rows + U_CARRY
    strips = []
    for c0 in range(0, CONV_CH, BLOCK):
        win = ubuf[r0:r0 + n, c0:c0 + BLOCK]
        acc = None
        for s in range(8):
            sh = win if s == 0 else pltpu.roll(win, n - s, axis=0)
            for a0 in range(0, U_CARRY + 8, 8):
                w = a0 + s - U_SHIFT
                if 0 <= w < CONV_WIDTH:
                    term = cw_ref[w:w + 1, c0:c0 + BLOCK] * sh[a0:a0 + rows]
                    acc = term if acc is None else acc + term
        strips.append(acc)
    return jnp.concatenate(strips, axis=1)


def _ln_silu(acc, lng, lnb):
    mu = jnp.mean(acc, axis=-1, keepdims=True)
    xc = acc - mu
    y = xc * lax.rsqrt(jnp.mean(xc * xc, axis=-1, keepdims=True) + EPS)
    y = y * lng + lnb
    return y * _sigmoid(y)


def _prompt_mixer_body(relb_ref, sink_ref, x_ref, meta_ref, g1_ref, win_ref, bucket_ref,
                       cw_ref, cb_ref, lng_ref, lnb_ref, wout_ref,
                       x2_ref, nk_ref, nv_ref, nc_ref,
                       kbuf, vbuf, ubuf, qbuf, mix, bias_s, *, tile):
    i = pl.program_id(0)
    last = pl.num_programs(0) - 1
    g1 = g1_ref[...]

    @pl.when(i == 0)
    def _init():
        bucket = bucket_ref[...]
        row = lax.broadcasted_iota(jnp.int32, (BLOCK, 2 * BLOCK), 0)
        col = lax.broadcasted_iota(jnp.int32, (BLOCK, 2 * BLOCK), 1)
        dist = row + BLOCK - col
        band = (dist >= 0) & (dist <= WINDOW)
        band_first = band & (col >= PAD)
        for h in range(N_HEADS):
            b = _bias_from_buckets(bucket, relb_ref, h)
            bias_s[0, h] = jnp.where(band, b, -jnp.inf)
            bias_s[1, h] = jnp.where(band_first, b, -jnp.inf)
        hm = _rms(meta_ref[...], g1).astype(BF16)
        pm = jnp.dot(hm, win_ref[:, O_K:], preferred_element_type=F32)
        kbuf[0:PAD, :] = jnp.zeros((PAD, KV_W), BF16)
        vbuf[0:PAD, :] = jnp.zeros((PAD, KV_W), BF16)
        kbuf[PAD:BLOCK, :] = pm[:, 0:KV_W].astype(BF16)
        vbuf[PAD:BLOCK, :] = pm[:, KV_W:2 * KV_W].astype(BF16)
        um = pm[:, 2 * KV_W:2 * KV_W + CONV_CH] * _sigmoid(pm[:, 2 * KV_W + CONV_CH:])
        ubuf[0:U_CARRY - N_META, :] = jnp.zeros((U_CARRY - N_META, CONV_CH), F32)
        ubuf[U_CARRY - N_META:U_CARRY, :] = um

    x = x_ref[...]
    h = _rms(x, g1).astype(BF16)
    q = jnp.dot(h, win_ref[:, 0:ATTN_W], preferred_element_type=F32) * SCALE
    qbuf[...] = q.astype(BF16)
    kv = jnp.dot(h, win_ref[:, O_K:O_A], preferred_element_type=F32)
    kbuf[BLOCK:BLOCK + tile, :] = kv[:, 0:KV_W].astype(BF16)
    vbuf[BLOCK:BLOCK + tile, :] = kv[:, KV_W:].astype(BF16)

    @pl.when(i == last)
    def _new_kv():
        nk_ref[...] = kv[tile - WINDOW:, 0:KV_W]
        nv_ref[...] = kv[tile - WINDOW:, KV_W:]

    a = jnp.dot(h, win_ref[:, O_A:O_B], preferred_element_type=F32)
    b = jnp.dot(h, win_ref[:, O_B:], preferred_element_type=F32)
    ubuf[U_CARRY:U_CARRY + tile, :] = a * _sigmoid(b)

    @pl.when(i == last)
    def _new_conv():
        nc_ref[...] = ubuf[U_CARRY + tile - (CONV_WIDTH - 1):U_CARRY + tile, :]

    cb, lng, lnb = cb_ref[...], lng_ref[...], lnb_ref[...]
    for r0 in range(0, tile, CONV_ROWS):
        c = _ln_silu(_conv_rows(ubuf, cw_ref, r0, CONV_ROWS) + cb, lng, lnb)
        mix[r0:r0 + CONV_ROWS, ATTN_W:] = c.astype(BF16)

    for blk in range(tile // BLOCK):
        r0 = blk * BLOCK
        sel = jnp.where(i == 0, 1, 0) if blk == 0 else 0
        for kvh in range(N_KV_HEADS):
            c0 = kvh * HEAD_DIM
            qg = jnp.concatenate(
                [qbuf[r0:r0 + BLOCK, (kvh * GROUP + g) * HEAD_DIM:(kvh * GROUP + g + 1) * HEAD_DIM]
                 for g in range(GROUP)], axis=0)
            kk = kbuf[r0:r0 + 2 * BLOCK, c0:c0 + HEAD_DIM]
            vv = vbuf[r0:r0 + 2 * BLOCK, c0:c0 + HEAD_DIM]
            s = lax.dot_general(qg, kk, NT_DIMS, preferred_element_type=F32)
            ps, ls = [], []
            for g in range(GROUP):
                hd = kvh * GROUP + g
                sg = s[g * BLOCK:(g + 1) * BLOCK] + bias_s[sel, hd]
                sk = sink_ref[hd]
                m = jnp.maximum(jnp.max(sg, axis=-1, keepdims=True), sk)
                p = jnp.exp(sg - m)
                ls.append(jnp.sum(p, axis=-1, keepdims=True) + jnp.exp(sk - m))
                ps.append(p.astype(BF16))
            o = jnp.dot(jnp.concatenate(ps, axis=0), vv, preferred_element_type=F32)
            for g in range(GROUP):
                hd = kvh * GROUP + g
                og = o[g * BLOCK:(g + 1) * BLOCK] / ls[g]
                mix[r0:r0 + BLOCK, hd * HEAD_DIM:(hd + 1) * HEAD_DIM] = og.astype(BF16)

    x2_ref[...] = x + jnp.dot(mix[...], wout_ref[...], preferred_element_type=F32)

    kbuf[0:BLOCK, :] = kbuf[tile:tile + BLOCK, :]
    vbuf[0:BLOCK, :] = vbuf[tile:tile + BLOCK, :]
    ubuf[0:U_CARRY, :] = ubuf[tile:tile + U_CARRY, :]


def _const_spec(shape):
    return pl.BlockSpec(shape, lambda i: (0,) * len(shape), pipeline_mode=pl.Buffered(1))


def _smem_spec():
    return pl.BlockSpec(memory_space=pltpu.SMEM)


def _prompt_mixer(x, meta, rel_bias, sinks, g1, w_in, bucket, cw, cb, lng, lnb, w_out):
    seq = x.shape[0]
    tile = PROMPT_TILE
    body = functools.partial(_prompt_mixer_body, tile=tile)
    return pl.pallas_call(
        body,
        grid=(seq // tile,),
        in_specs=[
            _smem_spec(), _smem_spec(),
            pl.BlockSpec((tile, D_MODEL), lambda i: (i, 0)),
            _const_spec((N_META, D_MODEL)),
            _const_spec((1, D_MODEL)),
            _const_spec((D_MODEL, IN_W)),
            _const_spec((BLOCK, 2 * BLOCK)),
            _const_spec((CONV_WIDTH, CONV_CH)),
            _const_spec((1, CONV_CH)),
            _const_spec((1, CONV_CH)),
            _const_spec((1, CONV_CH)),
            _const_spec((ATTN_W + CONV_CH, D_MODEL)),
        ],
        out_specs=[
            pl.BlockSpec((tile, D_MODEL), lambda i: (i, 0)),
            pl.BlockSpec((WINDOW, KV_W), lambda i: (0, 0)),
            pl.BlockSpec((WINDOW, KV_W), lambda i: (0, 0)),
            pl.BlockSpec((CONV_WIDTH - 1, CONV_CH), lambda i: (0, 0)),
        ],
        out_shape=[
            jax.ShapeDtypeStruct((seq, D_MODEL), F32),
            jax.ShapeDtypeStruct((WINDOW, KV_W), F32),
            jax.ShapeDtypeStruct((WINDOW, KV_W), F32),
            jax.ShapeDtypeStruct((CONV_WIDTH - 1, CONV_CH), F32),
        ],
        scratch_shapes=[
            pltpu.VMEM((BLOCK + tile, KV_W), BF16),
            pltpu.VMEM((BLOCK + tile, KV_W), BF16),
            pltpu.VMEM((U_CARRY + tile, CONV_CH), F32),
            pltpu.VMEM((tile, ATTN_W), BF16),
            pltpu.VMEM((tile, ATTN_W + CONV_CH), BF16),
            pltpu.VMEM((2, N_HEADS, BLOCK, 2 * BLOCK), F32),
        ],
        compiler_params=pltpu.CompilerParams(
            dimension_semantics=("arbitrary",), vmem_limit_bytes=V7X_VMEM_LIMIT_BYTES),
        name="prompt_mixer",
    )(rel_bias, sinks, x, meta, g1, w_in, bucket, cw, cb, lng, lnb, w_out)


def _ffn_body(x_ref, g2_ref, wup_ref, wdn_ref, gf_ref, y_ref):
    x = x_ref[...]
    h = _rms(x, g2_ref[...]).astype(BF16)
    acc = x
    for c0 in range(0, D_FF, FFN_CHUNK):
        hid = jnp.dot(h, wup_ref[:, c0:c0 + FFN_CHUNK], preferred_element_type=F32)
        hid = jnp.square(jnp.maximum(hid, 0.0)).astype(BF16)
        acc = acc + jnp.dot(hid, wdn_ref[c0:c0 + FFN_CHUNK, :], preferred_element_type=F32)
    y_ref[...] = _rms(acc, gf_ref[...])


def _ffn(x, g2, w_up, w_down, gf, tile):
    rows = x.shape[0]
    return pl.pallas_call(
        _ffn_body,
        grid=(rows // tile,),
        in_specs=[
            pl.BlockSpec((tile, D_MODEL), lambda i: (i, 0)),
            _const_spec((1, D_MODEL)),
            _const_spec((D_MODEL, D_FF)),
            _const_spec((D_FF, D_MODEL)),
            _const_spec((1, D_MODEL)),
        ],
        out_specs=pl.BlockSpec((tile, D_MODEL), lambda i: (i, 0)),
        out_shape=jax.ShapeDtypeStruct((rows, D_MODEL), F32),
        compiler_params=pltpu.CompilerParams(
            dimension_semantics=("arbitrary",), vmem_limit_bytes=V7X_VMEM_LIMIT_BYTES),
        name="ffn",
    )(x, g2, w_up, w_down, gf)


def _sample_mixer_body(relb_ref, sink_ref, x_ref, ck_ref, cv_ref, st_ref, g1_ref, win_ref,
                       bucket_ref, cw_ref, cb_ref, lng_ref, lnb_ref, wout_ref,
                       x2_ref, nk_ref, nv_ref, nc_ref,
                       pbuf, mix, convbuf, bias_c, sink_c, *, chunk):
    i = pl.program_id(0)
    last = pl.num_programs(0) - 1
    rows_h = chunk * HEAD_ROWS

    @pl.when(i == 0)
    def _init():
        h = _rms(x_ref[...], g1_ref[...]).astype(BF16)
        pbuf[...] = jnp.dot(h, win_ref[...], preferred_element_type=F32)
        bucket = bucket_ref[...]
        rid = lax.broadcasted_iota(jnp.int32, (HEAD_ROWS, 1), 0)
        bias = jnp.zeros((HEAD_ROWS, 2 * BLOCK), F32)
        sk = jnp.zeros((HEAD_ROWS, 1), F32)
        for hd in range(N_HEADS):
            bias = jnp.where(rid == hd, _bias_from_buckets(bucket, relb_ref, hd), bias)
            sk = jnp.where(rid == hd, sink_ref[hd], sk)
        bias_c[...] = bias
        sink_c[...] = sk

    r0 = pl.multiple_of(i * chunk, chunk)
    pr = pbuf[pl.ds(r0, chunk), :]
    q = pr[:, 0:ATTN_W] * SCALE
    knew = pr[:, O_K:O_V]
    vnew = pr[:, O_V:O_A]
    unew = pr[:, O_A:O_B] * _sigmoid(pr[:, O_B:])

    def per_head(t):
        n = t.shape[-1]
        return jnp.broadcast_to(t[:, None, :], (chunk, HEAD_ROWS, n)).reshape(rows_h, n)

    hid = lax.broadcasted_iota(jnp.int32, (rows_h, 1), 0) % HEAD_ROWS
    lane = lax.broadcasted_iota(jnp.int32, (1, BLOCK), 1)
    qrep = per_head(q)
    qsum = jnp.zeros((rows_h, BLOCK), F32)
    for c in range(ATTN_W // BLOCK):
        piece = qrep[:, c * BLOCK:(c + 1) * BLOCK]
        in_head = (lane // HEAD_DIM + 2 * c) == hid
        qsum = qsum + jnp.where(in_head, piece, 0.0)
    keep = (hid % 2) == (hid // GROUP)
    qf = jnp.where(keep, qsum, pltpu.roll(qsum, HEAD_DIM, axis=1))
    qf_b = qf.astype(BF16)

    bias = jnp.broadcast_to(bias_c[...][None], (chunk, HEAD_ROWS, 2 * BLOCK)).reshape(rows_h, 2 * BLOCK)
    sk = jnp.broadcast_to(sink_c[...][None], (chunk, HEAD_ROWS, 1)).reshape(rows_h, 1)

    s_rows = []
    for b in range(chunk):
        kb = ck_ref[b].astype(BF16)
        s_rows.append(lax.dot_general(qf_b[b * HEAD_ROWS:(b + 1) * HEAD_ROWS], kb, NT_DIMS,
                                      preferred_element_type=F32))
    s_c = jnp.concatenate(s_rows, axis=0) + bias[:, 0:BLOCK]
    s_n = jnp.sum(qf * per_head(knew), axis=-1, keepdims=True) + bias[:, BLOCK:BLOCK + 1]
    m = jnp.maximum(jnp.maximum(jnp.max(s_c, axis=-1, keepdims=True), s_n), sk)
    p_c = jnp.exp(s_c - m)
    p_n = jnp.exp(s_n - m)
    l = jnp.sum(p_c, axis=-1, keepdims=True) + p_n + jnp.exp(sk - m)
    p_cb = p_c.astype(BF16)
    o_rows = []
    for b in range(chunk):
        vb = cv_ref[b].astype(BF16)
        o_rows.append(jnp.dot(p_cb[b * HEAD_ROWS:(b + 1) * HEAD_ROWS], vb, preferred_element_type=F32))
    o = (jnp.concatenate(o_rows, axis=0) + p_n * per_head(vnew)) / l
    o = jnp.where(keep, o, pltpu.roll(o, HEAD_DIM, axis=1))
    o = jnp.where((lane // HEAD_DIM) == (hid % 2), o, 0.0)
    wide = jnp.concatenate([jnp.where(hid // 2 == c, o, 0.0) for c in range(ATTN_W // BLOCK)], axis=1)
    gi = lax.broadcasted_iota(jnp.int32, (chunk, rows_h), 0)
    gj = lax.broadcasted_iota(jnp.int32, (chunk, rows_h), 1)
    gather = jnp.where(gj // HEAD_ROWS == gi, 1.0, 0.0).astype(BF16)
    ao = jnp.dot(gather, wide.astype(BF16), preferred_element_type=F32)
    mix[pl.ds(r0, chunk), 0:ATTN_W] = ao

    cw_hist = cw_ref[0:CONV_WIDTH - 1, :]
    cw_new = cw_ref[CONV_WIDTH - 1:CONV_WIDTH, :]
    for b in range(chunk):
        convbuf[b:b + 1, :] = jnp.sum(st_ref[b] * cw_hist, axis=0, keepdims=True)
        nk_ref[b, 0:WINDOW - 1, :] = ck_ref[b, 1:WINDOW, :]
        nk_ref[b, WINDOW - 1:WINDOW, :] = knew[b:b + 1, :]
        nv_ref[b, 0:WINDOW - 1, :] = cv_ref[b, 1:WINDOW, :]
        nv_ref[b, WINDOW - 1:WINDOW, :] = vnew[b:b + 1, :]
        nc_ref[b, 0:CONV_WIDTH - 2, :] = st_ref[b, 1:CONV_WIDTH - 1, :]
        nc_ref[b, CONV_WIDTH - 2:CONV_WIDTH - 1, :] = unew[b:b + 1, :]
    acc = convbuf[...] + cw_new * unew + cb_ref[...]
    mu = jnp.mean(acc, axis=-1, keepdims=True)
    xc = acc - mu
    y = xc * lax.rsqrt(jnp.mean(xc * xc, axis=-1, keepdims=True) + EPS)
    y = y * lng_ref[...] + lnb_ref[...]
    mix[pl.ds(r0, chunk), ATTN_W:] = y * _sigmoid(y)

    @pl.when(i == last)
    def _out():
        x2_ref[...] = x_ref[...] + jnp.dot(mix[...].astype(BF16), wout_ref[...],
                                           preferred_element_type=F32)


def _sample_mixer(x, ck, cv, st, rel_bias, sinks, g1, w_in, bucket, cw, cb, lng, lnb, w_out):
    nb = x.shape[0]
    chunk = SAMPLE_CHUNK
    body = functools.partial(_sample_mixer_body, chunk=chunk)
    return pl.pallas_call(
        body,
        grid=(nb // chunk,),
        in_specs=[
            _smem_spec(), _smem_spec(),
            _const_spec((nb, D_MODEL)),
            pl.BlockSpec((chunk, WINDOW, KV_W), lambda i: (i, 0, 0)),
            pl.BlockSpec((chunk, WINDOW, KV_W), lambda i: (i, 0, 0)),
            pl.BlockSpec((chunk, CONV_WIDTH - 1, CONV_CH), lambda i: (i, 0, 0)),
            _const_spec((1, D_MODEL)),
            _const_spec((D_MODEL, IN_W)),
            _const_spec((1, 2 * BLOCK)),
            _const_spec((CONV_WIDTH, CONV_CH)),
            _const_spec((1, CONV_CH)),
            _const_spec((1, CONV_CH)),
            _const_spec((1, CONV_CH)),
            _const_spec((ATTN_W + CONV_CH, D_MODEL)),
        ],
        out_specs=[
            pl.BlockSpec((nb, D_MODEL), lambda i: (0, 0)),
            pl.BlockSpec((chunk, WINDOW, KV_W), lambda i: (i, 0, 0)),
            pl.BlockSpec((chunk, WINDOW, KV_W), lambda i: (i, 0, 0)),
            pl.BlockSpec((chunk, CONV_WIDTH - 1, CONV_CH), lambda i: (i, 0, 0)),
        ],
        out_shape=[
            jax.ShapeDtypeStruct((nb, D_MODEL), F32),
            jax.ShapeDtypeStruct((nb, WINDOW, KV_W), F32),
            jax.ShapeDtypeStruct((nb, WINDOW, KV_W), F32),
            jax.ShapeDtypeStruct((nb, CONV_WIDTH - 1, CONV_CH), F32),
        ],
        scratch_shapes=[
            pltpu.VMEM((nb, IN_W), F32),
            pltpu.VMEM((nb, ATTN_W + CONV_CH), F32),
            pltpu.VMEM((chunk, CONV_CH), F32),
            pltpu.VMEM((HEAD_ROWS, 2 * BLOCK), F32),
            pltpu.VMEM((HEAD_ROWS, 1), F32),
        ],
        compiler_params=pltpu.CompilerParams(
            dimension_semantics=("arbitrary",), vmem_limit_bytes=V7X_VMEM_LIMIT_BYTES),
        name="sample_mixer",
    )(rel_bias, sinks, x, ck, cv, st, g1, w_in, bucket, cw, cb, lng, lnb, w_out)


def kernel(x_prompt, x_sample, cache_k, cache_v, state_conv, meta_tokens, rel_bias, norm1_g, w_in,
           attn_sinks, conv_w, conv_b, conv_ln_g, conv_ln_b, w_out, norm2_g, w_up, w_down, norm_f_g):
    batch, seq, _ = x_prompt.shape
    nb, dec_seq, _ = x_sample.shape
    assert batch == 1 and dec_seq == 1 and w_in.shape[0] == 1
    assert seq % PROMPT_TILE == 0 and nb % SAMPLE_CHUNK == 0

    w_in_b = w_in[0].astype(BF16)
    w_out_b = w_out[0].astype(BF16)
    w_up_b = w_up[0].astype(BF16)
    w_down_b = w_down[0].astype(BF16)
    g1 = norm1_g[0][None]
    g2 = norm2_g[0][None]
    gf = norm_f_g[None]
    cw, cb = conv_w[0], conv_b[0][None]
    lng, lnb = conv_ln_g[0][None], conv_ln_b[0][None]
    sinks = attn_sinks[0]

    dist_p = jnp.arange(BLOCK)[:, None] + BLOCK - jnp.arange(2 * BLOCK)[None, :]
    bucket_p = _t5_bucket(jnp.clip(dist_p, 0, WINDOW)).astype(jnp.int32)
    lane = jnp.arange(2 * BLOCK)
    dist_s = jnp.where(lane < WINDOW, WINDOW - lane, 0)
    bucket_s = jnp.where(lane <= WINDOW, _t5_bucket(jnp.clip(dist_s, 0, WINDOW)), -1)
    bucket_s = bucket_s.astype(jnp.int32)[None]

    x2_p, nk_p, nv_p, nc_p = _prompt_mixer(x_prompt[0], meta_tokens, rel_bias, sinks, g1, w_in_b,
                                           bucket_p, cw, cb, lng, lnb, w_out_b)
    y_p = _ffn(x2_p, g2, w_up_b, w_down_b, gf, PROMPT_TILE)

    ck = cache_k[0].reshape(nb, WINDOW, KV_W)
    cv = cache_v[0].reshape(nb, WINDOW, KV_W)
    x2_s, nk_s, nv_s, nc_s = _sample_mixer(x_sample[:, 0], ck, cv, state_conv[0], rel_bias, sinks, g1,
                                           w_in_b, bucket_s, cw, cb, lng, lnb, w_out_b)
    y_s = _ffn(x2_s, g2, w_up_b, w_down_b, gf, nb)

    kv_shape = (1, WINDOW, N_KV_HEADS, HEAD_DIM)
    return (y_p[None], y_s[:, None],
            nk_p.reshape((1,) + kv_shape), nv_p.reshape((1,) + kv_shape), nc_p[None, None],
            nk_s.reshape((1, nb) + kv_shape[1:]), nv_s.reshape((1, nb) + kv_shape[1:]), nc_s[None])
```

```python
import functools
import math

import jax
import jax.numpy as jnp
from jax import lax
from jax.experimental import pallas as pl
from jax.experimental.pallas import tpu as pltpu

D_MODEL = 1024
N_HEADS = 8
N_KV_HEADS = 2
HEAD_DIM = 64
GROUP = N_HEADS // N_KV_HEADS
ATTN_W = N_HEADS * HEAD_DIM
KV_W = N_KV_HEADS * HEAD_DIM
CONV_CH = D_MODEL - ATTN_W
IN_W = ATTN_W + 2 * KV_W + 2 * CONV_CH
CONV_WIDTH = 31
WINDOW = 128
BLOCK = 128
N_BUCKETS = 32
MAX_DISTANCE = WINDOW
N_META = 16
D_FF = 4 * D_MODEL
EPS = 1e-6
SCALE = HEAD_DIM ** -0.5

O_K = ATTN_W
O_V = ATTN_W + KV_W
O_A = ATTN_W + 2 * KV_W
O_B = O_A + CONV_CH

PAD = (-N_META) % BLOCK
U_CARRY = 32
U_SHIFT = U_CARRY - (CONV_WIDTH - 1)

V7X_VMEM_LIMIT_BYTES = 60 * 1024 * 1024

PROMPT_TILE = 512
FFN_CHUNK = 1024
CONV_ROWS = 64
SAMPLE_CHUNK = 16
HEAD_ROWS = 16

BF16 = jnp.bfloat16
F32 = jnp.float32
NT_DIMS = (((1,), (1,)), ((), ()))


def _t5_bucket(d):
    max_exact = N_BUCKETS // 2
    d_f = jnp.maximum(d, 1).astype(jnp.float32)
    large = max_exact + (jnp.log(d_f / max_exact) / math.log(MAX_DISTANCE / max_exact)
                         * (N_BUCKETS - max_exact)).astype(jnp.int32)
    large = jnp.minimum(large, N_BUCKETS - 1)
    return jnp.where(d < max_exact, d, large)


def _rms(x, g):
    y = x * lax.rsqrt(jnp.mean(x * x, axis=-1, keepdims=True) + EPS)
    return y * g


def _sigmoid(x):
    return 1.0 / (1.0 + jnp.exp(-x))


def _bias_from_buckets(bucket, relb_ref, h):
    b = jnp.zeros(bucket.shape, F32)
    for bk in range(N_BUCKETS):
        b = jnp.where(bucket == bk, relb_ref[bk, h], b)
    return b


def _conv_rows(ubuf, cw_ref, r0, rows):
    n = rows + U_CARRY
    strips = []
    for c0 in range(0, CONV_CH, BLOCK):
        win = ubuf[r0:r0 + n, c0:c0 + BLOCK]
        acc = None
        for s in range(8):
            sh = win if s == 0 else pltpu.roll(win, n - s, axis=0)
            for a0 in range(0, U_CARRY + 8, 8):
                w = a0 + s - U_SHIFT
                if 0 <= w < CONV_WIDTH:
                    term = cw_ref[w:w + 1, c0:c0 + BLOCK] * sh[a0:a0 + rows]
                    acc = term if acc is None else acc + term
        strips.append(acc)
    return jnp.concatenate(strips, axis=1)


def _ln_silu(acc, lng, lnb):
    mu = jnp.mean(acc, axis=-1, keepdims=True)
    xc = acc - mu
    y = xc * lax.rsqrt(jnp.mean(xc * xc, axis=-1, keepdims=True) + EPS)
    y = y * lng + lnb
    return y * _sigmoid(y)


def _zero_after(v, prev=None):
    u = pltpu.bitcast(v, jnp.uint32)
    t = prev
    for r0 in range(0, u.shape[0], 8):
        for c0 in range(0, u.shape[1], BLOCK):
            piece = u[r0:r0 + 8, c0:c0 + BLOCK]
            t = piece if t is None else t | piece
    return (t >> 16) >> 16


def _order_after(buf, zero):
    tile = pltpu.bitcast(buf[0:16, 0:BLOCK], jnp.uint32)
    buf[0:16, 0:BLOCK] = pltpu.bitcast(tile | zero, BF16)


def _ffn_rows(x, g2, wup_ref, wdn_ref, gf):
    h = _rms(x, g2).astype(BF16)
    acc = x
    for c0 in range(0, D_FF, FFN_CHUNK):
        hid = jnp.dot(h, wup_ref[:, c0:c0 + FFN_CHUNK], preferred_element_type=F32)
        hid = jnp.square(jnp.maximum(hid, 0.0)).astype(BF16)
        acc = acc + jnp.dot(hid, wdn_ref[c0:c0 + FFN_CHUNK, :], preferred_element_type=F32)
    return _rms(acc, gf)


def _prompt_layer_body(relb_ref, sink_ref, x_ref, meta_ref, g1_ref, win_ref, bucket_ref,
                       cw_ref, cb_ref, lng_ref, lnb_ref, wout_ref, g2_ref, wup_ref, wdn_ref, gf_ref,
                       y_ref, nk_ref, nv_ref, nc_ref,
                       kbuf, vbuf, ubuf, qbuf, mix, bias_s, x2buf, kvlast, hfbuf, hidbuf, *, tile):
    i = pl.program_id(0)
    n_tiles = pl.num_programs(0) - 1
    g1 = g1_ref[...]
    slot = i % 2

    @pl.when(i == 0)
    def _init():
        x2buf[1] = jnp.zeros((tile, D_MODEL), F32)
        bucket = bucket_ref[...]
        row = lax.broadcasted_iota(jnp.int32, (BLOCK, 2 * BLOCK), 0)
        col = lax.broadcasted_iota(jnp.int32, (BLOCK, 2 * BLOCK), 1)
        dist = row + BLOCK - col
        band = (dist >= 0) & (dist <= WINDOW)
        band_first = band & (col >= PAD)
        for h in range(N_HEADS):
            b = _bias_from_buckets(bucket, relb_ref, h)
            bias_s[0, h] = jnp.where(band, b, -jnp.inf)
            bias_s[1, h] = jnp.where(band_first, b, -jnp.inf)
        hm = _rms(meta_ref[...], g1).astype(BF16)
        pm = jnp.dot(hm, win_ref[:, O_K:], preferred_element_type=F32)
        kbuf[0:PAD, :] = jnp.zeros((PAD, KV_W), BF16)
        vbuf[0:PAD, :] = jnp.zeros((PAD, KV_W), BF16)
        kbuf[PAD:BLOCK, :] = pm[:, 0:KV_W].astype(BF16)
        vbuf[PAD:BLOCK, :] = pm[:, KV_W:2 * KV_W].astype(BF16)
        um = pm[:, 2 * KV_W:2 * KV_W + CONV_CH] * _sigmoid(pm[:, 2 * KV_W + CONV_CH:])
        ubuf[0:U_CARRY - N_META, :] = jnp.zeros((U_CARRY - N_META, CONV_CH), F32)
        ubuf[U_CARRY - N_META:U_CARRY, :] = um

    x = x_ref[...]
    h = _rms(x, g1).astype(BF16)
    q = jnp.dot(h, win_ref[:, 0:ATTN_W], preferred_element_type=F32) * SCALE
    qbuf[...] = q.astype(BF16)
    kv = jnp.dot(h, win_ref[:, O_K:O_A], preferred_element_type=F32)
    kbuf[BLOCK:BLOCK + tile, :] = kv[:, 0:KV_W].astype(BF16)
    vbuf[BLOCK:BLOCK + tile, :] = kv[:, KV_W:].astype(BF16)
    kvlast[...] = kv[tile - WINDOW:, :]

    a = jnp.dot(h, win_ref[:, O_A:O_B], preferred_element_type=F32)
    b = jnp.dot(h, win_ref[:, O_B:], preferred_element_type=F32)
    ubuf[U_CARRY:U_CARRY + tile, :] = a * _sigmoid(b)

    cb, lng, lnb = cb_ref[...], lng_ref[...], lnb_ref[...]

    def conv_chunk(r0):
        c = _ln_silu(_conv_rows(ubuf, cw_ref, r0, CONV_ROWS) + cb, lng, lnb)
        mix[r0:r0 + CONV_ROWS, ATTN_W:] = c.astype(BF16)

    def attn_scores(blk, kvh):
        r0 = blk * BLOCK
        c0 = kvh * HEAD_DIM
        qg = jnp.concatenate(
            [qbuf[r0:r0 + BLOCK, (kvh * GROUP + g) * HEAD_DIM:(kvh * GROUP + g + 1) * HEAD_DIM]
             for g in range(GROUP)], axis=0)
        kk = kbuf[r0:r0 + 2 * BLOCK, c0:c0 + HEAD_DIM]
        return lax.dot_general(qg, kk, NT_DIMS, preferred_element_type=F32)

    def attn_finish(blk, kvh, s):
        r0 = blk * BLOCK
        sel = jnp.where(i == 0, 1, 0) if blk == 0 else 0
        c0 = kvh * HEAD_DIM
        vv = vbuf[r0:r0 + 2 * BLOCK, c0:c0 + HEAD_DIM]
        ps, ls = [], []
        for g in range(GROUP):
            hd = kvh * GROUP + g
            sg = s[g * BLOCK:(g + 1) * BLOCK] + bias_s[sel, hd]
            sk = sink_ref[hd]
            m = jnp.maximum(jnp.max(sg, axis=-1, keepdims=True), sk)
            p = jnp.exp(sg - m)
            ls.append(jnp.sum(p, axis=-1, keepdims=True) + jnp.exp(sk - m))
            ps.append(p.astype(BF16))
        o = jnp.dot(jnp.concatenate(ps, axis=0), vv, preferred_element_type=F32)
        for g in range(GROUP):
            hd = kvh * GROUP + g
            og = o[g * BLOCK:(g + 1) * BLOCK] / ls[g]
            mix[r0:r0 + BLOCK, hd * HEAD_DIM:(hd + 1) * HEAD_DIM] = og.astype(BF16)

    conv_starts = list(range(0, tile, CONV_ROWS))
    attn_units = [(blk, kvh) for blk in range(tile // BLOCK) for kvh in range(N_KV_HEADS)]
    n_slots = 2 * (D_FF // FFN_CHUNK)
    xf = x2buf[1 - slot]
    hfbuf[...] = _rms(xf, g2_ref[...]).astype(BF16)
    acc = xf
    for k in range(n_slots):
        c0 = (k // 2) * FFN_CHUNK
        scores = [(blk, kvh, attn_scores(blk, kvh)) for blk, kvh in attn_units[k::n_slots]]
        if k % 2 == 0:
            hid = jnp.dot(hfbuf[...], wup_ref[:, c0:c0 + FFN_CHUNK], preferred_element_type=F32)
            hidbuf[...] = jnp.square(jnp.maximum(hid, 0.0)).astype(BF16)
        else:
            acc = acc + jnp.dot(hidbuf[...], wdn_ref[c0:c0 + FFN_CHUNK, :],
                                preferred_element_type=F32)
        zero = None
        for r0 in conv_starts[k::n_slots]:
            conv_chunk(r0)
            zero = _zero_after(mix[r0:r0 + CONV_ROWS, ATTN_W:], zero)
        for blk, kvh, s in scores:
            attn_finish(blk, kvh, s)
        if k + 1 < n_slots and zero is not None:
            _order_after(hfbuf if k % 2 else hidbuf, zero)
    y_ref[...] = _rms(acc, gf_ref[...])

    x2buf[slot] = x + jnp.dot(mix[...], wout_ref[...], preferred_element_type=F32)

    @pl.when(i == n_tiles - 1)
    def _new_caches():
        nk_ref[...] = kvlast[:, 0:KV_W]
        nv_ref[...] = kvlast[:, KV_W:]
        nc_ref[...] = ubuf[U_CARRY + tile - (CONV_WIDTH - 1):U_CARRY + tile, :]

    kbuf[0:BLOCK, :] = kbuf[tile:tile + BLOCK, :]
    vbuf[0:BLOCK, :] = vbuf[tile:tile + BLOCK, :]
    ubuf[0:U_CARRY, :] = ubuf[tile:tile + U_CARRY, :]


def _const_spec(shape):
    return pl.BlockSpec(shape, lambda i: (0,) * len(shape), pipeline_mode=pl.Buffered(1))


def _smem_spec():
    return pl.BlockSpec(memory_space=pltpu.SMEM)


def _prompt_layer(x, meta, rel_bias, sinks, g1, w_in, bucket, cw, cb, lng, lnb, w_out,
                  g2, w_up, w_down, gf):
    seq = x.shape[0]
    tile = PROMPT_TILE
    n_tiles = seq // tile
    body = functools.partial(_prompt_layer_body, tile=tile)
    return pl.pallas_call(
        body,
        grid=(n_tiles + 1,),
        in_specs=[
            _smem_spec(), _smem_spec(),
            pl.BlockSpec((tile, D_MODEL), lambda i: (jnp.minimum(i, n_tiles - 1), 0)),
            _const_spec((N_META, D_MODEL)),
            _const_spec((1, D_MODEL)),
            _const_spec((D_MODEL, IN_W)),
            _const_spec((BLOCK, 2 * BLOCK)),
            _const_spec((CONV_WIDTH, CONV_CH)),
            _const_spec((1, CONV_CH)),
            _const_spec((1, CONV_CH)),
            _const_spec((1, CONV_CH)),
            _const_spec((ATTN_W + CONV_CH, D_MODEL)),
            _const_spec((1, D_MODEL)),
            _const_spec((D_MODEL, D_FF)),
            _const_spec((D_FF, D_MODEL)),
            _const_spec((1, D_MODEL)),
        ],
        out_specs=[
            pl.BlockSpec((tile, D_MODEL), lambda i: (jnp.maximum(i - 1, 0), 0)),
            pl.BlockSpec((WINDOW, KV_W), lambda i: (0, 0)),
            pl.BlockSpec((WINDOW, KV_W), lambda i: (0, 0)),
            pl.BlockSpec((CONV_WIDTH - 1, CONV_CH), lambda i: (0, 0)),
        ],
        out_shape=[
            jax.ShapeDtypeStruct((seq, D_MODEL), F32),
            jax.ShapeDtypeStruct((WINDOW, KV_W), F32),
            jax.ShapeDtypeStruct((WINDOW, KV_W), F32),
            jax.ShapeDtypeStruct((CONV_WIDTH - 1, CONV_CH), F32),
        ],
        scratch_shapes=[
            pltpu.VMEM((BLOCK + tile, KV_W), BF16),
            pltpu.VMEM((BLOCK + tile, KV_W), BF16),
            pltpu.VMEM((U_CARRY + tile, CONV_CH), F32),
            pltpu.VMEM((tile, ATTN_W), BF16),
            pltpu.VMEM((tile, ATTN_W + CONV_CH), BF16),
            pltpu.VMEM((2, N_HEADS, BLOCK, 2 * BLOCK), F32),
            pltpu.VMEM((2, tile, D_MODEL), F32),
            pltpu.VMEM((WINDOW, 2 * KV_W), F32),
            pltpu.VMEM((tile, D_MODEL), BF16),
            pltpu.VMEM((tile, FFN_CHUNK), BF16),
        ],
        compiler_params=pltpu.CompilerParams(
            dimension_semantics=("arbitrary",), vmem_limit_bytes=V7X_VMEM_LIMIT_BYTES),
        name="prompt_layer",
    )(rel_bias, sinks, x, meta, g1, w_in, bucket, cw, cb, lng, lnb, w_out, g2, w_up, w_down, gf)


def _ffn_body(x_ref, g2_ref, wup_ref, wdn_ref, gf_ref, y_ref):
    y_ref[...] = _ffn_rows(x_ref[...], g2_ref[...], wup_ref, wdn_ref, gf_ref[...])


def _ffn(x, g2, w_up, w_down, gf, tile):
    rows = x.shape[0]
    return pl.pallas_call(
        _ffn_body,
        grid=(rows // tile,),
        in_specs=[
            pl.BlockSpec((tile, D_MODEL), lambda i: (i, 0)),
            _const_spec((1, D_MODEL)),
            _const_spec((D_MODEL, D_FF)),
            _const_spec((D_FF, D_MODEL)),
            _const_spec((1, D_MODEL)),
        ],
        out_specs=pl.BlockSpec((tile, D_MODEL), lambda i: (i, 0)),
        out_shape=jax.ShapeDtypeStruct((rows, D_MODEL), F32),
        compiler_params=pltpu.CompilerParams(
            dimension_semantics=("arbitrary",), vmem_limit_bytes=V7X_VMEM_LIMIT_BYTES),
        name="ffn",
    )(x, g2, w_up, w_down, gf)


def _sample_mixer_body(relb_ref, sink_ref, x_ref, ck_ref, cv_ref, st_ref, g1_ref, win_ref,
                       bucket_ref, cw_ref, cb_ref, lng_ref, lnb_ref, wout_ref,
                       x2_ref, nk_ref, nv_ref, nc_ref,
                       pbuf, mix, convbuf, bias_c, sink_c, *, chunk):
    i = pl.program_id(0)
    last = pl.num_programs(0) - 1
    rows_h = chunk * HEAD_ROWS

    @pl.when(i == 0)
    def _init():
        h = _rms(x_ref[...], g1_ref[...]).astype(BF16)
        pbuf[...] = jnp.dot(h, win_ref[...], preferred_element_type=F32)
        bucket = bucket_ref[...]
        rid = lax.broadcasted_iota(jnp.int32, (HEAD_ROWS, 1), 0)
        bias = jnp.zeros((HEAD_ROWS, 2 * BLOCK), F32)
        sk = jnp.zeros((HEAD_ROWS, 1), F32)
        for hd in range(N_HEADS):
            bias = jnp.where(rid == hd, _bias_from_buckets(bucket, relb_ref, hd), bias)
            sk = jnp.where(rid == hd, sink_ref[hd], sk)
        bias_c[...] = bias
        sink_c[...] = sk

    r0 = pl.multiple_of(i * chunk, chunk)
    pr = pbuf[pl.ds(r0, chunk), :]
    q = pr[:, 0:ATTN_W] * SCALE
    knew = pr[:, O_K:O_V]
    vnew = pr[:, O_V:O_A]
    unew = pr[:, O_A:O_B] * _sigmoid(pr[:, O_B:])

    def per_head(t):
        n = t.shape[-1]
        return jnp.broadcast_to(t[:, None, :], (chunk, HEAD_ROWS, n)).reshape(rows_h, n)

    hid = lax.broadcasted_iota(jnp.int32, (rows_h, 1), 0) % HEAD_ROWS
    lane = lax.broadcasted_iota(jnp.int32, (1, BLOCK), 1)
    qrep = per_head(q)
    qsum = jnp.zeros((rows_h, BLOCK), F32)
    for c in range(ATTN_W // BLOCK):
        piece = qrep[:, c * BLOCK:(c + 1) * BLOCK]
        in_head = (lane // HEAD_DIM + 2 * c) == hid
        qsum = qsum + jnp.where(in_head, piece, 0.0)
    keep = (hid % 2) == (hid // GROUP)
    qf = jnp.where(keep, qsum, pltpu.roll(qsum, HEAD_DIM, axis=1))
    qf_b = qf.astype(BF16)

    bias = jnp.broadcast_to(bias_c[...][None], (chunk, HEAD_ROWS, 2 * BLOCK)).reshape(rows_h, 2 * BLOCK)
    sk = jnp.broadcast_to(sink_c[...][None], (chunk, HEAD_ROWS, 1)).reshape(rows_h, 1)

    s_rows = []
    for b in range(chunk):
        kb = ck_ref[b].astype(BF16)
        s_rows.append(lax.dot_general(qf_b[b * HEAD_ROWS:(b + 1) * HEAD_ROWS], kb, NT_DIMS,
                                      preferred_element_type=F32))
    s_c = jnp.concatenate(s_rows, axis=0) + bias[:, 0:BLOCK]
    s_n = jnp.sum(qf * per_head(knew), axis=-1, keepdims=True) + bias[:, BLOCK:BLOCK + 1]
    m = jnp.maximum(jnp.maximum(jnp.max(s_c, axis=-1, keepdims=True), s_n), sk)
    p_c = jnp.exp(s_c - m)
    p_n = jnp.exp(s_n - m)
    l = jnp.sum(p_c, axis=-1, keepdims=True) + p_n + jnp.exp(sk - m)
    p_cb = p_c.astype(BF16)
    o_rows = []
    for b in range(chunk):
        vb = cv_ref[b].astype(BF16)
        o_rows.append(jnp.dot(p_cb[b * HEAD_ROWS:(b + 1) * HEAD_ROWS], vb, preferred_element_type=F32))
    o = (jnp.concatenate(o_rows, axis=0) + p_n * per_head(vnew)) / l
    o = jnp.where(keep, o, pltpu.roll(o, HEAD_DIM, axis=1))
    o = jnp.where((lane // HEAD_DIM) == (hid % 2), o, 0.0)
    wide = jnp.concatenate([jnp.where(hid // 2 == c, o, 0.0) for c in range(ATTN_W // BLOCK)], axis=1)
    gi = lax.broadcasted_iota(jnp.int32, (chunk, rows_h), 0)
    gj = lax.broadcasted_iota(jnp.int32, (chunk, rows_h), 1)
    gather = jnp.where(gj // HEAD_ROWS == gi, 1.0, 0.0).astype(BF16)
    ao = jnp.dot(gather, wide.astype(BF16), preferred_element_type=F32)
    mix[pl.ds(r0, chunk), 0:ATTN_W] = ao

    cw_hist = cw_ref[0:CONV_WIDTH - 1, :]
    cw_new = cw_ref[CONV_WIDTH - 1:CONV_WIDTH, :]
    for b in range(chunk):
        convbuf[b:b + 1, :] = jnp.sum(st_ref[b] * cw_hist, axis=0, keepdims=True)
        nk_ref[b, 0:WINDOW - 1, :] = ck_ref[b, 1:WINDOW, :]
        nk_ref[b, WINDOW - 1:WINDOW, :] = knew[b:b + 1, :]
        nv_ref[b, 0:WINDOW - 1, :] = cv_ref[b, 1:WINDOW, :]
        nv_ref[b, WINDOW - 1:WINDOW, :] = vnew[b:b + 1, :]
        nc_ref[b, 0:CONV_WIDTH - 2, :] = st_ref[b, 1:CONV_WIDTH - 1, :]
        nc_ref[b, CONV_WIDTH - 2:CONV_WIDTH - 1, :] = unew[b:b + 1, :]
    acc = convbuf[...] + cw_new * unew + cb_ref[...]
    mu = jnp.mean(acc, axis=-1, keepdims=True)
    xc = acc - mu
    y = xc * lax.rsqrt(jnp.mean(xc * xc, axis=-1, keepdims=True) + EPS)
    y = y * lng_ref[...] + lnb_ref[...]
    mix[pl.ds(r0, chunk), ATTN_W:] = y * _sigmoid(y)

    @pl.when(i == last)
    def _out():
        x2_ref[...] = x_ref[...] + jnp.dot(mix[...].astype(BF16), wout_ref[...],
                                           preferred_element_type=F32)


def _sample_mixer(x, ck, cv, st, rel_bias, sinks, g1, w_in, bucket, cw, cb, lng, lnb, w_out):
    nb = x.shape[0]
    chunk = SAMPLE_CHUNK
    body = functools.partial(_sample_mixer_body, chunk=chunk)
    return pl.pallas_call(
        body,
        grid=(nb // chunk,),
        in_specs=[
            _smem_spec(), _smem_spec(),
            _const_spec((nb, D_MODEL)),
            pl.BlockSpec((chunk, WINDOW, KV_W), lambda i: (i, 0, 0)),
            pl.BlockSpec((chunk, WINDOW, KV_W), lambda i: (i, 0, 0)),
            pl.BlockSpec((chunk, CONV_WIDTH - 1, CONV_CH), lambda i: (i, 0, 0)),
            _const_spec((1, D_MODEL)),
            _const_spec((D_MODEL, IN_W)),
            _const_spec((1, 2 * BLOCK)),
            _const_spec((CONV_WIDTH, CONV_CH)),
            _const_spec((1, CONV_CH)),
            _const_spec((1, CONV_CH)),
            _const_spec((1, CONV_CH)),
            _const_spec((ATTN_W + CONV_CH, D_MODEL)),
        ],
        out_specs=[
            pl.BlockSpec((nb, D_MODEL), lambda i: (0, 0)),
            pl.BlockSpec((chunk, WINDOW, KV_W), lambda i: (i, 0, 0)),
            pl.BlockSpec((chunk, WINDOW, KV_W), lambda i: (i, 0, 0)),
            pl.BlockSpec((chunk, CONV_WIDTH - 1, CONV_CH), lambda i: (i, 0, 0)),
        ],
        out_shape=[
            jax.ShapeDtypeStruct((nb, D_MODEL), F32),
            jax.ShapeDtypeStruct((nb, WINDOW, KV_W), F32),
            jax.ShapeDtypeStruct((nb, WINDOW, KV_W), F32),
            jax.ShapeDtypeStruct((nb, CONV_WIDTH - 1, CONV_CH), F32),
        ],
        scratch_shapes=[
            pltpu.VMEM((nb, IN_W), F32),
            pltpu.VMEM((nb, ATTN_W + CONV_CH), F32),
            pltpu.VMEM((chunk, CONV_CH), F32),
            pltpu.VMEM((HEAD_ROWS, 2 * BLOCK), F32),
            pltpu.VMEM((HEAD_ROWS, 1), F32),
        ],
        compiler_params=pltpu.CompilerParams(
            dimension_semantics=("arbitrary",), vmem_limit_bytes=V7X_VMEM_LIMIT_BYTES),
        name="sample_mixer",
    )(rel_bias, sinks, x, ck, cv, st, g1, w_in, bucket, cw, cb, lng, lnb, w_out)


def kernel(x_prompt, x_sample, cache_k, cache_v, state_conv, meta_tokens, rel_bias, norm1_g, w_in,
           attn_sinks, conv_w, conv_b, conv_ln_g, conv_ln_b, w_out, norm2_g, w_up, w_down, norm_f_g):
    batch, seq, _ = x_prompt.shape
    nb, dec_seq, _ = x_sample.shape
    assert batch == 1 and dec_seq == 1 and w_in.shape[0] == 1
    assert seq % PROMPT_TILE == 0 and nb % SAMPLE_CHUNK == 0

    w_in_b = w_in[0].astype(BF16)
    w_out_b = w_out[0].astype(BF16)
    w_up_b = w_up[0].astype(BF16)
    w_down_b = w_down[0].astype(BF16)
    g1 = norm1_g[0][None]
    g2 = norm2_g[0][None]
    gf = norm_f_g[None]
    cw, cb = conv_w[0], conv_b[0][None]
    lng, lnb = conv_ln_g[0][None], conv_ln_b[0][None]
    sinks = attn_sinks[0]

    dist_p = jnp.arange(BLOCK)[:, None] + BLOCK - jnp.arange(2 * BLOCK)[None, :]
    bucket_p = _t5_bucket(jnp.clip(dist_p, 0, WINDOW)).astype(jnp.int32)
    lane = jnp.arange(2 * BLOCK)
    dist_s = jnp.where(lane < WINDOW, WINDOW - lane, 0)
    bucket_s = jnp.where(lane <= WINDOW, _t5_bucket(jnp.clip(dist_s, 0, WINDOW)), -1)
    bucket_s = bucket_s.astype(jnp.int32)[None]

    y_p, nk_p, nv_p, nc_p = _prompt_layer(x_prompt[0], meta_tokens, rel_bias, sinks, g1, w_in_b,
                                          bucket_p, cw, cb, lng, lnb, w_out_b, g2, w_up_b, w_down_b, gf)

    ck = cache_k[0].reshape(nb, WINDOW, KV_W)
    cv = cache_v[0].reshape(nb, WINDOW, KV_W)
    x2_s, nk_s, nv_s, nc_s = _sample_mixer(x_sample[:, 0], ck, cv, state_conv[0], rel_bias, sinks, g1,
                                           w_in_b, bucket_s, cw, cb, lng, lnb, w_out_b)
    y_s = _ffn(x2_s, g2, w_up_b, w_down_b, gf, nb)

    kv_shape = (1, WINDOW, N_KV_HEADS, HEAD_DIM)
    return (y_p[None], y_s[:, None],
            nk_p.reshape((1,) + kv_shape), nv_p.reshape((1,) + kv_shape), nc_p[None, None],
            nk_s.reshape((1, nb) + kv_shape[1:]), nv_s.reshape((1, nb) + kv_shape[1:]), nc_s[None])
```

```python
import functools
import math

import jax
import jax.numpy as jnp
from jax import lax
from jax.experimental import pallas as pl
from jax.experimental.pallas import tpu as pltpu

D_MODEL = 1024
N_HEADS = 8
N_KV_HEADS = 2
HEAD_DIM = 64
GROUP = N_HEADS // N_KV_HEADS
ATTN_W = N_HEADS * HEAD_DIM
KV_W = N_KV_HEADS * HEAD_DIM
CONV_CH = D_MODEL - ATTN_W
IN_W = ATTN_W + 2 * KV_W + 2 * CONV_CH
CONV_WIDTH = 31
WINDOW = 128
BLOCK = 128
N_BUCKETS = 32
MAX_DISTANCE = WINDOW
N_META = 16
D_FF = 4 * D_MODEL
EPS = 1e-6
SCALE = HEAD_DIM ** -0.5
LOG2E = math.log2(math.e)

O_K = ATTN_W
O_V = ATTN_W + KV_W
O_A = ATTN_W + 2 * KV_W
O_B = O_A + CONV_CH

PAD = (-N_META) % BLOCK
U_CARRY = 32
U_SHIFT = U_CARRY - (CONV_WIDTH - 1)

V7X_VMEM_LIMIT_BYTES = 60 * 1024 * 1024

PROMPT_TILE = 512
FFN_CHUNK = 2048
CONV_ROWS = 32
SAMPLE_CHUNK = 16
HEAD_ROWS = 16

BF16 = jnp.bfloat16
F32 = jnp.float32
NT_DIMS = (((1,), (1,)), ((), ()))


def _t5_bucket(d):
    max_exact = N_BUCKETS // 2
    d_f = jnp.maximum(d, 1).astype(jnp.float32)
    large = max_exact + (jnp.log(d_f / max_exact) / math.log(MAX_DISTANCE / max_exact)
                         * (N_BUCKETS - max_exact)).astype(jnp.int32)
    large = jnp.minimum(large, N_BUCKETS - 1)
    return jnp.where(d < max_exact, d, large)


def _rms(x, g):
    y = x * lax.rsqrt(jnp.mean(x * x, axis=-1, keepdims=True) + EPS)
    return y * g


def _sigmoid(x):
    return 1.0 / (1.0 + jnp.exp(-x))


def _bias_from_buckets(bucket, relb_ref, h):
    b = jnp.zeros(bucket.shape, F32)
    for bk in range(N_BUCKETS):
        b = jnp.where(bucket == bk, relb_ref[bk, h], b)
    return b


def _conv_rows(ubuf, cw_ref, r0, rows, gate=None):
    n = rows + U_CARRY
    strips = []
    for c0 in range(0, CONV_CH, BLOCK):
        if gate is not None:
            for r in range(r0, r0 + n, 8):
                words = _as_words(ubuf[r:r + 8, c0:c0 + BLOCK]) | gate
                ubuf[r:r + 8, c0:c0 + BLOCK] = lax.bitcast_convert_type(words, F32)
        win = ubuf[r0:r0 + n, c0:c0 + BLOCK]
        acc = None
        for s in range(8):
            sh = win if s == 0 else pltpu.roll(win, n - s, axis=0)
            for a0 in range(0, U_CARRY + 8, 8):
                w = a0 + s - U_SHIFT
                if 0 <= w < CONV_WIDTH:
                    term = cw_ref[w:w + 1, c0:c0 + BLOCK] * sh[a0:a0 + rows]
                    acc = term if acc is None else acc + term
        strips.append(acc)
        if gate is not None:
            gate = _zero_after(acc)
    return jnp.concatenate(strips, axis=1)


def _ln_silu(acc, lng, lnb):
    mu = jnp.mean(acc, axis=-1, keepdims=True)
    xc = acc - mu
    y = xc * lax.rsqrt(jnp.mean(xc * xc, axis=-1, keepdims=True) + EPS)
    y = y * lng + lnb
    return y * _sigmoid(y)


def _as_words(v):
    return pltpu.bitcast(v, jnp.uint32) if v.dtype == BF16 else lax.bitcast_convert_type(v, jnp.uint32)


def _zero_after(v, prev=None):
    u = _as_words(v)
    t = prev
    for r0 in range(0, u.shape[0], 8):
        for c0 in range(0, u.shape[1], BLOCK):
            piece = u[r0:r0 + 8, c0:c0 + BLOCK]
            t = piece if t is None else t | piece
    return (t >> 16) >> 16


def _order_after(buf, r0, zero, lanes=BLOCK):
    rows = 16 if buf.dtype == BF16 else 8
    for c0 in range(0, lanes, BLOCK):
        words = _as_words(buf[r0:r0 + rows, c0:c0 + BLOCK]) | zero
        if buf.dtype == BF16:
            buf[r0:r0 + rows, c0:c0 + BLOCK] = pltpu.bitcast(words, BF16)
        else:
            buf[r0:r0 + rows, c0:c0 + BLOCK] = lax.bitcast_convert_type(words, buf.dtype)


def _ffn_rows(x, g2, wup_ref, wdn_ref, gf):
    h = _rms(x, g2).astype(BF16)
    acc = x
    for c0 in range(0, D_FF, FFN_CHUNK):
        hid = jnp.dot(h, wup_ref[:, c0:c0 + FFN_CHUNK], preferred_element_type=F32)
        hid = jnp.square(jnp.maximum(hid, 0.0)).astype(BF16)
        acc = acc + jnp.dot(hid, wdn_ref[c0:c0 + FFN_CHUNK, :], preferred_element_type=F32)
    return _rms(acc, gf)


def _prompt_layer_body(relb_ref, sink_ref, x_ref, meta_ref, g1_ref, win_ref, bucket_ref,
                       cw_ref, cb_ref, lng_ref, lnb_ref, wout_ref, g2_ref, wup_ref, wdn_ref, gf_ref,
                       y_ref, nk_ref, nv_ref, nc_ref,
                       kbuf, vtbuf, ubuf, qbuf, mix, bias_t, sink_t, x2buf, kvlast, hfbuf, hidbuf,
                       sbuf, pbuf, lbuf, *, tile):
    i = pl.program_id(0)
    n_tiles = pl.num_programs(0) - 1
    g1 = g1_ref[...]
    slot = i % 2

    @pl.when(i == 0)
    def _init():
        x2buf[1] = jnp.zeros((tile, D_MODEL), F32)
        bucket = bucket_ref[...]
        key = lax.broadcasted_iota(jnp.int32, (2 * BLOCK, BLOCK), 0)
        qry = lax.broadcasted_iota(jnp.int32, (2 * BLOCK, BLOCK), 1)
        dist = qry + BLOCK - key
        band = (dist >= 0) & (dist <= WINDOW)
        band_first = band & (key >= PAD)
        for h in range(N_HEADS):
            b = _bias_from_buckets(bucket, relb_ref, h)
            kvh, g = divmod(h, GROUP)
            b = b * LOG2E
            bias_t[0, kvh, :, g * BLOCK:(g + 1) * BLOCK] = jnp.where(band, b, -jnp.inf)
            bias_t[1, kvh, :, g * BLOCK:(g + 1) * BLOCK] = jnp.where(band_first, b, -jnp.inf)
            sink_t[kvh, :, g * BLOCK:(g + 1) * BLOCK] = jnp.full((8, BLOCK), sink_ref[h] * LOG2E, F32)
        hm = _rms(meta_ref[...], g1).astype(BF16)
        pm = jnp.dot(hm, win_ref[:, O_K:], preferred_element_type=F32)
        no_keys = jnp.zeros((PAD, KV_W), F32)
        k0 = jnp.concatenate([no_keys, pm[:, 0:KV_W]], axis=0)
        v0 = jnp.concatenate([no_keys, pm[:, KV_W:2 * KV_W]], axis=0)
        for kvh in range(N_KV_HEADS):
            kbuf[kvh, 0:BLOCK, :] = k0[:, kvh * HEAD_DIM:(kvh + 1) * HEAD_DIM].astype(BF16)
        vtbuf[:, 0:BLOCK] = v0.T.astype(BF16)
        um = pm[:, 2 * KV_W:2 * KV_W + CONV_CH] * _sigmoid(pm[:, 2 * KV_W + CONV_CH:])
        ubuf[0:U_CARRY - N_META, :] = jnp.zeros((U_CARRY - N_META, CONV_CH), F32)
        ubuf[U_CARRY - N_META:U_CARRY, :] = um

    x = x_ref[...]
    h = _rms(x, g1).astype(BF16)
    q = jnp.dot(h, win_ref[:, 0:ATTN_W], preferred_element_type=F32) * (SCALE * LOG2E)
    for hd in range(N_HEADS):
        qbuf[hd] = q[:, hd * HEAD_DIM:(hd + 1) * HEAD_DIM].astype(BF16)
    kv = jnp.dot(h, win_ref[:, O_K:O_A], preferred_element_type=F32)
    for kvh in range(N_KV_HEADS):
        kbuf[kvh, BLOCK:BLOCK + tile, :] = kv[:, kvh * HEAD_DIM:(kvh + 1) * HEAD_DIM].astype(BF16)
    vtbuf[:, BLOCK:BLOCK + tile] = kv[:, KV_W:].T.astype(BF16)
    kvlast[...] = kv[tile - WINDOW:, :]

    a = jnp.dot(h, win_ref[:, O_A:O_B], preferred_element_type=F32)
    b = jnp.dot(h, win_ref[:, O_B:], preferred_element_type=F32)
    ubuf[U_CARRY:U_CARRY + tile, :] = a * _sigmoid(b)

    cb, lng, lnb = cb_ref[...], lng_ref[...], lnb_ref[...]

    def conv_chunk(r0, gate=None):
        c = _ln_silu(_conv_rows(ubuf, cw_ref, r0, CONV_ROWS, gate) + cb, lng, lnb)
        mix[r0:r0 + CONV_ROWS, ATTN_W:] = c.astype(BF16)

    attn_units = [(blk, kvh) for blk in range(tile // BLOCK) for kvh in range(N_KV_HEADS)]

    def attn_scores(u):
        blk, kvh = attn_units[u]
        r0 = blk * BLOCK
        keys = kbuf[kvh, r0:r0 + 2 * BLOCK, :]
        qs = qbuf[kvh * GROUP:(kvh + 1) * GROUP, r0:r0 + BLOCK, :]
        qs = qs.reshape(GROUP * BLOCK, HEAD_DIM)
        sbuf[u] = lax.dot_general(keys, qs, NT_DIMS, preferred_element_type=F32)

    def attn_softmax(u):
        blk, kvh = attn_units[u]
        sel = jnp.where(i == 0, 1, 0) if blk == 0 else 0
        st = sbuf[u] + bias_t[sel, kvh]
        sk = sink_t[kvh, 0:1, :]
        m = jnp.maximum(jnp.max(st, axis=0, keepdims=True), sk)
        p = jnp.exp2(st - m)
        l = jnp.sum(p, axis=0, keepdims=True) + jnp.exp2(sk - m)
        pbuf[u] = p.astype(BF16)
        lbuf[u] = jnp.broadcast_to(l, (8, GROUP * BLOCK))

    def attn_values(u):
        blk, kvh = attn_units[u]
        r0 = blk * BLOCK
        vt = vtbuf[kvh * HEAD_DIM:(kvh + 1) * HEAD_DIM, r0:r0 + 2 * BLOCK]
        ot = jnp.dot(vt, pbuf[u], preferred_element_type=F32) / lbuf[u, 0:1, :]
        for pair in range(GROUP // 2):
            two = jnp.concatenate([ot[:, (2 * pair) * BLOCK:(2 * pair + 1) * BLOCK],
                                   ot[:, (2 * pair + 1) * BLOCK:(2 * pair + 2) * BLOCK]], axis=0)
            c0 = (kvh * GROUP + 2 * pair) * HEAD_DIM
            mix[r0:r0 + BLOCK, c0:c0 + 2 * HEAD_DIM] = two.T.astype(BF16)

    conv_starts = list(range(0, tile, CONV_ROWS))
    n_slots = 2 * (D_FF // FFN_CHUNK)
    half = n_slots // 2
    n_units = len(attn_units)
    for u in range(n_units):
        attn_scores(u)
    xf = x2buf[1 - slot]
    hfbuf[...] = _rms(xf, g2_ref[...]).astype(BF16)
    acc = xf
    prev_done = _zero_after(hfbuf[0:16, 0:BLOCK])
    for k in range(n_slots):
        c0 = (k // 2) * FFN_CHUNK
        if k == half:
            for u in range(n_units):
                attn_values(u)
        if k % 2 == 0:
            hid = jnp.dot(hfbuf[...], wup_ref[:, c0:c0 + FFN_CHUNK], preferred_element_type=F32)
            hid = jnp.maximum(hid.astype(BF16), 0.0)
            hidbuf[...] = hid * hid
            done = _zero_after(hidbuf[0:16, 0:BLOCK])
        else:
            acc = acc + jnp.dot(hidbuf[...], wdn_ref[c0:c0 + FFN_CHUNK, :],
                                preferred_element_type=F32)
            done = _zero_after(acc[0:8, 0:BLOCK])
        piece_done = None
        if k < half:
            for u in range(k * n_units // half, (k + 1) * n_units // half):
                _order_after(sbuf.at[u], 0, prev_done, lanes=GROUP * BLOCK)
                attn_softmax(u)
                piece_done = _zero_after(lbuf[u, :, 0:BLOCK], piece_done)
        per = len(conv_starts) // n_slots
        conv_gate = prev_done
        for r0 in conv_starts[k * per:(k + 1) * per]:
            conv_chunk(r0, conv_gate)
            conv_gate = _zero_after(mix[r0:r0 + CONV_ROWS, ATTN_W:])
            piece_done = conv_gate if piece_done is None else piece_done | conv_gate
        if k + 1 < n_slots:
            _order_after(hfbuf if k % 2 else hidbuf, 0, piece_done)
        prev_done = done
    y_ref[...] = _rms(acc, gf_ref[...])

    x2buf[slot] = x + jnp.dot(mix[...], wout_ref[...], preferred_element_type=F32)

    @pl.when(i == n_tiles - 1)
    def _new_caches():
        nk_ref[...] = kvlast[:, 0:KV_W]
        nv_ref[...] = kvlast[:, KV_W:]
        nc_ref[...] = ubuf[U_CARRY + tile - (CONV_WIDTH - 1):U_CARRY + tile, :]

    kbuf[:, 0:BLOCK, :] = kbuf[:, tile:tile + BLOCK, :]
    vtbuf[:, 0:BLOCK] = vtbuf[:, tile:tile + BLOCK]
    ubuf[0:U_CARRY, :] = ubuf[tile:tile + U_CARRY, :]


def _const_spec(shape):
    return pl.BlockSpec(shape, lambda i: (0,) * len(shape), pipeline_mode=pl.Buffered(1))


def _smem_spec():
    return pl.BlockSpec(memory_space=pltpu.SMEM)


def _prompt_layer(x, meta, rel_bias, sinks, g1, w_in, bucket, cw, cb, lng, lnb, w_out,
                  g2, w_up, w_down, gf):
    seq = x.shape[0]
    tile = PROMPT_TILE
    n_tiles = seq // tile
    n_units = (tile // BLOCK) * N_KV_HEADS
    body = functools.partial(_prompt_layer_body, tile=tile)
    return pl.pallas_call(
        body,
        grid=(n_tiles + 1,),
        in_specs=[
            _smem_spec(), _smem_spec(),
            pl.BlockSpec((tile, D_MODEL), lambda i: (jnp.minimum(i, n_tiles - 1), 0)),
            _const_spec((N_META, D_MODEL)),
            _const_spec((1, D_MODEL)),
            _const_spec((D_MODEL, IN_W)),
            _const_spec((2 * BLOCK, BLOCK)),
            _const_spec((CONV_WIDTH, CONV_CH)),
            _const_spec((1, CONV_CH)),
            _const_spec((1, CONV_CH)),
            _const_spec((1, CONV_CH)),
            _const_spec((ATTN_W + CONV_CH, D_MODEL)),
            _const_spec((1, D_MODEL)),
            _const_spec((D_MODEL, D_FF)),
            _const_spec((D_FF, D_MODEL)),
            _const_spec((1, D_MODEL)),
        ],
        out_specs=[
            pl.BlockSpec((tile, D_MODEL), lambda i: (jnp.maximum(i - 1, 0), 0)),
            pl.BlockSpec((WINDOW, KV_W), lambda i: (0, 0)),
            pl.BlockSpec((WINDOW, KV_W), lambda i: (0, 0)),
            pl.BlockSpec((CONV_WIDTH - 1, CONV_CH), lambda i: (0, 0)),
        ],
        out_shape=[
            jax.ShapeDtypeStruct((seq, D_MODEL), F32),
            jax.ShapeDtypeStruct((WINDOW, KV_W), F32),
            jax.ShapeDtypeStruct((WINDOW, KV_W), F32),
            jax.ShapeDtypeStruct((CONV_WIDTH - 1, CONV_CH), F32),
        ],
        scratch_shapes=[
            pltpu.VMEM((N_KV_HEADS, BLOCK + tile, HEAD_DIM), BF16),
            pltpu.VMEM((KV_W, BLOCK + tile), BF16),
            pltpu.VMEM((U_CARRY + tile, CONV_CH), F32),
            pltpu.VMEM((N_HEADS, tile, HEAD_DIM), BF16),
            pltpu.VMEM((tile, ATTN_W + CONV_CH), BF16),
            pltpu.VMEM((2, N_KV_HEADS, 2 * BLOCK, GROUP * BLOCK), F32),
            pltpu.VMEM((N_KV_HEADS, 8, GROUP * BLOCK), F32),
            pltpu.VMEM((2, tile, D_MODEL), F32),
            pltpu.VMEM((WINDOW, 2 * KV_W), F32),
            pltpu.VMEM((tile, D_MODEL), BF16),
            pltpu.VMEM((tile, FFN_CHUNK), BF16),
            pltpu.VMEM((n_units, 2 * BLOCK, GROUP * BLOCK), F32),
            pltpu.VMEM((n_units, 2 * BLOCK, GROUP * BLOCK), BF16),
            pltpu.VMEM((n_units, 8, GROUP * BLOCK), F32),
        ],
        compiler_params=pltpu.CompilerParams(
            dimension_semantics=("arbitrary",), vmem_limit_bytes=V7X_VMEM_LIMIT_BYTES),
        name="prompt_layer",
    )(rel_bias, sinks, x, meta, g1, w_in, bucket, cw, cb, lng, lnb, w_out, g2, w_up, w_down, gf)


def _ffn_body(x_ref, g2_ref, wup_ref, wdn_ref, gf_ref, y_ref):
    y_ref[...] = _ffn_rows(x_ref[...], g2_ref[...], wup_ref, wdn_ref, gf_ref[...])


def _ffn(x, g2, w_up, w_down, gf, tile):
    rows = x.shape[0]
    return pl.pallas_call(
        _ffn_body,
        grid=(rows // tile,),
        in_specs=[
            pl.BlockSpec((tile, D_MODEL), lambda i: (i, 0)),
            _const_spec((1, D_MODEL)),
            _const_spec((D_MODEL, D_FF)),
            _const_spec((D_FF, D_MODEL)),
            _const_spec((1, D_MODEL)),
        ],
        out_specs=pl.BlockSpec((tile, D_MODEL), lambda i: (i, 0)),
        out_shape=jax.ShapeDtypeStruct((rows, D_MODEL), F32),
        compiler_params=pltpu.CompilerParams(
            dimension_semantics=("arbitrary",), vmem_limit_bytes=V7X_VMEM_LIMIT_BYTES),
        name="ffn",
    )(x, g2, w_up, w_down, gf)


def _sample_mixer_body(relb_ref, sink_ref, x_ref, ck_ref, cv_ref, st_ref, g1_ref, win_ref,
                       bucket_ref, cw_ref, cb_ref, lng_ref, lnb_ref, wout_ref,
                       x2_ref, nk_ref, nv_ref, nc_ref,
                       pbuf, mix, convbuf, bias_c, sink_c, *, chunk):
    i = pl.program_id(0)
    last = pl.num_programs(0) - 1
    rows_h = chunk * HEAD_ROWS

    @pl.when(i == 0)
    def _init():
        h = _rms(x_ref[...], g1_ref[...]).astype(BF16)
        pbuf[...] = jnp.dot(h, win_ref[...], preferred_element_type=F32)
        bucket = bucket_ref[...]
        rid = lax.broadcasted_iota(jnp.int32, (HEAD_ROWS, 1), 0)
        bias = jnp.zeros((HEAD_ROWS, 2 * BLOCK), F32)
        sk = jnp.zeros((HEAD_ROWS, 1), F32)
        for hd in range(N_HEADS):
            bias = jnp.where(rid == hd, _bias_from_buckets(bucket, relb_ref, hd), bias)
            sk = jnp.where(rid == hd, sink_ref[hd], sk)
        bias_c[...] = bias
        sink_c[...] = sk

    r0 = pl.multiple_of(i * chunk, chunk)
    pr = pbuf[pl.ds(r0, chunk), :]
    q = pr[:, 0:ATTN_W] * SCALE
    knew = pr[:, O_K:O_V]
    vnew = pr[:, O_V:O_A]
    unew = pr[:, O_A:O_B] * _sigmoid(pr[:, O_B:])

    def per_head(t):
        n = t.shape[-1]
        return jnp.broadcast_to(t[:, None, :], (chunk, HEAD_ROWS, n)).reshape(rows_h, n)

    hid = lax.broadcasted_iota(jnp.int32, (rows_h, 1), 0) % HEAD_ROWS
    lane = lax.broadcasted_iota(jnp.int32, (1, BLOCK), 1)
    qrep = per_head(q)
    qsum = jnp.zeros((rows_h, BLOCK), F32)
    for c in range(ATTN_W // BLOCK):
        piece = qrep[:, c * BLOCK:(c + 1) * BLOCK]
        in_head = (lane // HEAD_DIM + 2 * c) == hid
        qsum = qsum + jnp.where(in_head, piece, 0.0)
    keep = (hid % 2) == (hid // GROUP)
    qf = jnp.where(keep, qsum, pltpu.roll(qsum, HEAD_DIM, axis=1))
    qf_b = qf.astype(BF16)

    bias = jnp.broadcast_to(bias_c[...][None], (chunk, HEAD_ROWS, 2 * BLOCK)).reshape(rows_h, 2 * BLOCK)
    sk = jnp.broadcast_to(sink_c[...][None], (chunk, HEAD_ROWS, 1)).reshape(rows_h, 1)

    s_rows = []
    for b in range(chunk):
        kb = ck_ref[b].astype(BF16)
        s_rows.append(lax.dot_general(qf_b[b * HEAD_ROWS:(b + 1) * HEAD_ROWS], kb, NT_DIMS,
                                      preferred_element_type=F32))
    s_c = jnp.concatenate(s_rows, axis=0) + bias[:, 0:BLOCK]
    s_n = jnp.sum(qf * per_head(knew), axis=-1, keepdims=True) + bias[:, BLOCK:BLOCK + 1]
    m = jnp.maximum(jnp.maximum(jnp.max(s_c, axis=-1, keepdims=True), s_n), sk)
    p_c = jnp.exp(s_c - m)
    p_n = jnp.exp(s_n - m)
    l = jnp.sum(p_c, axis=-1, keepdims=True) + p_n + jnp.exp(sk - m)
    p_cb = p_c.astype(BF16)
    o_rows = []
    for b in range(chunk):
        vb = cv_ref[b].astype(BF16)
        o_rows.append(jnp.dot(p_cb[b * HEAD_ROWS:(b + 1) * HEAD_ROWS], vb, preferred_element_type=F32))
    o = (jnp.concatenate(o_rows, axis=0) + p_n * per_head(vnew)) / l
    o = jnp.where(keep, o, pltpu.roll(o, HEAD_DIM, axis=1))
    o = jnp.where((lane // HEAD_DIM) == (hid % 2), o, 0.0)
    wide = jnp.concatenate([jnp.where(hid // 2 == c, o, 0.0) for c in range(ATTN_W // BLOCK)], axis=1)
    gi = lax.broadcasted_iota(jnp.int32, (chunk, rows_h), 0)
    gj = lax.broadcasted_iota(jnp.int32, (chunk, rows_h), 1)
    gather = jnp.where(gj // HEAD_ROWS == gi, 1.0, 0.0).astype(BF16)
    ao = jnp.dot(gather, wide.astype(BF16), preferred_element_type=F32)
    mix[pl.ds(r0, chunk), 0:ATTN_W] = ao

    cw_hist = cw_ref[0:CONV_WIDTH - 1, :]
    cw_new = cw_ref[CONV_WIDTH - 1:CONV_WIDTH, :]
    for b in range(chunk):
        convbuf[b:b + 1, :] = jnp.sum(st_ref[b] * cw_hist, axis=0, keepdims=True)
        nk_ref[b, 0:WINDOW - 1, :] = ck_ref[b, 1:WINDOW, :]
        nk_ref[b, WINDOW - 1:WINDOW, :] = knew[b:b + 1, :]
        nv_ref[b, 0:WINDOW - 1, :] = cv_ref[b, 1:WINDOW, :]
        nv_ref[b, WINDOW - 1:WINDOW, :] = vnew[b:b + 1, :]
        nc_ref[b, 0:CONV_WIDTH - 2, :] = st_ref[b, 1:CONV_WIDTH - 1, :]
        nc_ref[b, CONV_WIDTH - 2:CONV_WIDTH - 1, :] = unew[b:b + 1, :]
    acc = convbuf[...] + cw_new * unew + cb_ref[...]
    mu = jnp.mean(acc, axis=-1, keepdims=True)
    xc = acc - mu
    y = xc * lax.rsqrt(jnp.mean(xc * xc, axis=-1, keepdims=True) + EPS)
    y = y * lng_ref[...] + lnb_ref[...]
    mix[pl.ds(r0, chunk), ATTN_W:] = y * _sigmoid(y)

    @pl.when(i == last)
    def _out():
        x2_ref[...] = x_ref[...] + jnp.dot(mix[...].astype(BF16), wout_ref[...],
                                           preferred_element_type=F32)


def _sample_mixer(x, ck, cv, st, rel_bias, sinks, g1, w_in, bucket, cw, cb, lng, lnb, w_out):
    nb = x.shape[0]
    chunk = SAMPLE_CHUNK
    body = functools.partial(_sample_mixer_body, chunk=chunk)
    return pl.pallas_call(
        body,
        grid=(nb // chunk,),
        in_specs=[
            _smem_spec(), _smem_spec(),
            _const_spec((nb, D_MODEL)),
            pl.BlockSpec((chunk, WINDOW, KV_W), lambda i: (i, 0, 0)),
            pl.BlockSpec((chunk, WINDOW, KV_W), lambda i: (i, 0, 0)),
            pl.BlockSpec((chunk, CONV_WIDTH - 1, CONV_CH), lambda i: (i, 0, 0)),
            _const_spec((1, D_MODEL)),
            _const_spec((D_MODEL, IN_W)),
            _const_spec((1, 2 * BLOCK)),
            _const_spec((CONV_WIDTH, CONV_CH)),
            _const_spec((1, CONV_CH)),
            _const_spec((1, CONV_CH)),
            _const_spec((1, CONV_CH)),
            _const_spec((ATTN_W + CONV_CH, D_MODEL)),
        ],
        out_specs=[
            pl.BlockSpec((nb, D_MODEL), lambda i: (0, 0)),
            pl.BlockSpec((chunk, WINDOW, KV_W), lambda i: (i, 0, 0)),
            pl.BlockSpec((chunk, WINDOW, KV_W), lambda i: (i, 0, 0)),
            pl.BlockSpec((chunk, CONV_WIDTH - 1, CONV_CH), lambda i: (i, 0, 0)),
        ],
        out_shape=[
            jax.ShapeDtypeStruct((nb, D_MODEL), F32),
            jax.ShapeDtypeStruct((nb, WINDOW, KV_W), F32),
            jax.ShapeDtypeStruct((nb, WINDOW, KV_W), F32),
            jax.ShapeDtypeStruct((nb, CONV_WIDTH - 1, CONV_CH), F32),
        ],
        scratch_shapes=[
            pltpu.VMEM((nb, IN_W), F32),
            pltpu.VMEM((nb, ATTN_W + CONV_CH), F32),
            pltpu.VMEM((chunk, CONV_CH), F32),
            pltpu.VMEM((HEAD_ROWS, 2 * BLOCK), F32),
            pltpu.VMEM((HEAD_ROWS, 1), F32),
        ],
        compiler_params=pltpu.CompilerParams(
            dimension_semantics=("arbitrary",), vmem_limit_bytes=V7X_VMEM_LIMIT_BYTES),
        name="sample_mixer",
    )(rel_bias, sinks, x, ck, cv, st, g1, w_in, bucket, cw, cb, lng, lnb, w_out)


def kernel(x_prompt, x_sample, cache_k, cache_v, state_conv, meta_tokens, rel_bias, norm1_g, w_in,
           attn_sinks, conv_w, conv_b, conv_ln_g, conv_ln_b, w_out, norm2_g, w_up, w_down, norm_f_g):
    batch, seq, _ = x_prompt.shape
    nb, dec_seq, _ = x_sample.shape
    assert batch == 1 and dec_seq == 1 and w_in.shape[0] == 1
    assert seq % PROMPT_TILE == 0 and nb % SAMPLE_CHUNK == 0

    w_in_b = w_in[0].astype(BF16)
    w_out_b = w_out[0].astype(BF16)
    w_up_b = w_up[0].astype(BF16)
    w_down_b = w_down[0].astype(BF16)
    g1 = norm1_g[0][None]
    g2 = norm2_g[0][None]
    gf = norm_f_g[None]
    cw, cb = conv_w[0], conv_b[0][None]
    lng, lnb = conv_ln_g[0][None], conv_ln_b[0][None]
    sinks = attn_sinks[0]

    dist_p = jnp.arange(BLOCK)[None, :] + BLOCK - jnp.arange(2 * BLOCK)[:, None]
    bucket_p = _t5_bucket(jnp.clip(dist_p, 0, WINDOW)).astype(jnp.int32)
    lane = jnp.arange(2 * BLOCK)
    dist_s = jnp.where(lane < WINDOW, WINDOW - lane, 0)
    bucket_s = jnp.where(lane <= WINDOW, _t5_bucket(jnp.clip(dist_s, 0, WINDOW)), -1)
    bucket_s = bucket_s.astype(jnp.int32)[None]

    y_p, nk_p, nv_p, nc_p = _prompt_layer(x_prompt[0], meta_tokens, rel_bias, sinks, g1, w_in_b,
                                          bucket_p, cw, cb, lng, lnb, w_out_b, g2, w_up_b, w_down_b, gf)

    ck = cache_k[0].reshape(nb, WINDOW, KV_W)
    cv = cache_v[0].reshape(nb, WINDOW, KV_W)
    x2_s, nk_s, nv_s, nc_s = _sample_mixer(x_sample[:, 0], ck, cv, state_conv[0], rel_bias, sinks, g1,
                                           w_in_b, bucket_s, cw, cb, lng, lnb, w_out_b)
    y_s = _ffn(x2_s, g2, w_up_b, w_down_b, gf, nb)

    kv_shape = (1, WINDOW, N_KV_HEADS, HEAD_DIM)
    return (y_p[None], y_s[:, None],
            nk_p.reshape((1,) + kv_shape), nv_p.reshape((1,) + kv_shape), nc_p[None, None],
            nk_s.reshape((1, nb) + kv_shape[1:]), nv_s.reshape((1, nb) + kv_shape[1:]), nc_s[None])
```

```python
import functools
import math

import jax
import jax.numpy as jnp
from jax import lax
from jax.experimental import pallas as pl
from jax.experimental.pallas import tpu as pltpu

D_MODEL = 1024
N_HEADS = 8
N_KV_HEADS = 2
HEAD_DIM = 64
GROUP = N_HEADS // N_KV_HEADS
ATTN_W = N_HEADS * HEAD_DIM
KV_W = N_KV_HEADS * HEAD_DIM
CONV_CH = D_MODEL - ATTN_W
IN_W = ATTN_W + 2 * KV_W + 2 * CONV_CH
CONV_WIDTH = 31
WINDOW = 128
BLOCK = 128
N_BUCKETS = 32
MAX_DISTANCE = WINDOW
N_META = 16
D_FF = 4 * D_MODEL
EPS = 1e-6
SCALE = HEAD_DIM ** -0.5

O_K = ATTN_W
O_V = ATTN_W + KV_W
O_A = ATTN_W + 2 * KV_W
O_B = O_A + CONV_CH

PAD = (-N_META) % BLOCK
U_CARRY = 32
U_SHIFT = U_CARRY - (CONV_WIDTH - 1)

V7X_VMEM_LIMIT_BYTES = 60 * 1024 * 1024

PROMPT_TILE = 512
FFN_CHUNK = 1024
CONV_ROWS = 64
SAMPLE_CHUNK = 16
HEAD_ROWS = 16

BF16 = jnp.bfloat16
F32 = jnp.float32
NT_DIMS = (((1,), (1,)), ((), ()))


def _t5_bucket(d):
    max_exact = N_BUCKETS // 2
    d_f = jnp.maximum(d, 1).astype(jnp.float32)
    large = max_exact + (jnp.log(d_f / max_exact) / math.log(MAX_DISTANCE / max_exact)
                         * (N_BUCKETS - max_exact)).astype(jnp.int32)
    large = jnp.minimum(large, N_BUCKETS - 1)
    return jnp.where(d < max_exact, d, large)


def _rms(x, g):
    y = x * lax.rsqrt(jnp.mean(x * x, axis=-1, keepdims=True) + EPS)
    return y * g


def _sigmoid(x):
    return 1.0 / (1.0 + jnp.exp(-x))


def _bias_from_buckets(bucket, relb_ref, h):
    b = jnp.zeros(bucket.shape, F32)
    for bk in range(N_BUCKETS):
        b = jnp.where(bucket == bk, relb_ref[bk, h], b)
    return b


def _conv_rows(ubuf, cw_ref, r0, rows):
    n = rows + U_CARRY
    strips = []
    for c0 in range(0, CONV_CH, BLOCK):
        win = ubuf[r0:r0 + n, c0:c0 + BLOCK]
        acc = None
        for s in range(8):
            sh = win if s == 0 else pltpu.roll(win, n - s, axis=0)
            for a0 in range(0, U_CARRY + 8, 8):
                w = a0 + s - U_SHIFT
                if 0 <= w < CONV_WIDTH:
                    term = cw_ref[w:w + 1, c0:c0 + BLOCK] * sh[a0:a0 + rows]
                    acc = term if acc is None else acc + term
        strips.append(acc)
    return jnp.concatenate(strips, axis=1)


def _ln_silu(acc, lng, lnb):
    mu = jnp.mean(acc, axis=-1, keepdims=True)
    xc = acc - mu
    y = xc * lax.rsqrt(jnp.mean(xc * xc, axis=-1, keepdims=True) + EPS)
    y = y * lng + lnb
    return y * _sigmoid(y)


def _zero_after(v, prev=None):
    u = pltpu.bitcast(v, jnp.uint32)
    t = prev
    for r0 in range(0, u.shape[0], 8):
        for c0 in range(0, u.shape[1], BLOCK):
            piece = u[r0:r0 + 8, c0:c0 + BLOCK]
            t = piece if t is None else t | piece
    return (t >> 16) >> 16


def _order_after(buf, zero):
    tile = pltpu.bitcast(buf[0:16, 0:BLOCK], jnp.uint32)
    buf[0:16, 0:BLOCK] = pltpu.bitcast(tile | zero, BF16)


def _ffn_rows(x, g2, wup_ref, wdn_ref, gf):
    h = _rms(x, g2).astype(BF16)
    acc = x
    for c0 in range(0, D_FF, FFN_CHUNK):
        hid = jnp.dot(h, wup_ref[:, c0:c0 + FFN_CHUNK], preferred_element_type=F32)
        hid = jnp.square(jnp.maximum(hid, 0.0)).astype(BF16)
        acc = acc + jnp.dot(hid, wdn_ref[c0:c0 + FFN_CHUNK, :], preferred_element_type=F32)
    return _rms(acc, gf)


def _prompt_layer_body(relb_ref, sink_ref, x_ref, meta_ref, g1_ref, win_ref, bucket_ref,
                       cw_ref, cb_ref, lng_ref, lnb_ref, wout_ref, g2_ref, wup_ref, wdn_ref, gf_ref,
                       x2s_ref, y_ref, nk_ref, nv_ref, nc_ref, ys_ref,
                       kbuf, vbuf, ubuf, qbuf, mix, bias_s, x2buf, kvlast, hfbuf, hidbuf, *, tile):
    i = pl.program_id(0)
    n_tiles = pl.num_programs(0) - 1
    g1 = g1_ref[...]
    slot = i % 2

    @pl.when(i == 0)
    def _init():
        x2buf[1] = jnp.zeros((tile, D_MODEL), F32)
        bucket = bucket_ref[...]
        row = lax.broadcasted_iota(jnp.int32, (BLOCK, 2 * BLOCK), 0)
        col = lax.broadcasted_iota(jnp.int32, (BLOCK, 2 * BLOCK), 1)
        dist = row + BLOCK - col
        band = (dist >= 0) & (dist <= WINDOW)
        band_first = band & (col >= PAD)
        for h in range(N_HEADS):
            b = _bias_from_buckets(bucket, relb_ref, h)
            bias_s[0, h] = jnp.where(band, b, -jnp.inf)
            bias_s[1, h] = jnp.where(band_first, b, -jnp.inf)
        hm = _rms(meta_ref[...], g1).astype(BF16)
        pm = jnp.dot(hm, win_ref[:, O_K:], preferred_element_type=F32)
        kbuf[0:PAD, :] = jnp.zeros((PAD, KV_W), BF16)
        vbuf[0:PAD, :] = jnp.zeros((PAD, KV_W), BF16)
        kbuf[PAD:BLOCK, :] = pm[:, 0:KV_W].astype(BF16)
        vbuf[PAD:BLOCK, :] = pm[:, KV_W:2 * KV_W].astype(BF16)
        um = pm[:, 2 * KV_W:2 * KV_W + CONV_CH] * _sigmoid(pm[:, 2 * KV_W + CONV_CH:])
        ubuf[0:U_CARRY - N_META, :] = jnp.zeros((U_CARRY - N_META, CONV_CH), F32)
        ubuf[U_CARRY - N_META:U_CARRY, :] = um

    x = x_ref[...]
    h = _rms(x, g1).astype(BF16)
    q = jnp.dot(h, win_ref[:, 0:ATTN_W], preferred_element_type=F32) * SCALE
    qbuf[...] = q.astype(BF16)
    kv = jnp.dot(h, win_ref[:, O_K:O_A], preferred_element_type=F32)
    kbuf[BLOCK:BLOCK + tile, :] = kv[:, 0:KV_W].astype(BF16)
    vbuf[BLOCK:BLOCK + tile, :] = kv[:, KV_W:].astype(BF16)
    kvlast[...] = kv[tile - WINDOW:, :]

    a = jnp.dot(h, win_ref[:, O_A:O_B], preferred_element_type=F32)
    b = jnp.dot(h, win_ref[:, O_B:], preferred_element_type=F32)
    ubuf[U_CARRY:U_CARRY + tile, :] = a * _sigmoid(b)

    cb, lng, lnb = cb_ref[...], lng_ref[...], lnb_ref[...]

    def conv_chunk(r0):
        c = _ln_silu(_conv_rows(ubuf, cw_ref, r0, CONV_ROWS) + cb, lng, lnb)
        mix[r0:r0 + CONV_ROWS, ATTN_W:] = c.astype(BF16)

    def attn_scores(blk, kvh):
        r0 = blk * BLOCK
        c0 = kvh * HEAD_DIM
        qg = jnp.concatenate(
            [qbuf[r0:r0 + BLOCK, (kvh * GROUP + g) * HEAD_DIM:(kvh * GROUP + g + 1) * HEAD_DIM]
             for g in range(GROUP)], axis=0)
        kk = kbuf[r0:r0 + 2 * BLOCK, c0:c0 + HEAD_DIM]
        return lax.dot_general(qg, kk, NT_DIMS, preferred_element_type=F32)

    def attn_finish(blk, kvh, s):
        r0 = blk * BLOCK
        sel = jnp.where(i == 0, 1, 0) if blk == 0 else 0
        c0 = kvh * HEAD_DIM
        vv = vbuf[r0:r0 + 2 * BLOCK, c0:c0 + HEAD_DIM]
        ps, ls = [], []
        for g in range(GROUP):
            hd = kvh * GROUP + g
            sg = s[g * BLOCK:(g + 1) * BLOCK] + bias_s[sel, hd]
            sk = sink_ref[hd]
            m = jnp.maximum(jnp.max(sg, axis=-1, keepdims=True), sk)
            p = jnp.exp(sg - m)
            ls.append(jnp.sum(p, axis=-1, keepdims=True) + jnp.exp(sk - m))
            ps.append(p.astype(BF16))
        o = jnp.dot(jnp.concatenate(ps, axis=0), vv, preferred_element_type=F32)
        for g in range(GROUP):
            hd = kvh * GROUP + g
            og = o[g * BLOCK:(g + 1) * BLOCK] / ls[g]
            mix[r0:r0 + BLOCK, hd * HEAD_DIM:(hd + 1) * HEAD_DIM] = og.astype(BF16)

    conv_starts = list(range(0, tile, CONV_ROWS))
    attn_units = [(blk, kvh) for blk in range(tile // BLOCK) for kvh in range(N_KV_HEADS)]
    n_slots = 2 * (D_FF // FFN_CHUNK)
    xf = x2buf[1 - slot]
    hfbuf[...] = _rms(xf, g2_ref[...]).astype(BF16)
    acc = xf
    for k in range(n_slots):
        c0 = (k // 2) * FFN_CHUNK
        scores = [(blk, kvh, attn_scores(blk, kvh)) for blk, kvh in attn_units[k::n_slots]]
        if k % 2 == 0:
            hid = jnp.dot(hfbuf[...], wup_ref[:, c0:c0 + FFN_CHUNK], preferred_element_type=F32)
            hidbuf[...] = jnp.square(jnp.maximum(hid, 0.0)).astype(BF16)
        else:
            acc = acc + jnp.dot(hidbuf[...], wdn_ref[c0:c0 + FFN_CHUNK, :],
                                preferred_element_type=F32)
        zero = None
        for r0 in conv_starts[k::n_slots]:
            conv_chunk(r0)
            zero = _zero_after(mix[r0:r0 + CONV_ROWS, ATTN_W:], zero)
        for blk, kvh, s in scores:
            attn_finish(blk, kvh, s)
        if k + 1 < n_slots and zero is not None:
            _order_after(hfbuf if k % 2 else hidbuf, zero)
    y_ref[...] = _rms(acc, gf_ref[...])

    x2buf[slot] = x + jnp.dot(mix[...], wout_ref[...], preferred_element_type=F32)

    @pl.when(i == n_tiles - 1)
    def _new_caches():
        nk_ref[...] = kvlast[:, 0:KV_W]
        nv_ref[...] = kvlast[:, KV_W:]
        nc_ref[...] = ubuf[U_CARRY + tile - (CONV_WIDTH - 1):U_CARRY + tile, :]

    kbuf[0:BLOCK, :] = kbuf[tile:tile + BLOCK, :]
    vbuf[0:BLOCK, :] = vbuf[tile:tile + BLOCK, :]
    ubuf[0:U_CARRY, :] = ubuf[tile:tile + U_CARRY, :]

    @pl.when(i == n_tiles)
    def _sample_ffn():
        ys_ref[...] = _ffn_rows(x2s_ref[...], g2_ref[...], wup_ref, wdn_ref, gf_ref[...])


def _const_spec(shape):
    return pl.BlockSpec(shape, lambda i: (0,) * len(shape), pipeline_mode=pl.Buffered(1))


def _smem_spec():
    return pl.BlockSpec(memory_space=pltpu.SMEM)


def _prompt_layer(x, meta, rel_bias, sinks, g1, w_in, bucket, cw, cb, lng, lnb, w_out,
                  g2, w_up, w_down, gf, x2_s):
    seq = x.shape[0]
    nb = x2_s.shape[0]
    tile = PROMPT_TILE
    n_tiles = seq // tile
    body = functools.partial(_prompt_layer_body, tile=tile)
    return pl.pallas_call(
        body,
        grid=(n_tiles + 1,),
        in_specs=[
            _smem_spec(), _smem_spec(),
            pl.BlockSpec((tile, D_MODEL), lambda i: (jnp.minimum(i, n_tiles - 1), 0)),
            _const_spec((N_META, D_MODEL)),
            _const_spec((1, D_MODEL)),
            _const_spec((D_MODEL, IN_W)),
            _const_spec((BLOCK, 2 * BLOCK)),
            _const_spec((CONV_WIDTH, CONV_CH)),
            _const_spec((1, CONV_CH)),
            _const_spec((1, CONV_CH)),
            _const_spec((1, CONV_CH)),
            _const_spec((ATTN_W + CONV_CH, D_MODEL)),
            _const_spec((1, D_MODEL)),
            _const_spec((D_MODEL, D_FF)),
            _const_spec((D_FF, D_MODEL)),
            _const_spec((1, D_MODEL)),
            _const_spec((nb, D_MODEL)),
        ],
        out_specs=[
            pl.BlockSpec((tile, D_MODEL), lambda i: (jnp.maximum(i - 1, 0), 0)),
            pl.BlockSpec((WINDOW, KV_W), lambda i: (0, 0)),
            pl.BlockSpec((WINDOW, KV_W), lambda i: (0, 0)),
            pl.BlockSpec((CONV_WIDTH - 1, CONV_CH), lambda i: (0, 0)),
            pl.BlockSpec((nb, D_MODEL), lambda i: (0, 0)),
        ],
        out_shape=[
            jax.ShapeDtypeStruct((seq, D_MODEL), F32),
            jax.ShapeDtypeStruct((WINDOW, KV_W), F32),
            jax.ShapeDtypeStruct((WINDOW, KV_W), F32),
            jax.ShapeDtypeStruct((CONV_WIDTH - 1, CONV_CH), F32),
            jax.ShapeDtypeStruct((nb, D_MODEL), F32),
        ],
        scratch_shapes=[
            pltpu.VMEM((BLOCK + tile, KV_W), BF16),
            pltpu.VMEM((BLOCK + tile, KV_W), BF16),
            pltpu.VMEM((U_CARRY + tile, CONV_CH), F32),
            pltpu.VMEM((tile, ATTN_W), BF16),
            pltpu.VMEM((tile, ATTN_W + CONV_CH), BF16),
            pltpu.VMEM((2, N_HEADS, BLOCK, 2 * BLOCK), F32),
            pltpu.VMEM((2, tile, D_MODEL), F32),
            pltpu.VMEM((WINDOW, 2 * KV_W), F32),
            pltpu.VMEM((tile, D_MODEL), BF16),
            pltpu.VMEM((tile, FFN_CHUNK), BF16),
        ],
        compiler_params=pltpu.CompilerParams(
            dimension_semantics=("arbitrary",), vmem_limit_bytes=V7X_VMEM_LIMIT_BYTES),
        name="prompt_layer",
    )(rel_bias, sinks, x, meta, g1, w_in, bucket, cw, cb, lng, lnb, w_out, g2, w_up, w_down, gf,
      x2_s)


def _sample_mixer_body(relb_ref, sink_ref, x_ref, ck_ref, cv_ref, st_ref, g1_ref, win_ref,
                       bucket_ref, cw_ref, cb_ref, lng_ref, lnb_ref, wout_ref,
                       x2_ref, nk_ref, nv_ref, nc_ref,
                       pbuf, mix, convbuf, bias_c, sink_c, *, chunk):
    i = pl.program_id(0)
    last = pl.num_programs(0) - 1
    rows_h = chunk * HEAD_ROWS

    @pl.when(i == 0)
    def _init():
        h = _rms(x_ref[...], g1_ref[...]).astype(BF16)
        pbuf[...] = jnp.dot(h, win_ref[...], preferred_element_type=F32)
        bucket = bucket_ref[...]
        rid = lax.broadcasted_iota(jnp.int32, (HEAD_ROWS, 1), 0)
        bias = jnp.zeros((HEAD_ROWS, 2 * BLOCK), F32)
        sk = jnp.zeros((HEAD_ROWS, 1), F32)
        for hd in range(N_HEADS):
            bias = jnp.where(rid == hd, _bias_from_buckets(bucket, relb_ref, hd), bias)
            sk = jnp.where(rid == hd, sink_ref[hd], sk)
        bias_c[...] = bias
        sink_c[...] = sk

    r0 = pl.multiple_of(i * chunk, chunk)
    pr = pbuf[pl.ds(r0, chunk), :]
    q = pr[:, 0:ATTN_W] * SCALE
    knew = pr[:, O_K:O_V]
    vnew = pr[:, O_V:O_A]
    unew = pr[:, O_A:O_B] * _sigmoid(pr[:, O_B:])

    def per_head(t):
        n = t.shape[-1]
        return jnp.broadcast_to(t[:, None, :], (chunk, HEAD_ROWS, n)).reshape(rows_h, n)

    hid = lax.broadcasted_iota(jnp.int32, (rows_h, 1), 0) % HEAD_ROWS
    lane = lax.broadcasted_iota(jnp.int32, (1, BLOCK), 1)
    qrep = per_head(q)
    qsum = jnp.zeros((rows_h, BLOCK), F32)
    for c in range(ATTN_W // BLOCK):
        piece = qrep[:, c * BLOCK:(c + 1) * BLOCK]
        in_head = (lane // HEAD_DIM + 2 * c) == hid
        qsum = qsum + jnp.where(in_head, piece, 0.0)
    keep = (hid % 2) == (hid // GROUP)
    qf = jnp.where(keep, qsum, pltpu.roll(qsum, HEAD_DIM, axis=1))
    qf_b = qf.astype(BF16)

    bias = jnp.broadcast_to(bias_c[...][None], (chunk, HEAD_ROWS, 2 * BLOCK)).reshape(rows_h, 2 * BLOCK)
    sk = jnp.broadcast_to(sink_c[...][None], (chunk, HEAD_ROWS, 1)).reshape(rows_h, 1)

    s_rows = []
    for b in range(chunk):
        kb = ck_ref[b].astype(BF16)
        s_rows.append(lax.dot_general(qf_b[b * HEAD_ROWS:(b + 1) * HEAD_ROWS], kb, NT_DIMS,
                                      preferred_element_type=F32))
    s_c = jnp.concatenate(s_rows, axis=0) + bias[:, 0:BLOCK]
    s_n = jnp.sum(qf * per_head(knew), axis=-1, keepdims=True) + bias[:, BLOCK:BLOCK + 1]
    m = jnp.maximum(jnp.maximum(jnp.max(s_c, axis=-1, keepdims=True), s_n), sk)
    p_c = jnp.exp(s_c - m)
    p_n = jnp.exp(s_n - m)
    l = jnp.sum(p_c, axis=-1, keepdims=True) + p_n + jnp.exp(sk - m)
    p_cb = p_c.astype(BF16)
    o_rows = []
    for b in range(chunk):
        vb = cv_ref[b].astype(BF16)
        o_rows.append(jnp.dot(p_cb[b * HEAD_ROWS:(b + 1) * HEAD_ROWS], vb, preferred_element_type=F32))
    o = (jnp.concatenate(o_rows, axis=0) + p_n * per_head(vnew)) / l
    o = jnp.where(keep, o, pltpu.roll(o, HEAD_DIM, axis=1))
    o = jnp.where((lane // HEAD_DIM) == (hid % 2), o, 0.0)
    wide = jnp.concatenate([jnp.where(hid // 2 == c, o, 0.0) for c in range(ATTN_W // BLOCK)], axis=1)
    gi = lax.broadcasted_iota(jnp.int32, (chunk, rows_h), 0)
    gj = lax.broadcasted_iota(jnp.int32, (chunk, rows_h), 1)
    gather = jnp.where(gj // HEAD_ROWS == gi, 1.0, 0.0).astype(BF16)
    ao = jnp.dot(gather, wide.astype(BF16), preferred_element_type=F32)
    mix[pl.ds(r0, chunk), 0:ATTN_W] = ao

    cw_hist = cw_ref[0:CONV_WIDTH - 1, :]
    cw_new = cw_ref[CONV_WIDTH - 1:CONV_WIDTH, :]
    for b in range(chunk):
        convbuf[b:b + 1, :] = jnp.sum(st_ref[b] * cw_hist, axis=0, keepdims=True)
        nk_ref[b, 0:WINDOW - 1, :] = ck_ref[b, 1:WINDOW, :]
        nk_ref[b, WINDOW - 1:WINDOW, :] = knew[b:b + 1, :]
        nv_ref[b, 0:WINDOW - 1, :] = cv_ref[b, 1:WINDOW, :]
        nv_ref[b, WINDOW - 1:WINDOW, :] = vnew[b:b + 1, :]
        nc_ref[b, 0:CONV_WIDTH - 2, :] = st_ref[b, 1:CONV_WIDTH - 1, :]
        nc_ref[b, CONV_WIDTH - 2:CONV_WIDTH - 1, :] = unew[b:b + 1, :]
    acc = convbuf[...] + cw_new * unew + cb_ref[...]
    mu = jnp.mean(acc, axis=-1, keepdims=True)
    xc = acc - mu
    y = xc * lax.rsqrt(jnp.mean(xc * xc, axis=-1, keepdims=True) + EPS)
    y = y * lng_ref[...] + lnb_ref[...]
    mix[pl.ds(r0, chunk), ATTN_W:] = y * _sigmoid(y)

    @pl.when(i == last)
    def _out():
        x2_ref[...] = x_ref[...] + jnp.dot(mix[...].astype(BF16), wout_ref[...],
                                           preferred_element_type=F32)


def _sample_mixer(x, ck, cv, st, rel_bias, sinks, g1, w_in, bucket, cw, cb, lng, lnb, w_out):
    nb = x.shape[0]
    chunk = SAMPLE_CHUNK
    body = functools.partial(_sample_mixer_body, chunk=chunk)
    return pl.pallas_call(
        body,
        grid=(nb // chunk,),
        in_specs=[
            _smem_spec(), _smem_spec(),
            _const_spec((nb, D_MODEL)),
            pl.BlockSpec((chunk, WINDOW, KV_W), lambda i: (i, 0, 0)),
            pl.BlockSpec((chunk, WINDOW, KV_W), lambda i: (i, 0, 0)),
            pl.BlockSpec((chunk, CONV_WIDTH - 1, CONV_CH), lambda i: (i, 0, 0)),
            _const_spec((1, D_MODEL)),
            _const_spec((D_MODEL, IN_W)),
            _const_spec((1, 2 * BLOCK)),
            _const_spec((CONV_WIDTH, CONV_CH)),
            _const_spec((1, CONV_CH)),
            _const_spec((1, CONV_CH)),
            _const_spec((1, CONV_CH)),
            _const_spec((ATTN_W + CONV_CH, D_MODEL)),
        ],
        out_specs=[
            pl.BlockSpec((nb, D_MODEL), lambda i: (0, 0)),
            pl.BlockSpec((chunk, WINDOW, KV_W), lambda i: (i, 0, 0)),
            pl.BlockSpec((chunk, WINDOW, KV_W), lambda i: (i, 0, 0)),
            pl.BlockSpec((chunk, CONV_WIDTH - 1, CONV_CH), lambda i: (i, 0, 0)),
        ],
        out_shape=[
            jax.ShapeDtypeStruct((nb, D_MODEL), F32),
            jax.ShapeDtypeStruct((nb, WINDOW, KV_W), F32),
            jax.ShapeDtypeStruct((nb, WINDOW, KV_W), F32),
            jax.ShapeDtypeStruct((nb, CONV_WIDTH - 1, CONV_CH), F32),
        ],
        scratch_shapes=[
            pltpu.VMEM((nb, IN_W), F32),
            pltpu.VMEM((nb, ATTN_W + CONV_CH), F32),
            pltpu.VMEM((chunk, CONV_CH), F32),
            pltpu.VMEM((HEAD_ROWS, 2 * BLOCK), F32),
            pltpu.VMEM((HEAD_ROWS, 1), F32),
        ],
        compiler_params=pltpu.CompilerParams(
            dimension_semantics=("arbitrary",), vmem_limit_bytes=V7X_VMEM_LIMIT_BYTES),
        name="sample_mixer",
    )(rel_bias, sinks, x, ck, cv, st, g1, w_in, bucket, cw, cb, lng, lnb, w_out)


def kernel(x_prompt, x_sample, cache_k, cache_v, state_conv, meta_tokens, rel_bias, norm1_g, w_in,
           attn_sinks, conv_w, conv_b, conv_ln_g, conv_ln_b, w_out, norm2_g, w_up, w_down, norm_f_g):
    batch, seq, _ = x_prompt.shape
    nb, dec_seq, _ = x_sample.shape
    assert batch == 1 and dec_seq == 1 and w_in.shape[0] == 1
    assert seq % PROMPT_TILE == 0 and nb % SAMPLE_CHUNK == 0

    w_in_b = w_in[0].astype(BF16)
    w_out_b = w_out[0].astype(BF16)
    w_up_b = w_up[0].astype(BF16)
    w_down_b = w_down[0].astype(BF16)
    g1 = norm1_g[0][None]
    g2 = norm2_g[0][None]
    gf = norm_f_g[None]
    cw, cb = conv_w[0], conv_b[0][None]
    lng, lnb = conv_ln_g[0][None], conv_ln_b[0][None]
    sinks = attn_sinks[0]

    dist_p = jnp.arange(BLOCK)[:, None] + BLOCK - jnp.arange(2 * BLOCK)[None, :]
    bucket_p = _t5_bucket(jnp.clip(dist_p, 0, WINDOW)).astype(jnp.int32)
    lane = jnp.arange(2 * BLOCK)
    dist_s = jnp.where(lane < WINDOW, WINDOW - lane, 0)
    bucket_s = jnp.where(lane <= WINDOW, _t5_bucket(jnp.clip(dist_s, 0, WINDOW)), -1)
    bucket_s = bucket_s.astype(jnp.int32)[None]

    ck = cache_k[0].reshape(nb, WINDOW, KV_W)
    cv = cache_v[0].reshape(nb, WINDOW, KV_W)
    x2_s, nk_s, nv_s, nc_s = _sample_mixer(x_sample[:, 0], ck, cv, state_conv[0], rel_bias, sinks, g1,
                                           w_in_b, bucket_s, cw, cb, lng, lnb, w_out_b)
    y_p, nk_p, nv_p, nc_p, y_s = _prompt_layer(x_prompt[0], meta_tokens, rel_bias, sinks, g1, w_in_b,
                                               bucket_p, cw, cb, lng, lnb, w_out_b, g2, w_up_b,
                                               w_down_b, gf, x2_s)

    kv_shape = (1, WINDOW, N_KV_HEADS, HEAD_DIM)
    return (y_p[None], y_s[:, None],
            nk_p.reshape((1,) + kv_shape), nv_p.reshape((1,) + kv_shape), nc_p[None, None],
            nk_s.reshape((1, nb) + kv_shape[1:]), nv_s.reshape((1, nb) + kv_shape[1:]), nc_s[None])
```

```python
import functools
import math

import jax
import jax.numpy as jnp
from jax import lax
from jax.experimental import pallas as pl
from jax.experimental.pallas import tpu as pltpu

D_MODEL = 1024
N_HEADS = 8
N_KV_HEADS = 2
HEAD_DIM = 64
GROUP = N_HEADS // N_KV_HEADS
ATTN_W = N_HEADS * HEAD_DIM
KV_W = N_KV_HEADS * HEAD_DIM
CONV_CH = D_MODEL - ATTN_W
IN_W = ATTN_W + 2 * KV_W + 2 * CONV_CH
CONV_WIDTH = 31
WINDOW = 128
BLOCK = 128
N_BUCKETS = 32
MAX_DISTANCE = WINDOW
N_META = 16
D_FF = 4 * D_MODEL
EPS = 1e-6
SCALE = HEAD_DIM ** -0.5

O_K = ATTN_W
O_V = ATTN_W + KV_W
O_A = ATTN_W + 2 * KV_W
O_B = O_A + CONV_CH

PAD = (-N_META) % BLOCK
U_CARRY = 32
U_SHIFT = U_CARRY - (CONV_WIDTH - 1)

V7X_VMEM_LIMIT_BYTES = 60 * 1024 * 1024

PROMPT_TILE = 512
FFN_CHUNK = 1024
CONV_ROWS = 64
SAMPLE_CHUNK = 16
HEAD_ROWS = 16

BF16 = jnp.bfloat16
F32 = jnp.float32
NT_DIMS = (((1,), (1,)), ((), ()))


def _t5_bucket(d):
    max_exact = N_BUCKETS // 2
    d_f = jnp.maximum(d, 1).astype(jnp.float32)
    large = max_exact + (jnp.log(d_f / max_exact) / math.log(MAX_DISTANCE / max_exact)
                         * (N_BUCKETS - max_exact)).astype(jnp.int32)
    large = jnp.minimum(large, N_BUCKETS - 1)
    return jnp.where(d < max_exact, d, large)


def _rms(x, g):
    y = x * lax.rsqrt(jnp.mean(x * x, axis=-1, keepdims=True) + EPS)
    return y * g


def _sigmoid(x):
    return 1.0 / (1.0 + jnp.exp(-x))


def _bias_from_buckets(bucket, relb_ref, h):
    b = jnp.zeros(bucket.shape, F32)
    for bk in range(N_BUCKETS):
        b = jnp.where(bucket == bk, relb_ref[bk, h], b)
    return b


def _conv_rows(ubuf, cw_ref, r0, rows):
    n = rows + U_CARRY
    strips = []
    for c0 in range(0, CONV_CH, BLOCK):
        win = ubuf[r0:r0 + n, c0:c0 + BLOCK]
        acc = None
        for s in range(8):
            sh = win if s == 0 else pltpu.roll(win, n - s, axis=0)
            for a0 in range(0, U_CARRY + 8, 8):
                w = a0 + s - U_SHIFT
                if 0 <= w < CONV_WIDTH:
                    term = cw_ref[w:w + 1, c0:c0 + BLOCK] * sh[a0:a0 + rows]
                    acc = term if acc is None else acc + term
        strips.append(acc)
    return jnp.concatenate(strips, axis=1)


def _ln_silu(acc, lng, lnb):
    mu = jnp.mean(acc, axis=-1, keepdims=True)
    xc = acc - mu
    y = xc * lax.rsqrt(jnp.mean(xc * xc, axis=-1, keepdims=True) + EPS)
    y = y * lng + lnb
    return y * _sigmoid(y)


def _zero_after(v, prev=None):
    u = pltpu.bitcast(v, jnp.uint32)
    t = prev
    for r0 in range(0, u.shape[0], 8):
        for c0 in range(0, u.shape[1], BLOCK):
            piece = u[r0:r0 + 8, c0:c0 + BLOCK]
            t = piece if t is None else t | piece
    return (t >> 16) >> 16


def _order_after(buf, zero):
    tile = pltpu.bitcast(buf[0:16, 0:BLOCK], jnp.uint32)
    buf[0:16, 0:BLOCK] = pltpu.bitcast(tile | zero, BF16)


def _prompt_layer_body(relb_ref, sink_ref, x_ref, meta_ref, g1_ref, win_ref, bucket_ref,
                       cw_ref, cb_ref, lng_ref, lnb_ref, wout_ref, g2_ref, wup_ref, wdn_ref, gf_ref,
                       x2s_ref, y_ref, nk_ref, nv_ref, nc_ref, ys_ref,
                       kbuf, vbuf, ubuf, qbuf, mix, bias_s, x2buf, kvlast, hfbuf, hidbuf, ysbuf,
                       *, tile):
    i = pl.program_id(0)
    n_tiles = pl.num_programs(0) - 1
    g1 = g1_ref[...]
    slot = i % 2

    @pl.when(i == 0)
    def _init():
        nb = x2s_ref.shape[0]
        x2buf[1, 0:nb, :] = x2s_ref[...]
        x2buf[1, nb:tile, :] = jnp.zeros((tile - nb, D_MODEL), F32)
        bucket = bucket_ref[...]
        row = lax.broadcasted_iota(jnp.int32, (BLOCK, 2 * BLOCK), 0)
        col = lax.broadcasted_iota(jnp.int32, (BLOCK, 2 * BLOCK), 1)
        dist = row + BLOCK - col
        band = (dist >= 0) & (dist <= WINDOW)
        band_first = band & (col >= PAD)
        for h in range(N_HEADS):
            b = _bias_from_buckets(bucket, relb_ref, h)
            bias_s[0, h] = jnp.where(band, b, -jnp.inf)
            bias_s[1, h] = jnp.where(band_first, b, -jnp.inf)
        hm = _rms(meta_ref[...], g1).astype(BF16)
        pm = jnp.dot(hm, win_ref[:, O_K:], preferred_element_type=F32)
        kbuf[0:PAD, :] = jnp.zeros((PAD, KV_W), BF16)
        vbuf[0:PAD, :] = jnp.zeros((PAD, KV_W), BF16)
        kbuf[PAD:BLOCK, :] = pm[:, 0:KV_W].astype(BF16)
        vbuf[PAD:BLOCK, :] = pm[:, KV_W:2 * KV_W].astype(BF16)
        um = pm[:, 2 * KV_W:2 * KV_W + CONV_CH] * _sigmoid(pm[:, 2 * KV_W + CONV_CH:])
        ubuf[0:U_CARRY - N_META, :] = jnp.zeros((U_CARRY - N_META, CONV_CH), F32)
        ubuf[U_CARRY - N_META:U_CARRY, :] = um

    x = x_ref[...]
    h = _rms(x, g1).astype(BF16)
    q = jnp.dot(h, win_ref[:, 0:ATTN_W], preferred_element_type=F32) * SCALE
    qbuf[...] = q.astype(BF16)
    kv = jnp.dot(h, win_ref[:, O_K:O_A], preferred_element_type=F32)
    kbuf[BLOCK:BLOCK + tile, :] = kv[:, 0:KV_W].astype(BF16)
    vbuf[BLOCK:BLOCK + tile, :] = kv[:, KV_W:].astype(BF16)
    kvlast[...] = kv[tile - WINDOW:, :]

    a = jnp.dot(h, win_ref[:, O_A:O_B], preferred_element_type=F32)
    b = jnp.dot(h, win_ref[:, O_B:], preferred_element_type=F32)
    ubuf[U_CARRY:U_CARRY + tile, :] = a * _sigmoid(b)

    cb, lng, lnb = cb_ref[...], lng_ref[...], lnb_ref[...]

    def conv_chunk(r0):
        c = _ln_silu(_conv_rows(ubuf, cw_ref, r0, CONV_ROWS) + cb, lng, lnb)
        mix[r0:r0 + CONV_ROWS, ATTN_W:] = c.astype(BF16)

    def attn_scores(blk, kvh):
        r0 = blk * BLOCK
        c0 = kvh * HEAD_DIM
        qg = jnp.concatenate(
            [qbuf[r0:r0 + BLOCK, (kvh * GROUP + g) * HEAD_DIM:(kvh * GROUP + g + 1) * HEAD_DIM]
             for g in range(GROUP)], axis=0)
        kk = kbuf[r0:r0 + 2 * BLOCK, c0:c0 + HEAD_DIM]
        return lax.dot_general(qg, kk, NT_DIMS, preferred_element_type=F32)

    def attn_finish(blk, kvh, s):
        r0 = blk * BLOCK
        sel = jnp.where(i == 0, 1, 0) if blk == 0 else 0
        c0 = kvh * HEAD_DIM
        vv = vbuf[r0:r0 + 2 * BLOCK, c0:c0 + HEAD_DIM]
        ps, ls = [], []
        for g in range(GROUP):
            hd = kvh * GROUP + g
            sg = s[g * BLOCK:(g + 1) * BLOCK] + bias_s[sel, hd]
            sk = sink_ref[hd]
            m = jnp.maximum(jnp.max(sg, axis=-1, keepdims=True), sk)
            p = jnp.exp(sg - m)
            ls.append(jnp.sum(p, axis=-1, keepdims=True) + jnp.exp(sk - m))
            ps.append(p.astype(BF16))
        o = jnp.dot(jnp.concatenate(ps, axis=0), vv, preferred_element_type=F32)
        for g in range(GROUP):
            hd = kvh * GROUP + g
            og = o[g * BLOCK:(g + 1) * BLOCK] / ls[g]
            mix[r0:r0 + BLOCK, hd * HEAD_DIM:(hd + 1) * HEAD_DIM] = og.astype(BF16)

    conv_starts = list(range(0, tile, CONV_ROWS))
    attn_units = [(blk, kvh) for blk in range(tile // BLOCK) for kvh in range(N_KV_HEADS)]
    n_slots = 2 * (D_FF // FFN_CHUNK)
    xf = x2buf[1 - slot]
    hfbuf[...] = _rms(xf, g2_ref[...]).astype(BF16)
    acc = xf
    for k in range(n_slots):
        c0 = (k // 2) * FFN_CHUNK
        scores = [(blk, kvh, attn_scores(blk, kvh)) for blk, kvh in attn_units[k::n_slots]]
        if k % 2 == 0:
            hid = jnp.dot(hfbuf[...], wup_ref[:, c0:c0 + FFN_CHUNK], preferred_element_type=F32)
            hidbuf[...] = jnp.square(jnp.maximum(hid, 0.0)).astype(BF16)
        else:
            acc = acc + jnp.dot(hidbuf[...], wdn_ref[c0:c0 + FFN_CHUNK, :],
                                preferred_element_type=F32)
        zero = None
        for r0 in conv_starts[k::n_slots]:
            conv_chunk(r0)
            zero = _zero_after(mix[r0:r0 + CONV_ROWS, ATTN_W:], zero)
        for blk, kvh, s in scores:
            attn_finish(blk, kvh, s)
        if k + 1 < n_slots and zero is not None:
            _order_after(hfbuf if k % 2 else hidbuf, zero)
    y = _rms(acc, gf_ref[...])
    y_ref[...] = y
    ysbuf[...] = y[0:ysbuf.shape[0]]

    x2buf[slot] = x + jnp.dot(mix[...], wout_ref[...], preferred_element_type=F32)

    @pl.when(i == n_tiles - 1)
    def _new_caches():
        nk_ref[...] = kvlast[:, 0:KV_W]
        nv_ref[...] = kvlast[:, KV_W:]
        nc_ref[...] = ubuf[U_CARRY + tile - (CONV_WIDTH - 1):U_CARRY + tile, :]

    kbuf[0:BLOCK, :] = kbuf[tile:tile + BLOCK, :]
    vbuf[0:BLOCK, :] = vbuf[tile:tile + BLOCK, :]
    ubuf[0:U_CARRY, :] = ubuf[tile:tile + U_CARRY, :]

    @pl.when(i == 0)
    def _sample_out():
        ys_ref[...] = ysbuf[...]


def _const_spec(shape):
    return pl.BlockSpec(shape, lambda i: (0,) * len(shape), pipeline_mode=pl.Buffered(1))


def _smem_spec():
    return pl.BlockSpec(memory_space=pltpu.SMEM)


def _prompt_layer(x, meta, rel_bias, sinks, g1, w_in, bucket, cw, cb, lng, lnb, w_out,
                  g2, w_up, w_down, gf, x2_s):
    seq = x.shape[0]
    nb = x2_s.shape[0]
    tile = PROMPT_TILE
    n_tiles = seq // tile
    body = functools.partial(_prompt_layer_body, tile=tile)
    return pl.pallas_call(
        body,
        grid=(n_tiles + 1,),
        in_specs=[
            _smem_spec(), _smem_spec(),
            pl.BlockSpec((tile, D_MODEL), lambda i: (jnp.minimum(i, n_tiles - 1), 0)),
            _const_spec((N_META, D_MODEL)),
            _const_spec((1, D_MODEL)),
            _const_spec((D_MODEL, IN_W)),
            _const_spec((BLOCK, 2 * BLOCK)),
            _const_spec((CONV_WIDTH, CONV_CH)),
            _const_spec((1, CONV_CH)),
            _const_spec((1, CONV_CH)),
            _const_spec((1, CONV_CH)),
            _const_spec((ATTN_W + CONV_CH, D_MODEL)),
            _const_spec((1, D_MODEL)),
            _const_spec((D_MODEL, D_FF)),
            _const_spec((D_FF, D_MODEL)),
            _const_spec((1, D_MODEL)),
            _const_spec((nb, D_MODEL)),
        ],
        out_specs=[
            pl.BlockSpec((tile, D_MODEL), lambda i: (jnp.maximum(i - 1, 0), 0)),
            pl.BlockSpec((WINDOW, KV_W), lambda i: (0, 0)),
            pl.BlockSpec((WINDOW, KV_W), lambda i: (0, 0)),
            pl.BlockSpec((CONV_WIDTH - 1, CONV_CH), lambda i: (0, 0)),
            pl.BlockSpec((nb, D_MODEL), lambda i: (0, 0)),
        ],
        out_shape=[
            jax.ShapeDtypeStruct((seq, D_MODEL), F32),
            jax.ShapeDtypeStruct((WINDOW, KV_W), F32),
            jax.ShapeDtypeStruct((WINDOW, KV_W), F32),
            jax.ShapeDtypeStruct((CONV_WIDTH - 1, CONV_CH), F32),
            jax.ShapeDtypeStruct((nb, D_MODEL), F32),
        ],
        scratch_shapes=[
            pltpu.VMEM((BLOCK + tile, KV_W), BF16),
            pltpu.VMEM((BLOCK + tile, KV_W), BF16),
            pltpu.VMEM((U_CARRY + tile, CONV_CH), F32),
            pltpu.VMEM((tile, ATTN_W), BF16),
            pltpu.VMEM((tile, ATTN_W + CONV_CH), BF16),
            pltpu.VMEM((2, N_HEADS, BLOCK, 2 * BLOCK), F32),
            pltpu.VMEM((2, tile, D_MODEL), F32),
            pltpu.VMEM((WINDOW, 2 * KV_W), F32),
            pltpu.VMEM((tile, D_MODEL), BF16),
            pltpu.VMEM((tile, FFN_CHUNK), BF16),
            pltpu.VMEM((nb, D_MODEL), F32),
        ],
        compiler_params=pltpu.CompilerParams(
            dimension_semantics=("arbitrary",), vmem_limit_bytes=V7X_VMEM_LIMIT_BYTES),
        name="prompt_layer",
    )(rel_bias, sinks, x, meta, g1, w_in, bucket, cw, cb, lng, lnb, w_out, g2, w_up, w_down, gf,
      x2_s)


def _sample_mixer_body(relb_ref, sink_ref, x_ref, ck_ref, cv_ref, st_ref, g1_ref, win_ref,
                       bucket_ref, cw_ref, cb_ref, lng_ref, lnb_ref, wout_ref,
                       x2_ref, nk_ref, nv_ref, nc_ref,
                       pbuf, mix, convbuf, bias_c, sink_c, *, chunk):
    i = pl.program_id(0)
    last = pl.num_programs(0) - 1
    rows_h = chunk * HEAD_ROWS

    @pl.when(i == 0)
    def _init():
        h = _rms(x_ref[...], g1_ref[...]).astype(BF16)
        pbuf[...] = jnp.dot(h, win_ref[...], preferred_element_type=F32)
        bucket = bucket_ref[...]
        rid = lax.broadcasted_iota(jnp.int32, (HEAD_ROWS, 1), 0)
        bias = jnp.zeros((HEAD_ROWS, 2 * BLOCK), F32)
        sk = jnp.zeros((HEAD_ROWS, 1), F32)
        for hd in range(N_HEADS):
            bias = jnp.where(rid == hd, _bias_from_buckets(bucket, relb_ref, hd), bias)
            sk = jnp.where(rid == hd, sink_ref[hd], sk)
        bias_c[...] = bias
        sink_c[...] = sk

    r0 = pl.multiple_of(i * chunk, chunk)
    pr = pbuf[pl.ds(r0, chunk), :]
    q = pr[:, 0:ATTN_W] * SCALE
    knew = pr[:, O_K:O_V]
    vnew = pr[:, O_V:O_A]
    unew = pr[:, O_A:O_B] * _sigmoid(pr[:, O_B:])

    def per_head(t):
        n = t.shape[-1]
        return jnp.broadcast_to(t[:, None, :], (chunk, HEAD_ROWS, n)).reshape(rows_h, n)

    hid = lax.broadcasted_iota(jnp.int32, (rows_h, 1), 0) % HEAD_ROWS
    lane = lax.broadcasted_iota(jnp.int32, (1, BLOCK), 1)
    qrep = per_head(q)
    qsum = jnp.zeros((rows_h, BLOCK), F32)
    for c in range(ATTN_W // BLOCK):
        piece = qrep[:, c * BLOCK:(c + 1) * BLOCK]
        in_head = (lane // HEAD_DIM + 2 * c) == hid
        qsum = qsum + jnp.where(in_head, piece, 0.0)
    keep = (hid % 2) == (hid // GROUP)
    qf = jnp.where(keep, qsum, pltpu.roll(qsum, HEAD_DIM, axis=1))
    qf_b = qf.astype(BF16)

    bias = jnp.broadcast_to(bias_c[...][None], (chunk, HEAD_ROWS, 2 * BLOCK)).reshape(rows_h, 2 * BLOCK)
    sk = jnp.broadcast_to(sink_c[...][None], (chunk, HEAD_ROWS, 1)).reshape(rows_h, 1)

    s_rows = []
    for b in range(chunk):
        kb = ck_ref[b].astype(BF16)
        s_rows.append(lax.dot_general(qf_b[b * HEAD_ROWS:(b + 1) * HEAD_ROWS], kb, NT_DIMS,
                                      preferred_element_type=F32))
    s_c = jnp.concatenate(s_rows, axis=0) + bias[:, 0:BLOCK]
    s_n = jnp.sum(qf * per_head(knew), axis=-1, keepdims=True) + bias[:, BLOCK:BLOCK + 1]
    m = jnp.maximum(jnp.maximum(jnp.max(s_c, axis=-1, keepdims=True), s_n), sk)
    p_c = jnp.exp(s_c - m)
    p_n = jnp.exp(s_n - m)
    l = jnp.sum(p_c, axis=-1, keepdims=True) + p_n + jnp.exp(sk - m)
    p_cb = p_c.astype(BF16)
    o_rows = []
    for b in range(chunk):
        vb = cv_ref[b].astype(BF16)
        o_rows.append(jnp.dot(p_cb[b * HEAD_ROWS:(b + 1) * HEAD_ROWS], vb, preferred_element_type=F32))
    o = (jnp.concatenate(o_rows, axis=0) + p_n * per_head(vnew)) / l
    o = jnp.where(keep, o, pltpu.roll(o, HEAD_DIM, axis=1))
    o = jnp.where((lane // HEAD_DIM) == (hid % 2), o, 0.0)
    wide = jnp.concatenate([jnp.where(hid // 2 == c, o, 0.0) for c in range(ATTN_W // BLOCK)], axis=1)
    gi = lax.broadcasted_iota(jnp.int32, (chunk, rows_h), 0)
    gj = lax.broadcasted_iota(jnp.int32, (chunk, rows_h), 1)
    gather = jnp.where(gj // HEAD_ROWS == gi, 1.0, 0.0).astype(BF16)
    ao = jnp.dot(gather, wide.astype(BF16), preferred_element_type=F32)
    mix[pl.ds(r0, chunk), 0:ATTN_W] = ao

    cw_hist = cw_ref[0:CONV_WIDTH - 1, :]
    cw_new = cw_ref[CONV_WIDTH - 1:CONV_WIDTH, :]
    for b in range(chunk):
        convbuf[b:b + 1, :] = jnp.sum(st_ref[b] * cw_hist, axis=0, keepdims=True)
        nk_ref[b, 0:WINDOW - 1, :] = ck_ref[b, 1:WINDOW, :]
        nk_ref[b, WINDOW - 1:WINDOW, :] = knew[b:b + 1, :]
        nv_ref[b, 0:WINDOW - 1, :] = cv_ref[b, 1:WINDOW, :]
        nv_ref[b, WINDOW - 1:WINDOW, :] = vnew[b:b + 1, :]
        nc_ref[b, 0:CONV_WIDTH - 2, :] = st_ref[b, 1:CONV_WIDTH - 1, :]
        nc_ref[b, CONV_WIDTH - 2:CONV_WIDTH - 1, :] = unew[b:b + 1, :]
    acc = convbuf[...] + cw_new * unew + cb_ref[...]
    mu = jnp.mean(acc, axis=-1, keepdims=True)
    xc = acc - mu
    y = xc * lax.rsqrt(jnp.mean(xc * xc, axis=-1, keepdims=True) + EPS)
    y = y * lng_ref[...] + lnb_ref[...]
    mix[pl.ds(r0, chunk), ATTN_W:] = y * _sigmoid(y)

    @pl.when(i == last)
    def _out():
        x2_ref[...] = x_ref[...] + jnp.dot(mix[...].astype(BF16), wout_ref[...],
                                           preferred_element_type=F32)


def _sample_mixer(x, ck, cv, st, rel_bias, sinks, g1, w_in, bucket, cw, cb, lng, lnb, w_out):
    nb = x.shape[0]
    chunk = SAMPLE_CHUNK
    body = functools.partial(_sample_mixer_body, chunk=chunk)
    return pl.pallas_call(
        body,
        grid=(nb // chunk,),
        in_specs=[
            _smem_spec(), _smem_spec(),
            _const_spec((nb, D_MODEL)),
            pl.BlockSpec((chunk, WINDOW, KV_W), lambda i: (i, 0, 0)),
            pl.BlockSpec((chunk, WINDOW, KV_W), lambda i: (i, 0, 0)),
            pl.BlockSpec((chunk, CONV_WIDTH - 1, CONV_CH), lambda i: (i, 0, 0)),
            _const_spec((1, D_MODEL)),
            _const_spec((D_MODEL, IN_W)),
            _const_spec((1, 2 * BLOCK)),
            _const_spec((CONV_WIDTH, CONV_CH)),
            _const_spec((1, CONV_CH)),
            _const_spec((1, CONV_CH)),
            _const_spec((1, CONV_CH)),
            _const_spec((ATTN_W + CONV_CH, D_MODEL)),
        ],
        out_specs=[
            pl.BlockSpec((nb, D_MODEL), lambda i: (0, 0)),
            pl.BlockSpec((chunk, WINDOW, KV_W), lambda i: (i, 0, 0)),
            pl.BlockSpec((chunk, WINDOW, KV_W), lambda i: (i, 0, 0)),
            pl.BlockSpec((chunk, CONV_WIDTH - 1, CONV_CH), lambda i: (i, 0, 0)),
        ],
        out_shape=[
            jax.ShapeDtypeStruct((nb, D_MODEL), F32),
            jax.ShapeDtypeStruct((nb, WINDOW, KV_W), F32),
            jax.ShapeDtypeStruct((nb, WINDOW, KV_W), F32),
            jax.ShapeDtypeStruct((nb, CONV_WIDTH - 1, CONV_CH), F32),
        ],
        scratch_shapes=[
            pltpu.VMEM((nb, IN_W), F32),
            pltpu.VMEM((nb, ATTN_W + CONV_CH), F32),
            pltpu.VMEM((chunk, CONV_CH), F32),
            pltpu.VMEM((HEAD_ROWS, 2 * BLOCK), F32),
            pltpu.VMEM((HEAD_ROWS, 1), F32),
        ],
        compiler_params=pltpu.CompilerParams(
            dimension_semantics=("arbitrary",), vmem_limit_bytes=V7X_VMEM_LIMIT_BYTES),
        name="sample_mixer",
    )(rel_bias, sinks, x, ck, cv, st, g1, w_in, bucket, cw, cb, lng, lnb, w_out)


def kernel(x_prompt, x_sample, cache_k, cache_v, state_conv, meta_tokens, rel_bias, norm1_g, w_in,
           attn_sinks, conv_w, conv_b, conv_ln_g, conv_ln_b, w_out, norm2_g, w_up, w_down, norm_f_g):
    batch, seq, _ = x_prompt.shape
    nb, dec_seq, _ = x_sample.shape
    assert batch == 1 and dec_seq == 1 and w_in.shape[0] == 1
    assert seq % PROMPT_TILE == 0 and nb % SAMPLE_CHUNK == 0 and nb <= PROMPT_TILE

    w_in_b = w_in[0].astype(BF16)
    w_out_b = w_out[0].astype(BF16)
    w_up_b = w_up[0].astype(BF16)
    w_down_b = w_down[0].astype(BF16)
    g1 = norm1_g[0][None]
    g2 = norm2_g[0][None]
    gf = norm_f_g[None]
    cw, cb = conv_w[0], conv_b[0][None]
    lng, lnb = conv_ln_g[0][None], conv_ln_b[0][None]
    sinks = attn_sinks[0]

    dist_p = jnp.arange(BLOCK)[:, None] + BLOCK - jnp.arange(2 * BLOCK)[None, :]
    bucket_p = _t5_bucket(jnp.clip(dist_p, 0, WINDOW)).astype(jnp.int32)
    lane = jnp.arange(2 * BLOCK)
    dist_s = jnp.where(lane < WINDOW, WINDOW - lane, 0)
    bucket_s = jnp.where(lane <= WINDOW, _t5_bucket(jnp.clip(dist_s, 0, WINDOW)), -1)
    bucket_s = bucket_s.astype(jnp.int32)[None]

    ck = cache_k[0].reshape(nb, WINDOW, KV_W)
    cv = cache_v[0].reshape(nb, WINDOW, KV_W)
    x2_s, nk_s, nv_s, nc_s = _sample_mixer(x_sample[:, 0], ck, cv, state_conv[0], rel_bias, sinks, g1,
                                           w_in_b, bucket_s, cw, cb, lng, lnb, w_out_b)
    y_p, nk_p, nv_p, nc_p, y_s = _prompt_layer(x_prompt[0], meta_tokens, rel_bias, sinks, g1, w_in_b,
                                               bucket_p, cw, cb, lng, lnb, w_out_b, g2, w_up_b,
                                               w_down_b, gf, x2_s)

    kv_shape = (1, WINDOW, N_KV_HEADS, HEAD_DIM)
    return (y_p[None], y_s[:, None],
            nk_p.reshape((1,) + kv_shape), nv_p.reshape((1,) + kv_shape), nc_p[None, None],
            nk_s.reshape((1, nb) + kv_shape[1:]), nv_s.reshape((1, nb) + kv_shape[1:]), nc_s[None])
```

```python
import functools
import math

import jax
import jax.numpy as jnp
from jax import lax
from jax.experimental import pallas as pl
from jax.experimental.pallas import tpu as pltpu

D_MODEL = 1024
N_HEADS = 8
N_KV_HEADS = 2
HEAD_DIM = 64
GROUP = N_HEADS // N_KV_HEADS
ATTN_W = N_HEADS * HEAD_DIM
KV_W = N_KV_HEADS * HEAD_DIM
CONV_CH = D_MODEL - ATTN_W
IN_W = ATTN_W + 2 * KV_W + 2 * CONV_CH
CONV_WIDTH = 31
WINDOW = 128
BLOCK = 128
N_BUCKETS = 32
MAX_DISTANCE = WINDOW
N_META = 16
D_FF = 4 * D_MODEL
EPS = 1e-6
SCALE = HEAD_DIM ** -0.5

O_K = ATTN_W
O_V = ATTN_W + KV_W
O_A = ATTN_W + 2 * KV_W
O_B = O_A + CONV_CH

PAD = (-N_META) % BLOCK
U_CARRY = 32
U_SHIFT = U_CARRY - (CONV_WIDTH - 1)

V7X_VMEM_LIMIT_BYTES = 60 * 1024 * 1024

PROMPT_TILE = 512
UP_CHUNK = 1024
DOWN_CHUNK = 512
SLOT_ATTN_UNITS = (1, 1, 1, 1, 2, 2)
SLOT_CONV_CHUNKS = (1, 1, 1, 1, 2, 2)
CONV_ROWS = 64
SAMPLE_CHUNK = 16
HEAD_ROWS = 16

BF16 = jnp.bfloat16
F32 = jnp.float32
NT_DIMS = (((1,), (1,)), ((), ()))


def _t5_bucket(d):
    max_exact = N_BUCKETS // 2
    d_f = jnp.maximum(d, 1).astype(jnp.float32)
    large = max_exact + (jnp.log(d_f / max_exact) / math.log(MAX_DISTANCE / max_exact)
                         * (N_BUCKETS - max_exact)).astype(jnp.int32)
    large = jnp.minimum(large, N_BUCKETS - 1)
    return jnp.where(d < max_exact, d, large)


def _rms(x, g):
    y = x * lax.rsqrt(jnp.mean(x * x, axis=-1, keepdims=True) + EPS)
    return y * g


def _sigmoid(x):
    return 1.0 / (1.0 + jnp.exp(-x))


def _bias_from_buckets(bucket, relb_ref, h):
    b = jnp.zeros(bucket.shape, F32)
    for bk in range(N_BUCKETS):
        b = jnp.where(bucket == bk, relb_ref[bk, h], b)
    return b


def _conv_rows(ubuf, cw_ref, r0, rows):
    n = rows + U_CARRY
    strips = []
    for c0 in range(0, CONV_CH, BLOCK):
        win = ubuf[r0:r0 + n, c0:c0 + BLOCK]
        acc = None
        for s in range(8):
            sh = win if s == 0 else pltpu.roll(win, n - s, axis=0)
            for a0 in range(0, U_CARRY + 8, 8):
                w = a0 + s - U_SHIFT
                if 0 <= w < CONV_WIDTH:
                    term = cw_ref[w:w + 1, c0:c0 + BLOCK] * sh[a0:a0 + rows]
                    acc = term if acc is None else acc + term
        strips.append(acc)
    return jnp.concatenate(strips, axis=1)


def _ln_silu(acc, lng, lnb):
    mu = jnp.mean(acc, axis=-1, keepdims=True)
    xc = acc - mu
    y = xc * lax.rsqrt(jnp.mean(xc * xc, axis=-1, keepdims=True) + EPS)
    y = y * lng + lnb
    return y * _sigmoid(y)


def _zero_after(v, prev=None):
    u = pltpu.bitcast(v, jnp.uint32)
    t = prev
    for r0 in range(0, u.shape[0], 8):
        for c0 in range(0, u.shape[1], BLOCK):
            piece = u[r0:r0 + 8, c0:c0 + BLOCK]
            t = piece if t is None else t | piece
    return (t >> 16) >> 16


def _order_after(buf, zero):
    tile = pltpu.bitcast(buf[0:16, 0:BLOCK], jnp.uint32)
    buf[0:16, 0:BLOCK] = pltpu.bitcast(tile | zero, BF16)


def _prompt_layer_body(relb_ref, sink_ref, x_ref, meta_ref, g1_ref, win_ref, bucket_ref,
                       cw_ref, cb_ref, lng_ref, lnb_ref, wout_ref, g2_ref, wup_ref, wdn_ref, gf_ref,
                       x2s_ref, y_ref, nk_ref, nv_ref, nc_ref, ys_ref,
                       kbuf, vbuf, ubuf, qbuf, mix, bias_s, x2buf, kvlast, hfbuf, hidbuf, ysbuf,
                       *, tile):
    i = pl.program_id(0)
    n_tiles = pl.num_programs(0) - 1
    g1 = g1_ref[...]
    slot = i % 2

    @pl.when(i == 0)
    def _init():
        nb = x2s_ref.shape[0]
        x2buf[1, 0:nb, :] = x2s_ref[...]
        x2buf[1, nb:tile, :] = jnp.zeros((tile - nb, D_MODEL), F32)
        bucket = bucket_ref[...]
        row = lax.broadcasted_iota(jnp.int32, (BLOCK, 2 * BLOCK), 0)
        col = lax.broadcasted_iota(jnp.int32, (BLOCK, 2 * BLOCK), 1)
        dist = row + BLOCK - col
        band = (dist >= 0) & (dist <= WINDOW)
        band_first = band & (col >= PAD)
        for h in range(N_HEADS):
            b = _bias_from_buckets(bucket, relb_ref, h)
            bias_s[0, h] = jnp.where(band, b, -jnp.inf)
            bias_s[1, h] = jnp.where(band_first, b, -jnp.inf)
        hm = _rms(meta_ref[...], g1).astype(BF16)
        pm = jnp.dot(hm, win_ref[:, O_K:], preferred_element_type=F32)
        kbuf[0:PAD, :] = jnp.zeros((PAD, KV_W), BF16)
        vbuf[0:PAD, :] = jnp.zeros((PAD, KV_W), BF16)
        kbuf[PAD:BLOCK, :] = pm[:, 0:KV_W].astype(BF16)
        vbuf[PAD:BLOCK, :] = pm[:, KV_W:2 * KV_W].astype(BF16)
        um = pm[:, 2 * KV_W:2 * KV_W + CONV_CH] * _sigmoid(pm[:, 2 * KV_W + CONV_CH:])
        ubuf[0:U_CARRY - N_META, :] = jnp.zeros((U_CARRY - N_META, CONV_CH), F32)
        ubuf[U_CARRY - N_META:U_CARRY, :] = um

    x = x_ref[...]
    h = _rms(x, g1).astype(BF16)
    q = jnp.dot(h, win_ref[:, 0:ATTN_W], preferred_element_type=F32) * SCALE
    qbuf[...] = q.astype(BF16)
    kv = jnp.dot(h, win_ref[:, O_K:O_A], preferred_element_type=F32)
    kbuf[BLOCK:BLOCK + tile, :] = kv[:, 0:KV_W].astype(BF16)
    vbuf[BLOCK:BLOCK + tile, :] = kv[:, KV_W:].astype(BF16)
    kvlast[...] = kv[tile - WINDOW:, :]

    a = jnp.dot(h, win_ref[:, O_A:O_B], preferred_element_type=F32)
    b = jnp.dot(h, win_ref[:, O_B:], preferred_element_type=F32)
    ubuf[U_CARRY:U_CARRY + tile, :] = a * _sigmoid(b)

    cb, lng, lnb = cb_ref[...], lng_ref[...], lnb_ref[...]

    def conv_chunk(r0):
        c = _ln_silu(_conv_rows(ubuf, cw_ref, r0, CONV_ROWS) + cb, lng, lnb)
        mix[r0:r0 + CONV_ROWS, ATTN_W:] = c.astype(BF16)

    def attn_scores(blk, kvh):
        r0 = blk * BLOCK
        c0 = kvh * HEAD_DIM
        qg = jnp.concatenate(
            [qbuf[r0:r0 + BLOCK, (kvh * GROUP + g) * HEAD_DIM:(kvh * GROUP + g + 1) * HEAD_DIM]
             for g in range(GROUP)], axis=0)
        kk = kbuf[r0:r0 + 2 * BLOCK, c0:c0 + HEAD_DIM]
        return lax.dot_general(qg, kk, NT_DIMS, preferred_element_type=F32)

    def attn_finish(blk, kvh, s):
        r0 = blk * BLOCK
        sel = jnp.where(i == 0, 1, 0) if blk == 0 else 0
        c0 = kvh * HEAD_DIM
        vv = vbuf[r0:r0 + 2 * BLOCK, c0:c0 + HEAD_DIM]
        ps, ls = [], []
        for g in range(GROUP):
            hd = kvh * GROUP + g
            sg = s[g * BLOCK:(g + 1) * BLOCK] + bias_s[sel, hd]
            sk = sink_ref[hd]
            m = jnp.maximum(jnp.max(sg, axis=-1, keepdims=True), sk)
            p = jnp.exp(sg - m)
            ls.append(jnp.sum(p, axis=-1, keepdims=True) + jnp.exp(sk - m))
            ps.append(p.astype(BF16))
        o = jnp.dot(jnp.concatenate(ps, axis=0), vv, preferred_element_type=F32)
        for g in range(GROUP):
            hd = kvh * GROUP + g
            og = o[g * BLOCK:(g + 1) * BLOCK] / ls[g]
            mix[r0:r0 + BLOCK, hd * HEAD_DIM:(hd + 1) * HEAD_DIM] = og.astype(BF16)

    conv_starts = list(range(0, tile, CONV_ROWS))
    attn_units = [(blk, kvh) for blk in range(tile // BLOCK) for kvh in range(N_KV_HEADS)]
    n_up, n_down = D_FF // UP_CHUNK, D_MODEL // DOWN_CHUNK
    n_slots = n_up + n_down
    unit_iter, conv_iter = iter(attn_units), iter(conv_starts)
    xf = x2buf[1 - slot]
    hfbuf[...] = _rms(xf, g2_ref[...]).astype(BF16)
    x3_cols = []
    for k in range(n_slots):
        units = [next(unit_iter) for _ in range(SLOT_ATTN_UNITS[k])]
        scores = [(blk, kvh, attn_scores(blk, kvh)) for blk, kvh in units]
        if k < n_up:
            c0 = k * UP_CHUNK
            hid = jnp.dot(hfbuf[...], wup_ref[:, c0:c0 + UP_CHUNK], preferred_element_type=F32)
            hidbuf[:, c0:c0 + UP_CHUNK] = jnp.square(jnp.maximum(hid, 0.0)).astype(BF16)
        else:
            n0 = (k - n_up) * DOWN_CHUNK
            x3_cols.append(xf[:, n0:n0 + DOWN_CHUNK]
                           + jnp.dot(hidbuf[...], wdn_ref[:, n0:n0 + DOWN_CHUNK],
                                     preferred_element_type=F32))
        zero = None
        for r0 in [next(conv_iter) for _ in range(SLOT_CONV_CHUNKS[k])]:
            conv_chunk(r0)
            zero = _zero_after(mix[r0:r0 + CONV_ROWS, ATTN_W:], zero)
        for blk, kvh, s in scores:
            attn_finish(blk, kvh, s)
        if k + 1 < n_slots and zero is not None:
            _order_after(hfbuf if k + 1 < n_up else hidbuf, zero)
    acc = jnp.concatenate(x3_cols, axis=1)
    y = _rms(acc, gf_ref[...])
    y_ref[...] = y
    ysbuf[...] = y[0:ysbuf.shape[0]]

    x2buf[slot] = x + jnp.dot(mix[...], wout_ref[...], preferred_element_type=F32)

    @pl.when(i == n_tiles - 1)
    def _new_caches():
        nk_ref[...] = kvlast[:, 0:KV_W]
        nv_ref[...] = kvlast[:, KV_W:]
        nc_ref[...] = ubuf[U_CARRY + tile - (CONV_WIDTH - 1):U_CARRY + tile, :]

    kbuf[0:BLOCK, :] = kbuf[tile:tile + BLOCK, :]
    vbuf[0:BLOCK, :] = vbuf[tile:tile + BLOCK, :]
    ubuf[0:U_CARRY, :] = ubuf[tile:tile + U_CARRY, :]

    @pl.when(i == 0)
    def _sample_out():
        ys_ref[...] = ysbuf[...]


def _const_spec(shape):
    return pl.BlockSpec(shape, lambda i: (0,) * len(shape), pipeline_mode=pl.Buffered(1))


def _smem_spec():
    return pl.BlockSpec(memory_space=pltpu.SMEM)


def _prompt_layer(x, meta, rel_bias, sinks, g1, w_in, bucket, cw, cb, lng, lnb, w_out,
                  g2, w_up, w_down, gf, x2_s):
    seq = x.shape[0]
    nb = x2_s.shape[0]
    tile = PROMPT_TILE
    n_tiles = seq // tile
    body = functools.partial(_prompt_layer_body, tile=tile)
    return pl.pallas_call(
        body,
        grid=(n_tiles + 1,),
        in_specs=[
            _smem_spec(), _smem_spec(),
            pl.BlockSpec((tile, D_MODEL), lambda i: (jnp.minimum(i, n_tiles - 1), 0)),
            _const_spec((N_META, D_MODEL)),
            _const_spec((1, D_MODEL)),
            _const_spec((D_MODEL, IN_W)),
            _const_spec((BLOCK, 2 * BLOCK)),
            _const_spec((CONV_WIDTH, CONV_CH)),
            _const_spec((1, CONV_CH)),
            _const_spec((1, CONV_CH)),
            _const_spec((1, CONV_CH)),
            _const_spec((ATTN_W + CONV_CH, D_MODEL)),
            _const_spec((1, D_MODEL)),
            _const_spec((D_MODEL, D_FF)),
            _const_spec((D_FF, D_MODEL)),
            _const_spec((1, D_MODEL)),
            _const_spec((nb, D_MODEL)),
        ],
        out_specs=[
            pl.BlockSpec((tile, D_MODEL), lambda i: (jnp.maximum(i - 1, 0), 0)),
            pl.BlockSpec((WINDOW, KV_W), lambda i: (0, 0)),
            pl.BlockSpec((WINDOW, KV_W), lambda i: (0, 0)),
            pl.BlockSpec((CONV_WIDTH - 1, CONV_CH), lambda i: (0, 0)),
            pl.BlockSpec((nb, D_MODEL), lambda i: (0, 0)),
        ],
        out_shape=[
            jax.ShapeDtypeStruct((seq, D_MODEL), F32),
            jax.ShapeDtypeStruct((WINDOW, KV_W), F32),
            jax.ShapeDtypeStruct((WINDOW, KV_W), F32),
            jax.ShapeDtypeStruct((CONV_WIDTH - 1, CONV_CH), F32),
            jax.ShapeDtypeStruct((nb, D_MODEL), F32),
        ],
        scratch_shapes=[
            pltpu.VMEM((BLOCK + tile, KV_W), BF16),
            pltpu.VMEM((BLOCK + tile, KV_W), BF16),
            pltpu.VMEM((U_CARRY + tile, CONV_CH), F32),
            pltpu.VMEM((tile, ATTN_W), BF16),
            pltpu.VMEM((tile, ATTN_W + CONV_CH), BF16),
            pltpu.VMEM((2, N_HEADS, BLOCK, 2 * BLOCK), F32),
            pltpu.VMEM((2, tile, D_MODEL), F32),
            pltpu.VMEM((WINDOW, 2 * KV_W), F32),
            pltpu.VMEM((tile, D_MODEL), BF16),
            pltpu.VMEM((tile, D_FF), BF16),
            pltpu.VMEM((nb, D_MODEL), F32),
        ],
        compiler_params=pltpu.CompilerParams(
            dimension_semantics=("arbitrary",), vmem_limit_bytes=V7X_VMEM_LIMIT_BYTES),
        name="prompt_layer",
    )(rel_bias, sinks, x, meta, g1, w_in, bucket, cw, cb, lng, lnb, w_out, g2, w_up, w_down, gf,
      x2_s)


def _sample_mixer_body(relb_ref, sink_ref, x_ref, ck_ref, cv_ref, st_ref, g1_ref, win_ref,
                       bucket_ref, cw_ref, cb_ref, lng_ref, lnb_ref, wout_ref,
                       x2_ref, nk_ref, nv_ref, nc_ref,
                       pbuf, mix, convbuf, bias_c, sink_c, *, chunk):
    i = pl.program_id(0)
    last = pl.num_programs(0) - 1
    rows_h = chunk * HEAD_ROWS

    @pl.when(i == 0)
    def _init():
        h = _rms(x_ref[...], g1_ref[...]).astype(BF16)
        pbuf[...] = jnp.dot(h, win_ref[...], preferred_element_type=F32)
        bucket = bucket_ref[...]
        rid = lax.broadcasted_iota(jnp.int32, (HEAD_ROWS, 1), 0)
        bias = jnp.zeros((HEAD_ROWS, 2 * BLOCK), F32)
        sk = jnp.zeros((HEAD_ROWS, 1), F32)
        for hd in range(N_HEADS):
            bias = jnp.where(rid == hd, _bias_from_buckets(bucket, relb_ref, hd), bias)
            sk = jnp.where(rid == hd, sink_ref[hd], sk)
        bias_c[...] = bias
        sink_c[...] = sk

    r0 = pl.multiple_of(i * chunk, chunk)
    pr = pbuf[pl.ds(r0, chunk), :]
    q = pr[:, 0:ATTN_W] * SCALE
    knew = pr[:, O_K:O_V]
    vnew = pr[:, O_V:O_A]
    unew = pr[:, O_A:O_B] * _sigmoid(pr[:, O_B:])

    def per_head(t):
        n = t.shape[-1]
        return jnp.broadcast_to(t[:, None, :], (chunk, HEAD_ROWS, n)).reshape(rows_h, n)

    hid = lax.broadcasted_iota(jnp.int32, (rows_h, 1), 0) % HEAD_ROWS
    lane = lax.broadcasted_iota(jnp.int32, (1, BLOCK), 1)
    qrep = per_head(q)
    qsum = jnp.zeros((rows_h, BLOCK), F32)
    for c in range(ATTN_W // BLOCK):
        piece = qrep[:, c * BLOCK:(c + 1) * BLOCK]
        in_head = (lane // HEAD_DIM + 2 * c) == hid
        qsum = qsum + jnp.where(in_head, piece, 0.0)
    keep = (hid % 2) == (hid // GROUP)
    qf = jnp.where(keep, qsum, pltpu.roll(qsum, HEAD_DIM, axis=1))
    qf_b = qf.astype(BF16)

    bias = jnp.broadcast_to(bias_c[...][None], (chunk, HEAD_ROWS, 2 * BLOCK)).reshape(rows_h, 2 * BLOCK)
    sk = jnp.broadcast_to(sink_c[...][None], (chunk, HEAD_ROWS, 1)).reshape(rows_h, 1)

    s_rows = []
    for b in range(chunk):
        kb = ck_ref[b].astype(BF16)
        s_rows.append(lax.dot_general(qf_b[b * HEAD_ROWS:(b + 1) * HEAD_ROWS], kb, NT_DIMS,
                                      preferred_element_type=F32))
    s_c = jnp.concatenate(s_rows, axis=0) + bias[:, 0:BLOCK]
    s_n = jnp.sum(qf * per_head(knew), axis=-1, keepdims=True) + bias[:, BLOCK:BLOCK + 1]
    m = jnp.maximum(jnp.maximum(jnp.max(s_c, axis=-1, keepdims=True), s_n), sk)
    p_c = jnp.exp(s_c - m)
    p_n = jnp.exp(s_n - m)
    l = jnp.sum(p_c, axis=-1, keepdims=True) + p_n + jnp.exp(sk - m)
    p_cb = p_c.astype(BF16)
    o_rows = []
    for b in range(chunk):
        vb = cv_ref[b].astype(BF16)
        o_rows.append(jnp.dot(p_cb[b * HEAD_ROWS:(b + 1) * HEAD_ROWS], vb, preferred_element_type=F32))
    o = (jnp.concatenate(o_rows, axis=0) + p_n * per_head(vnew)) / l
    o = jnp.where(keep, o, pltpu.roll(o, HEAD_DIM, axis=1))
    o = jnp.where((lane // HEAD_DIM) == (hid % 2), o, 0.0)
    wide = jnp.concatenate([jnp.where(hid // 2 == c, o, 0.0) for c in range(ATTN_W // BLOCK)], axis=1)
    gi = lax.broadcasted_iota(jnp.int32, (chunk, rows_h), 0)
    gj = lax.broadcasted_iota(jnp.int32, (chunk, rows_h), 1)
    gather = jnp.where(gj // HEAD_ROWS == gi, 1.0, 0.0).astype(BF16)
    ao = jnp.dot(gather, wide.astype(BF16), preferred_element_type=F32)
    mix[pl.ds(r0, chunk), 0:ATTN_W] = ao

    cw_hist = cw_ref[0:CONV_WIDTH - 1, :]
    cw_new = cw_ref[CONV_WIDTH - 1:CONV_WIDTH, :]
    for b in range(chunk):
        convbuf[b:b + 1, :] = jnp.sum(st_ref[b] * cw_hist, axis=0, keepdims=True)
        nk_ref[b, 0:WINDOW - 1, :] = ck_ref[b, 1:WINDOW, :]
        nk_ref[b, WINDOW - 1:WINDOW, :] = knew[b:b + 1, :]
        nv_ref[b, 0:WINDOW - 1, :] = cv_ref[b, 1:WINDOW, :]
        nv_ref[b, WINDOW - 1:WINDOW, :] = vnew[b:b + 1, :]
        nc_ref[b, 0:CONV_WIDTH - 2, :] = st_ref[b, 1:CONV_WIDTH - 1, :]
        nc_ref[b, CONV_WIDTH - 2:CONV_WIDTH - 1, :] = unew[b:b + 1, :]
    acc = convbuf[...] + cw_new * unew + cb_ref[...]
    mu = jnp.mean(acc, axis=-1, keepdims=True)
    xc = acc - mu
    y = xc * lax.rsqrt(jnp.mean(xc * xc, axis=-1, keepdims=True) + EPS)
    y = y * lng_ref[...] + lnb_ref[...]
    mix[pl.ds(r0, chunk), ATTN_W:] = y * _sigmoid(y)

    @pl.when(i == last)
    def _out():
        x2_ref[...] = x_ref[...] + jnp.dot(mix[...].astype(BF16), wout_ref[...],
                                           preferred_element_type=F32)


def _sample_mixer(x, ck, cv, st, rel_bias, sinks, g1, w_in, bucket, cw, cb, lng, lnb, w_out):
    nb = x.shape[0]
    chunk = SAMPLE_CHUNK
    body = functools.partial(_sample_mixer_body, chunk=chunk)
    return pl.pallas_call(
        body,
        grid=(nb // chunk,),
        in_specs=[
            _smem_spec(), _smem_spec(),
            _const_spec((nb, D_MODEL)),
            pl.BlockSpec((chunk, WINDOW, KV_W), lambda i: (i, 0, 0)),
            pl.BlockSpec((chunk, WINDOW, KV_W), lambda i: (i, 0, 0)),
            pl.BlockSpec((chunk, CONV_WIDTH - 1, CONV_CH), lambda i: (i, 0, 0)),
            _const_spec((1, D_MODEL)),
            _const_spec((D_MODEL, IN_W)),
            _const_spec((1, 2 * BLOCK)),
            _const_spec((CONV_WIDTH, CONV_CH)),
            _const_spec((1, CONV_CH)),
            _const_spec((1, CONV_CH)),
            _const_spec((1, CONV_CH)),
            _const_spec((ATTN_W + CONV_CH, D_MODEL)),
        ],
        out_specs=[
            pl.BlockSpec((nb, D_MODEL), lambda i: (0, 0)),
            pl.BlockSpec((chunk, WINDOW, KV_W), lambda i: (i, 0, 0)),
            pl.BlockSpec((chunk, WINDOW, KV_W), lambda i: (i, 0, 0)),
            pl.BlockSpec((chunk, CONV_WIDTH - 1, CONV_CH), lambda i: (i, 0, 0)),
        ],
        out_shape=[
            jax.ShapeDtypeStruct((nb, D_MODEL), F32),
            jax.ShapeDtypeStruct((nb, WINDOW, KV_W), F32),
            jax.ShapeDtypeStruct((nb, WINDOW, KV_W), F32),
            jax.ShapeDtypeStruct((nb, CONV_WIDTH - 1, CONV_CH), F32),
        ],
        scratch_shapes=[
            pltpu.VMEM((nb, IN_W), F32),
            pltpu.VMEM((nb, ATTN_W + CONV_CH), F32),
            pltpu.VMEM((chunk, CONV_CH), F32),
            pltpu.VMEM((HEAD_ROWS, 2 * BLOCK), F32),
            pltpu.VMEM((HEAD_ROWS, 1), F32),
        ],
        compiler_params=pltpu.CompilerParams(
            dimension_semantics=("arbitrary",), vmem_limit_bytes=V7X_VMEM_LIMIT_BYTES),
        name="sample_mixer",
    )(rel_bias, sinks, x, ck, cv, st, g1, w_in, bucket, cw, cb, lng, lnb, w_out)


def kernel(x_prompt, x_sample, cache_k, cache_v, state_conv, meta_tokens, rel_bias, norm1_g, w_in,
           attn_sinks, conv_w, conv_b, conv_ln_g, conv_ln_b, w_out, norm2_g, w_up, w_down, norm_f_g):
    batch, seq, _ = x_prompt.shape
    nb, dec_seq, _ = x_sample.shape
    assert batch == 1 and dec_seq == 1 and w_in.shape[0] == 1
    assert seq % PROMPT_TILE == 0 and nb % SAMPLE_CHUNK == 0 and nb <= PROMPT_TILE
    assert len(SLOT_ATTN_UNITS) == len(SLOT_CONV_CHUNKS) == D_FF // UP_CHUNK + D_MODEL // DOWN_CHUNK
    assert sum(SLOT_ATTN_UNITS) == (PROMPT_TILE // BLOCK) * N_KV_HEADS
    assert sum(SLOT_CONV_CHUNKS) == PROMPT_TILE // CONV_ROWS

    w_in_b = w_in[0].astype(BF16)
    w_out_b = w_out[0].astype(BF16)
    w_up_b = w_up[0].astype(BF16)
    w_down_b = w_down[0].astype(BF16)
    g1 = norm1_g[0][None]
    g2 = norm2_g[0][None]
    gf = norm_f_g[None]
    cw, cb = conv_w[0], conv_b[0][None]
    lng, lnb = conv_ln_g[0][None], conv_ln_b[0][None]
    sinks = attn_sinks[0]

    dist_p = jnp.arange(BLOCK)[:, None] + BLOCK - jnp.arange(2 * BLOCK)[None, :]
    bucket_p = _t5_bucket(jnp.clip(dist_p, 0, WINDOW)).astype(jnp.int32)
    lane = jnp.arange(2 * BLOCK)
    dist_s = jnp.where(lane < WINDOW, WINDOW - lane, 0)
    bucket_s = jnp.where(lane <= WINDOW, _t5_bucket(jnp.clip(dist_s, 0, WINDOW)), -1)
    bucket_s = bucket_s.astype(jnp.int32)[None]

    ck = cache_k[0].reshape(nb, WINDOW, KV_W)
    cv = cache_v[0].reshape(nb, WINDOW, KV_W)
    x2_s, nk_s, nv_s, nc_s = _sample_mixer(x_sample[:, 0], ck, cv, state_conv[0], rel_bias, sinks, g1,
                                           w_in_b, bucket_s, cw, cb, lng, lnb, w_out_b)
    y_p, nk_p, nv_p, nc_p, y_s = _prompt_layer(x_prompt[0], meta_tokens, rel_bias, sinks, g1, w_in_b,
                                               bucket_p, cw, cb, lng, lnb, w_out_b, g2, w_up_b,
                                               w_down_b, gf, x2_s)

    kv_shape = (1, WINDOW, N_KV_HEADS, HEAD_DIM)
    return (y_p[None], y_s[:, None],
            nk_p.reshape((1,) + kv_shape), nv_p.reshape((1,) + kv_shape), nc_p[None, None],
            nk_s.reshape((1, nb) + kv_shape[1:]), nv_s.reshape((1, nb) + kv_shape[1:]), nc_s[None])
```

```python
import functools
import math

import jax
import jax.numpy as jnp
from jax import lax
from jax.experimental import pallas as pl
from jax.experimental.pallas import tpu as pltpu

D_MODEL = 1024
N_HEADS = 8
N_KV_HEADS = 2
HEAD_DIM = 64
GROUP = N_HEADS // N_KV_HEADS
ATTN_W = N_HEADS * HEAD_DIM
KV_W = N_KV_HEADS * HEAD_DIM
CONV_CH = D_MODEL - ATTN_W
IN_W = ATTN_W + 2 * KV_W + 2 * CONV_CH
CONV_WIDTH = 31
WINDOW = 128
BLOCK = 128
N_BUCKETS = 32
MAX_DISTANCE = WINDOW
N_META = 16
D_FF = 4 * D_MODEL
EPS = 1e-6
SCALE = HEAD_DIM ** -0.5

O_K = ATTN_W
O_V = ATTN_W + KV_W
O_A = ATTN_W + 2 * KV_W
O_B = O_A + CONV_CH

PAD = (-N_META) % BLOCK
U_CARRY = 32
U_SHIFT = U_CARRY - (CONV_WIDTH - 1)

V7X_VMEM_LIMIT_BYTES = 60 * 1024 * 1024

PROMPT_TILE = 512
UP_CHUNK = 1024
DOWN_CHUNK = 512
SLOT_ATTN_UNITS = (1, 1, 1, 1, 2, 2)
SLOT_CONV_CHUNKS = (1, 1, 1, 1, 2, 2)
CONV_ROWS = 64
SAMPLE_CHUNK = 16
HEAD_ROWS = 16

BF16 = jnp.bfloat16
F32 = jnp.float32
NT_DIMS = (((1,), (1,)), ((), ()))


def _t5_bucket(d):
    max_exact = N_BUCKETS // 2
    d_f = jnp.maximum(d, 1).astype(jnp.float32)
    large = max_exact + (jnp.log(d_f / max_exact) / math.log(MAX_DISTANCE / max_exact)
                         * (N_BUCKETS - max_exact)).astype(jnp.int32)
    large = jnp.minimum(large, N_BUCKETS - 1)
    return jnp.where(d < max_exact, d, large)


def _rms(x, g):
    y = x * lax.rsqrt(jnp.mean(x * x, axis=-1, keepdims=True) + EPS)
    return y * g


def _sigmoid(x):
    return 1.0 / (1.0 + jnp.exp(-x))


def _bias_from_buckets(bucket, relb_ref, h):
    b = jnp.zeros(bucket.shape, F32)
    for bk in range(N_BUCKETS):
        b = jnp.where(bucket == bk, relb_ref[bk, h], b)
    return b


def _conv_rows(ubuf, cw_ref, r0, rows):
    n = rows + U_CARRY
    strips = []
    for c0 in range(0, CONV_CH, BLOCK):
        win = ubuf[r0:r0 + n, c0:c0 + BLOCK]
        acc = None
        for s in range(8):
            sh = win if s == 0 else pltpu.roll(win, n - s, axis=0)
            for a0 in range(0, U_CARRY + 8, 8):
                w = a0 + s - U_SHIFT
                if 0 <= w < CONV_WIDTH:
                    term = cw_ref[w:w + 1, c0:c0 + BLOCK] * sh[a0:a0 + rows]
                    acc = term if acc is None else acc + term
        strips.append(acc)
    return jnp.concatenate(strips, axis=1)


def _ln_silu(acc, lng, lnb):
    mu = jnp.mean(acc, axis=-1, keepdims=True)
    xc = acc - mu
    y = xc * lax.rsqrt(jnp.mean(xc * xc, axis=-1, keepdims=True) + EPS)
    y = y * lng + lnb
    return y * _sigmoid(y)


def _zero_after(v, prev=None):
    u = pltpu.bitcast(v, jnp.uint32)
    t = prev
    for r0 in range(0, u.shape[0], 8):
        for c0 in range(0, u.shape[1], BLOCK):
            piece = u[r0:r0 + 8, c0:c0 + BLOCK]
            t = piece if t is None else t | piece
    return (t >> 16) >> 16


def _order_after(buf, zero):
    tile = pltpu.bitcast(buf[0:16, 0:BLOCK], jnp.uint32)
    buf[0:16, 0:BLOCK] = pltpu.bitcast(tile | zero, BF16)


def _prompt_layer_body(relb_ref, sink_ref, x_ref, meta_ref, g1_ref, win_ref, bucket_ref,
                       cw_ref, cb_ref, lng_ref, lnb_ref, wout_ref, g2_ref, wup_ref, wdn_ref, gf_ref,
                       x2s_ref, y_ref, nk_ref, nv_ref, nc_ref, ys_ref,
                       kbuf, vbuf, ubuf, qbuf, mix, bias_s, x2buf, kvlast, hfbuf, hidbuf, ysbuf,
                       *, tile):
    i = pl.program_id(0)
    n_tiles = pl.num_programs(0) - 1
    g1 = g1_ref[...]
    slot = i % 2

    @pl.when(i == 0)
    def _init():
        nb = x2s_ref.shape[0]
        x2buf[1, 0:nb, :] = x2s_ref[...]
        x2buf[1, nb:tile, :] = jnp.zeros((tile - nb, D_MODEL), F32)
        bucket = bucket_ref[...]
        row = lax.broadcasted_iota(jnp.int32, (BLOCK, 2 * BLOCK), 0)
        col = lax.broadcasted_iota(jnp.int32, (BLOCK, 2 * BLOCK), 1)
        dist = row + BLOCK - col
        band = (dist >= 0) & (dist <= WINDOW)
        band_first = band & (col >= PAD)
        for h in range(N_HEADS):
            b = _bias_from_buckets(bucket, relb_ref, h)
            bias_s[0, h] = jnp.where(band, b, -jnp.inf)
            bias_s[1, h] = jnp.where(band_first, b, -jnp.inf)
        hm = _rms(meta_ref[...], g1).astype(BF16)
        pm = jnp.dot(hm, win_ref[:, O_K:], preferred_element_type=F32)
        kbuf[0:PAD, :] = jnp.zeros((PAD, KV_W), BF16)
        vbuf[0:PAD, :] = jnp.zeros((PAD, KV_W), BF16)
        kbuf[PAD:BLOCK, :] = pm[:, 0:KV_W].astype(BF16)
        vbuf[PAD:BLOCK, :] = pm[:, KV_W:2 * KV_W].astype(BF16)
        um = pm[:, 2 * KV_W:2 * KV_W + CONV_CH] * _sigmoid(pm[:, 2 * KV_W + CONV_CH:])
        ubuf[0:U_CARRY - N_META, :] = jnp.zeros((U_CARRY - N_META, CONV_CH), F32)
        ubuf[U_CARRY - N_META:U_CARRY, :] = um

    x = x_ref[...]
    h = _rms(x, g1).astype(BF16)
    q = jnp.dot(h, win_ref[:, 0:ATTN_W], preferred_element_type=F32) * SCALE
    qbuf[...] = q.astype(BF16)
    kv = jnp.dot(h, win_ref[:, O_K:O_A], preferred_element_type=F32)
    kbuf[BLOCK:BLOCK + tile, :] = kv[:, 0:KV_W].astype(BF16)
    vbuf[BLOCK:BLOCK + tile, :] = kv[:, KV_W:].astype(BF16)
    kvlast[...] = kv[tile - WINDOW:, :]

    a = jnp.dot(h, win_ref[:, O_A:O_B], preferred_element_type=F32)
    b = jnp.dot(h, win_ref[:, O_B:], preferred_element_type=F32)
    ubuf[U_CARRY:U_CARRY + tile, :] = a * _sigmoid(b)

    cb, lng, lnb = cb_ref[...], lng_ref[...], lnb_ref[...]

    def conv_chunk(r0):
        c = _ln_silu(_conv_rows(ubuf, cw_ref, r0, CONV_ROWS) + cb, lng, lnb)
        mix[r0:r0 + CONV_ROWS, ATTN_W:] = c.astype(BF16)

    def attn_scores(blk, kvh):
        r0 = blk * BLOCK
        c0 = kvh * HEAD_DIM
        qg = jnp.concatenate(
            [qbuf[r0:r0 + BLOCK, (kvh * GROUP + g) * HEAD_DIM:(kvh * GROUP + g + 1) * HEAD_DIM]
             for g in range(GROUP)], axis=0)
        kk = kbuf[r0:r0 + 2 * BLOCK, c0:c0 + HEAD_DIM]
        return lax.dot_general(qg, kk, NT_DIMS, preferred_element_type=F32)

    def attn_finish(blk, kvh, s):
        r0 = blk * BLOCK
        sel = jnp.where(i == 0, 1, 0) if blk == 0 else 0
        c0 = kvh * HEAD_DIM
        vv = vbuf[r0:r0 + 2 * BLOCK, c0:c0 + HEAD_DIM]
        ps, ls = [], []
        for g in range(GROUP):
            hd = kvh * GROUP + g
            sg = s[g * BLOCK:(g + 1) * BLOCK] + bias_s[sel, hd]
            sk = sink_ref[hd]
            m = jnp.maximum(jnp.max(sg, axis=-1, keepdims=True), sk)
            p = jnp.exp(sg - m)
            ls.append(jnp.sum(p, axis=-1, keepdims=True) + jnp.exp(sk - m))
            ps.append(p.astype(BF16))
        o = jnp.dot(jnp.concatenate(ps, axis=0), vv, preferred_element_type=F32)
        for g in range(GROUP):
            hd = kvh * GROUP + g
            og = o[g * BLOCK:(g + 1) * BLOCK] / ls[g]
            mix[r0:r0 + BLOCK, hd * HEAD_DIM:(hd + 1) * HEAD_DIM] = og.astype(BF16)

    conv_starts = list(range(0, tile, CONV_ROWS))
    attn_units = [(blk, kvh) for blk in range(tile // BLOCK) for kvh in range(N_KV_HEADS)]
    n_up, n_down = D_FF // UP_CHUNK, D_MODEL // DOWN_CHUNK
    n_slots = n_up + n_down
    unit_iter, conv_iter = iter(attn_units), iter(conv_starts)
    xf = x2buf[1 - slot]
    hfbuf[...] = _rms(xf, g2_ref[...]).astype(BF16)
    x3_cols = []
    for k in range(n_slots):
        units = [next(unit_iter) for _ in range(SLOT_ATTN_UNITS[k])]
        scores = [(blk, kvh, attn_scores(blk, kvh)) for blk, kvh in units]
        if k < n_up:
            c0 = k * UP_CHUNK
            hid = jnp.dot(hfbuf[...], wup_ref[:, c0:c0 + UP_CHUNK], preferred_element_type=F32)
            hidbuf[:, c0:c0 + UP_CHUNK] = jnp.square(jnp.maximum(hid, 0.0)).astype(BF16)
        else:
            n0 = (k - n_up) * DOWN_CHUNK
            x3_cols.append(xf[:, n0:n0 + DOWN_CHUNK]
                           + jnp.dot(hidbuf[...], wdn_ref[:, n0:n0 + DOWN_CHUNK],
                                     preferred_element_type=F32))
        zero = None
        for r0 in [next(conv_iter) for _ in range(SLOT_CONV_CHUNKS[k])]:
            conv_chunk(r0)
            zero = _zero_after(mix[r0:r0 + CONV_ROWS, ATTN_W:], zero)
        for blk, kvh, s in scores:
            attn_finish(blk, kvh, s)
        if k + 1 < n_slots and zero is not None:
            _order_after(hfbuf if k + 1 < n_up else hidbuf, zero)
    acc = jnp.concatenate(x3_cols, axis=1)
    y = _rms(acc, gf_ref[...])
    y_ref[...] = y
    ysbuf[...] = y[0:ysbuf.shape[0]]

    x2buf[slot] = x + jnp.dot(mix[...], wout_ref[...], preferred_element_type=F32)

    @pl.when(i == n_tiles - 1)
    def _new_caches():
        nk_ref[...] = kvlast[:, 0:KV_W].T
        nv_ref[...] = kvlast[:, KV_W:].T
        nc_ref[...] = ubuf[U_CARRY + tile - (CONV_WIDTH - 1):U_CARRY + tile, :]

    kbuf[0:BLOCK, :] = kbuf[tile:tile + BLOCK, :]
    vbuf[0:BLOCK, :] = vbuf[tile:tile + BLOCK, :]
    ubuf[0:U_CARRY, :] = ubuf[tile:tile + U_CARRY, :]

    @pl.when(i == 0)
    def _sample_out():
        ys_ref[...] = ysbuf[...]


def _const_spec(shape):
    return pl.BlockSpec(shape, lambda i: (0,) * len(shape), pipeline_mode=pl.Buffered(1))


def _smem_spec():
    return pl.BlockSpec(memory_space=pltpu.SMEM)


def _prompt_layer(x, meta, rel_bias, sinks, g1, w_in, bucket, cw, cb, lng, lnb, w_out,
                  g2, w_up, w_down, gf, x2_s):
    seq = x.shape[0]
    nb = x2_s.shape[0]
    tile = PROMPT_TILE
    n_tiles = seq // tile
    body = functools.partial(_prompt_layer_body, tile=tile)
    return pl.pallas_call(
        body,
        grid=(n_tiles + 1,),
        in_specs=[
            _smem_spec(), _smem_spec(),
            pl.BlockSpec((tile, D_MODEL), lambda i: (jnp.minimum(i, n_tiles - 1), 0)),
            _const_spec((N_META, D_MODEL)),
            _const_spec((1, D_MODEL)),
            _const_spec((D_MODEL, IN_W)),
            _const_spec((BLOCK, 2 * BLOCK)),
            _const_spec((CONV_WIDTH, CONV_CH)),
            _const_spec((1, CONV_CH)),
            _const_spec((1, CONV_CH)),
            _const_spec((1, CONV_CH)),
            _const_spec((ATTN_W + CONV_CH, D_MODEL)),
            _const_spec((1, D_MODEL)),
            _const_spec((D_MODEL, D_FF)),
            _const_spec((D_FF, D_MODEL)),
            _const_spec((1, D_MODEL)),
            _const_spec((nb, D_MODEL)),
        ],
        out_specs=[
            pl.BlockSpec((tile, D_MODEL), lambda i: (jnp.maximum(i - 1, 0), 0)),
            pl.BlockSpec((KV_W, WINDOW), lambda i: (0, 0)),
            pl.BlockSpec((KV_W, WINDOW), lambda i: (0, 0)),
            pl.BlockSpec((CONV_WIDTH - 1, CONV_CH), lambda i: (0, 0)),
            pl.BlockSpec((nb, D_MODEL), lambda i: (0, 0)),
        ],
        out_shape=[
            jax.ShapeDtypeStruct((seq, D_MODEL), F32),
            jax.ShapeDtypeStruct((KV_W, WINDOW), F32),
            jax.ShapeDtypeStruct((KV_W, WINDOW), F32),
            jax.ShapeDtypeStruct((CONV_WIDTH - 1, CONV_CH), F32),
            jax.ShapeDtypeStruct((nb, D_MODEL), F32),
        ],
        scratch_shapes=[
            pltpu.VMEM((BLOCK + tile, KV_W), BF16),
            pltpu.VMEM((BLOCK + tile, KV_W), BF16),
            pltpu.VMEM((U_CARRY + tile, CONV_CH), F32),
            pltpu.VMEM((tile, ATTN_W), BF16),
            pltpu.VMEM((tile, ATTN_W + CONV_CH), BF16),
            pltpu.VMEM((2, N_HEADS, BLOCK, 2 * BLOCK), F32),
            pltpu.VMEM((2, tile, D_MODEL), F32),
            pltpu.VMEM((WINDOW, 2 * KV_W), F32),
            pltpu.VMEM((tile, D_MODEL), BF16),
            pltpu.VMEM((tile, D_FF), BF16),
            pltpu.VMEM((nb, D_MODEL), F32),
        ],
        compiler_params=pltpu.CompilerParams(
            dimension_semantics=("arbitrary",), vmem_limit_bytes=V7X_VMEM_LIMIT_BYTES),
        name="prompt_layer",
    )(rel_bias, sinks, x, meta, g1, w_in, bucket, cw, cb, lng, lnb, w_out, g2, w_up, w_down, gf,
      x2_s)


def _sample_mixer_body(relb_ref, sink_ref, x_ref, ckt_ref, cvt_ref, st_ref, g1_ref, win_ref,
                       bucket_ref, cw_ref, cb_ref, lng_ref, lnb_ref, wout_ref,
                       x2_ref, nkt_ref, nvt_ref, nc_ref,
                       pbuf, mix, bias_c, sink_c, *, chunk):
    i = pl.program_id(0)
    last = pl.num_programs(0) - 1
    rows_h = chunk * HEAD_ROWS

    @pl.when(i == 0)
    def _init():
        h = _rms(x_ref[...], g1_ref[...]).astype(BF16)
        pbuf[...] = jnp.dot(h, win_ref[...], preferred_element_type=F32)
        bucket = bucket_ref[...]
        rid = lax.broadcasted_iota(jnp.int32, (HEAD_ROWS, 1), 0)
        bias = jnp.zeros((HEAD_ROWS, 2 * BLOCK), F32)
        sk = jnp.zeros((HEAD_ROWS, 1), F32)
        for hd in range(N_HEADS):
            bias = jnp.where(rid == hd, _bias_from_buckets(bucket, relb_ref, hd), bias)
            sk = jnp.where(rid == hd, sink_ref[hd], sk)
        bias_c[...] = bias
        sink_c[...] = sk

    r0 = pl.multiple_of(i * chunk, chunk)
    pr = pbuf[pl.ds(r0, chunk), :]
    q = pr[:, 0:ATTN_W] * SCALE
    knew = pr[:, O_K:O_V]
    vnew = pr[:, O_V:O_A]
    unew = pr[:, O_A:O_B] * _sigmoid(pr[:, O_B:])

    def per_head(t):
        n = t.shape[-1]
        return jnp.broadcast_to(t[:, None, :], (chunk, HEAD_ROWS, n)).reshape(rows_h, n)

    hid = lax.broadcasted_iota(jnp.int32, (rows_h, 1), 0) % HEAD_ROWS
    lane = lax.broadcasted_iota(jnp.int32, (1, BLOCK), 1)
    qrep = per_head(q)
    qsum = jnp.zeros((rows_h, BLOCK), F32)
    for c in range(ATTN_W // BLOCK):
        piece = qrep[:, c * BLOCK:(c + 1) * BLOCK]
        in_head = (lane // HEAD_DIM + 2 * c) == hid
        qsum = qsum + jnp.where(in_head, piece, 0.0)
    keep = (hid % 2) == (hid // GROUP)
    qf = jnp.where(keep, qsum, pltpu.roll(qsum, HEAD_DIM, axis=1))
    qf_b = qf.astype(BF16)

    bias = jnp.broadcast_to(bias_c[...][None], (chunk, HEAD_ROWS, 2 * BLOCK)).reshape(rows_h, 2 * BLOCK)
    sk = jnp.broadcast_to(sink_c[...][None], (chunk, HEAD_ROWS, 1)).reshape(rows_h, 1)

    s_rows = []
    for b in range(chunk):
        kt = ckt_ref[b].astype(BF16)
        s_rows.append(jnp.dot(qf_b[b * HEAD_ROWS:(b + 1) * HEAD_ROWS], kt, preferred_element_type=F32))
    s_c = jnp.concatenate(s_rows, axis=0) + bias[:, 0:BLOCK]
    s_n = jnp.sum(qf * per_head(knew), axis=-1, keepdims=True) + bias[:, BLOCK:BLOCK + 1]
    m = jnp.maximum(jnp.maximum(jnp.max(s_c, axis=-1, keepdims=True), s_n), sk)
    p_c = jnp.exp(s_c - m)
    p_n = jnp.exp(s_n - m)
    l = jnp.sum(p_c, axis=-1, keepdims=True) + p_n + jnp.exp(sk - m)
    p_cb = p_c.astype(BF16)
    o_rows = []
    for b in range(chunk):
        vt = cvt_ref[b].astype(BF16)
        o_rows.append(lax.dot_general(p_cb[b * HEAD_ROWS:(b + 1) * HEAD_ROWS], vt, NT_DIMS,
                                      preferred_element_type=F32))
    o = (jnp.concatenate(o_rows, axis=0) + p_n * per_head(vnew)) / l
    o = jnp.where(keep, o, pltpu.roll(o, HEAD_DIM, axis=1))
    o = jnp.where((lane // HEAD_DIM) == (hid % 2), o, 0.0)
    wide = jnp.concatenate([jnp.where(hid // 2 == c, o, 0.0) for c in range(ATTN_W // BLOCK)], axis=1)
    gi = lax.broadcasted_iota(jnp.int32, (chunk, rows_h), 0)
    gj = lax.broadcasted_iota(jnp.int32, (chunk, rows_h), 1)
    gather = jnp.where(gj // HEAD_ROWS == gi, 1.0, 0.0).astype(BF16)
    ao = jnp.dot(gather, wide.astype(BF16), preferred_element_type=F32)
    mix[pl.ds(r0, chunk), 0:ATTN_W] = ao

    pad = jnp.zeros((BLOCK - chunk, KV_W), F32)
    knew_t = jnp.concatenate([knew, pad], axis=0).T
    vnew_t = jnp.concatenate([vnew, pad], axis=0).T
    newest = lane == WINDOW - 1
    for b in range(chunk):
        kcol = jnp.broadcast_to(knew_t[:, b:b + 1], (KV_W, WINDOW))
        vcol = jnp.broadcast_to(vnew_t[:, b:b + 1], (KV_W, WINDOW))
        nkt_ref[b] = jnp.where(newest, kcol, pltpu.roll(ckt_ref[b], WINDOW - 1, axis=1))
        nvt_ref[b] = jnp.where(newest, vcol, pltpu.roll(cvt_ref[b], WINDOW - 1, axis=1))

    acc = cb_ref[...] + cw_ref[CONV_WIDTH - 1:CONV_WIDTH, :] * unew
    for w in range(CONV_WIDTH - 1):
        acc = acc + cw_ref[w:w + 1, :] * st_ref[w]
    nc_ref[0:CONV_WIDTH - 2] = st_ref[1:CONV_WIDTH - 1]
    nc_ref[CONV_WIDTH - 2] = unew
    mu = jnp.mean(acc, axis=-1, keepdims=True)
    xc = acc - mu
    y = xc * lax.rsqrt(jnp.mean(xc * xc, axis=-1, keepdims=True) + EPS)
    y = y * lng_ref[...] + lnb_ref[...]
    mix[pl.ds(r0, chunk), ATTN_W:] = y * _sigmoid(y)

    @pl.when(i == last)
    def _out():
        x2_ref[...] = x_ref[...] + jnp.dot(mix[...].astype(BF16), wout_ref[...],
                                           preferred_element_type=F32)


def _sample_mixer(x, ckt, cvt, st, rel_bias, sinks, g1, w_in, bucket, cw, cb, lng, lnb, w_out):
    nb = x.shape[0]
    chunk = SAMPLE_CHUNK
    body = functools.partial(_sample_mixer_body, chunk=chunk)
    cache_spec = pl.BlockSpec((chunk, KV_W, WINDOW), lambda i: (i, 0, 0))
    state_spec = pl.BlockSpec((CONV_WIDTH - 1, chunk, CONV_CH), lambda i: (0, i, 0))
    return pl.pallas_call(
        body,
        grid=(nb // chunk,),
        in_specs=[
            _smem_spec(), _smem_spec(),
            _const_spec((nb, D_MODEL)),
            cache_spec, cache_spec, state_spec,
            _const_spec((1, D_MODEL)),
            _const_spec((D_MODEL, IN_W)),
            _const_spec((1, 2 * BLOCK)),
            _const_spec((CONV_WIDTH, CONV_CH)),
            _const_spec((1, CONV_CH)),
            _const_spec((1, CONV_CH)),
            _const_spec((1, CONV_CH)),
            _const_spec((ATTN_W + CONV_CH, D_MODEL)),
        ],
        out_specs=[
            pl.BlockSpec((nb, D_MODEL), lambda i: (0, 0)),
            cache_spec, cache_spec, state_spec,
        ],
        out_shape=[
            jax.ShapeDtypeStruct((nb, D_MODEL), F32),
            jax.ShapeDtypeStruct((nb, KV_W, WINDOW), F32),
            jax.ShapeDtypeStruct((nb, KV_W, WINDOW), F32),
            jax.ShapeDtypeStruct((CONV_WIDTH - 1, nb, CONV_CH), F32),
        ],
        scratch_shapes=[
            pltpu.VMEM((nb, IN_W), F32),
            pltpu.VMEM((nb, ATTN_W + CONV_CH), F32),
            pltpu.VMEM((HEAD_ROWS, 2 * BLOCK), F32),
            pltpu.VMEM((HEAD_ROWS, 1), F32),
        ],
        compiler_params=pltpu.CompilerParams(
            dimension_semantics=("arbitrary",), vmem_limit_bytes=V7X_VMEM_LIMIT_BYTES),
        name="sample_mixer",
    )(rel_bias, sinks, x, ckt, cvt, st, g1, w_in, bucket, cw, cb, lng, lnb, w_out)


def kernel(x_prompt, x_sample, cache_k, cache_v, state_conv, meta_tokens, rel_bias, norm1_g, w_in,
           attn_sinks, conv_w, conv_b, conv_ln_g, conv_ln_b, w_out, norm2_g, w_up, w_down, norm_f_g):
    batch, seq, _ = x_prompt.shape
    nb, dec_seq, _ = x_sample.shape
    assert batch == 1 and dec_seq == 1 and w_in.shape[0] == 1
    assert seq % PROMPT_TILE == 0 and nb % SAMPLE_CHUNK == 0 and nb <= PROMPT_TILE
    assert len(SLOT_ATTN_UNITS) == len(SLOT_CONV_CHUNKS) == D_FF // UP_CHUNK + D_MODEL // DOWN_CHUNK
    assert sum(SLOT_ATTN_UNITS) == (PROMPT_TILE // BLOCK) * N_KV_HEADS
    assert sum(SLOT_CONV_CHUNKS) == PROMPT_TILE // CONV_ROWS

    w_in_b = w_in[0].astype(BF16)
    w_out_b = w_out[0].astype(BF16)
    w_up_b = w_up[0].astype(BF16)
    w_down_b = w_down[0].astype(BF16)
    g1 = norm1_g[0][None]
    g2 = norm2_g[0][None]
    gf = norm_f_g[None]
    cw, cb = conv_w[0], conv_b[0][None]
    lng, lnb = conv_ln_g[0][None], conv_ln_b[0][None]
    sinks = attn_sinks[0]

    dist_p = jnp.arange(BLOCK)[:, None] + BLOCK - jnp.arange(2 * BLOCK)[None, :]
    bucket_p = _t5_bucket(jnp.clip(dist_p, 0, WINDOW)).astype(jnp.int32)
    lane = jnp.arange(2 * BLOCK)
    dist_s = jnp.where(lane < WINDOW, WINDOW - lane, 0)
    bucket_s = jnp.where(lane <= WINDOW, _t5_bucket(jnp.clip(dist_s, 0, WINDOW)), -1)
    bucket_s = bucket_s.astype(jnp.int32)[None]

    ckt = jnp.transpose(cache_k[0], (0, 2, 3, 1)).reshape(nb, KV_W, WINDOW)
    cvt = jnp.transpose(cache_v[0], (0, 2, 3, 1)).reshape(nb, KV_W, WINDOW)
    st = jnp.transpose(state_conv[0], (1, 0, 2))
    x2_s, nkt_s, nvt_s, nct_s = _sample_mixer(x_sample[:, 0], ckt, cvt, st, rel_bias, sinks, g1,
                                              w_in_b, bucket_s, cw, cb, lng, lnb, w_out_b)
    nk_s = jnp.transpose(nkt_s.reshape(nb, N_KV_HEADS, HEAD_DIM, WINDOW), (0, 3, 1, 2))
    nv_s = jnp.transpose(nvt_s.reshape(nb, N_KV_HEADS, HEAD_DIM, WINDOW), (0, 3, 1, 2))
    nc_s = jnp.transpose(nct_s, (1, 0, 2))
    y_p, nkt_p, nvt_p, nc_p, y_s = _prompt_layer(x_prompt[0], meta_tokens, rel_bias, sinks, g1, w_in_b,
                                               bucket_p, cw, cb, lng, lnb, w_out_b, g2, w_up_b,
                                               w_down_b, gf, x2_s)

    def to_cache(t):
        return jnp.transpose(t.reshape(N_KV_HEADS, HEAD_DIM, WINDOW), (2, 0, 1))[None, None]

    return (y_p[None], y_s[:, None],
            to_cache(nkt_p), to_cache(nvt_p), nc_p[None, None],
            nk_s[None], nv_s[None], nc_s[None])
```

```python
import functools
import math

import jax
import jax.numpy as jnp
from jax import lax
from jax.experimental import pallas as pl
from jax.experimental.pallas import tpu as pltpu

D_MODEL = 1024
N_HEADS = 8
N_KV_HEADS = 2
HEAD_DIM = 64
GROUP = N_HEADS // N_KV_HEADS
ATTN_W = N_HEADS * HEAD_DIM
KV_W = N_KV_HEADS * HEAD_DIM
CONV_CH = D_MODEL - ATTN_W
IN_W = ATTN_W + 2 * KV_W + 2 * CONV_CH
CONV_WIDTH = 31
WINDOW = 128
BLOCK = 128
N_BUCKETS = 32
MAX_DISTANCE = WINDOW
N_META = 16
D_FF = 4 * D_MODEL
EPS = 1e-6
SCALE = HEAD_DIM ** -0.5

O_K = ATTN_W
O_V = ATTN_W + KV_W
O_A = ATTN_W + 2 * KV_W
O_B = O_A + CONV_CH

PAD = (-N_META) % BLOCK
U_CARRY = 32
U_SHIFT = U_CARRY - (CONV_WIDTH - 1)

V7X_VMEM_LIMIT_BYTES = 60 * 1024 * 1024

PROMPT_TILE = 512
UP_CHUNK = 1024
DOWN_CHUNK = 512
SLOT_ATTN_UNITS = (2, 1, 1, 1, 2, 1)
SLOT_CONV_CHUNKS = (0, 1, 1, 2, 2, 2)
CONV_ROWS = 64
SAMPLE_CHUNK = 16
HEAD_ROWS = 16

BF16 = jnp.bfloat16
F32 = jnp.float32
NT_DIMS = (((1,), (1,)), ((), ()))


def _t5_bucket(d):
    max_exact = N_BUCKETS // 2
    d_f = jnp.maximum(d, 1).astype(jnp.float32)
    large = max_exact + (jnp.log(d_f / max_exact) / math.log(MAX_DISTANCE / max_exact)
                         * (N_BUCKETS - max_exact)).astype(jnp.int32)
    large = jnp.minimum(large, N_BUCKETS - 1)
    return jnp.where(d < max_exact, d, large)


def _rms(x, g):
    y = x * lax.rsqrt(jnp.mean(x * x, axis=-1, keepdims=True) + EPS)
    return y * g


def _sigmoid(x):
    return 1.0 / (1.0 + jnp.exp(-x))


def _bias_from_buckets(bucket, relb_ref, h):
    b = jnp.zeros(bucket.shape, F32)
    for bk in range(N_BUCKETS):
        b = jnp.where(bucket == bk, relb_ref[bk, h], b)
    return b


def _conv_rows(ubuf, cw_ref, r0, rows):
    n = rows + U_CARRY
    strips = []
    for c0 in range(0, CONV_CH, BLOCK):
        win = ubuf[r0:r0 + n, c0:c0 + BLOCK]
        acc = None
        for s in range(8):
            sh = win if s == 0 else pltpu.roll(win, n - s, axis=0)
            for a0 in range(0, U_CARRY + 8, 8):
                w = a0 + s - U_SHIFT
                if 0 <= w < CONV_WIDTH:
                    term = cw_ref[w:w + 1, c0:c0 + BLOCK] * sh[a0:a0 + rows]
                    acc = term if acc is None else acc + term
        strips.append(acc)
    return jnp.concatenate(strips, axis=1)


def _ln_silu(acc, lng, lnb):
    mu = jnp.mean(acc, axis=-1, keepdims=True)
    xc = acc - mu
    y = xc * lax.rsqrt(jnp.mean(xc * xc, axis=-1, keepdims=True) + EPS)
    y = y * lng + lnb
    return y * _sigmoid(y)


def _zero_after(v, prev=None):
    u = pltpu.bitcast(v, jnp.uint32)
    t = prev
    for r0 in range(0, u.shape[0], 8):
        for c0 in range(0, u.shape[1], BLOCK):
            piece = u[r0:r0 + 8, c0:c0 + BLOCK]
            t = piece if t is None else t | piece
    return (t >> 16) >> 16


def _order_after(buf, zero):
    tile = pltpu.bitcast(buf[0:16, 0:BLOCK], jnp.uint32)
    buf[0:16, 0:BLOCK] = pltpu.bitcast(tile | zero, BF16)


def _prompt_layer_body(relb_ref, sink_ref, x_ref, meta_ref, g1_ref, win_ref, bucket_ref,
                       cw_ref, cb_ref, lng_ref, lnb_ref, wout_ref, g2_ref, wup_ref, wdn_ref, gf_ref,
                       x2s_ref, xnext_ref, y_ref, nk_ref, nv_ref, nc_ref, ys_ref,
                       kbuf, vbuf, ubuf, qbuf, mix, bias_s, x2buf, kvlast, hfbuf, hidbuf, ysbuf, hbuf,
                       *, tile):
    i = pl.program_id(0)
    n_tiles = pl.num_programs(0) - 1
    g1 = g1_ref[...]
    slot = i % 2

    @pl.when(i == 0)
    def _init():
        hbuf[...] = _rms(x_ref[...], g1).astype(BF16)
        nb = x2s_ref.shape[0]
        x2buf[1, 0:nb, :] = x2s_ref[...]
        x2buf[1, nb:tile, :] = jnp.zeros((tile - nb, D_MODEL), F32)
        bucket = bucket_ref[...]
        row = lax.broadcasted_iota(jnp.int32, (BLOCK, 2 * BLOCK), 0)
        col = lax.broadcasted_iota(jnp.int32, (BLOCK, 2 * BLOCK), 1)
        dist = row + BLOCK - col
        band = (dist >= 0) & (dist <= WINDOW)
        band_first = band & (col >= PAD)
        for h in range(N_HEADS):
            b = _bias_from_buckets(bucket, relb_ref, h)
            bias_s[0, h] = jnp.where(band, b, -jnp.inf)
            bias_s[1, h] = jnp.where(band_first, b, -jnp.inf)
        hm = _rms(meta_ref[...], g1).astype(BF16)
        pm = jnp.dot(hm, win_ref[:, O_K:], preferred_element_type=F32)
        kbuf[0:PAD, :] = jnp.zeros((PAD, KV_W), BF16)
        vbuf[0:PAD, :] = jnp.zeros((PAD, KV_W), BF16)
        kbuf[PAD:BLOCK, :] = pm[:, 0:KV_W].astype(BF16)
        vbuf[PAD:BLOCK, :] = pm[:, KV_W:2 * KV_W].astype(BF16)
        um = pm[:, 2 * KV_W:2 * KV_W + CONV_CH] * _sigmoid(pm[:, 2 * KV_W + CONV_CH:])
        ubuf[0:U_CARRY - N_META, :] = jnp.zeros((U_CARRY - N_META, CONV_CH), F32)
        ubuf[U_CARRY - N_META:U_CARRY, :] = um

    x = x_ref[...]
    h = hbuf[...]
    q = jnp.dot(h, win_ref[:, 0:ATTN_W], preferred_element_type=F32) * SCALE
    qbuf[...] = q.astype(BF16)
    kv = jnp.dot(h, win_ref[:, O_K:O_A], preferred_element_type=F32)
    kbuf[BLOCK:BLOCK + tile, :] = kv[:, 0:KV_W].astype(BF16)
    vbuf[BLOCK:BLOCK + tile, :] = kv[:, KV_W:].astype(BF16)
    kvlast[...] = kv[tile - WINDOW:, :]

    a = jnp.dot(h, win_ref[:, O_A:O_B], preferred_element_type=F32)
    b = jnp.dot(h, win_ref[:, O_B:], preferred_element_type=F32)
    ubuf[U_CARRY:U_CARRY + tile, :] = a * _sigmoid(b)

    cb, lng, lnb = cb_ref[...], lng_ref[...], lnb_ref[...]

    def conv_chunk(r0):
        c = _ln_silu(_conv_rows(ubuf, cw_ref, r0, CONV_ROWS) + cb, lng, lnb)
        mix[r0:r0 + CONV_ROWS, ATTN_W:] = c.astype(BF16)

    def attn_scores(blk, kvh):
        r0 = blk * BLOCK
        c0 = kvh * HEAD_DIM
        qg = jnp.concatenate(
            [qbuf[r0:r0 + BLOCK, (kvh * GROUP + g) * HEAD_DIM:(kvh * GROUP + g + 1) * HEAD_DIM]
             for g in range(GROUP)], axis=0)
        kk = kbuf[r0:r0 + 2 * BLOCK, c0:c0 + HEAD_DIM]
        return lax.dot_general(qg, kk, NT_DIMS, preferred_element_type=F32)

    def attn_finish(blk, kvh, s):
        r0 = blk * BLOCK
        sel = jnp.where(i == 0, 1, 0) if blk == 0 else 0
        c0 = kvh * HEAD_DIM
        vv = vbuf[r0:r0 + 2 * BLOCK, c0:c0 + HEAD_DIM]
        ps, ls = [], []
        for g in range(GROUP):
            hd = kvh * GROUP + g
            sg = s[g * BLOCK:(g + 1) * BLOCK] + bias_s[sel, hd]
            sk = sink_ref[hd]
            m = jnp.maximum(jnp.max(sg, axis=-1, keepdims=True), sk)
            p = jnp.exp(sg - m)
            ls.append(jnp.sum(p, axis=-1, keepdims=True) + jnp.exp(sk - m))
            ps.append(p.astype(BF16))
        o = jnp.dot(jnp.concatenate(ps, axis=0), vv, preferred_element_type=F32)
        for g in range(GROUP):
            hd = kvh * GROUP + g
            og = o[g * BLOCK:(g + 1) * BLOCK] / ls[g]
            mix[r0:r0 + BLOCK, hd * HEAD_DIM:(hd + 1) * HEAD_DIM] = og.astype(BF16)

    conv_starts = list(range(0, tile, CONV_ROWS))
    attn_units = [(blk, kvh) for blk in range(tile // BLOCK) for kvh in range(N_KV_HEADS)]
    n_up, n_down = D_FF // UP_CHUNK, D_MODEL // DOWN_CHUNK
    n_slots = n_up + n_down
    unit_iter, conv_iter = iter(attn_units), iter(conv_starts)
    xf = x2buf[1 - slot]
    hfbuf[...] = _rms(xf, g2_ref[...]).astype(BF16)
    x3_cols = []
    for k in range(n_slots):
        units = [next(unit_iter) for _ in range(SLOT_ATTN_UNITS[k])]
        scores = [(blk, kvh, attn_scores(blk, kvh)) for blk, kvh in units]
        if k < n_up:
            c0 = k * UP_CHUNK
            hid = jnp.dot(hfbuf[...], wup_ref[:, c0:c0 + UP_CHUNK], preferred_element_type=F32)
            hidbuf[:, c0:c0 + UP_CHUNK] = jnp.square(jnp.maximum(hid, 0.0)).astype(BF16)
        else:
            n0 = (k - n_up) * DOWN_CHUNK
            x3_cols.append(xf[:, n0:n0 + DOWN_CHUNK]
                           + jnp.dot(hidbuf[...], wdn_ref[:, n0:n0 + DOWN_CHUNK],
                                     preferred_element_type=F32))
        zero = None
        for r0 in [next(conv_iter) for _ in range(SLOT_CONV_CHUNKS[k])]:
            conv_chunk(r0)
            zero = _zero_after(mix[r0:r0 + CONV_ROWS, ATTN_W:], zero)
        for blk, kvh, s in scores:
            attn_finish(blk, kvh, s)
        if k == n_up:
            hbuf[...] = _rms(xnext_ref[...], g1).astype(BF16)
            zero = _zero_after(hbuf[0:16, 0:BLOCK], zero)
        if k + 1 < n_slots and zero is not None:
            _order_after(hfbuf if k + 1 < n_up else hidbuf, zero)
    acc = jnp.concatenate(x3_cols, axis=1)
    y = _rms(acc, gf_ref[...])
    y_ref[...] = y
    ysbuf[...] = y[0:ysbuf.shape[0]]

    x2buf[slot] = x + jnp.dot(mix[...], wout_ref[...], preferred_element_type=F32)

    @pl.when(i == n_tiles - 1)
    def _new_caches():
        nk_ref[...] = kvlast[:, 0:KV_W].T
        nv_ref[...] = kvlast[:, KV_W:].T
        nc_ref[...] = ubuf[U_CARRY + tile - (CONV_WIDTH - 1):U_CARRY + tile, :]

    kbuf[0:BLOCK, :] = kbuf[tile:tile + BLOCK, :]
    vbuf[0:BLOCK, :] = vbuf[tile:tile + BLOCK, :]
    ubuf[0:U_CARRY, :] = ubuf[tile:tile + U_CARRY, :]

    @pl.when(i == 0)
    def _sample_out():
        ys_ref[...] = ysbuf[...]


def _const_spec(shape):
    return pl.BlockSpec(shape, lambda i: (0,) * len(shape), pipeline_mode=pl.Buffered(1))


def _smem_spec():
    return pl.BlockSpec(memory_space=pltpu.SMEM)


def _prompt_layer(x, meta, rel_bias, sinks, g1, w_in, bucket, cw, cb, lng, lnb, w_out,
                  g2, w_up, w_down, gf, x2_s):
    seq = x.shape[0]
    nb = x2_s.shape[0]
    tile = PROMPT_TILE
    n_tiles = seq // tile
    body = functools.partial(_prompt_layer_body, tile=tile)
    return pl.pallas_call(
        body,
        grid=(n_tiles + 1,),
        in_specs=[
            _smem_spec(), _smem_spec(),
            pl.BlockSpec((tile, D_MODEL), lambda i: (jnp.minimum(i, n_tiles - 1), 0)),
            _const_spec((N_META, D_MODEL)),
            _const_spec((1, D_MODEL)),
            _const_spec((D_MODEL, IN_W)),
            _const_spec((BLOCK, 2 * BLOCK)),
            _const_spec((CONV_WIDTH, CONV_CH)),
            _const_spec((1, CONV_CH)),
            _const_spec((1, CONV_CH)),
            _const_spec((1, CONV_CH)),
            _const_spec((ATTN_W + CONV_CH, D_MODEL)),
            _const_spec((1, D_MODEL)),
            _const_spec((D_MODEL, D_FF)),
            _const_spec((D_FF, D_MODEL)),
            _const_spec((1, D_MODEL)),
            _const_spec((nb, D_MODEL)),
            pl.BlockSpec((tile, D_MODEL), lambda i: (jnp.minimum(i + 1, n_tiles - 1), 0)),
        ],
        out_specs=[
            pl.BlockSpec((tile, D_MODEL), lambda i: (jnp.maximum(i - 1, 0), 0)),
            pl.BlockSpec((KV_W, WINDOW), lambda i: (0, 0)),
            pl.BlockSpec((KV_W, WINDOW), lambda i: (0, 0)),
            pl.BlockSpec((CONV_WIDTH - 1, CONV_CH), lambda i: (0, 0)),
            pl.BlockSpec((nb, D_MODEL), lambda i: (0, 0)),
        ],
        out_shape=[
            jax.ShapeDtypeStruct((seq, D_MODEL), F32),
            jax.ShapeDtypeStruct((KV_W, WINDOW), F32),
            jax.ShapeDtypeStruct((KV_W, WINDOW), F32),
            jax.ShapeDtypeStruct((CONV_WIDTH - 1, CONV_CH), F32),
            jax.ShapeDtypeStruct((nb, D_MODEL), F32),
        ],
        scratch_shapes=[
            pltpu.VMEM((BLOCK + tile, KV_W), BF16),
            pltpu.VMEM((BLOCK + tile, KV_W), BF16),
            pltpu.VMEM((U_CARRY + tile, CONV_CH), F32),
            pltpu.VMEM((tile, ATTN_W), BF16),
            pltpu.VMEM((tile, ATTN_W + CONV_CH), BF16),
            pltpu.VMEM((2, N_HEADS, BLOCK, 2 * BLOCK), F32),
            pltpu.VMEM((2, tile, D_MODEL), F32),
            pltpu.VMEM((WINDOW, 2 * KV_W), F32),
            pltpu.VMEM((tile, D_MODEL), BF16),
            pltpu.VMEM((tile, D_FF), BF16),
            pltpu.VMEM((nb, D_MODEL), F32),
            pltpu.VMEM((tile, D_MODEL), BF16),
        ],
        compiler_params=pltpu.CompilerParams(
            dimension_semantics=("arbitrary",), vmem_limit_bytes=V7X_VMEM_LIMIT_BYTES),
        name="prompt_layer",
    )(rel_bias, sinks, x, meta, g1, w_in, bucket, cw, cb, lng, lnb, w_out, g2, w_up, w_down, gf,
      x2_s, x)


def _sample_mixer_body(relb_ref, sink_ref, x_ref, ckt_ref, cvt_ref, st_ref, g1_ref, win_ref,
                       bucket_ref, cw_ref, cb_ref, lng_ref, lnb_ref, wout_ref,
                       x2_ref, nkt_ref, nvt_ref, nc_ref,
                       pbuf, mix, bias_c, sink_c, *, chunk):
    i = pl.program_id(0)
    last = pl.num_programs(0) - 1
    rows_h = chunk * HEAD_ROWS

    @pl.when(i == 0)
    def _init():
        h = _rms(x_ref[...], g1_ref[...]).astype(BF16)
        pbuf[...] = jnp.dot(h, win_ref[...], preferred_element_type=F32)
        bucket = bucket_ref[...]
        rid = lax.broadcasted_iota(jnp.int32, (HEAD_ROWS, 1), 0)
        bias = jnp.zeros((HEAD_ROWS, 2 * BLOCK), F32)
        sk = jnp.zeros((HEAD_ROWS, 1), F32)
        for hd in range(N_HEADS):
            bias = jnp.where(rid == hd, _bias_from_buckets(bucket, relb_ref, hd), bias)
            sk = jnp.where(rid == hd, sink_ref[hd], sk)
        bias_c[...] = bias
        sink_c[...] = sk

    r0 = pl.multiple_of(i * chunk, chunk)
    pr = pbuf[pl.ds(r0, chunk), :]
    q = pr[:, 0:ATTN_W] * SCALE
    knew = pr[:, O_K:O_V]
    vnew = pr[:, O_V:O_A]
    unew = pr[:, O_A:O_B] * _sigmoid(pr[:, O_B:])

    def per_head(t):
        n = t.shape[-1]
        return jnp.broadcast_to(t[:, None, :], (chunk, HEAD_ROWS, n)).reshape(rows_h, n)

    hid = lax.broadcasted_iota(jnp.int32, (rows_h, 1), 0) % HEAD_ROWS
    lane = lax.broadcasted_iota(jnp.int32, (1, BLOCK), 1)
    qrep = per_head(q)
    qsum = jnp.zeros((rows_h, BLOCK), F32)
    for c in range(ATTN_W // BLOCK):
        piece = qrep[:, c * BLOCK:(c + 1) * BLOCK]
        in_head = (lane // HEAD_DIM + 2 * c) == hid
        qsum = qsum + jnp.where(in_head, piece, 0.0)
    keep = (hid % 2) == (hid // GROUP)
    qf = jnp.where(keep, qsum, pltpu.roll(qsum, HEAD_DIM, axis=1))
    qf_b = qf.astype(BF16)

    bias = jnp.broadcast_to(bias_c[...][None], (chunk, HEAD_ROWS, 2 * BLOCK)).reshape(rows_h, 2 * BLOCK)
    sk = jnp.broadcast_to(sink_c[...][None], (chunk, HEAD_ROWS, 1)).reshape(rows_h, 1)

    s_rows = []
    for b in range(chunk):
        kt = ckt_ref[b].astype(BF16)
        s_rows.append(jnp.dot(qf_b[b * HEAD_ROWS:(b + 1) * HEAD_ROWS], kt, preferred_element_type=F32))
    s_c = jnp.concatenate(s_rows, axis=0) + bias[:, 0:BLOCK]
    s_n = jnp.sum(qf * per_head(knew), axis=-1, keepdims=True) + bias[:, BLOCK:BLOCK + 1]
    m = jnp.maximum(jnp.maximum(jnp.max(s_c, axis=-1, keepdims=True), s_n), sk)
    p_c = jnp.exp(s_c - m)
    p_n = jnp.exp(s_n - m)
    l = jnp.sum(p_c, axis=-1, keepdims=True) + p_n + jnp.exp(sk - m)
    p_cb = p_c.astype(BF16)
    o_rows = []
    for b in range(chunk):
        vt = cvt_ref[b].astype(BF16)
        o_rows.append(lax.dot_general(p_cb[b * HEAD_ROWS:(b + 1) * HEAD_ROWS], vt, NT_DIMS,
                                      preferred_element_type=F32))
    o = (jnp.concatenate(o_rows, axis=0) + p_n * per_head(vnew)) / l
    o = jnp.where(keep, o, pltpu.roll(o, HEAD_DIM, axis=1))
    o = jnp.where((lane // HEAD_DIM) == (hid % 2), o, 0.0)
    wide = jnp.concatenate([jnp.where(hid // 2 == c, o, 0.0) for c in range(ATTN_W // BLOCK)], axis=1)
    gi = lax.broadcasted_iota(jnp.int32, (chunk, rows_h), 0)
    gj = lax.broadcasted_iota(jnp.int32, (chunk, rows_h), 1)
    gather = jnp.where(gj // HEAD_ROWS == gi, 1.0, 0.0).astype(BF16)
    ao = jnp.dot(gather, wide.astype(BF16), preferred_element_type=F32)
    mix[pl.ds(r0, chunk), 0:ATTN_W] = ao

    pad = jnp.zeros((BLOCK - chunk, KV_W), F32)
    knew_t = jnp.concatenate([knew, pad], axis=0).T
    vnew_t = jnp.concatenate([vnew, pad], axis=0).T
    newest = lane == WINDOW - 1
    for b in range(chunk):
        kcol = jnp.broadcast_to(knew_t[:, b:b + 1], (KV_W, WINDOW))
        vcol = jnp.broadcast_to(vnew_t[:, b:b + 1], (KV_W, WINDOW))
        nkt_ref[b] = jnp.where(newest, kcol, pltpu.roll(ckt_ref[b], WINDOW - 1, axis=1))
        nvt_ref[b] = jnp.where(newest, vcol, pltpu.roll(cvt_ref[b], WINDOW - 1, axis=1))

    acc = cb_ref[...] + cw_ref[CONV_WIDTH - 1:CONV_WIDTH, :] * unew
    for w in range(CONV_WIDTH - 1):
        acc = acc + cw_ref[w:w + 1, :] * st_ref[w]
    nc_ref[0:CONV_WIDTH - 2] = st_ref[1:CONV_WIDTH - 1]
    nc_ref[CONV_WIDTH - 2] = unew
    mu = jnp.mean(acc, axis=-1, keepdims=True)
    xc = acc - mu
    y = xc * lax.rsqrt(jnp.mean(xc * xc, axis=-1, keepdims=True) + EPS)
    y = y * lng_ref[...] + lnb_ref[...]
    mix[pl.ds(r0, chunk), ATTN_W:] = y * _sigmoid(y)

    @pl.when(i == last)
    def _out():
        x2_ref[...] = x_ref[...] + jnp.dot(mix[...].astype(BF16), wout_ref[...],
                                           preferred_element_type=F32)


def _sample_mixer(x, ckt, cvt, st, rel_bias, sinks, g1, w_in, bucket, cw, cb, lng, lnb, w_out):
    nb = x.shape[0]
    chunk = SAMPLE_CHUNK
    body = functools.partial(_sample_mixer_body, chunk=chunk)
    cache_spec = pl.BlockSpec((chunk, KV_W, WINDOW), lambda i: (i, 0, 0))
    state_spec = pl.BlockSpec((CONV_WIDTH - 1, chunk, CONV_CH), lambda i: (0, i, 0))
    return pl.pallas_call(
        body,
        grid=(nb // chunk,),
        in_specs=[
            _smem_spec(), _smem_spec(),
            _const_spec((nb, D_MODEL)),
            cache_spec, cache_spec, state_spec,
            _const_spec((1, D_MODEL)),
            _const_spec((D_MODEL, IN_W)),
            _const_spec((1, 2 * BLOCK)),
            _const_spec((CONV_WIDTH, CONV_CH)),
            _const_spec((1, CONV_CH)),
            _const_spec((1, CONV_CH)),
            _const_spec((1, CONV_CH)),
            _const_spec((ATTN_W + CONV_CH, D_MODEL)),
        ],
        out_specs=[
            pl.BlockSpec((nb, D_MODEL), lambda i: (0, 0)),
            cache_spec, cache_spec, state_spec,
        ],
        out_shape=[
            jax.ShapeDtypeStruct((nb, D_MODEL), F32),
            jax.ShapeDtypeStruct((nb, KV_W, WINDOW), F32),
            jax.ShapeDtypeStruct((nb, KV_W, WINDOW), F32),
            jax.ShapeDtypeStruct((CONV_WIDTH - 1, nb, CONV_CH), F32),
        ],
        scratch_shapes=[
            pltpu.VMEM((nb, IN_W), F32),
            pltpu.VMEM((nb, ATTN_W + CONV_CH), F32),
            pltpu.VMEM((HEAD_ROWS, 2 * BLOCK), F32),
            pltpu.VMEM((HEAD_ROWS, 1), F32),
        ],
        compiler_params=pltpu.CompilerParams(
            dimension_semantics=("arbitrary",), vmem_limit_bytes=V7X_VMEM_LIMIT_BYTES),
        name="sample_mixer",
    )(rel_bias, sinks, x, ckt, cvt, st, g1, w_in, bucket, cw, cb, lng, lnb, w_out)


def kernel(x_prompt, x_sample, cache_k, cache_v, state_conv, meta_tokens, rel_bias, norm1_g, w_in,
           attn_sinks, conv_w, conv_b, conv_ln_g, conv_ln_b, w_out, norm2_g, w_up, w_down, norm_f_g):
    batch, seq, _ = x_prompt.shape
    nb, dec_seq, _ = x_sample.shape
    assert batch == 1 and dec_seq == 1 and w_in.shape[0] == 1
    assert seq % PROMPT_TILE == 0 and nb % SAMPLE_CHUNK == 0 and nb <= PROMPT_TILE
    assert len(SLOT_ATTN_UNITS) == len(SLOT_CONV_CHUNKS) == D_FF // UP_CHUNK + D_MODEL // DOWN_CHUNK
    assert sum(SLOT_ATTN_UNITS) == (PROMPT_TILE // BLOCK) * N_KV_HEADS
    assert sum(SLOT_CONV_CHUNKS) == PROMPT_TILE // CONV_ROWS

    w_in_b = w_in[0].astype(BF16)
    w_out_b = w_out[0].astype(BF16)
    w_up_b = w_up[0].astype(BF16)
    w_down_b = w_down[0].astype(BF16)
    g1 = norm1_g[0][None]
    g2 = norm2_g[0][None]
    gf = norm_f_g[None]
    cw, cb = conv_w[0], conv_b[0][None]
    lng, lnb = conv_ln_g[0][None], conv_ln_b[0][None]
    sinks = attn_sinks[0]

    dist_p = jnp.arange(BLOCK)[:, None] + BLOCK - jnp.arange(2 * BLOCK)[None, :]
    bucket_p = _t5_bucket(jnp.clip(dist_p, 0, WINDOW)).astype(jnp.int32)
    lane = jnp.arange(2 * BLOCK)
    dist_s = jnp.where(lane < WINDOW, WINDOW - lane, 0)
    bucket_s = jnp.where(lane <= WINDOW, _t5_bucket(jnp.clip(dist_s, 0, WINDOW)), -1)
    bucket_s = bucket_s.astype(jnp.int32)[None]

    ckt = jnp.transpose(cache_k[0], (0, 2, 3, 1)).reshape(nb, KV_W, WINDOW)
    cvt = jnp.transpose(cache_v[0], (0, 2, 3, 1)).reshape(nb, KV_W, WINDOW)
    st = jnp.transpose(state_conv[0], (1, 0, 2))
    x2_s, nkt_s, nvt_s, nct_s = _sample_mixer(x_sample[:, 0], ckt, cvt, st, rel_bias, sinks, g1,
                                              w_in_b, bucket_s, cw, cb, lng, lnb, w_out_b)
    nk_s = jnp.transpose(nkt_s.reshape(nb, N_KV_HEADS, HEAD_DIM, WINDOW), (0, 3, 1, 2))
    nv_s = jnp.transpose(nvt_s.reshape(nb, N_KV_HEADS, HEAD_DIM, WINDOW), (0, 3, 1, 2))
    nc_s = jnp.transpose(nct_s, (1, 0, 2))
    y_p, nkt_p, nvt_p, nc_p, y_s = _prompt_layer(x_prompt[0], meta_tokens, rel_bias, sinks, g1, w_in_b,
                                               bucket_p, cw, cb, lng, lnb, w_out_b, g2, w_up_b,
                                               w_down_b, gf, x2_s)

    def to_cache(t):
        return jnp.transpose(t.reshape(N_KV_HEADS, HEAD_DIM, WINDOW), (2, 0, 1))[None, None]

    return (y_p[None], y_s[:, None],
            to_cache(nkt_p), to_cache(nvt_p), nc_p[None, None],
            nk_s[None], nv_s[None], nc_s[None])
```

```python
import functools
import math

import jax
import jax.numpy as jnp
from jax import lax
from jax.experimental import pallas as pl
from jax.experimental.pallas import tpu as pltpu

D_MODEL = 1024
N_HEADS = 8
N_KV_HEADS = 2
HEAD_DIM = 64
GROUP = N_HEADS // N_KV_HEADS
ATTN_W = N_HEADS * HEAD_DIM
KV_W = N_KV_HEADS * HEAD_DIM
CONV_CH = D_MODEL - ATTN_W
IN_W = ATTN_W + 2 * KV_W + 2 * CONV_CH
CONV_WIDTH = 31
WINDOW = 128
BLOCK = 128
N_BUCKETS = 32
MAX_DISTANCE = WINDOW
N_META = 16
D_FF = 4 * D_MODEL
EPS = 1e-6
SCALE = HEAD_DIM ** -0.5

O_K = ATTN_W
O_V = ATTN_W + KV_W
O_A = ATTN_W + 2 * KV_W
O_B = O_A + CONV_CH

PAD = (-N_META) % BLOCK
U_CARRY = 32
U_SHIFT = U_CARRY - (CONV_WIDTH - 1)

V7X_VMEM_LIMIT_BYTES = 60 * 1024 * 1024

PROMPT_TILE = 512
UP_CHUNK = 1024
DOWN_CHUNK = 512
SLOT_ATTN_UNITS = (1, 1, 1, 1, 2, 2)
SLOT_CONV_CHUNKS = (1, 1, 1, 1, 2, 2)
CONV_ROWS = 64
SAMPLE_CHUNK = 16
HEAD_ROWS = 16

BF16 = jnp.bfloat16
F32 = jnp.float32
NT_DIMS = (((1,), (1,)), ((), ()))


def _t5_bucket(d):
    max_exact = N_BUCKETS // 2
    d_f = jnp.maximum(d, 1).astype(jnp.float32)
    large = max_exact + (jnp.log(d_f / max_exact) / math.log(MAX_DISTANCE / max_exact)
                         * (N_BUCKETS - max_exact)).astype(jnp.int32)
    large = jnp.minimum(large, N_BUCKETS - 1)
    return jnp.where(d < max_exact, d, large)


def _rms(x, g):
    y = x * lax.rsqrt(jnp.mean(x * x, axis=-1, keepdims=True) + EPS)
    return y * g


def _sigmoid(x):
    return 1.0 / (1.0 + jnp.exp(-x))


def _bias_from_buckets(bucket, relb_ref, h):
    b = jnp.zeros(bucket.shape, F32)
    for bk in range(N_BUCKETS):
        b = jnp.where(bucket == bk, relb_ref[bk, h], b)
    return b


def _conv_rows(ubuf, cw_ref, r0, rows):
    n = rows + U_CARRY
    strips = []
    for c0 in range(0, CONV_CH, BLOCK):
        win = ubuf[r0:r0 + n, c0:c0 + BLOCK]
        acc = None
        for s in range(8):
            sh = win if s == 0 else pltpu.roll(win, n - s, axis=0)
            for a0 in range(0, U_CARRY + 8, 8):
                w = a0 + s - U_SHIFT
                if 0 <= w < CONV_WIDTH:
                    term = cw_ref[w:w + 1, c0:c0 + BLOCK] * sh[a0:a0 + rows]
                    acc = term if acc is None else acc + term
        strips.append(acc)
    return jnp.concatenate(strips, axis=1)


def _ln_silu(acc, lng, lnb):
    mu = jnp.mean(acc, axis=-1, keepdims=True)
    xc = acc - mu
    y = xc * lax.rsqrt(jnp.mean(xc * xc, axis=-1, keepdims=True) + EPS)
    y = y * lng + lnb
    return y * _sigmoid(y)


def _zero_after(v, prev=None):
    u = pltpu.bitcast(v, jnp.uint32)
    t = prev
    for r0 in range(0, u.shape[0], 8):
        for c0 in range(0, u.shape[1], BLOCK):
            piece = u[r0:r0 + 8, c0:c0 + BLOCK]
            t = piece if t is None else t | piece
    return (t >> 16) >> 16


def _order_after(buf, zero):
    tile = pltpu.bitcast(buf[0:16, 0:BLOCK], jnp.uint32)
    buf[0:16, 0:BLOCK] = pltpu.bitcast(tile | zero, BF16)


def _prompt_layer_body(*refs, tile):
    i = pl.program_id(0)
    n_tiles = pl.num_programs(0) - 1
    pl.when(i < n_tiles)(functools.partial(_prompt_step, *refs, tile=tile))
    pl.when(i == n_tiles)(functools.partial(_prompt_last_ffn, *refs, tile=tile))


def _prompt_last_ffn(relb_ref, sink_ref, x_ref, meta_ref, g1_ref, win_ref, bucket_ref,
                     cw_ref, cb_ref, lng_ref, lnb_ref, wout_ref, g2_ref, wup_ref, wdn_ref, gf_ref,
                     x2s_ref, y_ref, nk_ref, nv_ref, nc_ref, ys_ref,
                     kbuf, vbuf, ubuf, qbuf, mix, bias_s, x2buf, kvlast, hfbuf, hidbuf, ysbuf,
                     *, tile):
    xf = x2buf[1 - pl.program_id(0) % 2]
    hf = _rms(xf, g2_ref[...]).astype(BF16)
    for c0 in range(0, D_FF, UP_CHUNK):
        hid = jnp.dot(hf, wup_ref[:, c0:c0 + UP_CHUNK], preferred_element_type=F32)
        hidbuf[:, c0:c0 + UP_CHUNK] = jnp.square(jnp.maximum(hid, 0.0)).astype(BF16)
    cols = [xf[:, n0:n0 + DOWN_CHUNK]
            + jnp.dot(hidbuf[...], wdn_ref[:, n0:n0 + DOWN_CHUNK], preferred_element_type=F32)
            for n0 in range(0, D_MODEL, DOWN_CHUNK)]
    y_ref[...] = _rms(jnp.concatenate(cols, axis=1), gf_ref[...])


def _prompt_step(relb_ref, sink_ref, x_ref, meta_ref, g1_ref, win_ref, bucket_ref,
                 cw_ref, cb_ref, lng_ref, lnb_ref, wout_ref, g2_ref, wup_ref, wdn_ref, gf_ref,
                 x2s_ref, y_ref, nk_ref, nv_ref, nc_ref, ys_ref,
                 kbuf, vbuf, ubuf, qbuf, mix, bias_s, x2buf, kvlast, hfbuf, hidbuf, ysbuf,
                 *, tile):
    i = pl.program_id(0)
    n_tiles = pl.num_programs(0) - 1
    g1 = g1_ref[...]
    slot = i % 2

    @pl.when(i == 0)
    def _init():
        nb = x2s_ref.shape[0]
        x2buf[1, 0:nb, :] = x2s_ref[...]
        x2buf[1, nb:tile, :] = jnp.zeros((tile - nb, D_MODEL), F32)
        bucket = bucket_ref[...]
        row = lax.broadcasted_iota(jnp.int32, (BLOCK, 2 * BLOCK), 0)
        col = lax.broadcasted_iota(jnp.int32, (BLOCK, 2 * BLOCK), 1)
        dist = row + BLOCK - col
        band = (dist >= 0) & (dist <= WINDOW)
        band_first = band & (col >= PAD)
        for h in range(N_HEADS):
            b = _bias_from_buckets(bucket, relb_ref, h)
            bias_s[0, h] = jnp.where(band, b, -jnp.inf)
            bias_s[1, h] = jnp.where(band_first, b, -jnp.inf)
        hm = _rms(meta_ref[...], g1).astype(BF16)
        pm = jnp.dot(hm, win_ref[:, O_K:], preferred_element_type=F32)
        kbuf[0:PAD, :] = jnp.zeros((PAD, KV_W), BF16)
        vbuf[0:PAD, :] = jnp.zeros((PAD, KV_W), BF16)
        kbuf[PAD:BLOCK, :] = pm[:, 0:KV_W].astype(BF16)
        vbuf[PAD:BLOCK, :] = pm[:, KV_W:2 * KV_W].astype(BF16)
        um = pm[:, 2 * KV_W:2 * KV_W + CONV_CH] * _sigmoid(pm[:, 2 * KV_W + CONV_CH:])
        ubuf[0:U_CARRY - N_META, :] = jnp.zeros((U_CARRY - N_META, CONV_CH), F32)
        ubuf[U_CARRY - N_META:U_CARRY, :] = um

    x = x_ref[...]
    h = _rms(x, g1).astype(BF16)
    q = jnp.dot(h, win_ref[:, 0:ATTN_W], preferred_element_type=F32) * SCALE
    qbuf[...] = q.astype(BF16)
    kv = jnp.dot(h, win_ref[:, O_K:O_A], preferred_element_type=F32)
    kbuf[BLOCK:BLOCK + tile, :] = kv[:, 0:KV_W].astype(BF16)
    vbuf[BLOCK:BLOCK + tile, :] = kv[:, KV_W:].astype(BF16)
    kvlast[...] = kv[tile - WINDOW:, :]

    a = jnp.dot(h, win_ref[:, O_A:O_B], preferred_element_type=F32)
    b = jnp.dot(h, win_ref[:, O_B:], preferred_element_type=F32)
    ubuf[U_CARRY:U_CARRY + tile, :] = a * _sigmoid(b)

    cb, lng, lnb = cb_ref[...], lng_ref[...], lnb_ref[...]

    def conv_chunk(r0):
        c = _ln_silu(_conv_rows(ubuf, cw_ref, r0, CONV_ROWS) + cb, lng, lnb)
        mix[r0:r0 + CONV_ROWS, ATTN_W:] = c.astype(BF16)

    def attn_scores(blk, kvh):
        r0 = blk * BLOCK
        c0 = kvh * HEAD_DIM
        qg = jnp.concatenate(
            [qbuf[r0:r0 + BLOCK, (kvh * GROUP + g) * HEAD_DIM:(kvh * GROUP + g + 1) * HEAD_DIM]
             for g in range(GROUP)], axis=0)
        kk = kbuf[r0:r0 + 2 * BLOCK, c0:c0 + HEAD_DIM]
        return lax.dot_general(qg, kk, NT_DIMS, preferred_element_type=F32)

    def attn_finish(blk, kvh, s):
        r0 = blk * BLOCK
        sel = jnp.where(i == 0, 1, 0) if blk == 0 else 0
        c0 = kvh * HEAD_DIM
        vv = vbuf[r0:r0 + 2 * BLOCK, c0:c0 + HEAD_DIM]
        ps, ls = [], []
        for g in range(GROUP):
            hd = kvh * GROUP + g
            sg = s[g * BLOCK:(g + 1) * BLOCK] + bias_s[sel, hd]
            sk = sink_ref[hd]
            m = jnp.maximum(jnp.max(sg, axis=-1, keepdims=True), sk)
            p = jnp.exp(sg - m)
            ls.append(jnp.sum(p, axis=-1, keepdims=True) + jnp.exp(sk - m))
            ps.append(p.astype(BF16))
        o = jnp.dot(jnp.concatenate(ps, axis=0), vv, preferred_element_type=F32)
        for g in range(GROUP):
            hd = kvh * GROUP + g
            og = o[g * BLOCK:(g + 1) * BLOCK] / ls[g]
            mix[r0:r0 + BLOCK, hd * HEAD_DIM:(hd + 1) * HEAD_DIM] = og.astype(BF16)

    conv_starts = list(range(0, tile, CONV_ROWS))
    attn_units = [(blk, kvh) for blk in range(tile // BLOCK) for kvh in range(N_KV_HEADS)]
    n_up, n_down = D_FF // UP_CHUNK, D_MODEL // DOWN_CHUNK
    n_slots = n_up + n_down
    unit_iter, conv_iter = iter(attn_units), iter(conv_starts)
    xf = x2buf[1 - slot]
    hfbuf[...] = _rms(xf, g2_ref[...]).astype(BF16)
    x3_cols = []
    for k in range(n_slots):
        units = [next(unit_iter) for _ in range(SLOT_ATTN_UNITS[k])]
        scores = [(blk, kvh, attn_scores(blk, kvh)) for blk, kvh in units]
        if k < n_up:
            c0 = k * UP_CHUNK
            hid = jnp.dot(hfbuf[...], wup_ref[:, c0:c0 + UP_CHUNK], preferred_element_type=F32)
            hidbuf[:, c0:c0 + UP_CHUNK] = jnp.square(jnp.maximum(hid, 0.0)).astype(BF16)
        else:
            n0 = (k - n_up) * DOWN_CHUNK
            x3_cols.append(xf[:, n0:n0 + DOWN_CHUNK]
                           + jnp.dot(hidbuf[...], wdn_ref[:, n0:n0 + DOWN_CHUNK],
                                     preferred_element_type=F32))
        zero = None
        for r0 in [next(conv_iter) for _ in range(SLOT_CONV_CHUNKS[k])]:
            conv_chunk(r0)
            zero = _zero_after(mix[r0:r0 + CONV_ROWS, ATTN_W:], zero)
        for blk, kvh, s in scores:
            attn_finish(blk, kvh, s)
        if k + 1 < n_slots and zero is not None:
            _order_after(hfbuf if k + 1 < n_up else hidbuf, zero)
    acc = jnp.concatenate(x3_cols, axis=1)
    y = _rms(acc, gf_ref[...])
    y_ref[...] = y
    ysbuf[...] = y[0:ysbuf.shape[0]]

    x2buf[slot] = x + jnp.dot(mix[...], wout_ref[...], preferred_element_type=F32)

    @pl.when(i == n_tiles - 1)
    def _new_caches():
        nk_ref[...] = kvlast[:, 0:KV_W].T
        nv_ref[...] = kvlast[:, KV_W:].T
        nc_ref[...] = ubuf[U_CARRY + tile - (CONV_WIDTH - 1):U_CARRY + tile, :]

    kbuf[0:BLOCK, :] = kbuf[tile:tile + BLOCK, :]
    vbuf[0:BLOCK, :] = vbuf[tile:tile + BLOCK, :]
    ubuf[0:U_CARRY, :] = ubuf[tile:tile + U_CARRY, :]

    @pl.when(i == 0)
    def _sample_out():
        ys_ref[...] = ysbuf[...]


def _const_spec(shape):
    return pl.BlockSpec(shape, lambda i: (0,) * len(shape), pipeline_mode=pl.Buffered(1))


def _smem_spec():
    return pl.BlockSpec(memory_space=pltpu.SMEM)


def _prompt_layer(x, meta, rel_bias, sinks, g1, w_in, bucket, cw, cb, lng, lnb, w_out,
                  g2, w_up, w_down, gf, x2_s):
    seq = x.shape[0]
    nb = x2_s.shape[0]
    tile = PROMPT_TILE
    n_tiles = seq // tile
    body = functools.partial(_prompt_layer_body, tile=tile)
    return pl.pallas_call(
        body,
        grid=(n_tiles + 1,),
        in_specs=[
            _smem_spec(), _smem_spec(),
            pl.BlockSpec((tile, D_MODEL), lambda i: (jnp.minimum(i, n_tiles - 1), 0)),
            _const_spec((N_META, D_MODEL)),
            _const_spec((1, D_MODEL)),
            _const_spec((D_MODEL, IN_W)),
            _const_spec((BLOCK, 2 * BLOCK)),
            _const_spec((CONV_WIDTH, CONV_CH)),
            _const_spec((1, CONV_CH)),
            _const_spec((1, CONV_CH)),
            _const_spec((1, CONV_CH)),
            _const_spec((ATTN_W + CONV_CH, D_MODEL)),
            _const_spec((1, D_MODEL)),
            _const_spec((D_MODEL, D_FF)),
            _const_spec((D_FF, D_MODEL)),
            _const_spec((1, D_MODEL)),
            _const_spec((nb, D_MODEL)),
        ],
        out_specs=[
            pl.BlockSpec((tile, D_MODEL), lambda i: (jnp.maximum(i - 1, 0), 0)),
            pl.BlockSpec((KV_W, WINDOW), lambda i: (0, 0)),
            pl.BlockSpec((KV_W, WINDOW), lambda i: (0, 0)),
            pl.BlockSpec((CONV_WIDTH - 1, CONV_CH), lambda i: (0, 0)),
            pl.BlockSpec((nb, D_MODEL), lambda i: (0, 0)),
        ],
        out_shape=[
            jax.ShapeDtypeStruct((seq, D_MODEL), F32),
            jax.ShapeDtypeStruct((KV_W, WINDOW), F32),
            jax.ShapeDtypeStruct((KV_W, WINDOW), F32),
            jax.ShapeDtypeStruct((CONV_WIDTH - 1, CONV_CH), F32),
            jax.ShapeDtypeStruct((nb, D_MODEL), F32),
        ],
        scratch_shapes=[
            pltpu.VMEM((BLOCK + tile, KV_W), BF16),
            pltpu.VMEM((BLOCK + tile, KV_W), BF16),
            pltpu.VMEM((U_CARRY + tile, CONV_CH), F32),
            pltpu.VMEM((tile, ATTN_W), BF16),
            pltpu.VMEM((tile, ATTN_W + CONV_CH), BF16),
            pltpu.VMEM((2, N_HEADS, BLOCK, 2 * BLOCK), F32),
            pltpu.VMEM((2, tile, D_MODEL), F32),
            pltpu.VMEM((WINDOW, 2 * KV_W), F32),
            pltpu.VMEM((tile, D_MODEL), BF16),
            pltpu.VMEM((tile, D_FF), BF16),
            pltpu.VMEM((nb, D_MODEL), F32),
        ],
        compiler_params=pltpu.CompilerParams(
            dimension_semantics=("arbitrary",), vmem_limit_bytes=V7X_VMEM_LIMIT_BYTES),
        name="prompt_layer",
    )(rel_bias, sinks, x, meta, g1, w_in, bucket, cw, cb, lng, lnb, w_out, g2, w_up, w_down, gf,
      x2_s)


def _sample_mixer_body(relb_ref, sink_ref, x_ref, ckt_ref, cvt_ref, st_ref, g1_ref, win_ref,
                       bucket_ref, cw_ref, cb_ref, lng_ref, lnb_ref, wout_ref,
                       x2_ref, nkt_ref, nvt_ref, nc_ref,
                       pbuf, mix, bias_c, sink_c, *, chunk):
    i = pl.program_id(0)
    last = pl.num_programs(0) - 1
    rows_h = chunk * HEAD_ROWS

    @pl.when(i == 0)
    def _init():
        h = _rms(x_ref[...], g1_ref[...]).astype(BF16)
        pbuf[...] = jnp.dot(h, win_ref[...], preferred_element_type=F32)
        bucket = bucket_ref[...]
        rid = lax.broadcasted_iota(jnp.int32, (HEAD_ROWS, 1), 0)
        bias = jnp.zeros((HEAD_ROWS, 2 * BLOCK), F32)
        sk = jnp.zeros((HEAD_ROWS, 1), F32)
        for hd in range(N_HEADS):
            bias = jnp.where(rid == hd, _bias_from_buckets(bucket, relb_ref, hd), bias)
            sk = jnp.where(rid == hd, sink_ref[hd], sk)
        bias_c[...] = bias
        sink_c[...] = sk

    r0 = pl.multiple_of(i * chunk, chunk)
    pr = pbuf[pl.ds(r0, chunk), :]
    q = pr[:, 0:ATTN_W] * SCALE
    knew = pr[:, O_K:O_V]
    vnew = pr[:, O_V:O_A]
    unew = pr[:, O_A:O_B] * _sigmoid(pr[:, O_B:])

    def per_head(t):
        n = t.shape[-1]
        return jnp.broadcast_to(t[:, None, :], (chunk, HEAD_ROWS, n)).reshape(rows_h, n)

    hid = lax.broadcasted_iota(jnp.int32, (rows_h, 1), 0) % HEAD_ROWS
    lane = lax.broadcasted_iota(jnp.int32, (1, BLOCK), 1)
    qrep = per_head(q)
    qsum = jnp.zeros((rows_h, BLOCK), F32)
    for c in range(ATTN_W // BLOCK):
        piece = qrep[:, c * BLOCK:(c + 1) * BLOCK]
        in_head = (lane // HEAD_DIM + 2 * c) == hid
        qsum = qsum + jnp.where(in_head, piece, 0.0)
    keep = (hid % 2) == (hid // GROUP)
    qf = jnp.where(keep, qsum, pltpu.roll(qsum, HEAD_DIM, axis=1))
    qf_b = qf.astype(BF16)

    bias = jnp.broadcast_to(bias_c[...][None], (chunk, HEAD_ROWS, 2 * BLOCK)).reshape(rows_h, 2 * BLOCK)
    sk = jnp.broadcast_to(sink_c[...][None], (chunk, HEAD_ROWS, 1)).reshape(rows_h, 1)

    s_rows = []
    for b in range(chunk):
        kt = ckt_ref[b].astype(BF16)
        s_rows.append(jnp.dot(qf_b[b * HEAD_ROWS:(b + 1) * HEAD_ROWS], kt, preferred_element_type=F32))
    s_c = jnp.concatenate(s_rows, axis=0) + bias[:, 0:BLOCK]
    s_n = jnp.sum(qf * per_head(knew), axis=-1, keepdims=True) + bias[:, BLOCK:BLOCK + 1]
    m = jnp.maximum(jnp.maximum(jnp.max(s_c, axis=-1, keepdims=True), s_n), sk)
    p_c = jnp.exp(s_c - m)
    p_n = jnp.exp(s_n - m)
    l = jnp.sum(p_c, axis=-1, keepdims=True) + p_n + jnp.exp(sk - m)
    p_cb = p_c.astype(BF16)
    o_rows = []
    for b in range(chunk):
        vt = cvt_ref[b].astype(BF16)
        o_rows.append(lax.dot_general(p_cb[b * HEAD_ROWS:(b + 1) * HEAD_ROWS], vt, NT_DIMS,
                                      preferred_element_type=F32))
    o = (jnp.concatenate(o_rows, axis=0) + p_n * per_head(vnew)) / l
    o = jnp.where(keep, o, pltpu.roll(o, HEAD_DIM, axis=1))
    o = jnp.where((lane // HEAD_DIM) == (hid % 2), o, 0.0)
    wide = jnp.concatenate([jnp.where(hid // 2 == c, o, 0.0) for c in range(ATTN_W // BLOCK)], axis=1)
    gi = lax.broadcasted_iota(jnp.int32, (chunk, rows_h), 0)
    gj = lax.broadcasted_iota(jnp.int32, (chunk, rows_h), 1)
    gather = jnp.where(gj // HEAD_ROWS == gi, 1.0, 0.0).astype(BF16)
    ao = jnp.dot(gather, wide.astype(BF16), preferred_element_type=F32)
    mix[pl.ds(r0, chunk), 0:ATTN_W] = ao

    pad = jnp.zeros((BLOCK - chunk, KV_W), F32)
    knew_t = jnp.concatenate([knew, pad], axis=0).T
    vnew_t = jnp.concatenate([vnew, pad], axis=0).T
    newest = lane == WINDOW - 1
    for b in range(chunk):
        kcol = jnp.broadcast_to(knew_t[:, b:b + 1], (KV_W, WINDOW))
        vcol = jnp.broadcast_to(vnew_t[:, b:b + 1], (KV_W, WINDOW))
        nkt_ref[b] = jnp.where(newest, kcol, pltpu.roll(ckt_ref[b], WINDOW - 1, axis=1))
        nvt_ref[b] = jnp.where(newest, vcol, pltpu.roll(cvt_ref[b], WINDOW - 1, axis=1))

    acc = cb_ref[...] + cw_ref[CONV_WIDTH - 1:CONV_WIDTH, :] * unew
    for w in range(CONV_WIDTH - 1):
        acc = acc + cw_ref[w:w + 1, :] * st_ref[w]
    nc_ref[0:CONV_WIDTH - 2] = st_ref[1:CONV_WIDTH - 1]
    nc_ref[CONV_WIDTH - 2] = unew
    mu = jnp.mean(acc, axis=-1, keepdims=True)
    xc = acc - mu
    y = xc * lax.rsqrt(jnp.mean(xc * xc, axis=-1, keepdims=True) + EPS)
    y = y * lng_ref[...] + lnb_ref[...]
    mix[pl.ds(r0, chunk), ATTN_W:] = y * _sigmoid(y)

    @pl.when(i == last)
    def _out():
        x2_ref[...] = x_ref[...] + jnp.dot(mix[...].astype(BF16), wout_ref[...],
                                           preferred_element_type=F32)


def _sample_mixer(x, ckt, cvt, st, rel_bias, sinks, g1, w_in, bucket, cw, cb, lng, lnb, w_out):
    nb = x.shape[0]
    chunk = SAMPLE_CHUNK
    body = functools.partial(_sample_mixer_body, chunk=chunk)
    cache_spec = pl.BlockSpec((chunk, KV_W, WINDOW), lambda i: (i, 0, 0))
    state_spec = pl.BlockSpec((CONV_WIDTH - 1, chunk, CONV_CH), lambda i: (0, i, 0))
    return pl.pallas_call(
        body,
        grid=(nb // chunk,),
        in_specs=[
            _smem_spec(), _smem_spec(),
            _const_spec((nb, D_MODEL)),
            cache_spec, cache_spec, state_spec,
            _const_spec((1, D_MODEL)),
            _const_spec((D_MODEL, IN_W)),
            _const_spec((1, 2 * BLOCK)),
            _const_spec((CONV_WIDTH, CONV_CH)),
            _const_spec((1, CONV_CH)),
            _const_spec((1, CONV_CH)),
            _const_spec((1, CONV_CH)),
            _const_spec((ATTN_W + CONV_CH, D_MODEL)),
        ],
        out_specs=[
            pl.BlockSpec((nb, D_MODEL), lambda i: (0, 0)),
            cache_spec, cache_spec, state_spec,
        ],
        out_shape=[
            jax.ShapeDtypeStruct((nb, D_MODEL), F32),
            jax.ShapeDtypeStruct((nb, KV_W, WINDOW), F32),
            jax.ShapeDtypeStruct((nb, KV_W, WINDOW), F32),
            jax.ShapeDtypeStruct((CONV_WIDTH - 1, nb, CONV_CH), F32),
        ],
        scratch_shapes=[
            pltpu.VMEM((nb, IN_W), F32),
            pltpu.VMEM((nb, ATTN_W + CONV_CH), F32),
            pltpu.VMEM((HEAD_ROWS, 2 * BLOCK), F32),
            pltpu.VMEM((HEAD_ROWS, 1), F32),
        ],
        compiler_params=pltpu.CompilerParams(
            dimension_semantics=("arbitrary",), vmem_limit_bytes=V7X_VMEM_LIMIT_BYTES),
        name="sample_mixer",
    )(rel_bias, sinks, x, ckt, cvt, st, g1, w_in, bucket, cw, cb, lng, lnb, w_out)


def kernel(x_prompt, x_sample, cache_k, cache_v, state_conv, meta_tokens, rel_bias, norm1_g, w_in,
           attn_sinks, conv_w, conv_b, conv_ln_g, conv_ln_b, w_out, norm2_g, w_up, w_down, norm_f_g):
    batch, seq, _ = x_prompt.shape
    nb, dec_seq, _ = x_sample.shape
    assert batch == 1 and dec_seq == 1 and w_in.shape[0] == 1
    assert seq % PROMPT_TILE == 0 and nb % SAMPLE_CHUNK == 0 and nb <= PROMPT_TILE
    assert len(SLOT_ATTN_UNITS) == len(SLOT_CONV_CHUNKS) == D_FF // UP_CHUNK + D_MODEL // DOWN_CHUNK
    assert sum(SLOT_ATTN_UNITS) == (PROMPT_TILE // BLOCK) * N_KV_HEADS
    assert sum(SLOT_CONV_CHUNKS) == PROMPT_TILE // CONV_ROWS

    w_in_b = w_in[0].astype(BF16)
    w_out_b = w_out[0].astype(BF16)
    w_up_b = w_up[0].astype(BF16)
    w_down_b = w_down[0].astype(BF16)
    g1 = norm1_g[0][None]
    g2 = norm2_g[0][None]
    gf = norm_f_g[None]
    cw, cb = conv_w[0], conv_b[0][None]
    lng, lnb = conv_ln_g[0][None], conv_ln_b[0][None]
    sinks = attn_sinks[0]

    dist_p = jnp.arange(BLOCK)[:, None] + BLOCK - jnp.arange(2 * BLOCK)[None, :]
    bucket_p = _t5_bucket(jnp.clip(dist_p, 0, WINDOW)).astype(jnp.int32)
    lane = jnp.arange(2 * BLOCK)
    dist_s = jnp.where(lane < WINDOW, WINDOW - lane, 0)
    bucket_s = jnp.where(lane <= WINDOW, _t5_bucket(jnp.clip(dist_s, 0, WINDOW)), -1)
    bucket_s = bucket_s.astype(jnp.int32)[None]

    ckt = jnp.transpose(cache_k[0], (0, 2, 3, 1)).reshape(nb, KV_W, WINDOW)
    cvt = jnp.transpose(cache_v[0], (0, 2, 3, 1)).reshape(nb, KV_W, WINDOW)
    st = jnp.transpose(state_conv[0], (1, 0, 2))
    x2_s, nkt_s, nvt_s, nct_s = _sample_mixer(x_sample[:, 0], ckt, cvt, st, rel_bias, sinks, g1,
                                              w_in_b, bucket_s, cw, cb, lng, lnb, w_out_b)
    nk_s = jnp.transpose(nkt_s.reshape(nb, N_KV_HEADS, HEAD_DIM, WINDOW), (0, 3, 1, 2))
    nv_s = jnp.transpose(nvt_s.reshape(nb, N_KV_HEADS, HEAD_DIM, WINDOW), (0, 3, 1, 2))
    nc_s = jnp.transpose(nct_s, (1, 0, 2))
    y_p, nkt_p, nvt_p, nc_p, y_s = _prompt_layer(x_prompt[0], meta_tokens, rel_bias, sinks, g1, w_in_b,
                                               bucket_p, cw, cb, lng, lnb, w_out_b, g2, w_up_b,
                                               w_down_b, gf, x2_s)

    def to_cache(t):
        return jnp.transpose(t.reshape(N_KV_HEADS, HEAD_DIM, WINDOW), (2, 0, 1))[None, None]

    return (y_p[None], y_s[:, None],
            to_cache(nkt_p), to_cache(nvt_p), nc_p[None, None],
            nk_s[None], nv_s[None], nc_s[None])
```

```python
import functools
import math

import jax
import jax.numpy as jnp
from jax import lax
from jax.experimental import pallas as pl
from jax.experimental.pallas import tpu as pltpu

D_MODEL = 1024
N_HEADS = 8
N_KV_HEADS = 2
HEAD_DIM = 64
GROUP = N_HEADS // N_KV_HEADS
ATTN_W = N_HEADS * HEAD_DIM
KV_W = N_KV_HEADS * HEAD_DIM
CONV_CH = D_MODEL - ATTN_W
IN_W = ATTN_W + 2 * KV_W + 2 * CONV_CH
CONV_WIDTH = 31
WINDOW = 128
BLOCK = 128
N_BUCKETS = 32
MAX_DISTANCE = WINDOW
N_META = 16
D_FF = 4 * D_MODEL
EPS = 1e-6
SCALE = HEAD_DIM ** -0.5

O_K = ATTN_W
O_V = ATTN_W + KV_W
O_A = ATTN_W + 2 * KV_W
O_B = O_A + CONV_CH

PAD = (-N_META) % BLOCK
U_CARRY = 32
U_SHIFT = U_CARRY - (CONV_WIDTH - 1)

V7X_VMEM_LIMIT_BYTES = 60 * 1024 * 1024

PROMPT_TILE = 512
UP_CHUNK = 1024
DOWN_CHUNK = 512
SLOT_ATTN_UNITS = (1, 1, 1, 1, 2, 2)
SLOT_CONV_CHUNKS = (1, 1, 1, 1, 2, 2)
CONV_ROWS = 64
WEIGHT_STAGE_ELEMS = 256 * 1024
SAMPLE_CHUNK = 16
HEAD_ROWS = 16

BF16 = jnp.bfloat16
F32 = jnp.float32
NT_DIMS = (((1,), (1,)), ((), ()))


def _t5_bucket(d):
    max_exact = N_BUCKETS // 2
    d_f = jnp.maximum(d, 1).astype(jnp.float32)
    large = max_exact + (jnp.log(d_f / max_exact) / math.log(MAX_DISTANCE / max_exact)
                         * (N_BUCKETS - max_exact)).astype(jnp.int32)
    large = jnp.minimum(large, N_BUCKETS - 1)
    return jnp.where(d < max_exact, d, large)


def _rms(x, g):
    y = x * lax.rsqrt(jnp.mean(x * x, axis=-1, keepdims=True) + EPS)
    return y * g


def _sigmoid(x):
    return 1.0 / (1.0 + jnp.exp(-x))


def _bias_from_buckets(bucket, relb_ref, h):
    b = jnp.zeros(bucket.shape, F32)
    for bk in range(N_BUCKETS):
        b = jnp.where(bucket == bk, relb_ref[bk, h], b)
    return b


def _conv_rows(ubuf, cw_ref, r0, rows):
    n = rows + U_CARRY
    strips = []
    for c0 in range(0, CONV_CH, BLOCK):
        win = ubuf[r0:r0 + n, c0:c0 + BLOCK]
        acc = None
        for s in range(8):
            sh = win if s == 0 else pltpu.roll(win, n - s, axis=0)
            for a0 in range(0, U_CARRY + 8, 8):
                w = a0 + s - U_SHIFT
                if 0 <= w < CONV_WIDTH:
                    term = cw_ref[w:w + 1, c0:c0 + BLOCK] * sh[a0:a0 + rows]
                    acc = term if acc is None else acc + term
        strips.append(acc)
    return jnp.concatenate(strips, axis=1)


def _ln_silu(acc, lng, lnb):
    mu = jnp.mean(acc, axis=-1, keepdims=True)
    xc = acc - mu
    y = xc * lax.rsqrt(jnp.mean(xc * xc, axis=-1, keepdims=True) + EPS)
    y = y * lng + lnb
    return y * _sigmoid(y)


def _zero_after(v, prev=None):
    u = pltpu.bitcast(v, jnp.uint32)
    t = prev
    for r0 in range(0, u.shape[0], 8):
        for c0 in range(0, u.shape[1], BLOCK):
            piece = u[r0:r0 + 8, c0:c0 + BLOCK]
            t = piece if t is None else t | piece
    return (t >> 16) >> 16


def _order_after(buf, zero):
    tile = pltpu.bitcast(buf[0:16, 0:BLOCK], jnp.uint32)
    buf[0:16, 0:BLOCK] = pltpu.bitcast(tile | zero, BF16)


def _stream_cast(src_hbm, stage, dst, sem):
    rows = stage.shape[1]
    n_chunks = src_hbm.shape[0] // rows

    def copy(j):
        return pltpu.make_async_copy(src_hbm.at[pl.ds(j * rows, rows), :], stage.at[j % 2], sem.at[j % 2])

    copy(0).start()
    for j in range(n_chunks):
        if j + 1 < n_chunks:
            copy(j + 1).start()
        copy(j).wait()
        dst[j * rows:(j + 1) * rows, :] = stage[j % 2].astype(BF16)


def _prompt_layer_body(*refs, tile):
    i = pl.program_id(0)
    n_tiles = pl.num_programs(0) - 1
    pl.when(i < n_tiles)(functools.partial(_prompt_step, *refs, tile=tile))
    pl.when(i == n_tiles)(functools.partial(_prompt_last_ffn, *refs, tile=tile))


def _prompt_last_ffn(relb_ref, sink_ref, x_ref, meta_ref, g1_ref, win_ref, bucket_ref,
                     cw_ref, cb_ref, lng_ref, lnb_ref, wout_ref, g2_ref, wup_hbm, wdn_hbm, gf_ref,
                     x2s_ref, y_ref, nk_ref, nv_ref, nc_ref, ys_ref,
                     kbuf, vbuf, ubuf, qbuf, mix, bias_s, x2buf, kvlast, hfbuf, hidbuf, ysbuf,
                     wup_ref, wdn_ref, stage_up, stage_dn, wsem, *, tile):
    xf = x2buf[1 - pl.program_id(0) % 2]
    hf = _rms(xf, g2_ref[...]).astype(BF16)
    for c0 in range(0, D_FF, UP_CHUNK):
        hid = jnp.dot(hf, wup_ref[:, c0:c0 + UP_CHUNK], preferred_element_type=F32)
        hidbuf[:, c0:c0 + UP_CHUNK] = jnp.square(jnp.maximum(hid, 0.0)).astype(BF16)
    cols = [xf[:, n0:n0 + DOWN_CHUNK]
            + jnp.dot(hidbuf[...], wdn_ref[:, n0:n0 + DOWN_CHUNK], preferred_element_type=F32)
            for n0 in range(0, D_MODEL, DOWN_CHUNK)]
    y_ref[...] = _rms(jnp.concatenate(cols, axis=1), gf_ref[...])


def _prompt_step(relb_ref, sink_ref, x_ref, meta_ref, g1_ref, win_ref, bucket_ref,
                 cw_ref, cb_ref, lng_ref, lnb_ref, wout_ref, g2_ref, wup_hbm, wdn_hbm, gf_ref,
                 x2s_ref, y_ref, nk_ref, nv_ref, nc_ref, ys_ref,
                 kbuf, vbuf, ubuf, qbuf, mix, bias_s, x2buf, kvlast, hfbuf, hidbuf, ysbuf,
                 wup_ref, wdn_ref, stage_up, stage_dn, wsem, *, tile):
    i = pl.program_id(0)
    n_tiles = pl.num_programs(0) - 1
    g1 = g1_ref[...]
    slot = i % 2

    @pl.when(i == 0)
    def _init():
        _stream_cast(wup_hbm, stage_up, wup_ref, wsem)
        _stream_cast(wdn_hbm, stage_dn, wdn_ref, wsem)
        nb = x2s_ref.shape[0]
        x2buf[1, 0:nb, :] = x2s_ref[...]
        x2buf[1, nb:tile, :] = jnp.zeros((tile - nb, D_MODEL), F32)
        bucket = bucket_ref[...]
        row = lax.broadcasted_iota(jnp.int32, (BLOCK, 2 * BLOCK), 0)
        col = lax.broadcasted_iota(jnp.int32, (BLOCK, 2 * BLOCK), 1)
        dist = row + BLOCK - col
        band = (dist >= 0) & (dist <= WINDOW)
        band_first = band & (col >= PAD)
        for h in range(N_HEADS):
            b = _bias_from_buckets(bucket, relb_ref, h)
            bias_s[0, h] = jnp.where(band, b, -jnp.inf)
            bias_s[1, h] = jnp.where(band_first, b, -jnp.inf)
        hm = _rms(meta_ref[...], g1).astype(BF16)
        pm = jnp.dot(hm, win_ref[:, O_K:], preferred_element_type=F32)
        kbuf[0:PAD, :] = jnp.zeros((PAD, KV_W), BF16)
        vbuf[0:PAD, :] = jnp.zeros((PAD, KV_W), BF16)
        kbuf[PAD:BLOCK, :] = pm[:, 0:KV_W].astype(BF16)
        vbuf[PAD:BLOCK, :] = pm[:, KV_W:2 * KV_W].astype(BF16)
        um = pm[:, 2 * KV_W:2 * KV_W + CONV_CH] * _sigmoid(pm[:, 2 * KV_W + CONV_CH:])
        ubuf[0:U_CARRY - N_META, :] = jnp.zeros((U_CARRY - N_META, CONV_CH), F32)
        ubuf[U_CARRY - N_META:U_CARRY, :] = um

    x = x_ref[...]
    h = _rms(x, g1).astype(BF16)
    q = jnp.dot(h, win_ref[:, 0:ATTN_W], preferred_element_type=F32) * SCALE
    qbuf[...] = q.astype(BF16)
    kv = jnp.dot(h, win_ref[:, O_K:O_A], preferred_element_type=F32)
    kbuf[BLOCK:BLOCK + tile, :] = kv[:, 0:KV_W].astype(BF16)
    vbuf[BLOCK:BLOCK + tile, :] = kv[:, KV_W:].astype(BF16)
    kvlast[...] = kv[tile - WINDOW:, :]

    a = jnp.dot(h, win_ref[:, O_A:O_B], preferred_element_type=F32)
    b = jnp.dot(h, win_ref[:, O_B:], preferred_element_type=F32)
    ubuf[U_CARRY:U_CARRY + tile, :] = a * _sigmoid(b)

    cb, lng, lnb = cb_ref[...], lng_ref[...], lnb_ref[...]

    def conv_chunk(r0):
        c = _ln_silu(_conv_rows(ubuf, cw_ref, r0, CONV_ROWS) + cb, lng, lnb)
        mix[r0:r0 + CONV_ROWS, ATTN_W:] = c.astype(BF16)

    def attn_scores(blk, kvh):
        r0 = blk * BLOCK
        c0 = kvh * HEAD_DIM
        qg = jnp.concatenate(
            [qbuf[r0:r0 + BLOCK, (kvh * GROUP + g) * HEAD_DIM:(kvh * GROUP + g + 1) * HEAD_DIM]
             for g in range(GROUP)], axis=0)
        kk = kbuf[r0:r0 + 2 * BLOCK, c0:c0 + HEAD_DIM]
        return lax.dot_general(qg, kk, NT_DIMS, preferred_element_type=F32)

    def attn_finish(blk, kvh, s):
        r0 = blk * BLOCK
        sel = jnp.where(i == 0, 1, 0) if blk == 0 else 0
        c0 = kvh * HEAD_DIM
        vv = vbuf[r0:r0 + 2 * BLOCK, c0:c0 + HEAD_DIM]
        ps, ls = [], []
        for g in range(GROUP):
            hd = kvh * GROUP + g
            sg = s[g * BLOCK:(g + 1) * BLOCK] + bias_s[sel, hd]
            sk = sink_ref[hd]
            m = jnp.maximum(jnp.max(sg, axis=-1, keepdims=True), sk)
            p = jnp.exp(sg - m)
            ls.append(jnp.sum(p, axis=-1, keepdims=True) + jnp.exp(sk - m))
            ps.append(p.astype(BF16))
        o = jnp.dot(jnp.concatenate(ps, axis=0), vv, preferred_element_type=F32)
        for g in range(GROUP):
            hd = kvh * GROUP + g
            og = o[g * BLOCK:(g + 1) * BLOCK] / ls[g]
            mix[r0:r0 + BLOCK, hd * HEAD_DIM:(hd + 1) * HEAD_DIM] = og.astype(BF16)

    conv_starts = list(range(0, tile, CONV_ROWS))
    attn_units = [(blk, kvh) for blk in range(tile // BLOCK) for kvh in range(N_KV_HEADS)]
    n_up, n_down = D_FF // UP_CHUNK, D_MODEL // DOWN_CHUNK
    n_slots = n_up + n_down
    unit_iter, conv_iter = iter(attn_units), iter(conv_starts)
    xf = x2buf[1 - slot]
    hfbuf[...] = _rms(xf, g2_ref[...]).astype(BF16)
    x3_cols = []
    for k in range(n_slots):
        units = [next(unit_iter) for _ in range(SLOT_ATTN_UNITS[k])]
        scores = [(blk, kvh, attn_scores(blk, kvh)) for blk, kvh in units]
        if k < n_up:
            c0 = k * UP_CHUNK
            hid = jnp.dot(hfbuf[...], wup_ref[:, c0:c0 + UP_CHUNK], preferred_element_type=F32)
            hidbuf[:, c0:c0 + UP_CHUNK] = jnp.square(jnp.maximum(hid, 0.0)).astype(BF16)
        else:
            n0 = (k - n_up) * DOWN_CHUNK
            x3_cols.append(xf[:, n0:n0 + DOWN_CHUNK]
                           + jnp.dot(hidbuf[...], wdn_ref[:, n0:n0 + DOWN_CHUNK],
                                     preferred_element_type=F32))
        zero = None
        for r0 in [next(conv_iter) for _ in range(SLOT_CONV_CHUNKS[k])]:
            conv_chunk(r0)
            zero = _zero_after(mix[r0:r0 + CONV_ROWS, ATTN_W:], zero)
        for blk, kvh, s in scores:
            attn_finish(blk, kvh, s)
        if k + 1 < n_slots and zero is not None:
            _order_after(hfbuf if k + 1 < n_up else hidbuf, zero)
    acc = jnp.concatenate(x3_cols, axis=1)
    y = _rms(acc, gf_ref[...])
    y_ref[...] = y
    ysbuf[...] = y[0:ysbuf.shape[0]]

    x2buf[slot] = x + jnp.dot(mix[...], wout_ref[...], preferred_element_type=F32)

    @pl.when(i == n_tiles - 1)
    def _new_caches():
        nk_ref[...] = kvlast[:, 0:KV_W].T
        nv_ref[...] = kvlast[:, KV_W:].T
        nc_ref[...] = ubuf[U_CARRY + tile - (CONV_WIDTH - 1):U_CARRY + tile, :]

    kbuf[0:BLOCK, :] = kbuf[tile:tile + BLOCK, :]
    vbuf[0:BLOCK, :] = vbuf[tile:tile + BLOCK, :]
    ubuf[0:U_CARRY, :] = ubuf[tile:tile + U_CARRY, :]

    @pl.when(i == 0)
    def _sample_out():
        ys_ref[...] = ysbuf[...]


def _const_spec(shape):
    return pl.BlockSpec(shape, lambda i: (0,) * len(shape), pipeline_mode=pl.Buffered(1))


def _smem_spec():
    return pl.BlockSpec(memory_space=pltpu.SMEM)


def _prompt_layer(x, meta, rel_bias, sinks, g1, w_in, bucket, cw, cb, lng, lnb, w_out,
                  g2, w_up, w_down, gf, x2_s):
    seq = x.shape[0]
    nb = x2_s.shape[0]
    tile = PROMPT_TILE
    n_tiles = seq // tile
    body = functools.partial(_prompt_layer_body, tile=tile)
    return pl.pallas_call(
        body,
        grid=(n_tiles + 1,),
        in_specs=[
            _smem_spec(), _smem_spec(),
            pl.BlockSpec((tile, D_MODEL), lambda i: (jnp.minimum(i, n_tiles - 1), 0)),
            _const_spec((N_META, D_MODEL)),
            _const_spec((1, D_MODEL)),
            _const_spec((D_MODEL, IN_W)),
            _const_spec((BLOCK, 2 * BLOCK)),
            _const_spec((CONV_WIDTH, CONV_CH)),
            _const_spec((1, CONV_CH)),
            _const_spec((1, CONV_CH)),
            _const_spec((1, CONV_CH)),
            _const_spec((ATTN_W + CONV_CH, D_MODEL)),
            _const_spec((1, D_MODEL)),
            pl.BlockSpec(memory_space=pl.ANY),
            pl.BlockSpec(memory_space=pl.ANY),
            _const_spec((1, D_MODEL)),
            _const_spec((nb, D_MODEL)),
        ],
        out_specs=[
            pl.BlockSpec((tile, D_MODEL), lambda i: (jnp.maximum(i - 1, 0), 0)),
            pl.BlockSpec((KV_W, WINDOW), lambda i: (0, 0)),
            pl.BlockSpec((KV_W, WINDOW), lambda i: (0, 0)),
            pl.BlockSpec((CONV_WIDTH - 1, CONV_CH), lambda i: (0, 0)),
            pl.BlockSpec((nb, D_MODEL), lambda i: (0, 0)),
        ],
        out_shape=[
            jax.ShapeDtypeStruct((seq, D_MODEL), F32),
            jax.ShapeDtypeStruct((KV_W, WINDOW), F32),
            jax.ShapeDtypeStruct((KV_W, WINDOW), F32),
            jax.ShapeDtypeStruct((CONV_WIDTH - 1, CONV_CH), F32),
            jax.ShapeDtypeStruct((nb, D_MODEL), F32),
        ],
        scratch_shapes=[
            pltpu.VMEM((BLOCK + tile, KV_W), BF16),
            pltpu.VMEM((BLOCK + tile, KV_W), BF16),
            pltpu.VMEM((U_CARRY + tile, CONV_CH), F32),
            pltpu.VMEM((tile, ATTN_W), BF16),
            pltpu.VMEM((tile, ATTN_W + CONV_CH), BF16),
            pltpu.VMEM((2, N_HEADS, BLOCK, 2 * BLOCK), F32),
            pltpu.VMEM((2, tile, D_MODEL), F32),
            pltpu.VMEM((WINDOW, 2 * KV_W), F32),
            pltpu.VMEM((tile, D_MODEL), BF16),
            pltpu.VMEM((tile, D_FF), BF16),
            pltpu.VMEM((nb, D_MODEL), F32),
            pltpu.VMEM((D_MODEL, D_FF), BF16),
            pltpu.VMEM((D_FF, D_MODEL), BF16),
            pltpu.VMEM((2, WEIGHT_STAGE_ELEMS // D_FF, D_FF), F32),
            pltpu.VMEM((2, WEIGHT_STAGE_ELEMS // D_MODEL, D_MODEL), F32),
            pltpu.SemaphoreType.DMA((2,)),
        ],
        compiler_params=pltpu.CompilerParams(
            dimension_semantics=("arbitrary",), vmem_limit_bytes=V7X_VMEM_LIMIT_BYTES),
        name="prompt_layer",
    )(rel_bias, sinks, x, meta, g1, w_in, bucket, cw, cb, lng, lnb, w_out, g2, w_up, w_down, gf,
      x2_s)


def _sample_mixer_body(relb_ref, sink_ref, x_ref, ckt_ref, cvt_ref, st_ref, g1_ref, win_ref,
                       bucket_ref, cw_ref, cb_ref, lng_ref, lnb_ref, wout_ref,
                       x2_ref, nkt_ref, nvt_ref, nc_ref,
                       pbuf, mix, bias_c, sink_c, *, chunk):
    i = pl.program_id(0)
    last = pl.num_programs(0) - 1
    rows_h = chunk * HEAD_ROWS

    @pl.when(i == 0)
    def _init():
        h = _rms(x_ref[...], g1_ref[...]).astype(BF16)
        pbuf[...] = jnp.dot(h, win_ref[...], preferred_element_type=F32)
        bucket = bucket_ref[...]
        rid = lax.broadcasted_iota(jnp.int32, (HEAD_ROWS, 1), 0)
        bias = jnp.zeros((HEAD_ROWS, 2 * BLOCK), F32)
        sk = jnp.zeros((HEAD_ROWS, 1), F32)
        for hd in range(N_HEADS):
            bias = jnp.where(rid == hd, _bias_from_buckets(bucket, relb_ref, hd), bias)
            sk = jnp.where(rid == hd, sink_ref[hd], sk)
        bias_c[...] = bias
        sink_c[...] = sk

    r0 = pl.multiple_of(i * chunk, chunk)
    pr = pbuf[pl.ds(r0, chunk), :]
    q = pr[:, 0:ATTN_W] * SCALE
    knew = pr[:, O_K:O_V]
    vnew = pr[:, O_V:O_A]
    unew = pr[:, O_A:O_B] * _sigmoid(pr[:, O_B:])

    def per_head(t):
        n = t.shape[-1]
        return jnp.broadcast_to(t[:, None, :], (chunk, HEAD_ROWS, n)).reshape(rows_h, n)

    hid = lax.broadcasted_iota(jnp.int32, (rows_h, 1), 0) % HEAD_ROWS
    lane = lax.broadcasted_iota(jnp.int32, (1, BLOCK), 1)
    qrep = per_head(q)
    qsum = jnp.zeros((rows_h, BLOCK), F32)
    for c in range(ATTN_W // BLOCK):
        piece = qrep[:, c * BLOCK:(c + 1) * BLOCK]
        in_head = (lane // HEAD_DIM + 2 * c) == hid
        qsum = qsum + jnp.where(in_head, piece, 0.0)
    keep = (hid % 2) == (hid // GROUP)
    qf = jnp.where(keep, qsum, pltpu.roll(qsum, HEAD_DIM, axis=1))
    qf_b = qf.astype(BF16)

    bias = jnp.broadcast_to(bias_c[...][None], (chunk, HEAD_ROWS, 2 * BLOCK)).reshape(rows_h, 2 * BLOCK)
    sk = jnp.broadcast_to(sink_c[...][None], (chunk, HEAD_ROWS, 1)).reshape(rows_h, 1)

    s_rows = []
    for b in range(chunk):
        kt = ckt_ref[b].astype(BF16)
        s_rows.append(jnp.dot(qf_b[b * HEAD_ROWS:(b + 1) * HEAD_ROWS], kt, preferred_element_type=F32))
    s_c = jnp.concatenate(s_rows, axis=0) + bias[:, 0:BLOCK]
    s_n = jnp.sum(qf * per_head(knew), axis=-1, keepdims=True) + bias[:, BLOCK:BLOCK + 1]
    m = jnp.maximum(jnp.maximum(jnp.max(s_c, axis=-1, keepdims=True), s_n), sk)
    p_c = jnp.exp(s_c - m)
    p_n = jnp.exp(s_n - m)
    l = jnp.sum(p_c, axis=-1, keepdims=True) + p_n + jnp.exp(sk - m)
    p_cb = p_c.astype(BF16)
    o_rows = []
    for b in range(chunk):
        vt = cvt_ref[b].astype(BF16)
        o_rows.append(lax.dot_general(p_cb[b * HEAD_ROWS:(b + 1) * HEAD_ROWS], vt, NT_DIMS,
                                      preferred_element_type=F32))
    o = (jnp.concatenate(o_rows, axis=0) + p_n * per_head(vnew)) / l
    o = jnp.where(keep, o, pltpu.roll(o, HEAD_DIM, axis=1))
    o = jnp.where((lane // HEAD_DIM) == (hid % 2), o, 0.0)
    wide = jnp.concatenate([jnp.where(hid // 2 == c, o, 0.0) for c in range(ATTN_W // BLOCK)], axis=1)
    gi = lax.broadcasted_iota(jnp.int32, (chunk, rows_h), 0)
    gj = lax.broadcasted_iota(jnp.int32, (chunk, rows_h), 1)
    gather = jnp.where(gj // HEAD_ROWS == gi, 1.0, 0.0).astype(BF16)
    ao = jnp.dot(gather, wide.astype(BF16), preferred_element_type=F32)
    mix[pl.ds(r0, chunk), 0:ATTN_W] = ao

    pad = jnp.zeros((BLOCK - chunk, KV_W), F32)
    knew_t = jnp.concatenate([knew, pad], axis=0).T
    vnew_t = jnp.concatenate([vnew, pad], axis=0).T
    newest = lane == WINDOW - 1
    for b in range(chunk):
        kcol = jnp.broadcast_to(knew_t[:, b:b + 1], (KV_W, WINDOW))
        vcol = jnp.broadcast_to(vnew_t[:, b:b + 1], (KV_W, WINDOW))
        nkt_ref[b] = jnp.where(newest, kcol, pltpu.roll(ckt_ref[b], WINDOW - 1, axis=1))
        nvt_ref[b] = jnp.where(newest, vcol, pltpu.roll(cvt_ref[b], WINDOW - 1, axis=1))

    acc = cb_ref[...] + cw_ref[CONV_WIDTH - 1:CONV_WIDTH, :] * unew
    for w in range(CONV_WIDTH - 1):
        acc = acc + cw_ref[w:w + 1, :] * st_ref[w]
    nc_ref[0:CONV_WIDTH - 2] = st_ref[1:CONV_WIDTH - 1]
    nc_ref[CONV_WIDTH - 2] = unew
    mu = jnp.mean(acc, axis=-1, keepdims=True)
    xc = acc - mu
    y = xc * lax.rsqrt(jnp.mean(xc * xc, axis=-1, keepdims=True) + EPS)
    y = y * lng_ref[...] + lnb_ref[...]
    mix[pl.ds(r0, chunk), ATTN_W:] = y * _sigmoid(y)

    @pl.when(i == last)
    def _out():
        x2_ref[...] = x_ref[...] + jnp.dot(mix[...].astype(BF16), wout_ref[...],
                                           preferred_element_type=F32)


def _sample_mixer(x, ckt, cvt, st, rel_bias, sinks, g1, w_in, bucket, cw, cb, lng, lnb, w_out):
    nb = x.shape[0]
    chunk = SAMPLE_CHUNK
    body = functools.partial(_sample_mixer_body, chunk=chunk)
    cache_spec = pl.BlockSpec((chunk, KV_W, WINDOW), lambda i: (i, 0, 0))
    state_spec = pl.BlockSpec((CONV_WIDTH - 1, chunk, CONV_CH), lambda i: (0, i, 0))
    return pl.pallas_call(
        body,
        grid=(nb // chunk,),
        in_specs=[
            _smem_spec(), _smem_spec(),
            _const_spec((nb, D_MODEL)),
            cache_spec, cache_spec, state_spec,
            _const_spec((1, D_MODEL)),
            _const_spec((D_MODEL, IN_W)),
            _const_spec((1, 2 * BLOCK)),
            _const_spec((CONV_WIDTH, CONV_CH)),
            _const_spec((1, CONV_CH)),
            _const_spec((1, CONV_CH)),
            _const_spec((1, CONV_CH)),
            _const_spec((ATTN_W + CONV_CH, D_MODEL)),
        ],
        out_specs=[
            pl.BlockSpec((nb, D_MODEL), lambda i: (0, 0)),
            cache_spec, cache_spec, state_spec,
        ],
        out_shape=[
            jax.ShapeDtypeStruct((nb, D_MODEL), F32),
            jax.ShapeDtypeStruct((nb, KV_W, WINDOW), F32),
            jax.ShapeDtypeStruct((nb, KV_W, WINDOW), F32),
            jax.ShapeDtypeStruct((CONV_WIDTH - 1, nb, CONV_CH), F32),
        ],
        scratch_shapes=[
            pltpu.VMEM((nb, IN_W), F32),
            pltpu.VMEM((nb, ATTN_W + CONV_CH), F32),
            pltpu.VMEM((HEAD_ROWS, 2 * BLOCK), F32),
            pltpu.VMEM((HEAD_ROWS, 1), F32),
        ],
        compiler_params=pltpu.CompilerParams(
            dimension_semantics=("arbitrary",), vmem_limit_bytes=V7X_VMEM_LIMIT_BYTES),
        name="sample_mixer",
    )(rel_bias, sinks, x, ckt, cvt, st, g1, w_in, bucket, cw, cb, lng, lnb, w_out)


def kernel(x_prompt, x_sample, cache_k, cache_v, state_conv, meta_tokens, rel_bias, norm1_g, w_in,
           attn_sinks, conv_w, conv_b, conv_ln_g, conv_ln_b, w_out, norm2_g, w_up, w_down, norm_f_g):
    batch, seq, _ = x_prompt.shape
    nb, dec_seq, _ = x_sample.shape
    assert batch == 1 and dec_seq == 1 and w_in.shape[0] == 1
    assert seq % PROMPT_TILE == 0 and nb % SAMPLE_CHUNK == 0 and nb <= PROMPT_TILE
    assert len(SLOT_ATTN_UNITS) == len(SLOT_CONV_CHUNKS) == D_FF // UP_CHUNK + D_MODEL // DOWN_CHUNK
    assert sum(SLOT_ATTN_UNITS) == (PROMPT_TILE // BLOCK) * N_KV_HEADS
    assert sum(SLOT_CONV_CHUNKS) == PROMPT_TILE // CONV_ROWS

    w_in_b = w_in[0].astype(BF16)
    w_out_b = w_out[0].astype(BF16)
    g1 = norm1_g[0][None]
    g2 = norm2_g[0][None]
    gf = norm_f_g[None]
    cw, cb = conv_w[0], conv_b[0][None]
    lng, lnb = conv_ln_g[0][None], conv_ln_b[0][None]
    sinks = attn_sinks[0]

    dist_p = jnp.arange(BLOCK)[:, None] + BLOCK - jnp.arange(2 * BLOCK)[None, :]
    bucket_p = _t5_bucket(jnp.clip(dist_p, 0, WINDOW)).astype(jnp.int32)
    lane = jnp.arange(2 * BLOCK)
    dist_s = jnp.where(lane < WINDOW, WINDOW - lane, 0)
    bucket_s = jnp.where(lane <= WINDOW, _t5_bucket(jnp.clip(dist_s, 0, WINDOW)), -1)
    bucket_s = bucket_s.astype(jnp.int32)[None]

    ckt = jnp.transpose(cache_k[0], (0, 2, 3, 1)).reshape(nb, KV_W, WINDOW)
    cvt = jnp.transpose(cache_v[0], (0, 2, 3, 1)).reshape(nb, KV_W, WINDOW)
    st = jnp.transpose(state_conv[0], (1, 0, 2))
    x2_s, nkt_s, nvt_s, nct_s = _sample_mixer(x_sample[:, 0], ckt, cvt, st, rel_bias, sinks, g1,
                                              w_in_b, bucket_s, cw, cb, lng, lnb, w_out_b)
    nk_s = jnp.transpose(nkt_s.reshape(nb, N_KV_HEADS, HEAD_DIM, WINDOW), (0, 3, 1, 2))
    nv_s = jnp.transpose(nvt_s.reshape(nb, N_KV_HEADS, HEAD_DIM, WINDOW), (0, 3, 1, 2))
    nc_s = jnp.transpose(nct_s, (1, 0, 2))
    y_p, nkt_p, nvt_p, nc_p, y_s = _prompt_layer(x_prompt[0], meta_tokens, rel_bias, sinks, g1, w_in_b,
                                               bucket_p, cw, cb, lng, lnb, w_out_b, g2, w_up[0],
                                               w_down[0], gf, x2_s)

    def to_cache(t):
        return jnp.transpose(t.reshape(N_KV_HEADS, HEAD_DIM, WINDOW), (2, 0, 1))[None, None]

    return (y_p[None], y_s[:, None],
            to_cache(nkt_p), to_cache(nvt_p), nc_p[None, None],
            nk_s[None], nv_s[None], nc_s[None])
```

```python
import functools
import math

import jax
import jax.numpy as jnp
from jax import lax
from jax.experimental import pallas as pl
from jax.experimental.pallas import tpu as pltpu

D_MODEL = 1024
N_HEADS = 8
N_KV_HEADS = 2
HEAD_DIM = 64
GROUP = N_HEADS // N_KV_HEADS
ATTN_W = N_HEADS * HEAD_DIM
KV_W = N_KV_HEADS * HEAD_DIM
CONV_CH = D_MODEL - ATTN_W
IN_W = ATTN_W + 2 * KV_W + 2 * CONV_CH
CONV_WIDTH = 31
WINDOW = 128
BLOCK = 128
N_BUCKETS = 32
MAX_DISTANCE = WINDOW
N_META = 16
D_FF = 4 * D_MODEL
EPS = 1e-6
SCALE = HEAD_DIM ** -0.5

O_K = ATTN_W
O_V = ATTN_W + KV_W
O_A = ATTN_W + 2 * KV_W
O_B = O_A + CONV_CH

PAD = (-N_META) % BLOCK
U_CARRY = 32
U_SHIFT = U_CARRY - (CONV_WIDTH - 1)

V7X_VMEM_LIMIT_BYTES = 60 * 1024 * 1024

PROMPT_TILE = 512
UP_CHUNK = 1024
DOWN_CHUNK = 512
SLOT_ATTN_UNITS = (1, 1, 1, 1, 2, 2)
SLOT_CONV_CHUNKS = (1, 1, 1, 1, 2, 2)
CONV_ROWS = 64
WEIGHT_STAGE_BLOCK = (256, 1024)
WEIGHT_STAGE_DEPTH = 4
SAMPLE_CHUNK = 16
HEAD_ROWS = 16

BF16 = jnp.bfloat16
F32 = jnp.float32
NT_DIMS = (((1,), (1,)), ((), ()))


def _t5_bucket(d):
    max_exact = N_BUCKETS // 2
    d_f = jnp.maximum(d, 1).astype(jnp.float32)
    large = max_exact + (jnp.log(d_f / max_exact) / math.log(MAX_DISTANCE / max_exact)
                         * (N_BUCKETS - max_exact)).astype(jnp.int32)
    large = jnp.minimum(large, N_BUCKETS - 1)
    return jnp.where(d < max_exact, d, large)


def _rms(x, g):
    y = x * lax.rsqrt(jnp.mean(x * x, axis=-1, keepdims=True) + EPS)
    return y * g


def _sigmoid(x):
    return 1.0 / (1.0 + jnp.exp(-x))


def _bias_from_buckets(bucket, relb_ref, h):
    b = jnp.zeros(bucket.shape, F32)
    for bk in range(N_BUCKETS):
        b = jnp.where(bucket == bk, relb_ref[bk, h], b)
    return b


def _conv_rows(ubuf, cw_ref, r0, rows):
    n = rows + U_CARRY
    strips = []
    for c0 in range(0, CONV_CH, BLOCK):
        win = ubuf[r0:r0 + n, c0:c0 + BLOCK]
        acc = None
        for s in range(8):
            sh = win if s == 0 else pltpu.roll(win, n - s, axis=0)
            for a0 in range(0, U_CARRY + 8, 8):
                w = a0 + s - U_SHIFT
                if 0 <= w < CONV_WIDTH:
                    term = cw_ref[w:w + 1, c0:c0 + BLOCK] * sh[a0:a0 + rows]
                    acc = term if acc is None else acc + term
        strips.append(acc)
    return jnp.concatenate(strips, axis=1)


def _ln_silu(acc, lng, lnb):
    mu = jnp.mean(acc, axis=-1, keepdims=True)
    xc = acc - mu
    y = xc * lax.rsqrt(jnp.mean(xc * xc, axis=-1, keepdims=True) + EPS)
    y = y * lng + lnb
    return y * _sigmoid(y)


def _zero_after(v, prev=None):
    u = pltpu.bitcast(v, jnp.uint32)
    t = prev
    for r0 in range(0, u.shape[0], 8):
        for c0 in range(0, u.shape[1], BLOCK):
            piece = u[r0:r0 + 8, c0:c0 + BLOCK]
            t = piece if t is None else t | piece
    return (t >> 16) >> 16


def _order_after(buf, zero):
    tile = pltpu.bitcast(buf[0:16, 0:BLOCK], jnp.uint32)
    buf[0:16, 0:BLOCK] = pltpu.bitcast(tile | zero, BF16)


def _stream_cast(pairs, stage, sem):
    depth, rows, cols = stage.shape
    blocks = [(src, dst, r, c) for src, dst in pairs
              for r in range(0, src.shape[0], rows) for c in range(0, src.shape[1], cols)]

    def copy(j):
        src, _, r, c = blocks[j]
        return pltpu.make_async_copy(src.at[pl.ds(r, rows), pl.ds(c, cols)],
                                     stage.at[j % depth], sem.at[j % depth])

    for j in range(min(depth - 1, len(blocks))):
        copy(j).start()
    for j, (_, dst, r, c) in enumerate(blocks):
        if j + depth - 1 < len(blocks):
            copy(j + depth - 1).start()
        copy(j).wait()
        dst[r:r + rows, c:c + cols] = stage[j % depth].astype(BF16)


def _prompt_layer_body(*refs, tile):
    i = pl.program_id(0)
    n_tiles = pl.num_programs(0) - 1
    pl.when(i < n_tiles)(functools.partial(_prompt_step, *refs, tile=tile))
    pl.when(i == n_tiles)(functools.partial(_prompt_last_ffn, *refs, tile=tile))


def _prompt_last_ffn(relb_ref, sink_ref, x_ref, meta_ref, g1_ref, win_ref, bucket_ref,
                     cw_ref, cb_ref, lng_ref, lnb_ref, wout_ref, g2_ref, wup_hbm, wdn_hbm, gf_ref,
                     x2s_ref, y_ref, nk_ref, nv_ref, nc_ref, ys_ref,
                     kbuf, vbuf, ubuf, qbuf, mix, bias_s, x2buf, kvlast, hfbuf, hidbuf, ysbuf,
                     wup_ref, wdn_ref, stage, wsem, *, tile):
    xf = x2buf[1 - pl.program_id(0) % 2]
    hf = _rms(xf, g2_ref[...]).astype(BF16)
    for c0 in range(0, D_FF, UP_CHUNK):
        hid = jnp.dot(hf, wup_ref[:, c0:c0 + UP_CHUNK], preferred_element_type=F32)
        hidbuf[:, c0:c0 + UP_CHUNK] = jnp.square(jnp.maximum(hid, 0.0)).astype(BF16)
    cols = [xf[:, n0:n0 + DOWN_CHUNK]
            + jnp.dot(hidbuf[...], wdn_ref[:, n0:n0 + DOWN_CHUNK], preferred_element_type=F32)
            for n0 in range(0, D_MODEL, DOWN_CHUNK)]
    y_ref[...] = _rms(jnp.concatenate(cols, axis=1), gf_ref[...])


def _prompt_step(relb_ref, sink_ref, x_ref, meta_ref, g1_ref, win_ref, bucket_ref,
                 cw_ref, cb_ref, lng_ref, lnb_ref, wout_ref, g2_ref, wup_hbm, wdn_hbm, gf_ref,
                 x2s_ref, y_ref, nk_ref, nv_ref, nc_ref, ys_ref,
                 kbuf, vbuf, ubuf, qbuf, mix, bias_s, x2buf, kvlast, hfbuf, hidbuf, ysbuf,
                 wup_ref, wdn_ref, stage, wsem, *, tile):
    i = pl.program_id(0)
    n_tiles = pl.num_programs(0) - 1
    g1 = g1_ref[...]
    slot = i % 2

    @pl.when(i == 0)
    def _init():
        _stream_cast([(wup_hbm, wup_ref), (wdn_hbm, wdn_ref)], stage, wsem)
        nb = x2s_ref.shape[0]
        x2buf[1, 0:nb, :] = x2s_ref[...]
        x2buf[1, nb:tile, :] = jnp.zeros((tile - nb, D_MODEL), F32)
        bucket = bucket_ref[...]
        row = lax.broadcasted_iota(jnp.int32, (BLOCK, 2 * BLOCK), 0)
        col = lax.broadcasted_iota(jnp.int32, (BLOCK, 2 * BLOCK), 1)
        dist = row + BLOCK - col
        band = (dist >= 0) & (dist <= WINDOW)
        band_first = band & (col >= PAD)
        for h in range(N_HEADS):
            b = _bias_from_buckets(bucket, relb_ref, h)
            bias_s[0, h] = jnp.where(band, b, -jnp.inf)
            bias_s[1, h] = jnp.where(band_first, b, -jnp.inf)
        hm = _rms(meta_ref[...], g1).astype(BF16)
        pm = jnp.dot(hm, win_ref[:, O_K:], preferred_element_type=F32)
        kbuf[0:PAD, :] = jnp.zeros((PAD, KV_W), BF16)
        vbuf[0:PAD, :] = jnp.zeros((PAD, KV_W), BF16)
        kbuf[PAD:BLOCK, :] = pm[:, 0:KV_W].astype(BF16)
        vbuf[PAD:BLOCK, :] = pm[:, KV_W:2 * KV_W].astype(BF16)
        um = pm[:, 2 * KV_W:2 * KV_W + CONV_CH] * _sigmoid(pm[:, 2 * KV_W + CONV_CH:])
        ubuf[0:U_CARRY - N_META, :] = jnp.zeros((U_CARRY - N_META, CONV_CH), F32)
        ubuf[U_CARRY - N_META:U_CARRY, :] = um

    x = x_ref[...]
    h = _rms(x, g1).astype(BF16)
    q = jnp.dot(h, win_ref[:, 0:ATTN_W], preferred_element_type=F32) * SCALE
    qbuf[...] = q.astype(BF16)
    kv = jnp.dot(h, win_ref[:, O_K:O_A], preferred_element_type=F32)
    kbuf[BLOCK:BLOCK + tile, :] = kv[:, 0:KV_W].astype(BF16)
    vbuf[BLOCK:BLOCK + tile, :] = kv[:, KV_W:].astype(BF16)
    kvlast[...] = kv[tile - WINDOW:, :]

    a = jnp.dot(h, win_ref[:, O_A:O_B], preferred_element_type=F32)
    b = jnp.dot(h, win_ref[:, O_B:], preferred_element_type=F32)
    ubuf[U_CARRY:U_CARRY + tile, :] = a * _sigmoid(b)

    cb, lng, lnb = cb_ref[...], lng_ref[...], lnb_ref[...]

    def conv_chunk(r0):
        c = _ln_silu(_conv_rows(ubuf, cw_ref, r0, CONV_ROWS) + cb, lng, lnb)
        mix[r0:r0 + CONV_ROWS, ATTN_W:] = c.astype(BF16)

    def attn_scores(blk, kvh):
        r0 = blk * BLOCK
        c0 = kvh * HEAD_DIM
        qg = jnp.concatenate(
            [qbuf[r0:r0 + BLOCK, (kvh * GROUP + g) * HEAD_DIM:(kvh * GROUP + g + 1) * HEAD_DIM]
             for g in range(GROUP)], axis=0)
        kk = kbuf[r0:r0 + 2 * BLOCK, c0:c0 + HEAD_DIM]
        return lax.dot_general(qg, kk, NT_DIMS, preferred_element_type=F32)

    def attn_finish(blk, kvh, s):
        r0 = blk * BLOCK
        sel = jnp.where(i == 0, 1, 0) if blk == 0 else 0
        c0 = kvh * HEAD_DIM
        vv = vbuf[r0:r0 + 2 * BLOCK, c0:c0 + HEAD_DIM]
        ps, ls = [], []
        for g in range(GROUP):
            hd = kvh * GROUP + g
            sg = s[g * BLOCK:(g + 1) * BLOCK] + bias_s[sel, hd]
            sk = sink_ref[hd]
            m = jnp.maximum(jnp.max(sg, axis=-1, keepdims=True), sk)
            p = jnp.exp(sg - m)
            ls.append(jnp.sum(p, axis=-1, keepdims=True) + jnp.exp(sk - m))
            ps.append(p.astype(BF16))
        o = jnp.dot(jnp.concatenate(ps, axis=0), vv, preferred_element_type=F32)
        for g in range(GROUP):
            hd = kvh * GROUP + g
            og = o[g * BLOCK:(g + 1) * BLOCK] / ls[g]
            mix[r0:r0 + BLOCK, hd * HEAD_DIM:(hd + 1) * HEAD_DIM] = og.astype(BF16)

    conv_starts = list(range(0, tile, CONV_ROWS))
    attn_units = [(blk, kvh) for blk in range(tile // BLOCK) for kvh in range(N_KV_HEADS)]
    n_up, n_down = D_FF // UP_CHUNK, D_MODEL // DOWN_CHUNK
    n_slots = n_up + n_down
    unit_iter, conv_iter = iter(attn_units), iter(conv_starts)
    xf = x2buf[1 - slot]
    hfbuf[...] = _rms(xf, g2_ref[...]).astype(BF16)
    x3_cols = []
    for k in range(n_slots):
        units = [next(unit_iter) for _ in range(SLOT_ATTN_UNITS[k])]
        scores = [(blk, kvh, attn_scores(blk, kvh)) for blk, kvh in units]
        if k < n_up:
            c0 = k * UP_CHUNK
            hid = jnp.dot(hfbuf[...], wup_ref[:, c0:c0 + UP_CHUNK], preferred_element_type=F32)
            hidbuf[:, c0:c0 + UP_CHUNK] = jnp.square(jnp.maximum(hid, 0.0)).astype(BF16)
        else:
            n0 = (k - n_up) * DOWN_CHUNK
            x3_cols.append(xf[:, n0:n0 + DOWN_CHUNK]
                           + jnp.dot(hidbuf[...], wdn_ref[:, n0:n0 + DOWN_CHUNK],
                                     preferred_element_type=F32))
        zero = None
        for r0 in [next(conv_iter) for _ in range(SLOT_CONV_CHUNKS[k])]:
            conv_chunk(r0)
            zero = _zero_after(mix[r0:r0 + CONV_ROWS, ATTN_W:], zero)
        for blk, kvh, s in scores:
            attn_finish(blk, kvh, s)
        if k + 1 < n_slots and zero is not None:
            _order_after(hfbuf if k + 1 < n_up else hidbuf, zero)
    acc = jnp.concatenate(x3_cols, axis=1)
    y = _rms(acc, gf_ref[...])
    y_ref[...] = y
    ysbuf[...] = y[0:ysbuf.shape[0]]

    x2buf[slot] = x + jnp.dot(mix[...], wout_ref[...], preferred_element_type=F32)

    @pl.when(i == n_tiles - 1)
    def _new_caches():
        nk_ref[...] = kvlast[:, 0:KV_W].T
        nv_ref[...] = kvlast[:, KV_W:].T
        nc_ref[...] = ubuf[U_CARRY + tile - (CONV_WIDTH - 1):U_CARRY + tile, :]

    kbuf[0:BLOCK, :] = kbuf[tile:tile + BLOCK, :]
    vbuf[0:BLOCK, :] = vbuf[tile:tile + BLOCK, :]
    ubuf[0:U_CARRY, :] = ubuf[tile:tile + U_CARRY, :]

    @pl.when(i == 0)
    def _sample_out():
        ys_ref[...] = ysbuf[...]


def _const_spec(shape):
    return pl.BlockSpec(shape, lambda i: (0,) * len(shape), pipeline_mode=pl.Buffered(1))


def _smem_spec():
    return pl.BlockSpec(memory_space=pltpu.SMEM)


def _prompt_layer(x, meta, rel_bias, sinks, g1, w_in, bucket, cw, cb, lng, lnb, w_out,
                  g2, w_up, w_down, gf, x2_s):
    seq = x.shape[0]
    nb = x2_s.shape[0]
    tile = PROMPT_TILE
    n_tiles = seq // tile
    body = functools.partial(_prompt_layer_body, tile=tile)
    return pl.pallas_call(
        body,
        grid=(n_tiles + 1,),
        in_specs=[
            _smem_spec(), _smem_spec(),
            pl.BlockSpec((tile, D_MODEL), lambda i: (jnp.minimum(i, n_tiles - 1), 0)),
            _const_spec((N_META, D_MODEL)),
            _const_spec((1, D_MODEL)),
            _const_spec((D_MODEL, IN_W)),
            _const_spec((BLOCK, 2 * BLOCK)),
            _const_spec((CONV_WIDTH, CONV_CH)),
            _const_spec((1, CONV_CH)),
            _const_spec((1, CONV_CH)),
            _const_spec((1, CONV_CH)),
            _const_spec((ATTN_W + CONV_CH, D_MODEL)),
            _const_spec((1, D_MODEL)),
            pl.BlockSpec(memory_space=pl.ANY),
            pl.BlockSpec(memory_space=pl.ANY),
            _const_spec((1, D_MODEL)),
            _const_spec((nb, D_MODEL)),
        ],
        out_specs=[
            pl.BlockSpec((tile, D_MODEL), lambda i: (jnp.maximum(i - 1, 0), 0)),
            pl.BlockSpec((KV_W, WINDOW), lambda i: (0, 0)),
            pl.BlockSpec((KV_W, WINDOW), lambda i: (0, 0)),
            pl.BlockSpec((CONV_WIDTH - 1, CONV_CH), lambda i: (0, 0)),
            pl.BlockSpec((nb, D_MODEL), lambda i: (0, 0)),
        ],
        out_shape=[
            jax.ShapeDtypeStruct((seq, D_MODEL), F32),
            jax.ShapeDtypeStruct((KV_W, WINDOW), F32),
            jax.ShapeDtypeStruct((KV_W, WINDOW), F32),
            jax.ShapeDtypeStruct((CONV_WIDTH - 1, CONV_CH), F32),
            jax.ShapeDtypeStruct((nb, D_MODEL), F32),
        ],
        scratch_shapes=[
            pltpu.VMEM((BLOCK + tile, KV_W), BF16),
            pltpu.VMEM((BLOCK + tile, KV_W), BF16),
            pltpu.VMEM((U_CARRY + tile, CONV_CH), F32),
            pltpu.VMEM((tile, ATTN_W), BF16),
            pltpu.VMEM((tile, ATTN_W + CONV_CH), BF16),
            pltpu.VMEM((2, N_HEADS, BLOCK, 2 * BLOCK), F32),
            pltpu.VMEM((2, tile, D_MODEL), F32),
            pltpu.VMEM((WINDOW, 2 * KV_W), F32),
            pltpu.VMEM((tile, D_MODEL), BF16),
            pltpu.VMEM((tile, D_FF), BF16),
            pltpu.VMEM((nb, D_MODEL), F32),
            pltpu.VMEM((D_MODEL, D_FF), BF16),
            pltpu.VMEM((D_FF, D_MODEL), BF16),
            pltpu.VMEM((WEIGHT_STAGE_DEPTH,) + WEIGHT_STAGE_BLOCK, F32),
            pltpu.SemaphoreType.DMA((WEIGHT_STAGE_DEPTH,)),
        ],
        compiler_params=pltpu.CompilerParams(
            dimension_semantics=("arbitrary",), vmem_limit_bytes=V7X_VMEM_LIMIT_BYTES),
        name="prompt_layer",
    )(rel_bias, sinks, x, meta, g1, w_in, bucket, cw, cb, lng, lnb, w_out, g2, w_up, w_down, gf,
      x2_s)


def _sample_mixer_body(relb_ref, sink_ref, x_ref, ckt_ref, cvt_ref, st_ref, g1_ref, win_ref,
                       bucket_ref, cw_ref, cb_ref, lng_ref, lnb_ref, wout_ref,
                       x2_ref, nkt_ref, nvt_ref, nc_ref,
                       pbuf, mix, bias_c, sink_c, *, chunk):
    i = pl.program_id(0)
    last = pl.num_programs(0) - 1
    rows_h = chunk * HEAD_ROWS

    @pl.when(i == 0)
    def _init():
        h = _rms(x_ref[...], g1_ref[...]).astype(BF16)
        pbuf[...] = jnp.dot(h, win_ref[...], preferred_element_type=F32)
        bucket = bucket_ref[...]
        rid = lax.broadcasted_iota(jnp.int32, (HEAD_ROWS, 1), 0)
        bias = jnp.zeros((HEAD_ROWS, 2 * BLOCK), F32)
        sk = jnp.zeros((HEAD_ROWS, 1), F32)
        for hd in range(N_HEADS):
            bias = jnp.where(rid == hd, _bias_from_buckets(bucket, relb_ref, hd), bias)
            sk = jnp.where(rid == hd, sink_ref[hd], sk)
        bias_c[...] = bias
        sink_c[...] = sk

    r0 = pl.multiple_of(i * chunk, chunk)
    pr = pbuf[pl.ds(r0, chunk), :]
    q = pr[:, 0:ATTN_W] * SCALE
    knew = pr[:, O_K:O_V]
    vnew = pr[:, O_V:O_A]
    unew = pr[:, O_A:O_B] * _sigmoid(pr[:, O_B:])

    def per_head(t):
        n = t.shape[-1]
        return jnp.broadcast_to(t[:, None, :], (chunk, HEAD_ROWS, n)).reshape(rows_h, n)

    hid = lax.broadcasted_iota(jnp.int32, (rows_h, 1), 0) % HEAD_ROWS
    lane = lax.broadcasted_iota(jnp.int32, (1, BLOCK), 1)
    qrep = per_head(q)
    qsum = jnp.zeros((rows_h, BLOCK), F32)
    for c in range(ATTN_W // BLOCK):
        piece = qrep[:, c * BLOCK:(c + 1) * BLOCK]
        in_head = (lane // HEAD_DIM + 2 * c) == hid
        qsum = qsum + jnp.where(in_head, piece, 0.0)
    keep = (hid % 2) == (hid // GROUP)
    qf = jnp.where(keep, qsum, pltpu.roll(qsum, HEAD_DIM, axis=1))
    qf_b = qf.astype(BF16)

    bias = jnp.broadcast_to(bias_c[...][None], (chunk, HEAD_ROWS, 2 * BLOCK)).reshape(rows_h, 2 * BLOCK)
    sk = jnp.broadcast_to(sink_c[...][None], (chunk, HEAD_ROWS, 1)).reshape(rows_h, 1)

    s_rows = []
    for b in range(chunk):
        kt = ckt_ref[b].astype(BF16)
        s_rows.append(jnp.dot(qf_b[b * HEAD_ROWS:(b + 1) * HEAD_ROWS], kt, preferred_element_type=F32))
    s_c = jnp.concatenate(s_rows, axis=0) + bias[:, 0:BLOCK]
    s_n = jnp.sum(qf * per_head(knew), axis=-1, keepdims=True) + bias[:, BLOCK:BLOCK + 1]
    m = jnp.maximum(jnp.maximum(jnp.max(s_c, axis=-1, keepdims=True), s_n), sk)
    p_c = jnp.exp(s_c - m)
    p_n = jnp.exp(s_n - m)
    l = jnp.sum(p_c, axis=-1, keepdims=True) + p_n + jnp.exp(sk - m)
    p_cb = p_c.astype(BF16)
    o_rows = []
    for b in range(chunk):
        vt = cvt_ref[b].astype(BF16)
        o_rows.append(lax.dot_general(p_cb[b * HEAD_ROWS:(b + 1) * HEAD_ROWS], vt, NT_DIMS,
                                      preferred_element_type=F32))
    o = (jnp.concatenate(o_rows, axis=0) + p_n * per_head(vnew)) / l
    o = jnp.where(keep, o, pltpu.roll(o, HEAD_DIM, axis=1))
    o = jnp.where((lane // HEAD_DIM) == (hid % 2), o, 0.0)
    wide = jnp.concatenate([jnp.where(hid // 2 == c, o, 0.0) for c in range(ATTN_W // BLOCK)], axis=1)
    gi = lax.broadcasted_iota(jnp.int32, (chunk, rows_h), 0)
    gj = lax.broadcasted_iota(jnp.int32, (chunk, rows_h), 1)
    gather = jnp.where(gj // HEAD_ROWS == gi, 1.0, 0.0).astype(BF16)
    ao = jnp.dot(gather, wide.astype(BF16), preferred_element_type=F32)
    mix[pl.ds(r0, chunk), 0:ATTN_W] = ao

    pad = jnp.zeros((BLOCK - chunk, KV_W), F32)
    knew_t = jnp.concatenate([knew, pad], axis=0).T
    vnew_t = jnp.concatenate([vnew, pad], axis=0).T
    newest = lane == WINDOW - 1
    for b in range(chunk):
        kcol = jnp.broadcast_to(knew_t[:, b:b + 1], (KV_W, WINDOW))
        vcol = jnp.broadcast_to(vnew_t[:, b:b + 1], (KV_W, WINDOW))
        nkt_ref[b] = jnp.where(newest, kcol, pltpu.roll(ckt_ref[b], WINDOW - 1, axis=1))
        nvt_ref[b] = jnp.where(newest, vcol, pltpu.roll(cvt_ref[b], WINDOW - 1, axis=1))

    acc = cb_ref[...] + cw_ref[CONV_WIDTH - 1:CONV_WIDTH, :] * unew
    for w in range(CONV_WIDTH - 1):
        acc = acc + cw_ref[w:w + 1, :] * st_ref[w]
    nc_ref[0:CONV_WIDTH - 2] = st_ref[1:CONV_WIDTH - 1]
    nc_ref[CONV_WIDTH - 2] = unew
    mu = jnp.mean(acc, axis=-1, keepdims=True)
    xc = acc - mu
    y = xc * lax.rsqrt(jnp.mean(xc * xc, axis=-1, keepdims=True) + EPS)
    y = y * lng_ref[...] + lnb_ref[...]
    mix[pl.ds(r0, chunk), ATTN_W:] = y * _sigmoid(y)

    @pl.when(i == last)
    def _out():
        x2_ref[...] = x_ref[...] + jnp.dot(mix[...].astype(BF16), wout_ref[...],
                                           preferred_element_type=F32)


def _sample_mixer(x, ckt, cvt, st, rel_bias, sinks, g1, w_in, bucket, cw, cb, lng, lnb, w_out):
    nb = x.shape[0]
    chunk = SAMPLE_CHUNK
    body = functools.partial(_sample_mixer_body, chunk=chunk)
    cache_spec = pl.BlockSpec((chunk, KV_W, WINDOW), lambda i: (i, 0, 0))
    state_spec = pl.BlockSpec((CONV_WIDTH - 1, chunk, CONV_CH), lambda i: (0, i, 0))
    return pl.pallas_call(
        body,
        grid=(nb // chunk,),
        in_specs=[
            _smem_spec(), _smem_spec(),
            _const_spec((nb, D_MODEL)),
            cache_spec, cache_spec, state_spec,
            _const_spec((1, D_MODEL)),
            _const_spec((D_MODEL, IN_W)),
            _const_spec((1, 2 * BLOCK)),
            _const_spec((CONV_WIDTH, CONV_CH)),
            _const_spec((1, CONV_CH)),
            _const_spec((1, CONV_CH)),
            _const_spec((1, CONV_CH)),
            _const_spec((ATTN_W + CONV_CH, D_MODEL)),
        ],
        out_specs=[
            pl.BlockSpec((nb, D_MODEL), lambda i: (0, 0)),
            cache_spec, cache_spec, state_spec,
        ],
        out_shape=[
            jax.ShapeDtypeStruct((nb, D_MODEL), F32),
            jax.ShapeDtypeStruct((nb, KV_W, WINDOW), F32),
            jax.ShapeDtypeStruct((nb, KV_W, WINDOW), F32),
            jax.ShapeDtypeStruct((CONV_WIDTH - 1, nb, CONV_CH), F32),
        ],
        scratch_shapes=[
            pltpu.VMEM((nb, IN_W), F32),
            pltpu.VMEM((nb, ATTN_W + CONV_CH), F32),
            pltpu.VMEM((HEAD_ROWS, 2 * BLOCK), F32),
            pltpu.VMEM((HEAD_ROWS, 1), F32),
        ],
        compiler_params=pltpu.CompilerParams(
            dimension_semantics=("arbitrary",), vmem_limit_bytes=V7X_VMEM_LIMIT_BYTES),
        name="sample_mixer",
    )(rel_bias, sinks, x, ckt, cvt, st, g1, w_in, bucket, cw, cb, lng, lnb, w_out)


def kernel(x_prompt, x_sample, cache_k, cache_v, state_conv, meta_tokens, rel_bias, norm1_g, w_in,
           attn_sinks, conv_w, conv_b, conv_ln_g, conv_ln_b, w_out, norm2_g, w_up, w_down, norm_f_g):
    batch, seq, _ = x_prompt.shape
    nb, dec_seq, _ = x_sample.shape
    assert batch == 1 and dec_seq == 1 and w_in.shape[0] == 1
    assert seq % PROMPT_TILE == 0 and nb % SAMPLE_CHUNK == 0 and nb <= PROMPT_TILE
    assert len(SLOT_ATTN_UNITS) == len(SLOT_CONV_CHUNKS) == D_FF // UP_CHUNK + D_MODEL // DOWN_CHUNK
    assert sum(SLOT_ATTN_UNITS) == (PROMPT_TILE // BLOCK) * N_KV_HEADS
    assert sum(SLOT_CONV_CHUNKS) == PROMPT_TILE // CONV_ROWS

    w_in_b = w_in[0].astype(BF16)
    w_out_b = w_out[0].astype(BF16)
    g1 = norm1_g[0][None]
    g2 = norm2_g[0][None]
    gf = norm_f_g[None]
    cw, cb = conv_w[0], conv_b[0][None]
    lng, lnb = conv_ln_g[0][None], conv_ln_b[0][None]
    sinks = attn_sinks[0]

    dist_p = jnp.arange(BLOCK)[:, None] + BLOCK - jnp.arange(2 * BLOCK)[None, :]
    bucket_p = _t5_bucket(jnp.clip(dist_p, 0, WINDOW)).astype(jnp.int32)
    lane = jnp.arange(2 * BLOCK)
    dist_s = jnp.where(lane < WINDOW, WINDOW - lane, 0)
    bucket_s = jnp.where(lane <= WINDOW, _t5_bucket(jnp.clip(dist_s, 0, WINDOW)), -1)
    bucket_s = bucket_s.astype(jnp.int32)[None]

    ckt = jnp.transpose(cache_k[0], (0, 2, 3, 1)).reshape(nb, KV_W, WINDOW)
    cvt = jnp.transpose(cache_v[0], (0, 2, 3, 1)).reshape(nb, KV_W, WINDOW)
    st = jnp.transpose(state_conv[0], (1, 0, 2))
    x2_s, nkt_s, nvt_s, nct_s = _sample_mixer(x_sample[:, 0], ckt, cvt, st, rel_bias, sinks, g1,
                                              w_in_b, bucket_s, cw, cb, lng, lnb, w_out_b)
    nk_s = jnp.transpose(nkt_s.reshape(nb, N_KV_HEADS, HEAD_DIM, WINDOW), (0, 3, 1, 2))
    nv_s = jnp.transpose(nvt_s.reshape(nb, N_KV_HEADS, HEAD_DIM, WINDOW), (0, 3, 1, 2))
    nc_s = jnp.transpose(nct_s, (1, 0, 2))
    y_p, nkt_p, nvt_p, nc_p, y_s = _prompt_layer(x_prompt[0], meta_tokens, rel_bias, sinks, g1, w_in_b,
                                               bucket_p, cw, cb, lng, lnb, w_out_b, g2, w_up[0],
                                               w_down[0], gf, x2_s)

    def to_cache(t):
        return jnp.transpose(t.reshape(N_KV_HEADS, HEAD_DIM, WINDOW), (2, 0, 1))[None, None]

    return (y_p[None], y_s[:, None],
            to_cache(nkt_p), to_cache(nvt_p), nc_p[None, None],
            nk_s[None], nv_s[None], nc_s[None])
```

```python
import functools
import math

import jax
import jax.numpy as jnp
from jax import lax
from jax.experimental import pallas as pl
from jax.experimental.pallas import tpu as pltpu

D_MODEL = 1024
N_HEADS = 8
N_KV_HEADS = 2
HEAD_DIM = 64
GROUP = N_HEADS // N_KV_HEADS
ATTN_W = N_HEADS * HEAD_DIM
KV_W = N_KV_HEADS * HEAD_DIM
CONV_CH = D_MODEL - ATTN_W
IN_W = ATTN_W + 2 * KV_W + 2 * CONV_CH
CONV_WIDTH = 31
WINDOW = 128
BLOCK = 128
N_BUCKETS = 32
MAX_DISTANCE = WINDOW
N_META = 16
D_FF = 4 * D_MODEL
EPS = 1e-6
SCALE = HEAD_DIM ** -0.5

O_K = ATTN_W
O_V = ATTN_W + KV_W
O_A = ATTN_W + 2 * KV_W
O_B = O_A + CONV_CH

PAD = (-N_META) % BLOCK
U_CARRY = 32
U_SHIFT = U_CARRY - (CONV_WIDTH - 1)

V7X_VMEM_LIMIT_BYTES = 60 * 1024 * 1024

PROMPT_TILE = 512
UP_CHUNK = 1024
DOWN_CHUNK = 512
SLOT_ATTN_UNITS = (1, 1, 1, 1, 2, 2)
SLOT_CONV_CHUNKS = (1, 1, 1, 1, 2, 2)
CONV_ROWS = 64
WEIGHT_STAGE_BLOCK = (256, 1024)
WEIGHT_STAGE_DEPTH = 4
SAMPLE_CHUNK = 16
HEAD_ROWS = 16

BF16 = jnp.bfloat16
F32 = jnp.float32
NT_DIMS = (((1,), (1,)), ((), ()))


def _t5_bucket(d):
    max_exact = N_BUCKETS // 2
    d_f = jnp.maximum(d, 1).astype(jnp.float32)
    large = max_exact + (jnp.log(d_f / max_exact) / math.log(MAX_DISTANCE / max_exact)
                         * (N_BUCKETS - max_exact)).astype(jnp.int32)
    large = jnp.minimum(large, N_BUCKETS - 1)
    return jnp.where(d < max_exact, d, large)


def _rms(x, g):
    y = x * lax.rsqrt(jnp.mean(x * x, axis=-1, keepdims=True) + EPS)
    return y * g


def _sigmoid(x):
    return 1.0 / (1.0 + jnp.exp(-x))


def _bias_from_buckets(bucket, relb_ref, h):
    b = jnp.zeros(bucket.shape, F32)
    for bk in range(N_BUCKETS):
        b = jnp.where(bucket == bk, relb_ref[bk, h], b)
    return b


def _conv_rows(ubuf, cw_ref, r0, rows):
    n = rows + U_CARRY
    strips = []
    for c0 in range(0, CONV_CH, BLOCK):
        win = ubuf[r0:r0 + n, c0:c0 + BLOCK]
        acc = None
        for s in range(8):
            sh = win if s == 0 else pltpu.roll(win, n - s, axis=0)
            for a0 in range(0, U_CARRY + 8, 8):
                w = a0 + s - U_SHIFT
                if 0 <= w < CONV_WIDTH:
                    term = cw_ref[w:w + 1, c0:c0 + BLOCK] * sh[a0:a0 + rows]
                    acc = term if acc is None else acc + term
        strips.append(acc)
    return jnp.concatenate(strips, axis=1)


def _ln_silu(acc, lng, lnb):
    mu = jnp.mean(acc, axis=-1, keepdims=True)
    xc = acc - mu
    y = xc * lax.rsqrt(jnp.mean(xc * xc, axis=-1, keepdims=True) + EPS)
    y = y * lng + lnb
    return y * _sigmoid(y)


def _zero_after(v, prev=None):
    u = pltpu.bitcast(v, jnp.uint32)
    t = prev
    for r0 in range(0, u.shape[0], 8):
        for c0 in range(0, u.shape[1], BLOCK):
            piece = u[r0:r0 + 8, c0:c0 + BLOCK]
            t = piece if t is None else t | piece
    return (t >> 16) >> 16


def _order_after(buf, zero):
    tile = pltpu.bitcast(buf[0:16, 0:BLOCK], jnp.uint32)
    buf[0:16, 0:BLOCK] = pltpu.bitcast(tile | zero, BF16)


def _stream_cast(pairs, stage, sem):
    depth, rows, cols = stage.shape
    blocks = [(src, dst, r, c) for src, dst in pairs
              for r in range(0, src.shape[0], rows) for c in range(0, src.shape[1], cols)]

    def copy(j):
        src, _, r, c = blocks[j]
        return pltpu.make_async_copy(src.at[pl.ds(r, rows), pl.ds(c, cols)],
                                     stage.at[j % depth], sem.at[j % depth])

    for j in range(min(depth - 1, len(blocks))):
        copy(j).start()
    for j, (_, dst, r, c) in enumerate(blocks):
        if j + depth - 1 < len(blocks):
            copy(j + depth - 1).start()
        copy(j).wait()
        dst[r:r + rows, c:c + cols] = stage[j % depth].astype(BF16)


def _prompt_layer_body(*refs, tile):
    i = pl.program_id(0)
    n_tiles = pl.num_programs(0) - 1
    pl.when(i < n_tiles)(functools.partial(_prompt_step, *refs, tile=tile))
    pl.when(i == n_tiles)(functools.partial(_prompt_last_ffn, *refs, tile=tile))


def _prompt_last_ffn(relb_ref, sink_ref, x_ref, meta_ref, g1_ref, win_ref, bucket_ref,
                     cw_ref, cb_ref, lng_ref, lnb_ref, wout_ref, g2_ref, wup_hbm, wdn_hbm, gf_ref,
                     x2s_ref, y_ref, nk_ref, nv_ref, nc_ref, ys_ref,
                     kbuf, vbuf, ubuf, qbuf, mix, bias_s, x2buf, kvlast, hfbuf, hidbuf, ysbuf,
                     wup_ref, wdn_ref, stage, wsem, *, tile):
    xf = x2buf[1 - pl.program_id(0) % 2]
    hf = _rms(xf, g2_ref[...]).astype(BF16)
    for c0 in range(0, D_FF, UP_CHUNK):
        hid = jnp.dot(hf, wup_ref[:, c0:c0 + UP_CHUNK], preferred_element_type=F32)
        hidbuf[:, c0:c0 + UP_CHUNK] = jnp.square(jnp.maximum(hid, 0.0)).astype(BF16)
    cols = [xf[:, n0:n0 + DOWN_CHUNK]
            + jnp.dot(hidbuf[...], wdn_ref[:, n0:n0 + DOWN_CHUNK], preferred_element_type=F32)
            for n0 in range(0, D_MODEL, DOWN_CHUNK)]
    y_ref[...] = _rms(jnp.concatenate(cols, axis=1), gf_ref[...])


def _prompt_step(relb_ref, sink_ref, x_ref, meta_ref, g1_ref, win_ref, bucket_ref,
                 cw_ref, cb_ref, lng_ref, lnb_ref, wout_ref, g2_ref, wup_hbm, wdn_hbm, gf_ref,
                 x2s_ref, y_ref, nk_ref, nv_ref, nc_ref, ys_ref,
                 kbuf, vbuf, ubuf, qbuf, mix, bias_s, x2buf, kvlast, hfbuf, hidbuf, ysbuf,
                 wup_ref, wdn_ref, stage, wsem, *, tile):
    i = pl.program_id(0)
    n_tiles = pl.num_programs(0) - 1
    g1 = g1_ref[...]
    slot = i % 2

    @pl.when(i == 0)
    def _init():
        _stream_cast([(wup_hbm, wup_ref), (wdn_hbm, wdn_ref)], stage, wsem)
        nb = x2s_ref.shape[0]
        x2buf[1, 0:nb, :] = x2s_ref[...]
        x2buf[1, nb:tile, :] = jnp.zeros((tile - nb, D_MODEL), F32)
        bucket = bucket_ref[...]
        row = lax.broadcasted_iota(jnp.int32, (BLOCK, 2 * BLOCK), 0)
        col = lax.broadcasted_iota(jnp.int32, (BLOCK, 2 * BLOCK), 1)
        dist = row + BLOCK - col
        band = (dist >= 0) & (dist <= WINDOW)
        band_first = band & (col >= PAD)
        for h in range(N_HEADS):
            b = _bias_from_buckets(bucket, relb_ref, h)
            bias_s[0, h] = jnp.where(band, b, -jnp.inf)
            bias_s[1, h] = jnp.where(band_first, b, -jnp.inf)
        hm = _rms(meta_ref[...], g1).astype(BF16)
        pm = jnp.dot(hm, win_ref[:, O_K:], preferred_element_type=F32)
        kbuf[0:PAD, :] = jnp.zeros((PAD, KV_W), BF16)
        vbuf[0:PAD, :] = jnp.zeros((PAD, KV_W), BF16)
        kbuf[PAD:BLOCK, :] = pm[:, 0:KV_W].astype(BF16)
        vbuf[PAD:BLOCK, :] = pm[:, KV_W:2 * KV_W].astype(BF16)
        um = pm[:, 2 * KV_W:2 * KV_W + CONV_CH] * _sigmoid(pm[:, 2 * KV_W + CONV_CH:])
        ubuf[0:U_CARRY - N_META, :] = jnp.zeros((U_CARRY - N_META, CONV_CH), F32)
        ubuf[U_CARRY - N_META:U_CARRY, :] = um

    x = x_ref[...]
    h = _rms(x, g1).astype(BF16)
    q = jnp.dot(h, win_ref[:, 0:ATTN_W], preferred_element_type=F32) * SCALE
    qbuf[...] = q.astype(BF16)
    kv = jnp.dot(h, win_ref[:, O_K:O_A], preferred_element_type=F32)
    kbuf[BLOCK:BLOCK + tile, :] = kv[:, 0:KV_W].astype(BF16)
    vbuf[BLOCK:BLOCK + tile, :] = kv[:, KV_W:].astype(BF16)
    kvlast[...] = kv[tile - WINDOW:, :]

    a = jnp.dot(h, win_ref[:, O_A:O_B], preferred_element_type=F32)
    b = jnp.dot(h, win_ref[:, O_B:], preferred_element_type=F32)
    ubuf[U_CARRY:U_CARRY + tile, :] = a * _sigmoid(b)

    cb, lng, lnb = cb_ref[...], lng_ref[...], lnb_ref[...]

    def conv_chunk(r0):
        c = _ln_silu(_conv_rows(ubuf, cw_ref.at[0], r0, CONV_ROWS) + cb, lng, lnb)
        mix[r0:r0 + CONV_ROWS, ATTN_W:] = c.astype(BF16)

    def attn_scores(blk, kvh):
        r0 = blk * BLOCK
        c0 = kvh * HEAD_DIM
        qg = jnp.concatenate(
            [qbuf[r0:r0 + BLOCK, (kvh * GROUP + g) * HEAD_DIM:(kvh * GROUP + g + 1) * HEAD_DIM]
             for g in range(GROUP)], axis=0)
        kk = kbuf[r0:r0 + 2 * BLOCK, c0:c0 + HEAD_DIM]
        return lax.dot_general(qg, kk, NT_DIMS, preferred_element_type=F32)

    def attn_finish(blk, kvh, s):
        r0 = blk * BLOCK
        sel = jnp.where(i == 0, 1, 0) if blk == 0 else 0
        c0 = kvh * HEAD_DIM
        vv = vbuf[r0:r0 + 2 * BLOCK, c0:c0 + HEAD_DIM]
        ps, ls = [], []
        for g in range(GROUP):
            hd = kvh * GROUP + g
            sg = s[g * BLOCK:(g + 1) * BLOCK] + bias_s[sel, hd]
            sk = sink_ref[hd]
            m = jnp.maximum(jnp.max(sg, axis=-1, keepdims=True), sk)
            p = jnp.exp(sg - m)
            ls.append(jnp.sum(p, axis=-1, keepdims=True) + jnp.exp(sk - m))
            ps.append(p.astype(BF16))
        o = jnp.dot(jnp.concatenate(ps, axis=0), vv, preferred_element_type=F32)
        for g in range(GROUP):
            hd = kvh * GROUP + g
            og = o[g * BLOCK:(g + 1) * BLOCK] / ls[g]
            mix[r0:r0 + BLOCK, hd * HEAD_DIM:(hd + 1) * HEAD_DIM] = og.astype(BF16)

    conv_starts = list(range(0, tile, CONV_ROWS))
    attn_units = [(blk, kvh) for blk in range(tile // BLOCK) for kvh in range(N_KV_HEADS)]
    n_up, n_down = D_FF // UP_CHUNK, D_MODEL // DOWN_CHUNK
    n_slots = n_up + n_down
    unit_iter, conv_iter = iter(attn_units), iter(conv_starts)
    xf = x2buf[1 - slot]
    hfbuf[...] = _rms(xf, g2_ref[...]).astype(BF16)
    x3_cols = []
    for k in range(n_slots):
        units = [next(unit_iter) for _ in range(SLOT_ATTN_UNITS[k])]
        scores = [(blk, kvh, attn_scores(blk, kvh)) for blk, kvh in units]
        if k < n_up:
            c0 = k * UP_CHUNK
            hid = jnp.dot(hfbuf[...], wup_ref[:, c0:c0 + UP_CHUNK], preferred_element_type=F32)
            hidbuf[:, c0:c0 + UP_CHUNK] = jnp.square(jnp.maximum(hid, 0.0)).astype(BF16)
        else:
            n0 = (k - n_up) * DOWN_CHUNK
            x3_cols.append(xf[:, n0:n0 + DOWN_CHUNK]
                           + jnp.dot(hidbuf[...], wdn_ref[:, n0:n0 + DOWN_CHUNK],
                                     preferred_element_type=F32))
        zero = None
        for r0 in [next(conv_iter) for _ in range(SLOT_CONV_CHUNKS[k])]:
            conv_chunk(r0)
            zero = _zero_after(mix[r0:r0 + CONV_ROWS, ATTN_W:], zero)
        for blk, kvh, s in scores:
            attn_finish(blk, kvh, s)
        if k + 1 < n_slots and zero is not None:
            _order_after(hfbuf if k + 1 < n_up else hidbuf, zero)
    acc = jnp.concatenate(x3_cols, axis=1)
    y = _rms(acc, gf_ref[...])
    y_ref[...] = y
    ysbuf[...] = y[0:ysbuf.shape[0]]

    x2buf[slot] = x + jnp.dot(mix[...], wout_ref[...], preferred_element_type=F32)

    @pl.when(i == n_tiles - 1)
    def _new_caches():
        nk_ref[...] = kvlast[:, 0:KV_W].T
        nv_ref[...] = kvlast[:, KV_W:].T
        nc_ref[:, 0, :] = ubuf[U_CARRY + tile - (CONV_WIDTH - 1):U_CARRY + tile, :]

    kbuf[0:BLOCK, :] = kbuf[tile:tile + BLOCK, :]
    vbuf[0:BLOCK, :] = vbuf[tile:tile + BLOCK, :]
    ubuf[0:U_CARRY, :] = ubuf[tile:tile + U_CARRY, :]

    @pl.when(i == 0)
    def _sample_out():
        ys_ref[:, 0, :] = ysbuf[...]


def _const_spec(shape):
    return pl.BlockSpec(shape, lambda i: (0,) * len(shape), pipeline_mode=pl.Buffered(1))


def _smem_spec():
    return pl.BlockSpec(memory_space=pltpu.SMEM)


def _prompt_layer(x, meta, rel_bias, sinks, g1, w_in, bucket, cw, cb, lng, lnb, w_out,
                  g2, w_up, w_down, gf, x2_s):
    seq = x.shape[0]
    nb = x2_s.shape[0]
    tile = PROMPT_TILE
    n_tiles = seq // tile
    body = functools.partial(_prompt_layer_body, tile=tile)
    return pl.pallas_call(
        body,
        grid=(n_tiles + 1,),
        in_specs=[
            _smem_spec(), _smem_spec(),
            pl.BlockSpec((tile, D_MODEL), lambda i: (jnp.minimum(i, n_tiles - 1), 0)),
            _const_spec((N_META, D_MODEL)),
            _const_spec((1, D_MODEL)),
            _const_spec((D_MODEL, IN_W)),
            _const_spec((BLOCK, 2 * BLOCK)),
            _const_spec((1, CONV_WIDTH, CONV_CH)),
            _const_spec((1, CONV_CH)),
            _const_spec((1, CONV_CH)),
            _const_spec((1, CONV_CH)),
            _const_spec((ATTN_W + CONV_CH, D_MODEL)),
            _const_spec((1, D_MODEL)),
            pl.BlockSpec(memory_space=pl.ANY),
            pl.BlockSpec(memory_space=pl.ANY),
            _const_spec((1, D_MODEL)),
            _const_spec((nb, D_MODEL)),
        ],
        out_specs=[
            pl.BlockSpec((tile, D_MODEL), lambda i: (jnp.maximum(i - 1, 0), 0)),
            pl.BlockSpec((KV_W, WINDOW), lambda i: (0, 0)),
            pl.BlockSpec((KV_W, WINDOW), lambda i: (0, 0)),
            pl.BlockSpec((CONV_WIDTH - 1, 1, CONV_CH), lambda i: (0, 0, 0)),
            pl.BlockSpec((nb, 1, D_MODEL), lambda i: (0, 0, 0)),
        ],
        out_shape=[
            jax.ShapeDtypeStruct((seq, D_MODEL), F32),
            jax.ShapeDtypeStruct((KV_W, WINDOW), F32),
            jax.ShapeDtypeStruct((KV_W, WINDOW), F32),
            jax.ShapeDtypeStruct((CONV_WIDTH - 1, 1, CONV_CH), F32),
            jax.ShapeDtypeStruct((nb, 1, D_MODEL), F32),
        ],
        scratch_shapes=[
            pltpu.VMEM((BLOCK + tile, KV_W), BF16),
            pltpu.VMEM((BLOCK + tile, KV_W), BF16),
            pltpu.VMEM((U_CARRY + tile, CONV_CH), F32),
            pltpu.VMEM((tile, ATTN_W), BF16),
            pltpu.VMEM((tile, ATTN_W + CONV_CH), BF16),
            pltpu.VMEM((2, N_HEADS, BLOCK, 2 * BLOCK), F32),
            pltpu.VMEM((2, tile, D_MODEL), F32),
            pltpu.VMEM((WINDOW, 2 * KV_W), F32),
            pltpu.VMEM((tile, D_MODEL), BF16),
            pltpu.VMEM((tile, D_FF), BF16),
            pltpu.VMEM((nb, D_MODEL), F32),
            pltpu.VMEM((D_MODEL, D_FF), BF16),
            pltpu.VMEM((D_FF, D_MODEL), BF16),
            pltpu.VMEM((WEIGHT_STAGE_DEPTH,) + WEIGHT_STAGE_BLOCK, F32),
            pltpu.SemaphoreType.DMA((WEIGHT_STAGE_DEPTH,)),
        ],
        compiler_params=pltpu.CompilerParams(
            dimension_semantics=("arbitrary",), vmem_limit_bytes=V7X_VMEM_LIMIT_BYTES),
        name="prompt_layer",
    )(rel_bias, sinks, x, meta, g1, w_in, bucket, cw, cb, lng, lnb, w_out, g2, w_up, w_down, gf,
      x2_s)


def _sample_mixer_body(relb_ref, sink_ref, x_ref, ckt_ref, cvt_ref, st_ref, g1_ref, win_ref,
                       bucket_ref, cw_ref, cb_ref, lng_ref, lnb_ref, wout_ref,
                       x2_ref, nkt_ref, nvt_ref, nc_ref,
                       pbuf, mix, bias_c, sink_c, *, chunk):
    i = pl.program_id(0)
    last = pl.num_programs(0) - 1
    rows_h = chunk * HEAD_ROWS

    @pl.when(i == 0)
    def _init():
        h = _rms(x_ref[:, 0, :], g1_ref[...]).astype(BF16)
        pbuf[...] = jnp.dot(h, win_ref[...], preferred_element_type=F32)
        bucket = bucket_ref[...]
        rid = lax.broadcasted_iota(jnp.int32, (HEAD_ROWS, 1), 0)
        bias = jnp.zeros((HEAD_ROWS, 2 * BLOCK), F32)
        sk = jnp.zeros((HEAD_ROWS, 1), F32)
        for hd in range(N_HEADS):
            bias = jnp.where(rid == hd, _bias_from_buckets(bucket, relb_ref, hd), bias)
            sk = jnp.where(rid == hd, sink_ref[hd], sk)
        bias_c[...] = bias
        sink_c[...] = sk

    r0 = pl.multiple_of(i * chunk, chunk)
    pr = pbuf[pl.ds(r0, chunk), :]
    q = pr[:, 0:ATTN_W] * SCALE
    knew = pr[:, O_K:O_V]
    vnew = pr[:, O_V:O_A]
    unew = pr[:, O_A:O_B] * _sigmoid(pr[:, O_B:])

    def per_head(t):
        n = t.shape[-1]
        return jnp.broadcast_to(t[:, None, :], (chunk, HEAD_ROWS, n)).reshape(rows_h, n)

    hid = lax.broadcasted_iota(jnp.int32, (rows_h, 1), 0) % HEAD_ROWS
    lane = lax.broadcasted_iota(jnp.int32, (1, BLOCK), 1)
    qrep = per_head(q)
    qsum = jnp.zeros((rows_h, BLOCK), F32)
    for c in range(ATTN_W // BLOCK):
        piece = qrep[:, c * BLOCK:(c + 1) * BLOCK]
        in_head = (lane // HEAD_DIM + 2 * c) == hid
        qsum = qsum + jnp.where(in_head, piece, 0.0)
    keep = (hid % 2) == (hid // GROUP)
    qf = jnp.where(keep, qsum, pltpu.roll(qsum, HEAD_DIM, axis=1))
    qf_b = qf.astype(BF16)

    bias = jnp.broadcast_to(bias_c[...][None], (chunk, HEAD_ROWS, 2 * BLOCK)).reshape(rows_h, 2 * BLOCK)
    sk = jnp.broadcast_to(sink_c[...][None], (chunk, HEAD_ROWS, 1)).reshape(rows_h, 1)

    s_rows = []
    for b in range(chunk):
        kt = ckt_ref[b].astype(BF16)
        s_rows.append(jnp.dot(qf_b[b * HEAD_ROWS:(b + 1) * HEAD_ROWS], kt, preferred_element_type=F32))
    s_c = jnp.concatenate(s_rows, axis=0) + bias[:, 0:BLOCK]
    s_n = jnp.sum(qf * per_head(knew), axis=-1, keepdims=True) + bias[:, BLOCK:BLOCK + 1]
    m = jnp.maximum(jnp.maximum(jnp.max(s_c, axis=-1, keepdims=True), s_n), sk)
    p_c = jnp.exp(s_c - m)
    p_n = jnp.exp(s_n - m)
    l = jnp.sum(p_c, axis=-1, keepdims=True) + p_n + jnp.exp(sk - m)
    p_cb = p_c.astype(BF16)
    o_rows = []
    for b in range(chunk):
        vt = cvt_ref[b].astype(BF16)
        o_rows.append(lax.dot_general(p_cb[b * HEAD_ROWS:(b + 1) * HEAD_ROWS], vt, NT_DIMS,
                                      preferred_element_type=F32))
    o = (jnp.concatenate(o_rows, axis=0) + p_n * per_head(vnew)) / l
    o = jnp.where(keep, o, pltpu.roll(o, HEAD_DIM, axis=1))
    o = jnp.where((lane // HEAD_DIM) == (hid % 2), o, 0.0)
    wide = jnp.concatenate([jnp.where(hid // 2 == c, o, 0.0) for c in range(ATTN_W // BLOCK)], axis=1)
    gi = lax.broadcasted_iota(jnp.int32, (chunk, rows_h), 0)
    gj = lax.broadcasted_iota(jnp.int32, (chunk, rows_h), 1)
    gather = jnp.where(gj // HEAD_ROWS == gi, 1.0, 0.0).astype(BF16)
    ao = jnp.dot(gather, wide.astype(BF16), preferred_element_type=F32)
    mix[pl.ds(r0, chunk), 0:ATTN_W] = ao

    pad = jnp.zeros((BLOCK - chunk, KV_W), F32)
    knew_t = jnp.concatenate([knew, pad], axis=0).T
    vnew_t = jnp.concatenate([vnew, pad], axis=0).T
    newest = lane == WINDOW - 1
    for b in range(chunk):
        kcol = jnp.broadcast_to(knew_t[:, b:b + 1], (KV_W, WINDOW))
        vcol = jnp.broadcast_to(vnew_t[:, b:b + 1], (KV_W, WINDOW))
        nkt_ref[b] = jnp.where(newest, kcol, pltpu.roll(ckt_ref[b], WINDOW - 1, axis=1))
        nvt_ref[b] = jnp.where(newest, vcol, pltpu.roll(cvt_ref[b], WINDOW - 1, axis=1))

    acc = cb_ref[...] + cw_ref[0, CONV_WIDTH - 1:CONV_WIDTH, :] * unew
    for w in range(CONV_WIDTH - 1):
        acc = acc + cw_ref[0, w:w + 1, :] * st_ref[w]
    nc_ref[0:CONV_WIDTH - 2] = st_ref[1:CONV_WIDTH - 1]
    nc_ref[CONV_WIDTH - 2] = unew
    mu = jnp.mean(acc, axis=-1, keepdims=True)
    xc = acc - mu
    y = xc * lax.rsqrt(jnp.mean(xc * xc, axis=-1, keepdims=True) + EPS)
    y = y * lng_ref[...] + lnb_ref[...]
    mix[pl.ds(r0, chunk), ATTN_W:] = y * _sigmoid(y)

    @pl.when(i == last)
    def _out():
        x2_ref[...] = x_ref[:, 0, :] + jnp.dot(mix[...].astype(BF16), wout_ref[...],
                                           preferred_element_type=F32)


def _sample_mixer(x, ckt, cvt, st, rel_bias, sinks, g1, w_in, bucket, cw, cb, lng, lnb, w_out):
    nb = x.shape[0]
    chunk = SAMPLE_CHUNK
    body = functools.partial(_sample_mixer_body, chunk=chunk)
    cache_spec = pl.BlockSpec((chunk, KV_W, WINDOW), lambda i: (i, 0, 0))
    state_spec = pl.BlockSpec((CONV_WIDTH - 1, chunk, CONV_CH), lambda i: (0, i, 0))
    return pl.pallas_call(
        body,
        grid=(nb // chunk,),
        in_specs=[
            _smem_spec(), _smem_spec(),
            _const_spec((nb, 1, D_MODEL)),
            cache_spec, cache_spec, state_spec,
            _const_spec((1, D_MODEL)),
            _const_spec((D_MODEL, IN_W)),
            _const_spec((1, 2 * BLOCK)),
            _const_spec((1, CONV_WIDTH, CONV_CH)),
            _const_spec((1, CONV_CH)),
            _const_spec((1, CONV_CH)),
            _const_spec((1, CONV_CH)),
            _const_spec((ATTN_W + CONV_CH, D_MODEL)),
        ],
        out_specs=[
            pl.BlockSpec((nb, D_MODEL), lambda i: (0, 0)),
            cache_spec, cache_spec, state_spec,
        ],
        out_shape=[
            jax.ShapeDtypeStruct((nb, D_MODEL), F32),
            jax.ShapeDtypeStruct((nb, KV_W, WINDOW), F32),
            jax.ShapeDtypeStruct((nb, KV_W, WINDOW), F32),
            jax.ShapeDtypeStruct((CONV_WIDTH - 1, nb, CONV_CH), F32),
        ],
        scratch_shapes=[
            pltpu.VMEM((nb, IN_W), F32),
            pltpu.VMEM((nb, ATTN_W + CONV_CH), F32),
            pltpu.VMEM((HEAD_ROWS, 2 * BLOCK), F32),
            pltpu.VMEM((HEAD_ROWS, 1), F32),
        ],
        compiler_params=pltpu.CompilerParams(
            dimension_semantics=("arbitrary",), vmem_limit_bytes=V7X_VMEM_LIMIT_BYTES),
        name="sample_mixer",
    )(rel_bias, sinks, x, ckt, cvt, st, g1, w_in, bucket, cw, cb, lng, lnb, w_out)


def kernel(x_prompt, x_sample, cache_k, cache_v, state_conv, meta_tokens, rel_bias, norm1_g, w_in,
           attn_sinks, conv_w, conv_b, conv_ln_g, conv_ln_b, w_out, norm2_g, w_up, w_down, norm_f_g):
    batch, seq, _ = x_prompt.shape
    nb, dec_seq, _ = x_sample.shape
    assert batch == 1 and dec_seq == 1 and w_in.shape[0] == 1
    assert seq % PROMPT_TILE == 0 and nb % SAMPLE_CHUNK == 0 and nb <= PROMPT_TILE
    assert len(SLOT_ATTN_UNITS) == len(SLOT_CONV_CHUNKS) == D_FF // UP_CHUNK + D_MODEL // DOWN_CHUNK
    assert sum(SLOT_ATTN_UNITS) == (PROMPT_TILE // BLOCK) * N_KV_HEADS
    assert sum(SLOT_CONV_CHUNKS) == PROMPT_TILE // CONV_ROWS

    w_in_b = w_in[0].astype(BF16)
    w_out_b = w_out[0].astype(BF16)
    g1 = norm1_g[0][None]
    g2 = norm2_g[0][None]
    gf = norm_f_g[None]
    cw, cb = conv_w, conv_b[0][None]
    lng, lnb = conv_ln_g[0][None], conv_ln_b[0][None]
    sinks = attn_sinks[0]

    dist_p = jnp.arange(BLOCK)[:, None] + BLOCK - jnp.arange(2 * BLOCK)[None, :]
    bucket_p = _t5_bucket(jnp.clip(dist_p, 0, WINDOW)).astype(jnp.int32)
    lane = jnp.arange(2 * BLOCK)
    dist_s = jnp.where(lane < WINDOW, WINDOW - lane, 0)
    bucket_s = jnp.where(lane <= WINDOW, _t5_bucket(jnp.clip(dist_s, 0, WINDOW)), -1)
    bucket_s = bucket_s.astype(jnp.int32)[None]

    ckt = jnp.transpose(cache_k[0], (0, 2, 3, 1)).reshape(nb, KV_W, WINDOW)
    cvt = jnp.transpose(cache_v[0], (0, 2, 3, 1)).reshape(nb, KV_W, WINDOW)
    st = jnp.transpose(state_conv[0], (1, 0, 2))
    x2_s, nkt_s, nvt_s, nct_s = _sample_mixer(x_sample, ckt, cvt, st, rel_bias, sinks, g1,
                                              w_in_b, bucket_s, cw, cb, lng, lnb, w_out_b)
    nk_s = jnp.transpose(nkt_s.reshape(nb, N_KV_HEADS, HEAD_DIM, WINDOW), (0, 3, 1, 2))
    nv_s = jnp.transpose(nvt_s.reshape(nb, N_KV_HEADS, HEAD_DIM, WINDOW), (0, 3, 1, 2))
    nc_s = jnp.transpose(nct_s, (1, 0, 2))
    y_p, nkt_p, nvt_p, nc_p, y_s = _prompt_layer(x_prompt[0], meta_tokens, rel_bias, sinks, g1, w_in_b,
                                               bucket_p, cw, cb, lng, lnb, w_out_b, g2, w_up[0],
                                               w_down[0], gf, x2_s)

    def to_cache(t):
        return jnp.transpose(t.reshape(N_KV_HEADS, HEAD_DIM, WINDOW), (2, 0, 1))[None, None]

    return (y_p[None], y_s,
            to_cache(nkt_p), to_cache(nvt_p), jnp.transpose(nc_p, (1, 0, 2))[None],
            nk_s[None], nv_s[None], nc_s[None])
```

```python
import functools
import math

import jax
import jax.numpy as jnp
from jax import lax
from jax.experimental import pallas as pl
from jax.experimental.pallas import tpu as pltpu

D_MODEL = 1024
N_HEADS = 8
N_KV_HEADS = 2
HEAD_DIM = 64
GROUP = N_HEADS // N_KV_HEADS
ATTN_W = N_HEADS * HEAD_DIM
KV_W = N_KV_HEADS * HEAD_DIM
CONV_CH = D_MODEL - ATTN_W
IN_W = ATTN_W + 2 * KV_W + 2 * CONV_CH
CONV_WIDTH = 31
WINDOW = 128
BLOCK = 128
N_BUCKETS = 32
MAX_DISTANCE = WINDOW
N_META = 16
D_FF = 4 * D_MODEL
EPS = 1e-6
SCALE = HEAD_DIM ** -0.5

O_K = ATTN_W
O_V = ATTN_W + KV_W
O_A = ATTN_W + 2 * KV_W
O_B = O_A + CONV_CH

PAD = (-N_META) % BLOCK
U_CARRY = 32
U_SHIFT = U_CARRY - (CONV_WIDTH - 1)

V7X_VMEM_LIMIT_BYTES = 60 * 1024 * 1024

PROMPT_TILE = 512
UP_CHUNK = 1024
DOWN_CHUNK = 512
SLOT_ATTN_UNITS = (1, 1, 1, 1, 2, 2)
SLOT_CONV_CHUNKS = (1, 1, 1, 1, 2, 2)
CONV_ROWS = 64
WEIGHT_STAGE_BLOCK = (256, 1024)
WEIGHT_STAGE_DEPTH = 4
SAMPLE_CHUNK = 32
HEAD_ROWS = 16

BF16 = jnp.bfloat16
F32 = jnp.float32
NT_DIMS = (((1,), (1,)), ((), ()))


def _t5_bucket(d):
    max_exact = N_BUCKETS // 2
    d_f = jnp.maximum(d, 1).astype(jnp.float32)
    large = max_exact + (jnp.log(d_f / max_exact) / math.log(MAX_DISTANCE / max_exact)
                         * (N_BUCKETS - max_exact)).astype(jnp.int32)
    large = jnp.minimum(large, N_BUCKETS - 1)
    return jnp.where(d < max_exact, d, large)


def _rms(x, g):
    y = x * lax.rsqrt(jnp.mean(x * x, axis=-1, keepdims=True) + EPS)
    return y * g


def _sigmoid(x):
    return 1.0 / (1.0 + jnp.exp(-x))


def _bias_from_buckets(bucket, relb_ref, h):
    b = jnp.zeros(bucket.shape, F32)
    for bk in range(N_BUCKETS):
        b = jnp.where(bucket == bk, relb_ref[bk, h], b)
    return b


def _conv_rows(ubuf, cw_ref, r0, rows):
    n = rows + U_CARRY
    strips = []
    for c0 in range(0, CONV_CH, BLOCK):
        win = ubuf[r0:r0 + n, c0:c0 + BLOCK]
        acc = None
        for s in range(8):
            sh = win if s == 0 else pltpu.roll(win, n - s, axis=0)
            for a0 in range(0, U_CARRY + 8, 8):
                w = a0 + s - U_SHIFT
                if 0 <= w < CONV_WIDTH:
                    term = cw_ref[w:w + 1, c0:c0 + BLOCK] * sh[a0:a0 + rows]
                    acc = term if acc is None else acc + term
        strips.append(acc)
    return jnp.concatenate(strips, axis=1)


def _ln_silu(acc, lng, lnb):
    mu = jnp.mean(acc, axis=-1, keepdims=True)
    xc = acc - mu
    y = xc * lax.rsqrt(jnp.mean(xc * xc, axis=-1, keepdims=True) + EPS)
    y = y * lng + lnb
    return y * _sigmoid(y)


def _zero_after(v, prev=None):
    u = pltpu.bitcast(v, jnp.uint32)
    t = prev
    for r0 in range(0, u.shape[0], 8):
        for c0 in range(0, u.shape[1], BLOCK):
            piece = u[r0:r0 + 8, c0:c0 + BLOCK]
            t = piece if t is None else t | piece
    return (t >> 16) >> 16


def _order_after(buf, zero):
    tile = pltpu.bitcast(buf[0:16, 0:BLOCK], jnp.uint32)
    buf[0:16, 0:BLOCK] = pltpu.bitcast(tile | zero, BF16)


def _stream_cast(pairs, stage, sem):
    depth, rows, cols = stage.shape
    blocks = [(src, dst, r, c) for src, dst in pairs
              for r in range(0, src.shape[0], rows) for c in range(0, src.shape[1], cols)]

    def copy(j):
        src, _, r, c = blocks[j]
        return pltpu.make_async_copy(src.at[pl.ds(r, rows), pl.ds(c, cols)],
                                     stage.at[j % depth], sem.at[j % depth])

    for j in range(min(depth - 1, len(blocks))):
        copy(j).start()
    for j, (_, dst, r, c) in enumerate(blocks):
        if j + depth - 1 < len(blocks):
            copy(j + depth - 1).start()
        copy(j).wait()
        dst[r:r + rows, c:c + cols] = stage[j % depth].astype(BF16)


def _prompt_layer_body(*refs, tile):
    i = pl.program_id(0)
    n_tiles = pl.num_programs(0) - 1
    pl.when(i < n_tiles)(functools.partial(_prompt_step, *refs, tile=tile))
    pl.when(i == n_tiles)(functools.partial(_prompt_last_ffn, *refs, tile=tile))


def _prompt_last_ffn(relb_ref, sink_ref, x_ref, meta_ref, g1_ref, win_ref, bucket_ref,
                     cw_ref, cb_ref, lng_ref, lnb_ref, wout_ref, g2_ref, wup_hbm, wdn_hbm, gf_ref,
                     x2s_ref, y_ref, nk_ref, nv_ref, nc_ref, ys_ref,
                     kbuf, vbuf, ubuf, qbuf, mix, bias_s, x2buf, kvlast, hfbuf, hidbuf, ysbuf,
                     wup_ref, wdn_ref, stage, wsem, *, tile):
    xf = x2buf[1 - pl.program_id(0) % 2]
    hf = _rms(xf, g2_ref[...]).astype(BF16)
    for c0 in range(0, D_FF, UP_CHUNK):
        hid = jnp.dot(hf, wup_ref[:, c0:c0 + UP_CHUNK], preferred_element_type=F32)
        hidbuf[:, c0:c0 + UP_CHUNK] = jnp.square(jnp.maximum(hid, 0.0)).astype(BF16)
    cols = [xf[:, n0:n0 + DOWN_CHUNK]
            + jnp.dot(hidbuf[...], wdn_ref[:, n0:n0 + DOWN_CHUNK], preferred_element_type=F32)
            for n0 in range(0, D_MODEL, DOWN_CHUNK)]
    y_ref[...] = _rms(jnp.concatenate(cols, axis=1), gf_ref[...])


def _prompt_step(relb_ref, sink_ref, x_ref, meta_ref, g1_ref, win_ref, bucket_ref,
                 cw_ref, cb_ref, lng_ref, lnb_ref, wout_ref, g2_ref, wup_hbm, wdn_hbm, gf_ref,
                 x2s_ref, y_ref, nk_ref, nv_ref, nc_ref, ys_ref,
                 kbuf, vbuf, ubuf, qbuf, mix, bias_s, x2buf, kvlast, hfbuf, hidbuf, ysbuf,
                 wup_ref, wdn_ref, stage, wsem, *, tile):
    i = pl.program_id(0)
    n_tiles = pl.num_programs(0) - 1
    g1 = g1_ref[...]
    slot = i % 2

    @pl.when(i == 0)
    def _init():
        _stream_cast([(wup_hbm, wup_ref), (wdn_hbm, wdn_ref)], stage, wsem)
        nb = x2s_ref.shape[0]
        x2buf[1, 0:nb, :] = x2s_ref[...]
        x2buf[1, nb:tile, :] = jnp.zeros((tile - nb, D_MODEL), F32)
        bucket = bucket_ref[...]
        row = lax.broadcasted_iota(jnp.int32, (BLOCK, 2 * BLOCK), 0)
        col = lax.broadcasted_iota(jnp.int32, (BLOCK, 2 * BLOCK), 1)
        dist = row + BLOCK - col
        band = (dist >= 0) & (dist <= WINDOW)
        band_first = band & (col >= PAD)
        for h in range(N_HEADS):
            b = _bias_from_buckets(bucket, relb_ref, h)
            bias_s[0, h] = jnp.where(band, b, -jnp.inf)
            bias_s[1, h] = jnp.where(band_first, b, -jnp.inf)
        hm = _rms(meta_ref[...], g1).astype(BF16)
        pm = jnp.dot(hm, win_ref[:, O_K:], preferred_element_type=F32)
        kbuf[0:PAD, :] = jnp.zeros((PAD, KV_W), BF16)
        vbuf[0:PAD, :] = jnp.zeros((PAD, KV_W), BF16)
        kbuf[PAD:BLOCK, :] = pm[:, 0:KV_W].astype(BF16)
        vbuf[PAD:BLOCK, :] = pm[:, KV_W:2 * KV_W].astype(BF16)
        um = pm[:, 2 * KV_W:2 * KV_W + CONV_CH] * _sigmoid(pm[:, 2 * KV_W + CONV_CH:])
        ubuf[0:U_CARRY - N_META, :] = jnp.zeros((U_CARRY - N_META, CONV_CH), F32)
        ubuf[U_CARRY - N_META:U_CARRY, :] = um

    x = x_ref[...]
    h = _rms(x, g1).astype(BF16)
    q = jnp.dot(h, win_ref[:, 0:ATTN_W], preferred_element_type=F32) * SCALE
    qbuf[...] = q.astype(BF16)
    kv = jnp.dot(h, win_ref[:, O_K:O_A], preferred_element_type=F32)
    kbuf[BLOCK:BLOCK + tile, :] = kv[:, 0:KV_W].astype(BF16)
    vbuf[BLOCK:BLOCK + tile, :] = kv[:, KV_W:].astype(BF16)
    kvlast[...] = kv[tile - WINDOW:, :]

    a = jnp.dot(h, win_ref[:, O_A:O_B], preferred_element_type=F32)
    b = jnp.dot(h, win_ref[:, O_B:], preferred_element_type=F32)
    ubuf[U_CARRY:U_CARRY + tile, :] = a * _sigmoid(b)

    cb, lng, lnb = cb_ref[...], lng_ref[...], lnb_ref[...]

    def conv_chunk(r0):
        c = _ln_silu(_conv_rows(ubuf, cw_ref.at[0], r0, CONV_ROWS) + cb, lng, lnb)
        mix[r0:r0 + CONV_ROWS, ATTN_W:] = c.astype(BF16)

    def attn_scores(blk, kvh):
        r0 = blk * BLOCK
        c0 = kvh * HEAD_DIM
        qg = jnp.concatenate(
            [qbuf[r0:r0 + BLOCK, (kvh * GROUP + g) * HEAD_DIM:(kvh * GROUP + g + 1) * HEAD_DIM]
             for g in range(GROUP)], axis=0)
        kk = kbuf[r0:r0 + 2 * BLOCK, c0:c0 + HEAD_DIM]
        return lax.dot_general(qg, kk, NT_DIMS, preferred_element_type=F32)

    def attn_finish(blk, kvh, s):
        r0 = blk * BLOCK
        sel = jnp.where(i == 0, 1, 0) if blk == 0 else 0
        c0 = kvh * HEAD_DIM
        vv = vbuf[r0:r0 + 2 * BLOCK, c0:c0 + HEAD_DIM]
        ps, ls = [], []
        for g in range(GROUP):
            hd = kvh * GROUP + g
            sg = s[g * BLOCK:(g + 1) * BLOCK] + bias_s[sel, hd]
            sk = sink_ref[hd]
            m = jnp.maximum(jnp.max(sg, axis=-1, keepdims=True), sk)
            p = jnp.exp(sg - m)
            ls.append(jnp.sum(p, axis=-1, keepdims=True) + jnp.exp(sk - m))
            ps.append(p.astype(BF16))
        o = jnp.dot(jnp.concatenate(ps, axis=0), vv, preferred_element_type=F32)
        for g in range(GROUP):
            hd = kvh * GROUP + g
            og = o[g * BLOCK:(g + 1) * BLOCK] / ls[g]
            mix[r0:r0 + BLOCK, hd * HEAD_DIM:(hd + 1) * HEAD_DIM] = og.astype(BF16)

    conv_starts = list(range(0, tile, CONV_ROWS))
    attn_units = [(blk, kvh) for blk in range(tile // BLOCK) for kvh in range(N_KV_HEADS)]
    n_up, n_down = D_FF // UP_CHUNK, D_MODEL // DOWN_CHUNK
    n_slots = n_up + n_down
    unit_iter, conv_iter = iter(attn_units), iter(conv_starts)
    xf = x2buf[1 - slot]
    hfbuf[...] = _rms(xf, g2_ref[...]).astype(BF16)
    x3_cols = []
    for k in range(n_slots):
        units = [next(unit_iter) for _ in range(SLOT_ATTN_UNITS[k])]
        scores = [(blk, kvh, attn_scores(blk, kvh)) for blk, kvh in units]
        if k < n_up:
            c0 = k * UP_CHUNK
            hid = jnp.dot(hfbuf[...], wup_ref[:, c0:c0 + UP_CHUNK], preferred_element_type=F32)
            hidbuf[:, c0:c0 + UP_CHUNK] = jnp.square(jnp.maximum(hid, 0.0)).astype(BF16)
        else:
            n0 = (k - n_up) * DOWN_CHUNK
            x3_cols.append(xf[:, n0:n0 + DOWN_CHUNK]
                           + jnp.dot(hidbuf[...], wdn_ref[:, n0:n0 + DOWN_CHUNK],
                                     preferred_element_type=F32))
        zero = None
        for r0 in [next(conv_iter) for _ in range(SLOT_CONV_CHUNKS[k])]:
            conv_chunk(r0)
            zero = _zero_after(mix[r0:r0 + CONV_ROWS, ATTN_W:], zero)
        for blk, kvh, s in scores:
            attn_finish(blk, kvh, s)
        if k + 1 < n_slots and zero is not None:
            _order_after(hfbuf if k + 1 < n_up else hidbuf, zero)
    acc = jnp.concatenate(x3_cols, axis=1)
    y = _rms(acc, gf_ref[...])
    y_ref[...] = y
    ysbuf[...] = y[0:ysbuf.shape[0]]

    x2buf[slot] = x + jnp.dot(mix[...], wout_ref[...], preferred_element_type=F32)

    @pl.when(i == n_tiles - 1)
    def _new_caches():
        nk_ref[...] = kvlast[:, 0:KV_W].T
        nv_ref[...] = kvlast[:, KV_W:].T
        nc_ref[:, 0, :] = ubuf[U_CARRY + tile - (CONV_WIDTH - 1):U_CARRY + tile, :]

    kbuf[0:BLOCK, :] = kbuf[tile:tile + BLOCK, :]
    vbuf[0:BLOCK, :] = vbuf[tile:tile + BLOCK, :]
    ubuf[0:U_CARRY, :] = ubuf[tile:tile + U_CARRY, :]

    @pl.when(i == 0)
    def _sample_out():
        ys_ref[:, 0, :] = ysbuf[...]


def _const_spec(shape):
    return pl.BlockSpec(shape, lambda i: (0,) * len(shape), pipeline_mode=pl.Buffered(1))


def _smem_spec():
    return pl.BlockSpec(memory_space=pltpu.SMEM)


def _prompt_layer(x, meta, rel_bias, sinks, g1, w_in, bucket, cw, cb, lng, lnb, w_out,
                  g2, w_up, w_down, gf, x2_s):
    seq = x.shape[0]
    nb = x2_s.shape[0]
    tile = PROMPT_TILE
    n_tiles = seq // tile
    body = functools.partial(_prompt_layer_body, tile=tile)
    return pl.pallas_call(
        body,
        grid=(n_tiles + 1,),
        in_specs=[
            _smem_spec(), _smem_spec(),
            pl.BlockSpec((tile, D_MODEL), lambda i: (jnp.minimum(i, n_tiles - 1), 0)),
            _const_spec((N_META, D_MODEL)),
            _const_spec((1, D_MODEL)),
            _const_spec((D_MODEL, IN_W)),
            _const_spec((BLOCK, 2 * BLOCK)),
            _const_spec((1, CONV_WIDTH, CONV_CH)),
            _const_spec((1, CONV_CH)),
            _const_spec((1, CONV_CH)),
            _const_spec((1, CONV_CH)),
            _const_spec((ATTN_W + CONV_CH, D_MODEL)),
            _const_spec((1, D_MODEL)),
            pl.BlockSpec(memory_space=pl.ANY),
            pl.BlockSpec(memory_space=pl.ANY),
            _const_spec((1, D_MODEL)),
            _const_spec((nb, D_MODEL)),
        ],
        out_specs=[
            pl.BlockSpec((tile, D_MODEL), lambda i: (jnp.maximum(i - 1, 0), 0)),
            pl.BlockSpec((KV_W, WINDOW), lambda i: (0, 0)),
            pl.BlockSpec((KV_W, WINDOW), lambda i: (0, 0)),
            pl.BlockSpec((CONV_WIDTH - 1, 1, CONV_CH), lambda i: (0, 0, 0)),
            pl.BlockSpec((nb, 1, D_MODEL), lambda i: (0, 0, 0)),
        ],
        out_shape=[
            jax.ShapeDtypeStruct((seq, D_MODEL), F32),
            jax.ShapeDtypeStruct((KV_W, WINDOW), F32),
            jax.ShapeDtypeStruct((KV_W, WINDOW), F32),
            jax.ShapeDtypeStruct((CONV_WIDTH - 1, 1, CONV_CH), F32),
            jax.ShapeDtypeStruct((nb, 1, D_MODEL), F32),
        ],
        scratch_shapes=[
            pltpu.VMEM((BLOCK + tile, KV_W), BF16),
            pltpu.VMEM((BLOCK + tile, KV_W), BF16),
            pltpu.VMEM((U_CARRY + tile, CONV_CH), F32),
            pltpu.VMEM((tile, ATTN_W), BF16),
            pltpu.VMEM((tile, ATTN_W + CONV_CH), BF16),
            pltpu.VMEM((2, N_HEADS, BLOCK, 2 * BLOCK), F32),
            pltpu.VMEM((2, tile, D_MODEL), F32),
            pltpu.VMEM((WINDOW, 2 * KV_W), F32),
            pltpu.VMEM((tile, D_MODEL), BF16),
            pltpu.VMEM((tile, D_FF), BF16),
            pltpu.VMEM((nb, D_MODEL), F32),
            pltpu.VMEM((D_MODEL, D_FF), BF16),
            pltpu.VMEM((D_FF, D_MODEL), BF16),
            pltpu.VMEM((WEIGHT_STAGE_DEPTH,) + WEIGHT_STAGE_BLOCK, F32),
            pltpu.SemaphoreType.DMA((WEIGHT_STAGE_DEPTH,)),
        ],
        compiler_params=pltpu.CompilerParams(
            dimension_semantics=("arbitrary",), vmem_limit_bytes=V7X_VMEM_LIMIT_BYTES),
        name="prompt_layer",
    )(rel_bias, sinks, x, meta, g1, w_in, bucket, cw, cb, lng, lnb, w_out, g2, w_up, w_down, gf,
      x2_s)


def _sample_mixer_body(relb_ref, sink_ref, x_ref, ckt_ref, cvt_ref, st_ref, g1_ref, win_ref,
                       bucket_ref, cw_ref, cb_ref, lng_ref, lnb_ref, wout_ref,
                       x2_ref, nkt_ref, nvt_ref, nc_ref,
                       pbuf, mix, bias_c, sink_c, *, chunk):
    i = pl.program_id(0)
    last = pl.num_programs(0) - 1
    rows_h = chunk * HEAD_ROWS

    @pl.when(i == 0)
    def _init():
        h = _rms(x_ref[:, 0, :], g1_ref[...]).astype(BF16)
        pbuf[...] = jnp.dot(h, win_ref[...], preferred_element_type=F32)
        bucket = bucket_ref[...]
        rid = lax.broadcasted_iota(jnp.int32, (HEAD_ROWS, 1), 0)
        bias = jnp.zeros((HEAD_ROWS, 2 * BLOCK), F32)
        sk = jnp.zeros((HEAD_ROWS, 1), F32)
        for hd in range(N_HEADS):
            bias = jnp.where(rid == hd, _bias_from_buckets(bucket, relb_ref, hd), bias)
            sk = jnp.where(rid == hd, sink_ref[hd], sk)
        bias_c[...] = bias
        sink_c[...] = sk

    r0 = pl.multiple_of(i * chunk, chunk)
    pr = pbuf[pl.ds(r0, chunk), :]
    q = pr[:, 0:ATTN_W] * SCALE
    knew = pr[:, O_K:O_V]
    vnew = pr[:, O_V:O_A]
    unew = pr[:, O_A:O_B] * _sigmoid(pr[:, O_B:])

    def per_head(t):
        n = t.shape[-1]
        return jnp.broadcast_to(t[:, None, :], (chunk, HEAD_ROWS, n)).reshape(rows_h, n)

    hid = lax.broadcasted_iota(jnp.int32, (rows_h, 1), 0) % HEAD_ROWS
    lane = lax.broadcasted_iota(jnp.int32, (1, BLOCK), 1)
    qrep = per_head(q)
    qsum = jnp.zeros((rows_h, BLOCK), F32)
    for c in range(ATTN_W // BLOCK):
        piece = qrep[:, c * BLOCK:(c + 1) * BLOCK]
        in_head = (lane // HEAD_DIM + 2 * c) == hid
        qsum = qsum + jnp.where(in_head, piece, 0.0)
    keep = (hid % 2) == (hid // GROUP)
    qf = jnp.where(keep, qsum, pltpu.roll(qsum, HEAD_DIM, axis=1))
    qf_b = qf.astype(BF16)

    bias = jnp.broadcast_to(bias_c[...][None], (chunk, HEAD_ROWS, 2 * BLOCK)).reshape(rows_h, 2 * BLOCK)
    sk = jnp.broadcast_to(sink_c[...][None], (chunk, HEAD_ROWS, 1)).reshape(rows_h, 1)

    s_rows = []
    for b in range(chunk):
        kt = ckt_ref[b].astype(BF16)
        s_rows.append(jnp.dot(qf_b[b * HEAD_ROWS:(b + 1) * HEAD_ROWS], kt, preferred_element_type=F32))
    s_c = jnp.concatenate(s_rows, axis=0) + bias[:, 0:BLOCK]
    s_n = jnp.sum(qf * per_head(knew), axis=-1, keepdims=True) + bias[:, BLOCK:BLOCK + 1]
    m = jnp.maximum(jnp.maximum(jnp.max(s_c, axis=-1, keepdims=True), s_n), sk)
    p_c = jnp.exp(s_c - m)
    p_n = jnp.exp(s_n - m)
    l = jnp.sum(p_c, axis=-1, keepdims=True) + p_n + jnp.exp(sk - m)
    p_cb = p_c.astype(BF16)
    o_rows = []
    for b in range(chunk):
        vt = cvt_ref[b].astype(BF16)
        o_rows.append(lax.dot_general(p_cb[b * HEAD_ROWS:(b + 1) * HEAD_ROWS], vt, NT_DIMS,
                                      preferred_element_type=F32))
    o = (jnp.concatenate(o_rows, axis=0) + p_n * per_head(vnew)) / l
    o = jnp.where(keep, o, pltpu.roll(o, HEAD_DIM, axis=1))
    o = jnp.where((lane // HEAD_DIM) == (hid % 2), o, 0.0)
    wide = jnp.concatenate([jnp.where(hid // 2 == c, o, 0.0) for c in range(ATTN_W // BLOCK)], axis=1)
    gi = lax.broadcasted_iota(jnp.int32, (chunk, rows_h), 0)
    gj = lax.broadcasted_iota(jnp.int32, (chunk, rows_h), 1)
    gather = jnp.where(gj // HEAD_ROWS == gi, 1.0, 0.0).astype(BF16)
    ao = jnp.dot(gather, wide.astype(BF16), preferred_element_type=F32)
    mix[pl.ds(r0, chunk), 0:ATTN_W] = ao

    pad = jnp.zeros((BLOCK - chunk, KV_W), F32)
    knew_t = jnp.concatenate([knew, pad], axis=0).T
    vnew_t = jnp.concatenate([vnew, pad], axis=0).T
    newest = lane == WINDOW - 1
    for b in range(chunk):
        kcol = jnp.broadcast_to(knew_t[:, b:b + 1], (KV_W, WINDOW))
        vcol = jnp.broadcast_to(vnew_t[:, b:b + 1], (KV_W, WINDOW))
        nkt_ref[b] = jnp.where(newest, kcol, pltpu.roll(ckt_ref[b], WINDOW - 1, axis=1))
        nvt_ref[b] = jnp.where(newest, vcol, pltpu.roll(cvt_ref[b], WINDOW - 1, axis=1))

    acc = cb_ref[...] + cw_ref[0, CONV_WIDTH - 1:CONV_WIDTH, :] * unew
    for w in range(CONV_WIDTH - 1):
        acc = acc + cw_ref[0, w:w + 1, :] * st_ref[w]
    nc_ref[0:CONV_WIDTH - 2] = st_ref[1:CONV_WIDTH - 1]
    nc_ref[CONV_WIDTH - 2] = unew
    mu = jnp.mean(acc, axis=-1, keepdims=True)
    xc = acc - mu
    y = xc * lax.rsqrt(jnp.mean(xc * xc, axis=-1, keepdims=True) + EPS)
    y = y * lng_ref[...] + lnb_ref[...]
    mix[pl.ds(r0, chunk), ATTN_W:] = y * _sigmoid(y)

    @pl.when(i == last)
    def _out():
        x2_ref[...] = x_ref[:, 0, :] + jnp.dot(mix[...].astype(BF16), wout_ref[...],
                                           preferred_element_type=F32)


def _sample_mixer(x, ckt, cvt, st, rel_bias, sinks, g1, w_in, bucket, cw, cb, lng, lnb, w_out):
    nb = x.shape[0]
    chunk = SAMPLE_CHUNK
    body = functools.partial(_sample_mixer_body, chunk=chunk)
    cache_spec = pl.BlockSpec((chunk, KV_W, WINDOW), lambda i: (i, 0, 0))
    state_spec = pl.BlockSpec((CONV_WIDTH - 1, chunk, CONV_CH), lambda i: (0, i, 0))
    return pl.pallas_call(
        body,
        grid=(nb // chunk,),
        in_specs=[
            _smem_spec(), _smem_spec(),
            _const_spec((nb, 1, D_MODEL)),
            cache_spec, cache_spec, state_spec,
            _const_spec((1, D_MODEL)),
            _const_spec((D_MODEL, IN_W)),
            _const_spec((1, 2 * BLOCK)),
            _const_spec((1, CONV_WIDTH, CONV_CH)),
            _const_spec((1, CONV_CH)),
            _const_spec((1, CONV_CH)),
            _const_spec((1, CONV_CH)),
            _const_spec((ATTN_W + CONV_CH, D_MODEL)),
        ],
        out_specs=[
            pl.BlockSpec((nb, D_MODEL), lambda i: (0, 0)),
            cache_spec, cache_spec, state_spec,
        ],
        out_shape=[
            jax.ShapeDtypeStruct((nb, D_MODEL), F32),
            jax.ShapeDtypeStruct((nb, KV_W, WINDOW), F32),
            jax.ShapeDtypeStruct((nb, KV_W, WINDOW), F32),
            jax.ShapeDtypeStruct((CONV_WIDTH - 1, nb, CONV_CH), F32),
        ],
        scratch_shapes=[
            pltpu.VMEM((nb, IN_W), F32),
            pltpu.VMEM((nb, ATTN_W + CONV_CH), F32),
            pltpu.VMEM((HEAD_ROWS, 2 * BLOCK), F32),
            pltpu.VMEM((HEAD_ROWS, 1), F32),
        ],
        compiler_params=pltpu.CompilerParams(
            dimension_semantics=("arbitrary",), vmem_limit_bytes=V7X_VMEM_LIMIT_BYTES),
        name="sample_mixer",
    )(rel_bias, sinks, x, ckt, cvt, st, g1, w_in, bucket, cw, cb, lng, lnb, w_out)


def kernel(x_prompt, x_sample, cache_k, cache_v, state_conv, meta_tokens, rel_bias, norm1_g, w_in,
           attn_sinks, conv_w, conv_b, conv_ln_g, conv_ln_b, w_out, norm2_g, w_up, w_down, norm_f_g):
    batch, seq, _ = x_prompt.shape
    nb, dec_seq, _ = x_sample.shape
    assert batch == 1 and dec_seq == 1 and w_in.shape[0] == 1
    assert seq % PROMPT_TILE == 0 and nb % SAMPLE_CHUNK == 0 and nb <= PROMPT_TILE
    assert len(SLOT_ATTN_UNITS) == len(SLOT_CONV_CHUNKS) == D_FF // UP_CHUNK + D_MODEL // DOWN_CHUNK
    assert sum(SLOT_ATTN_UNITS) == (PROMPT_TILE // BLOCK) * N_KV_HEADS
    assert sum(SLOT_CONV_CHUNKS) == PROMPT_TILE // CONV_ROWS

    w_in_b = w_in[0].astype(BF16)
    w_out_b = w_out[0].astype(BF16)
    g1 = norm1_g[0][None]
    g2 = norm2_g[0][None]
    gf = norm_f_g[None]
    cw, cb = conv_w, conv_b[0][None]
    lng, lnb = conv_ln_g[0][None], conv_ln_b[0][None]
    sinks = attn_sinks[0]

    dist_p = jnp.arange(BLOCK)[:, None] + BLOCK - jnp.arange(2 * BLOCK)[None, :]
    bucket_p = _t5_bucket(jnp.clip(dist_p, 0, WINDOW)).astype(jnp.int32)
    lane = jnp.arange(2 * BLOCK)
    dist_s = jnp.where(lane < WINDOW, WINDOW - lane, 0)
    bucket_s = jnp.where(lane <= WINDOW, _t5_bucket(jnp.clip(dist_s, 0, WINDOW)), -1)
    bucket_s = bucket_s.astype(jnp.int32)[None]

    ckt = jnp.transpose(cache_k[0], (0, 2, 3, 1)).reshape(nb, KV_W, WINDOW)
    cvt = jnp.transpose(cache_v[0], (0, 2, 3, 1)).reshape(nb, KV_W, WINDOW)
    st = jnp.transpose(state_conv[0], (1, 0, 2))
    x2_s, nkt_s, nvt_s, nct_s = _sample_mixer(x_sample, ckt, cvt, st, rel_bias, sinks, g1,
                                              w_in_b, bucket_s, cw, cb, lng, lnb, w_out_b)
    nk_s = jnp.transpose(nkt_s.reshape(nb, N_KV_HEADS, HEAD_DIM, WINDOW), (0, 3, 1, 2))
    nv_s = jnp.transpose(nvt_s.reshape(nb, N_KV_HEADS, HEAD_DIM, WINDOW), (0, 3, 1, 2))
    nc_s = jnp.transpose(nct_s, (1, 0, 2))
    y_p, nkt_p, nvt_p, nc_p, y_s = _prompt_layer(x_prompt[0], meta_tokens, rel_bias, sinks, g1, w_in_b,
                                               bucket_p, cw, cb, lng, lnb, w_out_b, g2, w_up[0],
                                               w_down[0], gf, x2_s)

    def to_cache(t):
        return jnp.transpose(t.reshape(N_KV_HEADS, HEAD_DIM, WINDOW), (2, 0, 1))[None, None]

    return (y_p[None], y_s,
            to_cache(nkt_p), to_cache(nvt_p), jnp.transpose(nc_p, (1, 0, 2))[None],
            nk_s[None], nv_s[None], nc_s[None])
```

```python
import functools
import math

import jax
import jax.numpy as jnp
from jax import lax
from jax.experimental import pallas as pl
from jax.experimental.pallas import tpu as pltpu

D_MODEL = 1024
N_HEADS = 8
N_KV_HEADS = 2
HEAD_DIM = 64
GROUP = N_HEADS // N_KV_HEADS
ATTN_W = N_HEADS * HEAD_DIM
KV_W = N_KV_HEADS * HEAD_DIM
CONV_CH = D_MODEL - ATTN_W
IN_W = ATTN_W + 2 * KV_W + 2 * CONV_CH
CONV_WIDTH = 31
WINDOW = 128
BLOCK = 128
N_BUCKETS = 32
MAX_DISTANCE = WINDOW
N_META = 16
D_FF = 4 * D_MODEL
EPS = 1e-6
SCALE = HEAD_DIM ** -0.5
LOG2E = math.log2(math.e)

O_K = ATTN_W
O_V = ATTN_W + KV_W
O_A = ATTN_W + 2 * KV_W
O_B = O_A + CONV_CH

PAD = (-N_META) % BLOCK
U_CARRY = 32
U_SHIFT = U_CARRY - (CONV_WIDTH - 1)

V7X_VMEM_LIMIT_BYTES = 60 * 1024 * 1024

PROMPT_TILE = 512
UP_CHUNK = 1024
DOWN_CHUNK = 512
SLOT_ATTN_UNITS = (1, 1, 1, 1, 2, 2)
SLOT_CONV_CHUNKS = (1, 1, 1, 1, 2, 2)
CONV_ROWS = 64
WEIGHT_STAGE_BLOCK = (256, 1024)
WEIGHT_STAGE_DEPTH = 4
SAMPLE_CHUNK = 32
HEAD_ROWS = 16

BF16 = jnp.bfloat16
F32 = jnp.float32
NT_DIMS = (((1,), (1,)), ((), ()))


def _t5_bucket(d):
    max_exact = N_BUCKETS // 2
    d_f = jnp.maximum(d, 1).astype(jnp.float32)
    large = max_exact + (jnp.log(d_f / max_exact) / math.log(MAX_DISTANCE / max_exact)
                         * (N_BUCKETS - max_exact)).astype(jnp.int32)
    large = jnp.minimum(large, N_BUCKETS - 1)
    return jnp.where(d < max_exact, d, large)


def _rms(x, g):
    y = x * lax.rsqrt(jnp.mean(x * x, axis=-1, keepdims=True) + EPS)
    return y * g


def _relu_sq_bf16(x):
    r = jnp.maximum(x.astype(BF16), 0.0)
    return r * r


def _sigmoid(x):
    return 1.0 / (1.0 + jnp.exp(-x))


def _bias_from_buckets(bucket, relb_ref, h):
    b = jnp.zeros(bucket.shape, F32)
    for bk in range(N_BUCKETS):
        b = jnp.where(bucket == bk, relb_ref[bk, h], b)
    return b


def _conv_rows(ubuf, cw_ref, r0, rows):
    n = rows + U_CARRY
    strips = []
    for c0 in range(0, CONV_CH, BLOCK):
        win = ubuf[r0:r0 + n, c0:c0 + BLOCK]
        acc = None
        for s in range(8):
            sh = win if s == 0 else pltpu.roll(win, n - s, axis=0)
            for a0 in range(0, U_CARRY + 8, 8):
                w = a0 + s - U_SHIFT
                if 0 <= w < CONV_WIDTH:
                    term = cw_ref[w:w + 1, c0:c0 + BLOCK] * sh[a0:a0 + rows]
                    acc = term if acc is None else acc + term
        strips.append(acc)
    return jnp.concatenate(strips, axis=1)


def _ln_silu(acc, lng, lnb):
    mu = jnp.mean(acc, axis=-1, keepdims=True)
    xc = acc - mu
    y = xc * lax.rsqrt(jnp.mean(xc * xc, axis=-1, keepdims=True) + EPS)
    y = y * lng + lnb
    return y * _sigmoid(y)


def _zero_after(v, prev=None):
    u = pltpu.bitcast(v, jnp.uint32)
    t = prev
    for r0 in range(0, u.shape[0], 8):
        for c0 in range(0, u.shape[1], BLOCK):
            piece = u[r0:r0 + 8, c0:c0 + BLOCK]
            t = piece if t is None else t | piece
    return (t >> 16) >> 16


def _order_after(buf, zero):
    tile = pltpu.bitcast(buf[0:16, 0:BLOCK], jnp.uint32)
    buf[0:16, 0:BLOCK] = pltpu.bitcast(tile | zero, BF16)


def _stream_cast(pairs, stage, sem):
    depth, rows, cols = stage.shape
    blocks = [(src, dst, r, c) for src, dst in pairs
              for r in range(0, src.shape[0], rows) for c in range(0, src.shape[1], cols)]

    def copy(j):
        src, _, r, c = blocks[j]
        return pltpu.make_async_copy(src.at[pl.ds(r, rows), pl.ds(c, cols)],
                                     stage.at[j % depth], sem.at[j % depth])

    for j in range(min(depth - 1, len(blocks))):
        copy(j).start()
    for j, (_, dst, r, c) in enumerate(blocks):
        if j + depth - 1 < len(blocks):
            copy(j + depth - 1).start()
        copy(j).wait()
        dst[r:r + rows, c:c + cols] = stage[j % depth].astype(BF16)


def _prompt_layer_body(*refs, tile):
    i = pl.program_id(0)
    n_tiles = pl.num_programs(0) - 1
    pl.when(i < n_tiles)(functools.partial(_prompt_step, *refs, tile=tile))
    pl.when(i == n_tiles)(functools.partial(_prompt_last_ffn, *refs, tile=tile))


def _prompt_last_ffn(relb_ref, sink_ref, x_ref, meta_ref, g1_ref, win_ref, bucket_ref,
                     cw_ref, cb_ref, lng_ref, lnb_ref, wout_ref, g2_ref, wup_hbm, wdn_hbm, gf_ref,
                     x2s_ref, y_ref, nk_ref, nv_ref, nc_ref, ys_ref,
                     kbuf, vbuf, ubuf, qbuf, mix, bias_s, x2buf, kvlast, hfbuf, hidbuf, ysbuf,
                     wup_ref, wdn_ref, stage, wsem, *, tile):
    xf = x2buf[1 - pl.program_id(0) % 2]
    hf = _rms(xf, g2_ref[...]).astype(BF16)
    for c0 in range(0, D_FF, UP_CHUNK):
        hid = jnp.dot(hf, wup_ref[:, c0:c0 + UP_CHUNK], preferred_element_type=F32)
        hidbuf[:, c0:c0 + UP_CHUNK] = _relu_sq_bf16(hid)
    cols = [xf[:, n0:n0 + DOWN_CHUNK]
            + jnp.dot(hidbuf[...], wdn_ref[:, n0:n0 + DOWN_CHUNK], preferred_element_type=F32)
            for n0 in range(0, D_MODEL, DOWN_CHUNK)]
    y_ref[...] = _rms(jnp.concatenate(cols, axis=1), gf_ref[...])


def _prompt_step(relb_ref, sink_ref, x_ref, meta_ref, g1_ref, win_ref, bucket_ref,
                 cw_ref, cb_ref, lng_ref, lnb_ref, wout_ref, g2_ref, wup_hbm, wdn_hbm, gf_ref,
                 x2s_ref, y_ref, nk_ref, nv_ref, nc_ref, ys_ref,
                 kbuf, vbuf, ubuf, qbuf, mix, bias_s, x2buf, kvlast, hfbuf, hidbuf, ysbuf,
                 wup_ref, wdn_ref, stage, wsem, *, tile):
    i = pl.program_id(0)
    n_tiles = pl.num_programs(0) - 1
    g1 = g1_ref[...]
    slot = i % 2

    @pl.when(i == 0)
    def _init():
        _stream_cast([(wup_hbm, wup_ref), (wdn_hbm, wdn_ref)], stage, wsem)
        nb = x2s_ref.shape[0]
        x2buf[1, 0:nb, :] = x2s_ref[...]
        x2buf[1, nb:tile, :] = jnp.zeros((tile - nb, D_MODEL), F32)
        bucket = bucket_ref[...]
        row = lax.broadcasted_iota(jnp.int32, (BLOCK, 2 * BLOCK), 0)
        col = lax.broadcasted_iota(jnp.int32, (BLOCK, 2 * BLOCK), 1)
        dist = row + BLOCK - col
        band = (dist >= 0) & (dist <= WINDOW)
        band_first = band & (col >= PAD)
        for h in range(N_HEADS):
            b = _bias_from_buckets(bucket, relb_ref, h)
            b = b * LOG2E
            bias_s[0, h] = jnp.where(band, b, -jnp.inf)
            bias_s[1, h] = jnp.where(band_first, b, -jnp.inf)
        hm = _rms(meta_ref[...], g1).astype(BF16)
        pm = jnp.dot(hm, win_ref[:, O_K:], preferred_element_type=F32)
        kbuf[0:PAD, :] = jnp.zeros((PAD, KV_W), BF16)
        vbuf[0:PAD, :] = jnp.zeros((PAD, KV_W), BF16)
        kbuf[PAD:BLOCK, :] = pm[:, 0:KV_W].astype(BF16)
        vbuf[PAD:BLOCK, :] = pm[:, KV_W:2 * KV_W].astype(BF16)
        um = pm[:, 2 * KV_W:2 * KV_W + CONV_CH] * _sigmoid(pm[:, 2 * KV_W + CONV_CH:])
        ubuf[0:U_CARRY - N_META, :] = jnp.zeros((U_CARRY - N_META, CONV_CH), F32)
        ubuf[U_CARRY - N_META:U_CARRY, :] = um

    x = x_ref[...]
    h = _rms(x, g1).astype(BF16)
    q = jnp.dot(h, win_ref[:, 0:ATTN_W], preferred_element_type=F32) * (SCALE * LOG2E)
    qbuf[...] = q.astype(BF16)
    kv = jnp.dot(h, win_ref[:, O_K:O_A], preferred_element_type=F32)
    kbuf[BLOCK:BLOCK + tile, :] = kv[:, 0:KV_W].astype(BF16)
    vbuf[BLOCK:BLOCK + tile, :] = kv[:, KV_W:].astype(BF16)
    kvlast[...] = kv[tile - WINDOW:, :]

    a = jnp.dot(h, win_ref[:, O_A:O_B], preferred_element_type=F32)
    b = jnp.dot(h, win_ref[:, O_B:], preferred_element_type=F32)
    ubuf[U_CARRY:U_CARRY + tile, :] = a * _sigmoid(b)

    cb, lng, lnb = cb_ref[...], lng_ref[...], lnb_ref[...]

    def conv_chunk(r0):
        c = _ln_silu(_conv_rows(ubuf, cw_ref.at[0], r0, CONV_ROWS) + cb, lng, lnb)
        mix[r0:r0 + CONV_ROWS, ATTN_W:] = c.astype(BF16)

    def attn_scores(blk, kvh):
        r0 = blk * BLOCK
        c0 = kvh * HEAD_DIM
        qg = jnp.concatenate(
            [qbuf[r0:r0 + BLOCK, (kvh * GROUP + g) * HEAD_DIM:(kvh * GROUP + g + 1) * HEAD_DIM]
             for g in range(GROUP)], axis=0)
        kk = kbuf[r0:r0 + 2 * BLOCK, c0:c0 + HEAD_DIM]
        return lax.dot_general(qg, kk, NT_DIMS, preferred_element_type=F32)

    def attn_finish(blk, kvh, s):
        r0 = blk * BLOCK
        sel = jnp.where(i == 0, 1, 0) if blk == 0 else 0
        c0 = kvh * HEAD_DIM
        vv = vbuf[r0:r0 + 2 * BLOCK, c0:c0 + HEAD_DIM]
        ps, ls = [], []
        for g in range(GROUP):
            hd = kvh * GROUP + g
            sg = s[g * BLOCK:(g + 1) * BLOCK] + bias_s[sel, hd]
            sk = sink_ref[hd] * LOG2E
            m = jnp.maximum(jnp.max(sg, axis=-1, keepdims=True), sk)
            p = jnp.exp2(sg - m)
            ls.append(jnp.sum(p, axis=-1, keepdims=True) + jnp.exp2(sk - m))
            ps.append(p.astype(BF16))
        o = jnp.dot(jnp.concatenate(ps, axis=0), vv, preferred_element_type=F32)
        for g in range(GROUP):
            hd = kvh * GROUP + g
            og = o[g * BLOCK:(g + 1) * BLOCK] / ls[g]
            mix[r0:r0 + BLOCK, hd * HEAD_DIM:(hd + 1) * HEAD_DIM] = og.astype(BF16)

    conv_starts = list(range(0, tile, CONV_ROWS))
    attn_units = [(blk, kvh) for blk in range(tile // BLOCK) for kvh in range(N_KV_HEADS)]
    n_up, n_down = D_FF // UP_CHUNK, D_MODEL // DOWN_CHUNK
    n_slots = n_up + n_down
    unit_iter, conv_iter = iter(attn_units), iter(conv_starts)
    xf = x2buf[1 - slot]
    hfbuf[...] = _rms(xf, g2_ref[...]).astype(BF16)
    x3_cols = []
    for k in range(n_slots):
        units = [next(unit_iter) for _ in range(SLOT_ATTN_UNITS[k])]
        scores = [(blk, kvh, attn_scores(blk, kvh)) for blk, kvh in units]
        if k < n_up:
            c0 = k * UP_CHUNK
            hid = jnp.dot(hfbuf[...], wup_ref[:, c0:c0 + UP_CHUNK], preferred_element_type=F32)
            hidbuf[:, c0:c0 + UP_CHUNK] = _relu_sq_bf16(hid)
        else:
            n0 = (k - n_up) * DOWN_CHUNK
            x3_cols.append(xf[:, n0:n0 + DOWN_CHUNK]
                           + jnp.dot(hidbuf[...], wdn_ref[:, n0:n0 + DOWN_CHUNK],
                                     preferred_element_type=F32))
        zero = None
        for r0 in [next(conv_iter) for _ in range(SLOT_CONV_CHUNKS[k])]:
            conv_chunk(r0)
            zero = _zero_after(mix[r0:r0 + CONV_ROWS, ATTN_W:], zero)
        for blk, kvh, s in scores:
            attn_finish(blk, kvh, s)
        if k + 1 < n_slots and zero is not None:
            _order_after(hfbuf if k + 1 < n_up else hidbuf, zero)
    acc = jnp.concatenate(x3_cols, axis=1)
    y = _rms(acc, gf_ref[...])
    y_ref[...] = y
    ysbuf[...] = y[0:ysbuf.shape[0]]

    x2buf[slot] = x + jnp.dot(mix[...], wout_ref[...], preferred_element_type=F32)

    @pl.when(i == n_tiles - 1)
    def _new_caches():
        nk_ref[...] = kvlast[:, 0:KV_W].T
        nv_ref[...] = kvlast[:, KV_W:].T
        nc_ref[:, 0, :] = ubuf[U_CARRY + tile - (CONV_WIDTH - 1):U_CARRY + tile, :]

    kbuf[0:BLOCK, :] = kbuf[tile:tile + BLOCK, :]
    vbuf[0:BLOCK, :] = vbuf[tile:tile + BLOCK, :]
    ubuf[0:U_CARRY, :] = ubuf[tile:tile + U_CARRY, :]

    @pl.when(i == 0)
    def _sample_out():
        ys_ref[:, 0, :] = ysbuf[...]


def _const_spec(shape):
    return pl.BlockSpec(shape, lambda i: (0,) * len(shape), pipeline_mode=pl.Buffered(1))


def _smem_spec():
    return pl.BlockSpec(memory_space=pltpu.SMEM)


def _prompt_layer(x, meta, rel_bias, sinks, g1, w_in, bucket, cw, cb, lng, lnb, w_out,
                  g2, w_up, w_down, gf, x2_s):
    seq = x.shape[0]
    nb = x2_s.shape[0]
    tile = PROMPT_TILE
    n_tiles = seq // tile
    body = functools.partial(_prompt_layer_body, tile=tile)
    return pl.pallas_call(
        body,
        grid=(n_tiles + 1,),
        in_specs=[
            _smem_spec(), _smem_spec(),
            pl.BlockSpec((tile, D_MODEL), lambda i: (jnp.minimum(i, n_tiles - 1), 0)),
            _const_spec((N_META, D_MODEL)),
            _const_spec((1, D_MODEL)),
            _const_spec((D_MODEL, IN_W)),
            _const_spec((BLOCK, 2 * BLOCK)),
            _const_spec((1, CONV_WIDTH, CONV_CH)),
            _const_spec((1, CONV_CH)),
            _const_spec((1, CONV_CH)),
            _const_spec((1, CONV_CH)),
            _const_spec((ATTN_W + CONV_CH, D_MODEL)),
            _const_spec((1, D_MODEL)),
            pl.BlockSpec(memory_space=pl.ANY),
            pl.BlockSpec(memory_space=pl.ANY),
            _const_spec((1, D_MODEL)),
            _const_spec((nb, D_MODEL)),
        ],
        out_specs=[
            pl.BlockSpec((tile, D_MODEL), lambda i: (jnp.maximum(i - 1, 0), 0)),
            pl.BlockSpec((KV_W, WINDOW), lambda i: (0, 0)),
            pl.BlockSpec((KV_W, WINDOW), lambda i: (0, 0)),
            pl.BlockSpec((CONV_WIDTH - 1, 1, CONV_CH), lambda i: (0, 0, 0)),
            pl.BlockSpec((nb, 1, D_MODEL), lambda i: (0, 0, 0)),
        ],
        out_shape=[
            jax.ShapeDtypeStruct((seq, D_MODEL), F32),
            jax.ShapeDtypeStruct((KV_W, WINDOW), F32),
            jax.ShapeDtypeStruct((KV_W, WINDOW), F32),
            jax.ShapeDtypeStruct((CONV_WIDTH - 1, 1, CONV_CH), F32),
            jax.ShapeDtypeStruct((nb, 1, D_MODEL), F32),
        ],
        scratch_shapes=[
            pltpu.VMEM((BLOCK + tile, KV_W), BF16),
            pltpu.VMEM((BLOCK + tile, KV_W), BF16),
            pltpu.VMEM((U_CARRY + tile, CONV_CH), F32),
            pltpu.VMEM((tile, ATTN_W), BF16),
            pltpu.VMEM((tile, ATTN_W + CONV_CH), BF16),
            pltpu.VMEM((2, N_HEADS, BLOCK, 2 * BLOCK), F32),
            pltpu.VMEM((2, tile, D_MODEL), F32),
            pltpu.VMEM((WINDOW, 2 * KV_W), F32),
            pltpu.VMEM((tile, D_MODEL), BF16),
            pltpu.VMEM((tile, D_FF), BF16),
            pltpu.VMEM((nb, D_MODEL), F32),
            pltpu.VMEM((D_MODEL, D_FF), BF16),
            pltpu.VMEM((D_FF, D_MODEL), BF16),
            pltpu.VMEM((WEIGHT_STAGE_DEPTH,) + WEIGHT_STAGE_BLOCK, F32),
            pltpu.SemaphoreType.DMA((WEIGHT_STAGE_DEPTH,)),
        ],
        compiler_params=pltpu.CompilerParams(
            dimension_semantics=("arbitrary",), vmem_limit_bytes=V7X_VMEM_LIMIT_BYTES),
        name="prompt_layer",
    )(rel_bias, sinks, x, meta, g1, w_in, bucket, cw, cb, lng, lnb, w_out, g2, w_up, w_down, gf,
      x2_s)


def _sample_mixer_body(relb_ref, sink_ref, x_ref, ckt_ref, cvt_ref, st_ref, g1_ref, win_ref,
                       bucket_ref, cw_ref, cb_ref, lng_ref, lnb_ref, wout_ref,
                       x2_ref, nkt_ref, nvt_ref, nc_ref,
                       pbuf, mix, bias_c, sink_c, *, chunk):
    i = pl.program_id(0)
    last = pl.num_programs(0) - 1
    rows_h = chunk * HEAD_ROWS

    @pl.when(i == 0)
    def _init():
        h = _rms(x_ref[:, 0, :], g1_ref[...]).astype(BF16)
        pbuf[...] = jnp.dot(h, win_ref[...], preferred_element_type=F32)
        bucket = bucket_ref[...]
        rid = lax.broadcasted_iota(jnp.int32, (HEAD_ROWS, 1), 0)
        bias = jnp.zeros((HEAD_ROWS, 2 * BLOCK), F32)
        sk = jnp.zeros((HEAD_ROWS, 1), F32)
        for hd in range(N_HEADS):
            bias = jnp.where(rid == hd, _bias_from_buckets(bucket, relb_ref, hd), bias)
            sk = jnp.where(rid == hd, sink_ref[hd], sk)
        bias_c[...] = bias
        sink_c[...] = sk

    r0 = pl.multiple_of(i * chunk, chunk)
    pr = pbuf[pl.ds(r0, chunk), :]
    q = pr[:, 0:ATTN_W] * SCALE
    knew = pr[:, O_K:O_V]
    vnew = pr[:, O_V:O_A]
    unew = pr[:, O_A:O_B] * _sigmoid(pr[:, O_B:])

    def per_head(t):
        n = t.shape[-1]
        return jnp.broadcast_to(t[:, None, :], (chunk, HEAD_ROWS, n)).reshape(rows_h, n)

    hid = lax.broadcasted_iota(jnp.int32, (rows_h, 1), 0) % HEAD_ROWS
    lane = lax.broadcasted_iota(jnp.int32, (1, BLOCK), 1)
    qrep = per_head(q)
    qsum = jnp.zeros((rows_h, BLOCK), F32)
    for c in range(ATTN_W // BLOCK):
        piece = qrep[:, c * BLOCK:(c + 1) * BLOCK]
        in_head = (lane // HEAD_DIM + 2 * c) == hid
        qsum = qsum + jnp.where(in_head, piece, 0.0)
    keep = (hid % 2) == (hid // GROUP)
    qf = jnp.where(keep, qsum, pltpu.roll(qsum, HEAD_DIM, axis=1))
    qf_b = qf.astype(BF16)

    bias = jnp.broadcast_to(bias_c[...][None], (chunk, HEAD_ROWS, 2 * BLOCK)).reshape(rows_h, 2 * BLOCK)
    sk = jnp.broadcast_to(sink_c[...][None], (chunk, HEAD_ROWS, 1)).reshape(rows_h, 1)

    s_rows = []
    for b in range(chunk):
        kt = ckt_ref[b].astype(BF16)
        s_rows.append(jnp.dot(qf_b[b * HEAD_ROWS:(b + 1) * HEAD_ROWS], kt, preferred_element_type=F32))
    s_c = jnp.concatenate(s_rows, axis=0) + bias[:, 0:BLOCK]
    s_n = jnp.sum(qf * per_head(knew), axis=-1, keepdims=True) + bias[:, BLOCK:BLOCK + 1]
    m = jnp.maximum(jnp.maximum(jnp.max(s_c, axis=-1, keepdims=True), s_n), sk)
    p_c = jnp.exp(s_c - m)
    p_n = jnp.exp(s_n - m)
    l = jnp.sum(p_c, axis=-1, keepdims=True) + p_n + jnp.exp(sk - m)
    p_cb = p_c.astype(BF16)
    o_rows = []
    for b in range(chunk):
        vt = cvt_ref[b].astype(BF16)
        o_rows.append(lax.dot_general(p_cb[b * HEAD_ROWS:(b + 1) * HEAD_ROWS], vt, NT_DIMS,
                                      preferred_element_type=F32))
    o = (jnp.concatenate(o_rows, axis=0) + p_n * per_head(vnew)) / l
    o = jnp.where(keep, o, pltpu.roll(o, HEAD_DIM, axis=1))
    o = jnp.where((lane // HEAD_DIM) == (hid % 2), o, 0.0)
    wide = jnp.concatenate([jnp.where(hid // 2 == c, o, 0.0) for c in range(ATTN_W // BLOCK)], axis=1)
    gi = lax.broadcasted_iota(jnp.int32, (chunk, rows_h), 0)
    gj = lax.broadcasted_iota(jnp.int32, (chunk, rows_h), 1)
    gather = jnp.where(gj // HEAD_ROWS == gi, 1.0, 0.0).astype(BF16)
    ao = jnp.dot(gather, wide.astype(BF16), preferred_element_type=F32)
    mix[pl.ds(r0, chunk), 0:ATTN_W] = ao

    pad = jnp.zeros((BLOCK - chunk, KV_W), F32)
    knew_t = jnp.concatenate([knew, pad], axis=0).T
    vnew_t = jnp.concatenate([vnew, pad], axis=0).T
    newest = lane == WINDOW - 1
    for b in range(chunk):
        kcol = jnp.broadcast_to(knew_t[:, b:b + 1], (KV_W, WINDOW))
        vcol = jnp.broadcast_to(vnew_t[:, b:b + 1], (KV_W, WINDOW))
        nkt_ref[b] = jnp.where(newest, kcol, pltpu.roll(ckt_ref[b], WINDOW - 1, axis=1))
        nvt_ref[b] = jnp.where(newest, vcol, pltpu.roll(cvt_ref[b], WINDOW - 1, axis=1))

    acc = cb_ref[...] + cw_ref[0, CONV_WIDTH - 1:CONV_WIDTH, :] * unew
    for w in range(CONV_WIDTH - 1):
        acc = acc + cw_ref[0, w:w + 1, :] * st_ref[w]
    nc_ref[0:CONV_WIDTH - 2] = st_ref[1:CONV_WIDTH - 1]
    nc_ref[CONV_WIDTH - 2] = unew
    mu = jnp.mean(acc, axis=-1, keepdims=True)
    xc = acc - mu
    y = xc * lax.rsqrt(jnp.mean(xc * xc, axis=-1, keepdims=True) + EPS)
    y = y * lng_ref[...] + lnb_ref[...]
    mix[pl.ds(r0, chunk), ATTN_W:] = y * _sigmoid(y)

    @pl.when(i == last)
    def _out():
        x2_ref[...] = x_ref[:, 0, :] + jnp.dot(mix[...].astype(BF16), wout_ref[...],
                                           preferred_element_type=F32)


def _sample_mixer(x, ckt, cvt, st, rel_bias, sinks, g1, w_in, bucket, cw, cb, lng, lnb, w_out):
    nb = x.shape[0]
    chunk = SAMPLE_CHUNK
    body = functools.partial(_sample_mixer_body, chunk=chunk)
    cache_spec = pl.BlockSpec((chunk, KV_W, WINDOW), lambda i: (i, 0, 0))
    state_spec = pl.BlockSpec((CONV_WIDTH - 1, chunk, CONV_CH), lambda i: (0, i, 0))
    return pl.pallas_call(
        body,
        grid=(nb // chunk,),
        in_specs=[
            _smem_spec(), _smem_spec(),
            _const_spec((nb, 1, D_MODEL)),
            cache_spec, cache_spec, state_spec,
            _const_spec((1, D_MODEL)),
            _const_spec((D_MODEL, IN_W)),
            _const_spec((1, 2 * BLOCK)),
            _const_spec((1, CONV_WIDTH, CONV_CH)),
            _const_spec((1, CONV_CH)),
            _const_spec((1, CONV_CH)),
            _const_spec((1, CONV_CH)),
            _const_spec((ATTN_W + CONV_CH, D_MODEL)),
        ],
        out_specs=[
            pl.BlockSpec((nb, D_MODEL), lambda i: (0, 0)),
            cache_spec, cache_spec, state_spec,
        ],
        out_shape=[
            jax.ShapeDtypeStruct((nb, D_MODEL), F32),
            jax.ShapeDtypeStruct((nb, KV_W, WINDOW), F32),
            jax.ShapeDtypeStruct((nb, KV_W, WINDOW), F32),
            jax.ShapeDtypeStruct((CONV_WIDTH - 1, nb, CONV_CH), F32),
        ],
        scratch_shapes=[
            pltpu.VMEM((nb, IN_W), F32),
            pltpu.VMEM((nb, ATTN_W + CONV_CH), F32),
            pltpu.VMEM((HEAD_ROWS, 2 * BLOCK), F32),
            pltpu.VMEM((HEAD_ROWS, 1), F32),
        ],
        compiler_params=pltpu.CompilerParams(
            dimension_semantics=("arbitrary",), vmem_limit_bytes=V7X_VMEM_LIMIT_BYTES),
        name="sample_mixer",
    )(rel_bias, sinks, x, ckt, cvt, st, g1, w_in, bucket, cw, cb, lng, lnb, w_out)


def kernel(x_prompt, x_sample, cache_k, cache_v, state_conv, meta_tokens, rel_bias, norm1_g, w_in,
           attn_sinks, conv_w, conv_b, conv_ln_g, conv_ln_b, w_out, norm2_g, w_up, w_down, norm_f_g):
    batch, seq, _ = x_prompt.shape
    nb, dec_seq, _ = x_sample.shape
    assert batch == 1 and dec_seq == 1 and w_in.shape[0] == 1
    assert seq % PROMPT_TILE == 0 and nb % SAMPLE_CHUNK == 0 and nb <= PROMPT_TILE
    assert len(SLOT_ATTN_UNITS) == len(SLOT_CONV_CHUNKS) == D_FF // UP_CHUNK + D_MODEL // DOWN_CHUNK
    assert sum(SLOT_ATTN_UNITS) == (PROMPT_TILE // BLOCK) * N_KV_HEADS
    assert sum(SLOT_CONV_CHUNKS) == PROMPT_TILE // CONV_ROWS

    w_in_b = w_in[0].astype(BF16)
    w_out_b = w_out[0].astype(BF16)
    g1 = norm1_g[0][None]
    g2 = norm2_g[0][None]
    gf = norm_f_g[None]
    cw, cb = conv_w, conv_b[0][None]
    lng, lnb = conv_ln_g[0][None], conv_ln_b[0][None]
    sinks = attn_sinks[0]

    dist_p = jnp.arange(BLOCK)[:, None] + BLOCK - jnp.arange(2 * BLOCK)[None, :]
    bucket_p = _t5_bucket(jnp.clip(dist_p, 0, WINDOW)).astype(jnp.int32)
    lane = jnp.arange(2 * BLOCK)
    dist_s = jnp.where(lane < WINDOW, WINDOW - lane, 0)
    bucket_s = jnp.where(lane <= WINDOW, _t5_bucket(jnp.clip(dist_s, 0, WINDOW)), -1)
    bucket_s = bucket_s.astype(jnp.int32)[None]

    ckt = jnp.transpose(cache_k[0], (0, 2, 3, 1)).reshape(nb, KV_W, WINDOW)
    cvt = jnp.transpose(cache_v[0], (0, 2, 3, 1)).reshape(nb, KV_W, WINDOW)
    st = jnp.transpose(state_conv[0], (1, 0, 2))
    x2_s, nkt_s, nvt_s, nct_s = _sample_mixer(x_sample, ckt, cvt, st, rel_bias, sinks, g1,
                                              w_in_b, bucket_s, cw, cb, lng, lnb, w_out_b)
    nk_s = jnp.transpose(nkt_s.reshape(nb, N_KV_HEADS, HEAD_DIM, WINDOW), (0, 3, 1, 2))
    nv_s = jnp.transpose(nvt_s.reshape(nb, N_KV_HEADS, HEAD_DIM, WINDOW), (0, 3, 1, 2))
    nc_s = jnp.transpose(nct_s, (1, 0, 2))
    y_p, nkt_p, nvt_p, nc_p, y_s = _prompt_layer(x_prompt[0], meta_tokens, rel_bias, sinks, g1, w_in_b,
                                               bucket_p, cw, cb, lng, lnb, w_out_b, g2, w_up[0],
                                               w_down[0], gf, x2_s)

    def to_cache(t):
        return jnp.transpose(t.reshape(N_KV_HEADS, HEAD_DIM, WINDOW), (2, 0, 1))[None, None]

    return (y_p[None], y_s,
            to_cache(nkt_p), to_cache(nvt_p), jnp.transpose(nc_p, (1, 0, 2))[None],
            nk_s[None], nv_s[None], nc_s[None])
```

```python
import functools
import math

import jax
import jax.numpy as jnp
from jax import lax
from jax.experimental import pallas as pl
from jax.experimental.pallas import tpu as pltpu

D_MODEL = 1024
N_HEADS = 8
N_KV_HEADS = 2
HEAD_DIM = 64
GROUP = N_HEADS // N_KV_HEADS
ATTN_W = N_HEADS * HEAD_DIM
KV_W = N_KV_HEADS * HEAD_DIM
CONV_CH = D_MODEL - ATTN_W
IN_W = ATTN_W + 2 * KV_W + 2 * CONV_CH
CONV_WIDTH = 31
WINDOW = 128
BLOCK = 128
N_BUCKETS = 32
MAX_DISTANCE = WINDOW
N_META = 16
D_FF = 4 * D_MODEL
EPS = 1e-6
SCALE = HEAD_DIM ** -0.5
LOG2E = math.log2(math.e)

O_K = ATTN_W
O_V = ATTN_W + KV_W
O_A = ATTN_W + 2 * KV_W
O_B = O_A + CONV_CH

PAD = (-N_META) % BLOCK
U_CARRY = 32
U_SHIFT = U_CARRY - (CONV_WIDTH - 1)

V7X_VMEM_LIMIT_BYTES = 60 * 1024 * 1024

PROMPT_TILE = 512
UP_CHUNK = 1024
DOWN_CHUNK = 512
SLOT_ATTN_UNITS = (2, 2, 2, 2, 0, 0)
SLOT_CONV_CHUNKS = (0, 0, 0, 0, 4, 4)
CONV_ROWS = 64
WEIGHT_STAGE_BLOCK = (256, 1024)
WEIGHT_STAGE_DEPTH = 4
SAMPLE_CHUNK = 32
HEAD_ROWS = 16

BF16 = jnp.bfloat16
F32 = jnp.float32
NT_DIMS = (((1,), (1,)), ((), ()))


def _t5_bucket(d):
    max_exact = N_BUCKETS // 2
    d_f = jnp.maximum(d, 1).astype(jnp.float32)
    large = max_exact + (jnp.log(d_f / max_exact) / math.log(MAX_DISTANCE / max_exact)
                         * (N_BUCKETS - max_exact)).astype(jnp.int32)
    large = jnp.minimum(large, N_BUCKETS - 1)
    return jnp.where(d < max_exact, d, large)


def _rms(x, g):
    y = x * lax.rsqrt(jnp.mean(x * x, axis=-1, keepdims=True) + EPS)
    return y * g


def _relu_sq_bf16(x):
    r = jnp.maximum(x.astype(BF16), 0.0)
    return r * r


def _sigmoid(x):
    return 1.0 / (1.0 + jnp.exp(-x))


def _bias_from_buckets(bucket, relb_ref, h):
    b = jnp.zeros(bucket.shape, F32)
    for bk in range(N_BUCKETS):
        b = jnp.where(bucket == bk, relb_ref[bk, h], b)
    return b


def _conv_rows(ubuf, cw_ref, r0, rows):
    n = rows + U_CARRY
    strips = []
    for c0 in range(0, CONV_CH, BLOCK):
        win = ubuf[r0:r0 + n, c0:c0 + BLOCK]
        acc = None
        for s in range(8):
            sh = win if s == 0 else pltpu.roll(win, n - s, axis=0)
            for a0 in range(0, U_CARRY + 8, 8):
                w = a0 + s - U_SHIFT
                if 0 <= w < CONV_WIDTH:
                    term = cw_ref[w:w + 1, c0:c0 + BLOCK] * sh[a0:a0 + rows]
                    acc = term if acc is None else acc + term
        strips.append(acc)
    return jnp.concatenate(strips, axis=1)


def _ln_silu(acc, lng, lnb):
    mu = jnp.mean(acc, axis=-1, keepdims=True)
    xc = acc - mu
    y = xc * lax.rsqrt(jnp.mean(xc * xc, axis=-1, keepdims=True) + EPS)
    y = y * lng + lnb
    return y * _sigmoid(y)


def _zero_after(v, prev=None):
    u = pltpu.bitcast(v, jnp.uint32)
    t = prev
    for r0 in range(0, u.shape[0], 8):
        for c0 in range(0, u.shape[1], BLOCK):
            piece = u[r0:r0 + 8, c0:c0 + BLOCK]
            t = piece if t is None else t | piece
    return (t >> 16) >> 16


def _order_after(buf, zero):
    tile = pltpu.bitcast(buf[0:16, 0:BLOCK], jnp.uint32)
    buf[0:16, 0:BLOCK] = pltpu.bitcast(tile | zero, BF16)


def _stream_cast(pairs, stage, sem):
    depth, rows, cols = stage.shape
    blocks = [(src, dst, r, c) for src, dst in pairs
              for r in range(0, src.shape[0], rows) for c in range(0, src.shape[1], cols)]

    def copy(j):
        src, _, r, c = blocks[j]
        return pltpu.make_async_copy(src.at[pl.ds(r, rows), pl.ds(c, cols)],
                                     stage.at[j % depth], sem.at[j % depth])

    for j in range(min(depth - 1, len(blocks))):
        copy(j).start()
    for j, (_, dst, r, c) in enumerate(blocks):
        if j + depth - 1 < len(blocks):
            copy(j + depth - 1).start()
        copy(j).wait()
        dst[r:r + rows, c:c + cols] = stage[j % depth].astype(BF16)


def _prompt_layer_body(*refs, tile):
    i = pl.program_id(0)
    n_tiles = pl.num_programs(0) - 1
    pl.when(i < n_tiles)(functools.partial(_prompt_step, *refs, tile=tile))
    pl.when(i == n_tiles)(functools.partial(_prompt_last_ffn, *refs, tile=tile))


def _prompt_last_ffn(relb_ref, sink_ref, x_ref, meta_ref, g1_ref, win_ref, bucket_ref,
                     cw_ref, cb_ref, lng_ref, lnb_ref, wout_ref, g2_ref, wup_hbm, wdn_hbm, gf_ref,
                     x2s_ref, y_ref, nk_ref, nv_ref, nc_ref, ys_ref,
                     kbuf, vbuf, ubuf, qbuf, mix, bias_s, x2buf, kvlast, hfbuf, hidbuf, ysbuf,
                     wup_ref, wdn_ref, stage, wsem, *, tile):
    xf = x2buf[1 - pl.program_id(0) % 2]
    hf = _rms(xf, g2_ref[...]).astype(BF16)
    for c0 in range(0, D_FF, UP_CHUNK):
        hid = jnp.dot(hf, wup_ref[:, c0:c0 + UP_CHUNK], preferred_element_type=F32)
        hidbuf[:, c0:c0 + UP_CHUNK] = _relu_sq_bf16(hid)
    cols = [xf[:, n0:n0 + DOWN_CHUNK]
            + jnp.dot(hidbuf[...], wdn_ref[:, n0:n0 + DOWN_CHUNK], preferred_element_type=F32)
            for n0 in range(0, D_MODEL, DOWN_CHUNK)]
    y_ref[...] = _rms(jnp.concatenate(cols, axis=1), gf_ref[...])


def _prompt_step(relb_ref, sink_ref, x_ref, meta_ref, g1_ref, win_ref, bucket_ref,
                 cw_ref, cb_ref, lng_ref, lnb_ref, wout_ref, g2_ref, wup_hbm, wdn_hbm, gf_ref,
                 x2s_ref, y_ref, nk_ref, nv_ref, nc_ref, ys_ref,
                 kbuf, vbuf, ubuf, qbuf, mix, bias_s, x2buf, kvlast, hfbuf, hidbuf, ysbuf,
                 wup_ref, wdn_ref, stage, wsem, *, tile):
    i = pl.program_id(0)
    n_tiles = pl.num_programs(0) - 1
    g1 = g1_ref[...]
    slot = i % 2

    @pl.when(i == 0)
    def _init():
        _stream_cast([(wup_hbm, wup_ref), (wdn_hbm, wdn_ref)], stage, wsem)
        nb = x2s_ref.shape[0]
        x2buf[1, 0:nb, :] = x2s_ref[...]
        x2buf[1, nb:tile, :] = jnp.zeros((tile - nb, D_MODEL), F32)
        bucket = bucket_ref[...]
        row = lax.broadcasted_iota(jnp.int32, (BLOCK, 2 * BLOCK), 0)
        col = lax.broadcasted_iota(jnp.int32, (BLOCK, 2 * BLOCK), 1)
        dist = row + BLOCK - col
        band = (dist >= 0) & (dist <= WINDOW)
        band_first = band & (col >= PAD)
        for h in range(N_HEADS):
            b = _bias_from_buckets(bucket, relb_ref, h)
            b = b * LOG2E
            bias_s[0, h] = jnp.where(band, b, -jnp.inf)
            bias_s[1, h] = jnp.where(band_first, b, -jnp.inf)
        hm = _rms(meta_ref[...], g1).astype(BF16)
        pm = jnp.dot(hm, win_ref[:, O_K:], preferred_element_type=F32)
        kbuf[0:PAD, :] = jnp.zeros((PAD, KV_W), BF16)
        vbuf[0:PAD, :] = jnp.zeros((PAD, KV_W), BF16)
        kbuf[PAD:BLOCK, :] = pm[:, 0:KV_W].astype(BF16)
        vbuf[PAD:BLOCK, :] = pm[:, KV_W:2 * KV_W].astype(BF16)
        um = pm[:, 2 * KV_W:2 * KV_W + CONV_CH] * _sigmoid(pm[:, 2 * KV_W + CONV_CH:])
        ubuf[0:U_CARRY - N_META, :] = jnp.zeros((U_CARRY - N_META, CONV_CH), F32)
        ubuf[U_CARRY - N_META:U_CARRY, :] = um

    x = x_ref[...]
    h = _rms(x, g1).astype(BF16)
    q = jnp.dot(h, win_ref[:, 0:ATTN_W], preferred_element_type=F32) * (SCALE * LOG2E)
    qbuf[...] = q.astype(BF16)
    kv = jnp.dot(h, win_ref[:, O_K:O_A], preferred_element_type=F32)
    kbuf[BLOCK:BLOCK + tile, :] = kv[:, 0:KV_W].astype(BF16)
    vbuf[BLOCK:BLOCK + tile, :] = kv[:, KV_W:].astype(BF16)
    kvlast[...] = kv[tile - WINDOW:, :]

    a = jnp.dot(h, win_ref[:, O_A:O_B], preferred_element_type=F32)
    b = jnp.dot(h, win_ref[:, O_B:], preferred_element_type=F32)
    ubuf[U_CARRY:U_CARRY + tile, :] = a * _sigmoid(b)

    cb, lng, lnb = cb_ref[...], lng_ref[...], lnb_ref[...]

    def conv_chunk(r0):
        c = _ln_silu(_conv_rows(ubuf, cw_ref.at[0], r0, CONV_ROWS) + cb, lng, lnb)
        mix[r0:r0 + CONV_ROWS, ATTN_W:] = c.astype(BF16)

    def attn_scores(blk, kvh):
        r0 = blk * BLOCK
        c0 = kvh * HEAD_DIM
        qg = jnp.concatenate(
            [qbuf[r0:r0 + BLOCK, (kvh * GROUP + g) * HEAD_DIM:(kvh * GROUP + g + 1) * HEAD_DIM]
             for g in range(GROUP)], axis=0)
        kk = kbuf[r0:r0 + 2 * BLOCK, c0:c0 + HEAD_DIM]
        return lax.dot_general(qg, kk, NT_DIMS, preferred_element_type=F32)

    def attn_finish(blk, kvh, s):
        r0 = blk * BLOCK
        sel = jnp.where(i == 0, 1, 0) if blk == 0 else 0
        c0 = kvh * HEAD_DIM
        vv = vbuf[r0:r0 + 2 * BLOCK, c0:c0 + HEAD_DIM]
        ps, ls = [], []
        for g in range(GROUP):
            hd = kvh * GROUP + g
            sg = s[g * BLOCK:(g + 1) * BLOCK] + bias_s[sel, hd]
            sk = sink_ref[hd] * LOG2E
            m = jnp.maximum(jnp.max(sg, axis=-1, keepdims=True), sk)
            p = jnp.exp2(sg - m)
            ls.append(jnp.sum(p, axis=-1, keepdims=True) + jnp.exp2(sk - m))
            ps.append(p.astype(BF16))
        o = jnp.dot(jnp.concatenate(ps, axis=0), vv, preferred_element_type=F32)
        for g in range(GROUP):
            hd = kvh * GROUP + g
            og = o[g * BLOCK:(g + 1) * BLOCK] / ls[g]
            mix[r0:r0 + BLOCK, hd * HEAD_DIM:(hd + 1) * HEAD_DIM] = og.astype(BF16)

    conv_starts = list(range(0, tile, CONV_ROWS))
    attn_units = [(blk, kvh) for blk in range(tile // BLOCK) for kvh in range(N_KV_HEADS)]
    n_up, n_down = D_FF // UP_CHUNK, D_MODEL // DOWN_CHUNK
    n_slots = n_up + n_down
    unit_iter, conv_iter = iter(attn_units), iter(conv_starts)
    xf = x2buf[1 - slot]
    hfbuf[...] = _rms(xf, g2_ref[...]).astype(BF16)
    x3_cols = []
    for k in range(n_slots):
        units = [next(unit_iter) for _ in range(SLOT_ATTN_UNITS[k])]
        scores = [(blk, kvh, attn_scores(blk, kvh)) for blk, kvh in units]
        if k < n_up:
            c0 = k * UP_CHUNK
            hid = jnp.dot(hfbuf[...], wup_ref[:, c0:c0 + UP_CHUNK], preferred_element_type=F32)
            hidbuf[:, c0:c0 + UP_CHUNK] = _relu_sq_bf16(hid)
        else:
            n0 = (k - n_up) * DOWN_CHUNK
            x3_cols.append(xf[:, n0:n0 + DOWN_CHUNK]
                           + jnp.dot(hidbuf[...], wdn_ref[:, n0:n0 + DOWN_CHUNK],
                                     preferred_element_type=F32))
        zero = None
        for r0 in [next(conv_iter) for _ in range(SLOT_CONV_CHUNKS[k])]:
            conv_chunk(r0)
            zero = _zero_after(mix[r0:r0 + CONV_ROWS, ATTN_W:], zero)
        for blk, kvh, s in scores:
            attn_finish(blk, kvh, s)
        if k + 1 < n_slots and zero is not None:
            _order_after(hfbuf if k + 1 < n_up else hidbuf, zero)
    acc = jnp.concatenate(x3_cols, axis=1)
    y = _rms(acc, gf_ref[...])
    y_ref[...] = y
    ysbuf[...] = y[0:ysbuf.shape[0]]

    x2buf[slot] = x + jnp.dot(mix[...], wout_ref[...], preferred_element_type=F32)

    @pl.when(i == n_tiles - 1)
    def _new_caches():
        nk_ref[...] = kvlast[:, 0:KV_W].T
        nv_ref[...] = kvlast[:, KV_W:].T
        nc_ref[:, 0, :] = ubuf[U_CARRY + tile - (CONV_WIDTH - 1):U_CARRY + tile, :]

    kbuf[0:BLOCK, :] = kbuf[tile:tile + BLOCK, :]
    vbuf[0:BLOCK, :] = vbuf[tile:tile + BLOCK, :]
    ubuf[0:U_CARRY, :] = ubuf[tile:tile + U_CARRY, :]

    @pl.when(i == 0)
    def _sample_out():
        ys_ref[:, 0, :] = ysbuf[...]


def _const_spec(shape):
    return pl.BlockSpec(shape, lambda i: (0,) * len(shape), pipeline_mode=pl.Buffered(1))


def _smem_spec():
    return pl.BlockSpec(memory_space=pltpu.SMEM)


def _prompt_layer(x, meta, rel_bias, sinks, g1, w_in, bucket, cw, cb, lng, lnb, w_out,
                  g2, w_up, w_down, gf, x2_s):
    seq = x.shape[0]
    nb = x2_s.shape[0]
    tile = PROMPT_TILE
    n_tiles = seq // tile
    body = functools.partial(_prompt_layer_body, tile=tile)
    return pl.pallas_call(
        body,
        grid=(n_tiles + 1,),
        in_specs=[
            _smem_spec(), _smem_spec(),
            pl.BlockSpec((tile, D_MODEL), lambda i: (jnp.minimum(i, n_tiles - 1), 0)),
            _const_spec((N_META, D_MODEL)),
            _const_spec((1, D_MODEL)),
            _const_spec((D_MODEL, IN_W)),
            _const_spec((BLOCK, 2 * BLOCK)),
            _const_spec((1, CONV_WIDTH, CONV_CH)),
            _const_spec((1, CONV_CH)),
            _const_spec((1, CONV_CH)),
            _const_spec((1, CONV_CH)),
            _const_spec((ATTN_W + CONV_CH, D_MODEL)),
            _const_spec((1, D_MODEL)),
            pl.BlockSpec(memory_space=pl.ANY),
            pl.BlockSpec(memory_space=pl.ANY),
            _const_spec((1, D_MODEL)),
            _const_spec((nb, D_MODEL)),
        ],
        out_specs=[
            pl.BlockSpec((tile, D_MODEL), lambda i: (jnp.maximum(i - 1, 0), 0)),
            pl.BlockSpec((KV_W, WINDOW), lambda i: (0, 0)),
            pl.BlockSpec((KV_W, WINDOW), lambda i: (0, 0)),
            pl.BlockSpec((CONV_WIDTH - 1, 1, CONV_CH), lambda i: (0, 0, 0)),
            pl.BlockSpec((nb, 1, D_MODEL), lambda i: (0, 0, 0)),
        ],
        out_shape=[
            jax.ShapeDtypeStruct((seq, D_MODEL), F32),
            jax.ShapeDtypeStruct((KV_W, WINDOW), F32),
            jax.ShapeDtypeStruct((KV_W, WINDOW), F32),
            jax.ShapeDtypeStruct((CONV_WIDTH - 1, 1, CONV_CH), F32),
            jax.ShapeDtypeStruct((nb, 1, D_MODEL), F32),
        ],
        scratch_shapes=[
            pltpu.VMEM((BLOCK + tile, KV_W), BF16),
            pltpu.VMEM((BLOCK + tile, KV_W), BF16),
            pltpu.VMEM((U_CARRY + tile, CONV_CH), F32),
            pltpu.VMEM((tile, ATTN_W), BF16),
            pltpu.VMEM((tile, ATTN_W + CONV_CH), BF16),
            pltpu.VMEM((2, N_HEADS, BLOCK, 2 * BLOCK), F32),
            pltpu.VMEM((2, tile, D_MODEL), F32),
            pltpu.VMEM((WINDOW, 2 * KV_W), F32),
            pltpu.VMEM((tile, D_MODEL), BF16),
            pltpu.VMEM((tile, D_FF), BF16),
            pltpu.VMEM((nb, D_MODEL), F32),
            pltpu.VMEM((D_MODEL, D_FF), BF16),
            pltpu.VMEM((D_FF, D_MODEL), BF16),
            pltpu.VMEM((WEIGHT_STAGE_DEPTH,) + WEIGHT_STAGE_BLOCK, F32),
            pltpu.SemaphoreType.DMA((WEIGHT_STAGE_DEPTH,)),
        ],
        compiler_params=pltpu.CompilerParams(
            dimension_semantics=("arbitrary",), vmem_limit_bytes=V7X_VMEM_LIMIT_BYTES),
        name="prompt_layer",
    )(rel_bias, sinks, x, meta, g1, w_in, bucket, cw, cb, lng, lnb, w_out, g2, w_up, w_down, gf,
      x2_s)


def _sample_mixer_body(relb_ref, sink_ref, x_ref, ckt_ref, cvt_ref, st_ref, g1_ref, win_ref,
                       bucket_ref, cw_ref, cb_ref, lng_ref, lnb_ref, wout_ref,
                       x2_ref, nkt_ref, nvt_ref, nc_ref,
                       pbuf, mix, bias_c, sink_c, *, chunk):
    i = pl.program_id(0)
    last = pl.num_programs(0) - 1
    rows_h = chunk * HEAD_ROWS

    @pl.when(i == 0)
    def _init():
        h = _rms(x_ref[:, 0, :], g1_ref[...]).astype(BF16)
        pbuf[...] = jnp.dot(h, win_ref[...], preferred_element_type=F32)
        bucket = bucket_ref[...]
        rid = lax.broadcasted_iota(jnp.int32, (HEAD_ROWS, 1), 0)
        bias = jnp.zeros((HEAD_ROWS, 2 * BLOCK), F32)
        sk = jnp.zeros((HEAD_ROWS, 1), F32)
        for hd in range(N_HEADS):
            bias = jnp.where(rid == hd, _bias_from_buckets(bucket, relb_ref, hd), bias)
            sk = jnp.where(rid == hd, sink_ref[hd], sk)
        bias_c[...] = bias
        sink_c[...] = sk

    r0 = pl.multiple_of(i * chunk, chunk)
    pr = pbuf[pl.ds(r0, chunk), :]
    q = pr[:, 0:ATTN_W] * SCALE
    knew = pr[:, O_K:O_V]
    vnew = pr[:, O_V:O_A]
    unew = pr[:, O_A:O_B] * _sigmoid(pr[:, O_B:])

    def per_head(t):
        n = t.shape[-1]
        return jnp.broadcast_to(t[:, None, :], (chunk, HEAD_ROWS, n)).reshape(rows_h, n)

    hid = lax.broadcasted_iota(jnp.int32, (rows_h, 1), 0) % HEAD_ROWS
    lane = lax.broadcasted_iota(jnp.int32, (1, BLOCK), 1)
    qrep = per_head(q)
    qsum = jnp.zeros((rows_h, BLOCK), F32)
    for c in range(ATTN_W // BLOCK):
        piece = qrep[:, c * BLOCK:(c + 1) * BLOCK]
        in_head = (lane // HEAD_DIM + 2 * c) == hid
        qsum = qsum + jnp.where(in_head, piece, 0.0)
    keep = (hid % 2) == (hid // GROUP)
    qf = jnp.where(keep, qsum, pltpu.roll(qsum, HEAD_DIM, axis=1))
    qf_b = qf.astype(BF16)

    bias = jnp.broadcast_to(bias_c[...][None], (chunk, HEAD_ROWS, 2 * BLOCK)).reshape(rows_h, 2 * BLOCK)
    sk = jnp.broadcast_to(sink_c[...][None], (chunk, HEAD_ROWS, 1)).reshape(rows_h, 1)

    s_rows = []
    for b in range(chunk):
        kt = ckt_ref[b].astype(BF16)
        s_rows.append(jnp.dot(qf_b[b * HEAD_ROWS:(b + 1) * HEAD_ROWS], kt, preferred_element_type=F32))
    s_c = jnp.concatenate(s_rows, axis=0) + bias[:, 0:BLOCK]
    s_n = jnp.sum(qf * per_head(knew), axis=-1, keepdims=True) + bias[:, BLOCK:BLOCK + 1]
    m = jnp.maximum(jnp.maximum(jnp.max(s_c, axis=-1, keepdims=True), s_n), sk)
    p_c = jnp.exp(s_c - m)
    p_n = jnp.exp(s_n - m)
    l = jnp.sum(p_c, axis=-1, keepdims=True) + p_n + jnp.exp(sk - m)
    p_cb = p_c.astype(BF16)
    o_rows = []
    for b in range(chunk):
        vt = cvt_ref[b].astype(BF16)
        o_rows.append(lax.dot_general(p_cb[b * HEAD_ROWS:(b + 1) * HEAD_ROWS], vt, NT_DIMS,
                                      preferred_element_type=F32))
    o = (jnp.concatenate(o_rows, axis=0) + p_n * per_head(vnew)) / l
    o = jnp.where(keep, o, pltpu.roll(o, HEAD_DIM, axis=1))
    o = jnp.where((lane // HEAD_DIM) == (hid % 2), o, 0.0)
    wide = jnp.concatenate([jnp.where(hid // 2 == c, o, 0.0) for c in range(ATTN_W // BLOCK)], axis=1)
    gi = lax.broadcasted_iota(jnp.int32, (chunk, rows_h), 0)
    gj = lax.broadcasted_iota(jnp.int32, (chunk, rows_h), 1)
    gather = jnp.where(gj // HEAD_ROWS == gi, 1.0, 0.0).astype(BF16)
    ao = jnp.dot(gather, wide.astype(BF16), preferred_element_type=F32)
    mix[pl.ds(r0, chunk), 0:ATTN_W] = ao

    pad = jnp.zeros((BLOCK - chunk, KV_W), F32)
    knew_t = jnp.concatenate([knew, pad], axis=0).T
    vnew_t = jnp.concatenate([vnew, pad], axis=0).T
    newest = lane == WINDOW - 1
    for b in range(chunk):
        kcol = jnp.broadcast_to(knew_t[:, b:b + 1], (KV_W, WINDOW))
        vcol = jnp.broadcast_to(vnew_t[:, b:b + 1], (KV_W, WINDOW))
        nkt_ref[b] = jnp.where(newest, kcol, pltpu.roll(ckt_ref[b], WINDOW - 1, axis=1))
        nvt_ref[b] = jnp.where(newest, vcol, pltpu.roll(cvt_ref[b], WINDOW - 1, axis=1))

    acc = cb_ref[...] + cw_ref[0, CONV_WIDTH - 1:CONV_WIDTH, :] * unew
    for w in range(CONV_WIDTH - 1):
        acc = acc + cw_ref[0, w:w + 1, :] * st_ref[w]
    nc_ref[0:CONV_WIDTH - 2] = st_ref[1:CONV_WIDTH - 1]
    nc_ref[CONV_WIDTH - 2] = unew
    mu = jnp.mean(acc, axis=-1, keepdims=True)
    xc = acc - mu
    y = xc * lax.rsqrt(jnp.mean(xc * xc, axis=-1, keepdims=True) + EPS)
    y = y * lng_ref[...] + lnb_ref[...]
    mix[pl.ds(r0, chunk), ATTN_W:] = y * _sigmoid(y)

    @pl.when(i == last)
    def _out():
        x2_ref[...] = x_ref[:, 0, :] + jnp.dot(mix[...].astype(BF16), wout_ref[...],
                                           preferred_element_type=F32)


def _sample_mixer(x, ckt, cvt, st, rel_bias, sinks, g1, w_in, bucket, cw, cb, lng, lnb, w_out):
    nb = x.shape[0]
    chunk = SAMPLE_CHUNK
    body = functools.partial(_sample_mixer_body, chunk=chunk)
    cache_spec = pl.BlockSpec((chunk, KV_W, WINDOW), lambda i: (i, 0, 0))
    state_spec = pl.BlockSpec((CONV_WIDTH - 1, chunk, CONV_CH), lambda i: (0, i, 0))
    return pl.pallas_call(
        body,
        grid=(nb // chunk,),
        in_specs=[
            _smem_spec(), _smem_spec(),
            _const_spec((nb, 1, D_MODEL)),
            cache_spec, cache_spec, state_spec,
            _const_spec((1, D_MODEL)),
            _const_spec((D_MODEL, IN_W)),
            _const_spec((1, 2 * BLOCK)),
            _const_spec((1, CONV_WIDTH, CONV_CH)),
            _const_spec((1, CONV_CH)),
            _const_spec((1, CONV_CH)),
            _const_spec((1, CONV_CH)),
            _const_spec((ATTN_W + CONV_CH, D_MODEL)),
        ],
        out_specs=[
            pl.BlockSpec((nb, D_MODEL), lambda i: (0, 0)),
            cache_spec, cache_spec, state_spec,
        ],
        out_shape=[
            jax.ShapeDtypeStruct((nb, D_MODEL), F32),
            jax.ShapeDtypeStruct((nb, KV_W, WINDOW), F32),
            jax.ShapeDtypeStruct((nb, KV_W, WINDOW), F32),
            jax.ShapeDtypeStruct((CONV_WIDTH - 1, nb, CONV_CH), F32),
        ],
        scratch_shapes=[
            pltpu.VMEM((nb, IN_W), F32),
            pltpu.VMEM((nb, ATTN_W + CONV_CH), F32),
            pltpu.VMEM((HEAD_ROWS, 2 * BLOCK), F32),
            pltpu.VMEM((HEAD_ROWS, 1), F32),
        ],
        compiler_params=pltpu.CompilerParams(
            dimension_semantics=("arbitrary",), vmem_limit_bytes=V7X_VMEM_LIMIT_BYTES),
        name="sample_mixer",
    )(rel_bias, sinks, x, ckt, cvt, st, g1, w_in, bucket, cw, cb, lng, lnb, w_out)


def kernel(x_prompt, x_sample, cache_k, cache_v, state_conv, meta_tokens, rel_bias, norm1_g, w_in,
           attn_sinks, conv_w, conv_b, conv_ln_g, conv_ln_b, w_out, norm2_g, w_up, w_down, norm_f_g):
    batch, seq, _ = x_prompt.shape
    nb, dec_seq, _ = x_sample.shape
    assert batch == 1 and dec_seq == 1 and w_in.shape[0] == 1
    assert seq % PROMPT_TILE == 0 and nb % SAMPLE_CHUNK == 0 and nb <= PROMPT_TILE
    assert len(SLOT_ATTN_UNITS) == len(SLOT_CONV_CHUNKS) == D_FF // UP_CHUNK + D_MODEL // DOWN_CHUNK
    assert sum(SLOT_ATTN_UNITS) == (PROMPT_TILE // BLOCK) * N_KV_HEADS
    assert sum(SLOT_CONV_CHUNKS) == PROMPT_TILE // CONV_ROWS

    w_in_b = w_in[0].astype(BF16)
    w_out_b = w_out[0].astype(BF16)
    g1 = norm1_g[0][None]
    g2 = norm2_g[0][None]
    gf = norm_f_g[None]
    cw, cb = conv_w, conv_b[0][None]
    lng, lnb = conv_ln_g[0][None], conv_ln_b[0][None]
    sinks = attn_sinks[0]

    dist_p = jnp.arange(BLOCK)[:, None] + BLOCK - jnp.arange(2 * BLOCK)[None, :]
    bucket_p = _t5_bucket(jnp.clip(dist_p, 0, WINDOW)).astype(jnp.int32)
    lane = jnp.arange(2 * BLOCK)
    dist_s = jnp.where(lane < WINDOW, WINDOW - lane, 0)
    bucket_s = jnp.where(lane <= WINDOW, _t5_bucket(jnp.clip(dist_s, 0, WINDOW)), -1)
    bucket_s = bucket_s.astype(jnp.int32)[None]

    ckt = jnp.transpose(cache_k[0], (0, 2, 3, 1)).reshape(nb, KV_W, WINDOW)
    cvt = jnp.transpose(cache_v[0], (0, 2, 3, 1)).reshape(nb, KV_W, WINDOW)
    st = jnp.transpose(state_conv[0], (1, 0, 2))
    x2_s, nkt_s, nvt_s, nct_s = _sample_mixer(x_sample, ckt, cvt, st, rel_bias, sinks, g1,
                                              w_in_b, bucket_s, cw, cb, lng, lnb, w_out_b)
    nk_s = jnp.transpose(nkt_s.reshape(nb, N_KV_HEADS, HEAD_DIM, WINDOW), (0, 3, 1, 2))
    nv_s = jnp.transpose(nvt_s.reshape(nb, N_KV_HEADS, HEAD_DIM, WINDOW), (0, 3, 1, 2))
    nc_s = jnp.transpose(nct_s, (1, 0, 2))
    y_p, nkt_p, nvt_p, nc_p, y_s = _prompt_layer(x_prompt[0], meta_tokens, rel_bias, sinks, g1, w_in_b,
                                               bucket_p, cw, cb, lng, lnb, w_out_b, g2, w_up[0],
                                               w_down[0], gf, x2_s)

    def to_cache(t):
        return jnp.transpose(t.reshape(N_KV_HEADS, HEAD_DIM, WINDOW), (2, 0, 1))[None, None]

    return (y_p[None], y_s,
            to_cache(nkt_p), to_cache(nvt_p), jnp.transpose(nc_p, (1, 0, 2))[None],
            nk_s[None], nv_s[None], nc_s[None])
```

```python
import functools
import math

import jax
import jax.numpy as jnp
from jax import lax
from jax.experimental import pallas as pl
from jax.experimental.pallas import tpu as pltpu

D_MODEL = 1024
N_HEADS = 8
N_KV_HEADS = 2
HEAD_DIM = 64
GROUP = N_HEADS // N_KV_HEADS
ATTN_W = N_HEADS * HEAD_DIM
KV_W = N_KV_HEADS * HEAD_DIM
CONV_CH = D_MODEL - ATTN_W
IN_W = ATTN_W + 2 * KV_W + 2 * CONV_CH
CONV_WIDTH = 31
WINDOW = 128
BLOCK = 128
N_BUCKETS = 32
MAX_DISTANCE = WINDOW
N_META = 16
D_FF = 4 * D_MODEL
EPS = 1e-6
SCALE = HEAD_DIM ** -0.5
LOG2E = math.log2(math.e)

O_K = ATTN_W
O_V = ATTN_W + KV_W
O_A = ATTN_W + 2 * KV_W
O_B = O_A + CONV_CH

PAD = (-N_META) % BLOCK
U_CARRY = 32
U_SHIFT = U_CARRY - (CONV_WIDTH - 1)

V7X_VMEM_LIMIT_BYTES = 60 * 1024 * 1024

PROMPT_TILE = 512
UP_CHUNK = 1024
DOWN_CHUNK = 512
SLOT_ATTN_UNITS = (2, 2, 2, 2, 0, 0)
SLOT_CONV_CHUNKS = (0, 0, 0, 0, 4, 4)
CONV_ROWS = 64
WEIGHT_STAGE_BLOCK = (256, 1024)
WEIGHT_STAGE_DEPTH = 4
SAMPLE_CHUNK = 32
HEAD_ROWS = 16

BF16 = jnp.bfloat16
F32 = jnp.float32
NT_DIMS = (((1,), (1,)), ((), ()))


def _t5_bucket(d):
    max_exact = N_BUCKETS // 2
    d_f = jnp.maximum(d, 1).astype(jnp.float32)
    large = max_exact + (jnp.log(d_f / max_exact) / math.log(MAX_DISTANCE / max_exact)
                         * (N_BUCKETS - max_exact)).astype(jnp.int32)
    large = jnp.minimum(large, N_BUCKETS - 1)
    return jnp.where(d < max_exact, d, large)


def _rms(x, g):
    y = x * lax.rsqrt(jnp.mean(x * x, axis=-1, keepdims=True) + EPS)
    return y * g


def _relu_sq_bf16(x):
    r = jnp.maximum(x.astype(BF16), 0.0)
    return r * r


def _sigmoid(x):
    return 1.0 / (1.0 + jnp.exp(-x))


def _bias_from_buckets(bucket, relb_ref, h):
    b = jnp.zeros(bucket.shape, F32)
    for bk in range(N_BUCKETS):
        b = jnp.where(bucket == bk, relb_ref[bk, h], b)
    return b


def _conv_rows(ubuf, cw_ref, r0, rows):
    n = rows + U_CARRY
    strips = []
    for c0 in range(0, CONV_CH, BLOCK):
        win = ubuf[r0:r0 + n, c0:c0 + BLOCK]
        acc = None
        for s in range(8):
            sh = win if s == 0 else pltpu.roll(win, n - s, axis=0)
            for a0 in range(0, U_CARRY + 8, 8):
                w = a0 + s - U_SHIFT
                if 0 <= w < CONV_WIDTH:
                    term = cw_ref[w:w + 1, c0:c0 + BLOCK] * sh[a0:a0 + rows]
                    acc = term if acc is None else acc + term
        strips.append(acc)
    return jnp.concatenate(strips, axis=1)


def _ln_silu(acc, lng, lnb):
    mu = jnp.mean(acc, axis=-1, keepdims=True)
    xc = acc - mu
    y = xc * lax.rsqrt(jnp.mean(xc * xc, axis=-1, keepdims=True) + EPS)
    y = y * lng + lnb
    return y * _sigmoid(y)


def _zero_after(v, prev=None):
    u = pltpu.bitcast(v, jnp.uint32)
    t = prev
    for r0 in range(0, u.shape[0], 8):
        for c0 in range(0, u.shape[1], BLOCK):
            piece = u[r0:r0 + 8, c0:c0 + BLOCK]
            t = piece if t is None else t | piece
    return (t >> 16) >> 16


def _order_after(buf, zero):
    tile = pltpu.bitcast(buf[0:16, 0:BLOCK], jnp.uint32)
    buf[0:16, 0:BLOCK] = pltpu.bitcast(tile | zero, BF16)


def _stream_cast(pairs, stage, sem, fillers=()):
    fillers = list(fillers)
    depth, rows, cols = stage.shape
    blocks = [(src, dst, r, c) for src, dst in pairs
              for r in range(0, src.shape[0], rows) for c in range(0, src.shape[1], cols)]

    def copy(j):
        src, _, r, c = blocks[j]
        return pltpu.make_async_copy(src.at[pl.ds(r, rows), pl.ds(c, cols)],
                                     stage.at[j % depth], sem.at[j % depth])

    for j in range(min(depth - 1, len(blocks))):
        copy(j).start()
    for j, (_, dst, r, c) in enumerate(blocks):
        if j + depth - 1 < len(blocks):
            copy(j + depth - 1).start()
        if fillers:
            fillers.pop(0)()
        copy(j).wait()
        dst[r:r + rows, c:c + cols] = stage[j % depth].astype(BF16)
    for filler in fillers:
        filler()


def _prompt_layer_body(*refs, tile):
    i = pl.program_id(0)
    n_tiles = pl.num_programs(0) - 1
    pl.when(i < n_tiles)(functools.partial(_prompt_step, *refs, tile=tile))
    pl.when(i == n_tiles)(functools.partial(_prompt_last_ffn, *refs, tile=tile))


def _prompt_last_ffn(relb_ref, sink_ref, x_ref, meta_ref, g1_ref, win_ref, bucket_ref,
                     cw_ref, cb_ref, lng_ref, lnb_ref, wout_ref, g2_ref, wup_hbm, wdn_hbm, gf_ref,
                     x2s_ref, y_ref, nk_ref, nv_ref, nc_ref, ys_ref,
                     kbuf, vbuf, ubuf, qbuf, mix, bias_s, x2buf, kvlast, hfbuf, hidbuf, ysbuf,
                     wup_ref, wdn_ref, stage, wsem, *, tile):
    xf = x2buf[1 - pl.program_id(0) % 2]
    hf = _rms(xf, g2_ref[...]).astype(BF16)
    for c0 in range(0, D_FF, UP_CHUNK):
        hid = jnp.dot(hf, wup_ref[:, c0:c0 + UP_CHUNK], preferred_element_type=F32)
        hidbuf[:, c0:c0 + UP_CHUNK] = _relu_sq_bf16(hid)
    cols = [xf[:, n0:n0 + DOWN_CHUNK]
            + jnp.dot(hidbuf[...], wdn_ref[:, n0:n0 + DOWN_CHUNK], preferred_element_type=F32)
            for n0 in range(0, D_MODEL, DOWN_CHUNK)]
    y_ref[...] = _rms(jnp.concatenate(cols, axis=1), gf_ref[...])


def _prompt_step(relb_ref, sink_ref, x_ref, meta_ref, g1_ref, win_ref, bucket_ref,
                 cw_ref, cb_ref, lng_ref, lnb_ref, wout_ref, g2_ref, wup_hbm, wdn_hbm, gf_ref,
                 x2s_ref, y_ref, nk_ref, nv_ref, nc_ref, ys_ref,
                 kbuf, vbuf, ubuf, qbuf, mix, bias_s, x2buf, kvlast, hfbuf, hidbuf, ysbuf,
                 wup_ref, wdn_ref, stage, wsem, *, tile):
    i = pl.program_id(0)
    n_tiles = pl.num_programs(0) - 1
    g1 = g1_ref[...]
    slot = i % 2

    @pl.when(i == 0)
    def _init():
        def decode_rows():
            nb = x2s_ref.shape[0]
            x2buf[1, 0:nb, :] = x2s_ref[...]
            x2buf[1, nb:tile, :] = jnp.zeros((tile - nb, D_MODEL), F32)

        def bias_table(h):
            bucket = bucket_ref[...]
            row = lax.broadcasted_iota(jnp.int32, (BLOCK, 2 * BLOCK), 0)
            col = lax.broadcasted_iota(jnp.int32, (BLOCK, 2 * BLOCK), 1)
            dist = row + BLOCK - col
            band = (dist >= 0) & (dist <= WINDOW)
            band_first = band & (col >= PAD)
            b = _bias_from_buckets(bucket, relb_ref, h) * LOG2E
            bias_s[0, h] = jnp.where(band, b, -jnp.inf)
            bias_s[1, h] = jnp.where(band_first, b, -jnp.inf)

        _stream_cast([(wup_hbm, wup_ref), (wdn_hbm, wdn_ref)], stage, wsem,
                     fillers=[decode_rows] + [functools.partial(bias_table, h) for h in range(N_HEADS)])
        hm = _rms(meta_ref[...], g1).astype(BF16)
        pm = jnp.dot(hm, win_ref[:, O_K:], preferred_element_type=F32)
        kbuf[0:PAD, :] = jnp.zeros((PAD, KV_W), BF16)
        vbuf[0:PAD, :] = jnp.zeros((PAD, KV_W), BF16)
        kbuf[PAD:BLOCK, :] = pm[:, 0:KV_W].astype(BF16)
        vbuf[PAD:BLOCK, :] = pm[:, KV_W:2 * KV_W].astype(BF16)
        um = pm[:, 2 * KV_W:2 * KV_W + CONV_CH] * _sigmoid(pm[:, 2 * KV_W + CONV_CH:])
        ubuf[0:U_CARRY - N_META, :] = jnp.zeros((U_CARRY - N_META, CONV_CH), F32)
        ubuf[U_CARRY - N_META:U_CARRY, :] = um

    x = x_ref[...]
    h = _rms(x, g1).astype(BF16)
    q = jnp.dot(h, win_ref[:, 0:ATTN_W], preferred_element_type=F32) * (SCALE * LOG2E)
    qbuf[...] = q.astype(BF16)
    kv = jnp.dot(h, win_ref[:, O_K:O_A], preferred_element_type=F32)
    kbuf[BLOCK:BLOCK + tile, :] = kv[:, 0:KV_W].astype(BF16)
    vbuf[BLOCK:BLOCK + tile, :] = kv[:, KV_W:].astype(BF16)
    kvlast[...] = kv[tile - WINDOW:, :]

    a = jnp.dot(h, win_ref[:, O_A:O_B], preferred_element_type=F32)
    b = jnp.dot(h, win_ref[:, O_B:], preferred_element_type=F32)
    ubuf[U_CARRY:U_CARRY + tile, :] = a * _sigmoid(b)

    cb, lng, lnb = cb_ref[...], lng_ref[...], lnb_ref[...]

    def conv_chunk(r0):
        c = _ln_silu(_conv_rows(ubuf, cw_ref.at[0], r0, CONV_ROWS) + cb, lng, lnb)
        mix[r0:r0 + CONV_ROWS, ATTN_W:] = c.astype(BF16)

    def attn_scores(blk, kvh):
        r0 = blk * BLOCK
        c0 = kvh * HEAD_DIM
        qg = jnp.concatenate(
            [qbuf[r0:r0 + BLOCK, (kvh * GROUP + g) * HEAD_DIM:(kvh * GROUP + g + 1) * HEAD_DIM]
             for g in range(GROUP)], axis=0)
        kk = kbuf[r0:r0 + 2 * BLOCK, c0:c0 + HEAD_DIM]
        return lax.dot_general(qg, kk, NT_DIMS, preferred_element_type=F32)

    def attn_finish(blk, kvh, s):
        r0 = blk * BLOCK
        sel = jnp.where(i == 0, 1, 0) if blk == 0 else 0
        c0 = kvh * HEAD_DIM
        vv = vbuf[r0:r0 + 2 * BLOCK, c0:c0 + HEAD_DIM]
        ps, ls = [], []
        for g in range(GROUP):
            hd = kvh * GROUP + g
            sg = s[g * BLOCK:(g + 1) * BLOCK] + bias_s[sel, hd]
            sk = sink_ref[hd] * LOG2E
            m = jnp.maximum(jnp.max(sg, axis=-1, keepdims=True), sk)
            p = jnp.exp2(sg - m)
            ls.append(jnp.sum(p, axis=-1, keepdims=True) + jnp.exp2(sk - m))
            ps.append(p.astype(BF16))
        o = jnp.dot(jnp.concatenate(ps, axis=0), vv, preferred_element_type=F32)
        for g in range(GROUP):
            hd = kvh * GROUP + g
            og = o[g * BLOCK:(g + 1) * BLOCK] / ls[g]
            mix[r0:r0 + BLOCK, hd * HEAD_DIM:(hd + 1) * HEAD_DIM] = og.astype(BF16)

    conv_starts = list(range(0, tile, CONV_ROWS))
    attn_units = [(blk, kvh) for blk in range(tile // BLOCK) for kvh in range(N_KV_HEADS)]
    n_up, n_down = D_FF // UP_CHUNK, D_MODEL // DOWN_CHUNK
    n_slots = n_up + n_down
    unit_iter, conv_iter = iter(attn_units), iter(conv_starts)
    xf = x2buf[1 - slot]
    hfbuf[...] = _rms(xf, g2_ref[...]).astype(BF16)
    x3_cols = []
    for k in range(n_slots):
        units = [next(unit_iter) for _ in range(SLOT_ATTN_UNITS[k])]
        scores = [(blk, kvh, attn_scores(blk, kvh)) for blk, kvh in units]
        if k < n_up:
            c0 = k * UP_CHUNK
            hid = jnp.dot(hfbuf[...], wup_ref[:, c0:c0 + UP_CHUNK], preferred_element_type=F32)
            hidbuf[:, c0:c0 + UP_CHUNK] = _relu_sq_bf16(hid)
        else:
            n0 = (k - n_up) * DOWN_CHUNK
            x3_cols.append(xf[:, n0:n0 + DOWN_CHUNK]
                           + jnp.dot(hidbuf[...], wdn_ref[:, n0:n0 + DOWN_CHUNK],
                                     preferred_element_type=F32))
        zero = None
        for r0 in [next(conv_iter) for _ in range(SLOT_CONV_CHUNKS[k])]:
            conv_chunk(r0)
            zero = _zero_after(mix[r0:r0 + CONV_ROWS, ATTN_W:], zero)
        for blk, kvh, s in scores:
            attn_finish(blk, kvh, s)
        if k + 1 < n_slots and zero is not None:
            _order_after(hfbuf if k + 1 < n_up else hidbuf, zero)
    acc = jnp.concatenate(x3_cols, axis=1)
    y = _rms(acc, gf_ref[...])
    y_ref[...] = y
    ysbuf[...] = y[0:ysbuf.shape[0]]

    x2buf[slot] = x + jnp.dot(mix[...], wout_ref[...], preferred_element_type=F32)

    @pl.when(i == n_tiles - 1)
    def _new_caches():
        nk_ref[...] = kvlast[:, 0:KV_W].T
        nv_ref[...] = kvlast[:, KV_W:].T
        nc_ref[:, 0, :] = ubuf[U_CARRY + tile - (CONV_WIDTH - 1):U_CARRY + tile, :]

    kbuf[0:BLOCK, :] = kbuf[tile:tile + BLOCK, :]
    vbuf[0:BLOCK, :] = vbuf[tile:tile + BLOCK, :]
    ubuf[0:U_CARRY, :] = ubuf[tile:tile + U_CARRY, :]

    @pl.when(i == 0)
    def _sample_out():
        ys_ref[:, 0, :] = ysbuf[...]


def _const_spec(shape):
    return pl.BlockSpec(shape, lambda i: (0,) * len(shape), pipeline_mode=pl.Buffered(1))


def _smem_spec():
    return pl.BlockSpec(memory_space=pltpu.SMEM)


def _prompt_layer(x, meta, rel_bias, sinks, g1, w_in, bucket, cw, cb, lng, lnb, w_out,
                  g2, w_up, w_down, gf, x2_s):
    seq = x.shape[0]
    nb = x2_s.shape[0]
    tile = PROMPT_TILE
    n_tiles = seq // tile
    body = functools.partial(_prompt_layer_body, tile=tile)
    return pl.pallas_call(
        body,
        grid=(n_tiles + 1,),
        in_specs=[
            _smem_spec(), _smem_spec(),
            pl.BlockSpec((tile, D_MODEL), lambda i: (jnp.minimum(i, n_tiles - 1), 0)),
            _const_spec((N_META, D_MODEL)),
            _const_spec((1, D_MODEL)),
            _const_spec((D_MODEL, IN_W)),
            _const_spec((BLOCK, 2 * BLOCK)),
            _const_spec((1, CONV_WIDTH, CONV_CH)),
            _const_spec((1, CONV_CH)),
            _const_spec((1, CONV_CH)),
            _const_spec((1, CONV_CH)),
            _const_spec((ATTN_W + CONV_CH, D_MODEL)),
            _const_spec((1, D_MODEL)),
            pl.BlockSpec(memory_space=pl.ANY),
            pl.BlockSpec(memory_space=pl.ANY),
            _const_spec((1, D_MODEL)),
            _const_spec((nb, D_MODEL)),
        ],
        out_specs=[
            pl.BlockSpec((tile, D_MODEL), lambda i: (jnp.maximum(i - 1, 0), 0)),
            pl.BlockSpec((KV_W, WINDOW), lambda i: (0, 0)),
            pl.BlockSpec((KV_W, WINDOW), lambda i: (0, 0)),
            pl.BlockSpec((CONV_WIDTH - 1, 1, CONV_CH), lambda i: (0, 0, 0)),
            pl.BlockSpec((nb, 1, D_MODEL), lambda i: (0, 0, 0)),
        ],
        out_shape=[
            jax.ShapeDtypeStruct((seq, D_MODEL), F32),
            jax.ShapeDtypeStruct((KV_W, WINDOW), F32),
            jax.ShapeDtypeStruct((KV_W, WINDOW), F32),
            jax.ShapeDtypeStruct((CONV_WIDTH - 1, 1, CONV_CH), F32),
            jax.ShapeDtypeStruct((nb, 1, D_MODEL), F32),
        ],
        scratch_shapes=[
            pltpu.VMEM((BLOCK + tile, KV_W), BF16),
            pltpu.VMEM((BLOCK + tile, KV_W), BF16),
            pltpu.VMEM((U_CARRY + tile, CONV_CH), F32),
            pltpu.VMEM((tile, ATTN_W), BF16),
            pltpu.VMEM((tile, ATTN_W + CONV_CH), BF16),
            pltpu.VMEM((2, N_HEADS, BLOCK, 2 * BLOCK), F32),
            pltpu.VMEM((2, tile, D_MODEL), F32),
            pltpu.VMEM((WINDOW, 2 * KV_W), F32),
            pltpu.VMEM((tile, D_MODEL), BF16),
            pltpu.VMEM((tile, D_FF), BF16),
            pltpu.VMEM((nb, D_MODEL), F32),
            pltpu.VMEM((D_MODEL, D_FF), BF16),
            pltpu.VMEM((D_FF, D_MODEL), BF16),
            pltpu.VMEM((WEIGHT_STAGE_DEPTH,) + WEIGHT_STAGE_BLOCK, F32),
            pltpu.SemaphoreType.DMA((WEIGHT_STAGE_DEPTH,)),
        ],
        compiler_params=pltpu.CompilerParams(
            dimension_semantics=("arbitrary",), vmem_limit_bytes=V7X_VMEM_LIMIT_BYTES),
        name="prompt_layer",
    )(rel_bias, sinks, x, meta, g1, w_in, bucket, cw, cb, lng, lnb, w_out, g2, w_up, w_down, gf,
      x2_s)


def _sample_mixer_body(relb_ref, sink_ref, x_ref, ckt_ref, cvt_ref, st_ref, g1_ref, win_ref,
                       bucket_ref, cw_ref, cb_ref, lng_ref, lnb_ref, wout_ref,
                       x2_ref, nkt_ref, nvt_ref, nc_ref,
                       pbuf, mix, bias_c, sink_c, *, chunk):
    i = pl.program_id(0)
    last = pl.num_programs(0) - 1
    rows_h = chunk * HEAD_ROWS

    @pl.when(i == 0)
    def _init():
        h = _rms(x_ref[:, 0, :], g1_ref[...]).astype(BF16)
        pbuf[...] = jnp.dot(h, win_ref[...], preferred_element_type=F32)
        bucket = bucket_ref[...]
        rid = lax.broadcasted_iota(jnp.int32, (HEAD_ROWS, 1), 0)
        bias = jnp.zeros((HEAD_ROWS, 2 * BLOCK), F32)
        sk = jnp.zeros((HEAD_ROWS, 1), F32)
        for hd in range(N_HEADS):
            bias = jnp.where(rid == hd, _bias_from_buckets(bucket, relb_ref, hd), bias)
            sk = jnp.where(rid == hd, sink_ref[hd], sk)
        bias_c[...] = bias
        sink_c[...] = sk

    r0 = pl.multiple_of(i * chunk, chunk)
    pr = pbuf[pl.ds(r0, chunk), :]
    q = pr[:, 0:ATTN_W] * SCALE
    knew = pr[:, O_K:O_V]
    vnew = pr[:, O_V:O_A]
    unew = pr[:, O_A:O_B] * _sigmoid(pr[:, O_B:])

    def per_head(t):
        n = t.shape[-1]
        return jnp.broadcast_to(t[:, None, :], (chunk, HEAD_ROWS, n)).reshape(rows_h, n)

    hid = lax.broadcasted_iota(jnp.int32, (rows_h, 1), 0) % HEAD_ROWS
    lane = lax.broadcasted_iota(jnp.int32, (1, BLOCK), 1)
    qrep = per_head(q)
    qsum = jnp.zeros((rows_h, BLOCK), F32)
    for c in range(ATTN_W // BLOCK):
        piece = qrep[:, c * BLOCK:(c + 1) * BLOCK]
        in_head = (lane // HEAD_DIM + 2 * c) == hid
        qsum = qsum + jnp.where(in_head, piece, 0.0)
    keep = (hid % 2) == (hid // GROUP)
    qf = jnp.where(keep, qsum, pltpu.roll(qsum, HEAD_DIM, axis=1))
    qf_b = qf.astype(BF16)

    bias = jnp.broadcast_to(bias_c[...][None], (chunk, HEAD_ROWS, 2 * BLOCK)).reshape(rows_h, 2 * BLOCK)
    sk = jnp.broadcast_to(sink_c[...][None], (chunk, HEAD_ROWS, 1)).reshape(rows_h, 1)

    s_rows = []
    for b in range(chunk):
        kt = ckt_ref[b].astype(BF16)
        s_rows.append(jnp.dot(qf_b[b * HEAD_ROWS:(b + 1) * HEAD_ROWS], kt, preferred_element_type=F32))
    s_c = jnp.concatenate(s_rows, axis=0) + bias[:, 0:BLOCK]
    s_n = jnp.sum(qf * per_head(knew), axis=-1, keepdims=True) + bias[:, BLOCK:BLOCK + 1]
    m = jnp.maximum(jnp.maximum(jnp.max(s_c, axis=-1, keepdims=True), s_n), sk)
    p_c = jnp.exp(s_c - m)
    p_n = jnp.exp(s_n - m)
    l = jnp.sum(p_c, axis=-1, keepdims=True) + p_n + jnp.exp(sk - m)
    p_cb = p_c.astype(BF16)
    o_rows = []
    for b in range(chunk):
        vt = cvt_ref[b].astype(BF16)
        o_rows.append(lax.dot_general(p_cb[b * HEAD_ROWS:(b + 1) * HEAD_ROWS], vt, NT_DIMS,
                                      preferred_element_type=F32))
    o = (jnp.concatenate(o_rows, axis=0) + p_n * per_head(vnew)) / l
    o = jnp.where(keep, o, pltpu.roll(o, HEAD_DIM, axis=1))
    o = jnp.where((lane // HEAD_DIM) == (hid % 2), o, 0.0)
    wide = jnp.concatenate([jnp.where(hid // 2 == c, o, 0.0) for c in range(ATTN_W // BLOCK)], axis=1)
    gi = lax.broadcasted_iota(jnp.int32, (chunk, rows_h), 0)
    gj = lax.broadcasted_iota(jnp.int32, (chunk, rows_h), 1)
    gather = jnp.where(gj // HEAD_ROWS == gi, 1.0, 0.0).astype(BF16)
    ao = jnp.dot(gather, wide.astype(BF16), preferred_element_type=F32)
    mix[pl.ds(r0, chunk), 0:ATTN_W] = ao

    pad = jnp.zeros((BLOCK - chunk, KV_W), F32)
    knew_t = jnp.concatenate([knew, pad], axis=0).T
    vnew_t = jnp.concatenate([vnew, pad], axis=0).T
    newest = lane == WINDOW - 1
    for b in range(chunk):
        kcol = jnp.broadcast_to(knew_t[:, b:b + 1], (KV_W, WINDOW))
        vcol = jnp.broadcast_to(vnew_t[:, b:b + 1], (KV_W, WINDOW))
        nkt_ref[b] = jnp.where(newest, kcol, pltpu.roll(ckt_ref[b], WINDOW - 1, axis=1))
        nvt_ref[b] = jnp.where(newest, vcol, pltpu.roll(cvt_ref[b], WINDOW - 1, axis=1))

    acc = cb_ref[...] + cw_ref[0, CONV_WIDTH - 1:CONV_WIDTH, :] * unew
    for w in range(CONV_WIDTH - 1):
        acc = acc + cw_ref[0, w:w + 1, :] * st_ref[w]
    nc_ref[0:CONV_WIDTH - 2] = st_ref[1:CONV_WIDTH - 1]
    nc_ref[CONV_WIDTH - 2] = unew
    mu = jnp.mean(acc, axis=-1, keepdims=True)
    xc = acc - mu
    y = xc * lax.rsqrt(jnp.mean(xc * xc, axis=-1, keepdims=True) + EPS)
    y = y * lng_ref[...] + lnb_ref[...]
    mix[pl.ds(r0, chunk), ATTN_W:] = y * _sigmoid(y)

    @pl.when(i == last)
    def _out():
        x2_ref[...] = x_ref[:, 0, :] + jnp.dot(mix[...].astype(BF16), wout_ref[...],
                                           preferred_element_type=F32)


def _sample_mixer(x, ckt, cvt, st, rel_bias, sinks, g1, w_in, bucket, cw, cb, lng, lnb, w_out):
    nb = x.shape[0]
    chunk = SAMPLE_CHUNK
    body = functools.partial(_sample_mixer_body, chunk=chunk)
    cache_spec = pl.BlockSpec((chunk, KV_W, WINDOW), lambda i: (i, 0, 0))
    state_spec = pl.BlockSpec((CONV_WIDTH - 1, chunk, CONV_CH), lambda i: (0, i, 0))
    return pl.pallas_call(
        body,
        grid=(nb // chunk,),
        in_specs=[
            _smem_spec(), _smem_spec(),
            _const_spec((nb, 1, D_MODEL)),
            cache_spec, cache_spec, state_spec,
            _const_spec((1, D_MODEL)),
            _const_spec((D_MODEL, IN_W)),
            _const_spec((1, 2 * BLOCK)),
            _const_spec((1, CONV_WIDTH, CONV_CH)),
            _const_spec((1, CONV_CH)),
            _const_spec((1, CONV_CH)),
            _const_spec((1, CONV_CH)),
            _const_spec((ATTN_W + CONV_CH, D_MODEL)),
        ],
        out_specs=[
            pl.BlockSpec((nb, D_MODEL), lambda i: (0, 0)),
            cache_spec, cache_spec, state_spec,
        ],
        out_shape=[
            jax.ShapeDtypeStruct((nb, D_MODEL), F32),
            jax.ShapeDtypeStruct((nb, KV_W, WINDOW), F32),
            jax.ShapeDtypeStruct((nb, KV_W, WINDOW), F32),
            jax.ShapeDtypeStruct((CONV_WIDTH - 1, nb, CONV_CH), F32),
        ],
        scratch_shapes=[
            pltpu.VMEM((nb, IN_W), F32),
            pltpu.VMEM((nb, ATTN_W + CONV_CH), F32),
            pltpu.VMEM((HEAD_ROWS, 2 * BLOCK), F32),
            pltpu.VMEM((HEAD_ROWS, 1), F32),
        ],
        compiler_params=pltpu.CompilerParams(
            dimension_semantics=("arbitrary",), vmem_limit_bytes=V7X_VMEM_LIMIT_BYTES),
        name="sample_mixer",
    )(rel_bias, sinks, x, ckt, cvt, st, g1, w_in, bucket, cw, cb, lng, lnb, w_out)


def kernel(x_prompt, x_sample, cache_k, cache_v, state_conv, meta_tokens, rel_bias, norm1_g, w_in,
           attn_sinks, conv_w, conv_b, conv_ln_g, conv_ln_b, w_out, norm2_g, w_up, w_down, norm_f_g):
    batch, seq, _ = x_prompt.shape
    nb, dec_seq, _ = x_sample.shape
    assert batch == 1 and dec_seq == 1 and w_in.shape[0] == 1
    assert seq % PROMPT_TILE == 0 and nb % SAMPLE_CHUNK == 0 and nb <= PROMPT_TILE
    assert len(SLOT_ATTN_UNITS) == len(SLOT_CONV_CHUNKS) == D_FF // UP_CHUNK + D_MODEL // DOWN_CHUNK
    assert sum(SLOT_ATTN_UNITS) == (PROMPT_TILE // BLOCK) * N_KV_HEADS
    assert sum(SLOT_CONV_CHUNKS) == PROMPT_TILE // CONV_ROWS

    w_in_b = w_in[0].astype(BF16)
    w_out_b = w_out[0].astype(BF16)
    g1 = norm1_g[0][None]
    g2 = norm2_g[0][None]
    gf = norm_f_g[None]
    cw, cb = conv_w, conv_b[0][None]
    lng, lnb = conv_ln_g[0][None], conv_ln_b[0][None]
    sinks = attn_sinks[0]

    dist_p = jnp.arange(BLOCK)[:, None] + BLOCK - jnp.arange(2 * BLOCK)[None, :]
    bucket_p = _t5_bucket(jnp.clip(dist_p, 0, WINDOW)).astype(jnp.int32)
    lane = jnp.arange(2 * BLOCK)
    dist_s = jnp.where(lane < WINDOW, WINDOW - lane, 0)
    bucket_s = jnp.where(lane <= WINDOW, _t5_bucket(jnp.clip(dist_s, 0, WINDOW)), -1)
    bucket_s = bucket_s.astype(jnp.int32)[None]

    ckt = jnp.transpose(cache_k[0], (0, 2, 3, 1)).reshape(nb, KV_W, WINDOW)
    cvt = jnp.transpose(cache_v[0], (0, 2, 3, 1)).reshape(nb, KV_W, WINDOW)
    st = jnp.transpose(state_conv[0], (1, 0, 2))
    x2_s, nkt_s, nvt_s, nct_s = _sample_mixer(x_sample, ckt, cvt, st, rel_bias, sinks, g1,
                                              w_in_b, bucket_s, cw, cb, lng, lnb, w_out_b)
    nk_s = jnp.transpose(nkt_s.reshape(nb, N_KV_HEADS, HEAD_DIM, WINDOW), (0, 3, 1, 2))
    nv_s = jnp.transpose(nvt_s.reshape(nb, N_KV_HEADS, HEAD_DIM, WINDOW), (0, 3, 1, 2))
    nc_s = jnp.transpose(nct_s, (1, 0, 2))
    y_p, nkt_p, nvt_p, nc_p, y_s = _prompt_layer(x_prompt[0], meta_tokens, rel_bias, sinks, g1, w_in_b,
                                               bucket_p, cw, cb, lng, lnb, w_out_b, g2, w_up[0],
                                               w_down[0], gf, x2_s)

    def to_cache(t):
        return jnp.transpose(t.reshape(N_KV_HEADS, HEAD_DIM, WINDOW), (2, 0, 1))[None, None]

    return (y_p[None], y_s,
            to_cache(nkt_p), to_cache(nvt_p), jnp.transpose(nc_p, (1, 0, 2))[None],
            nk_s[None], nv_s[None], nc_s[None])
```

```python
import functools
import math

import jax
import jax.numpy as jnp
from jax import lax
from jax.experimental import pallas as pl
from jax.experimental.pallas import tpu as pltpu

D_MODEL = 1024
N_HEADS = 8
N_KV_HEADS = 2
HEAD_DIM = 64
GROUP = N_HEADS // N_KV_HEADS
ATTN_W = N_HEADS * HEAD_DIM
KV_W = N_KV_HEADS * HEAD_DIM
CONV_CH = D_MODEL - ATTN_W
IN_W = ATTN_W + 2 * KV_W + 2 * CONV_CH
CONV_WIDTH = 31
WINDOW = 128
BLOCK = 128
N_BUCKETS = 32
MAX_DISTANCE = WINDOW
N_META = 16
D_FF = 4 * D_MODEL
EPS = 1e-6
SCALE = HEAD_DIM ** -0.5
LOG2E = math.log2(math.e)

O_K = ATTN_W
O_V = ATTN_W + KV_W
O_A = ATTN_W + 2 * KV_W
O_B = O_A + CONV_CH

PAD = (-N_META) % BLOCK
U_CARRY = 32
U_SHIFT = U_CARRY - (CONV_WIDTH - 1)

V7X_VMEM_LIMIT_BYTES = 60 * 1024 * 1024

PROMPT_TILE = 512
UP_CHUNK = 1024
DOWN_CHUNK = 512
SLOT_ATTN_UNITS = (2, 2, 2, 2, 0, 0)
SLOT_CONV_CHUNKS = (0, 0, 0, 0, 4, 4)
CONV_ROWS = 64
WEIGHT_STAGE_BLOCK = (256, 1024)
WEIGHT_STAGE_DEPTH = 4
SAMPLE_CHUNK = 32
HEAD_ROWS = 16

BF16 = jnp.bfloat16
F32 = jnp.float32
NT_DIMS = (((1,), (1,)), ((), ()))


def _t5_bucket(d):
    max_exact = N_BUCKETS // 2
    d_f = jnp.maximum(d, 1).astype(jnp.float32)
    large = max_exact + (jnp.log(d_f / max_exact) / math.log(MAX_DISTANCE / max_exact)
                         * (N_BUCKETS - max_exact)).astype(jnp.int32)
    large = jnp.minimum(large, N_BUCKETS - 1)
    return jnp.where(d < max_exact, d, large)


def _rms(x, g):
    y = x * lax.rsqrt(jnp.mean(x * x, axis=-1, keepdims=True) + EPS)
    return y * g


def _relu_sq_bf16(x):
    r = jnp.maximum(x.astype(BF16), 0.0)
    return r * r


def _sigmoid(x):
    return 1.0 / (1.0 + jnp.exp(-x))


def _bias_from_buckets(bucket, relb_ref, h):
    b = jnp.zeros(bucket.shape, F32)
    for bk in range(N_BUCKETS):
        b = jnp.where(bucket == bk, relb_ref[bk, h], b)
    return b


def _conv_rows(ubuf, cw_ref, r0, rows):
    n = rows + U_CARRY
    strips = []
    for c0 in range(0, CONV_CH, BLOCK):
        win = ubuf[r0:r0 + n, c0:c0 + BLOCK]
        acc = None
        for s in range(8):
            sh = win if s == 0 else pltpu.roll(win, n - s, axis=0)
            for a0 in range(0, U_CARRY + 8, 8):
                w = a0 + s - U_SHIFT
                if 0 <= w < CONV_WIDTH:
                    term = cw_ref[w:w + 1, c0:c0 + BLOCK] * sh[a0:a0 + rows]
                    acc = term if acc is None else acc + term
        strips.append(acc)
    return jnp.concatenate(strips, axis=1)


def _ln_silu(acc, lng, lnb):
    mu = jnp.mean(acc, axis=-1, keepdims=True)
    xc = acc - mu
    y = xc * lax.rsqrt(jnp.mean(xc * xc, axis=-1, keepdims=True) + EPS)
    y = y * lng + lnb
    return y * _sigmoid(y)


def _zero_after(v, prev=None):
    u = pltpu.bitcast(v, jnp.uint32)
    t = prev
    for r0 in range(0, u.shape[0], 8):
        for c0 in range(0, u.shape[1], BLOCK):
            piece = u[r0:r0 + 8, c0:c0 + BLOCK]
            t = piece if t is None else t | piece
    return (t >> 16) >> 16


def _order_after(buf, zero):
    tile = pltpu.bitcast(buf[0:16, 0:BLOCK], jnp.uint32)
    buf[0:16, 0:BLOCK] = pltpu.bitcast(tile | zero, BF16)


def _stream_cast(pairs, stage, sem, fillers=()):
    fillers = list(fillers)
    depth, rows, cols = stage.shape
    blocks = [(src, dst, r, c, min(cols, src.shape[1] - c)) for src, dst in pairs
              for r in range(0, src.shape[0], rows) for c in range(0, src.shape[1], cols)]

    def copy(j):
        src, _, r, c, w = blocks[j]
        return pltpu.make_async_copy(src.at[pl.ds(r, rows), pl.ds(c, w)],
                                     stage.at[j % depth, :, pl.ds(0, w)], sem.at[j % depth])

    for j in range(min(depth - 1, len(blocks))):
        copy(j).start()
    for j, (_, dst, r, c, w) in enumerate(blocks):
        if j + depth - 1 < len(blocks):
            copy(j + depth - 1).start()
        if fillers:
            fillers.pop(0)()
        copy(j).wait()
        dst[r:r + rows, c:c + w] = stage[j % depth, :, 0:w].astype(BF16)
    for filler in fillers:
        filler()


def _prompt_layer_body(*refs, tile):
    i = pl.program_id(0)
    n_tiles = pl.num_programs(0) - 1
    pl.when(i < n_tiles)(functools.partial(_prompt_step, *refs, tile=tile))
    pl.when(i == n_tiles)(functools.partial(_prompt_last_ffn, *refs, tile=tile))


def _prompt_last_ffn(relb_ref, sink_ref, x_ref, meta_ref, g1_ref, win_hbm, bucket_ref,
                     cw_ref, cb_ref, lng_ref, lnb_ref, wout_hbm, g2_ref, wup_hbm, wdn_hbm, gf_ref,
                     x2s_ref, y_ref, nk_ref, nv_ref, nc_ref, ys_ref,
                     kbuf, vbuf, ubuf, qbuf, mix, bias_s, x2buf, kvlast, hfbuf, hidbuf, ysbuf,
                     wup_ref, wdn_ref, win_ref, wout_ref, stage, wsem, *, tile):
    xf = x2buf[1 - pl.program_id(0) % 2]
    hf = _rms(xf, g2_ref[...]).astype(BF16)
    for c0 in range(0, D_FF, UP_CHUNK):
        hid = jnp.dot(hf, wup_ref[:, c0:c0 + UP_CHUNK], preferred_element_type=F32)
        hidbuf[:, c0:c0 + UP_CHUNK] = _relu_sq_bf16(hid)
    cols = [xf[:, n0:n0 + DOWN_CHUNK]
            + jnp.dot(hidbuf[...], wdn_ref[:, n0:n0 + DOWN_CHUNK], preferred_element_type=F32)
            for n0 in range(0, D_MODEL, DOWN_CHUNK)]
    y_ref[...] = _rms(jnp.concatenate(cols, axis=1), gf_ref[...])


def _prompt_step(relb_ref, sink_ref, x_ref, meta_ref, g1_ref, win_hbm, bucket_ref,
                 cw_ref, cb_ref, lng_ref, lnb_ref, wout_hbm, g2_ref, wup_hbm, wdn_hbm, gf_ref,
                 x2s_ref, y_ref, nk_ref, nv_ref, nc_ref, ys_ref,
                 kbuf, vbuf, ubuf, qbuf, mix, bias_s, x2buf, kvlast, hfbuf, hidbuf, ysbuf,
                 wup_ref, wdn_ref, win_ref, wout_ref, stage, wsem, *, tile):
    i = pl.program_id(0)
    n_tiles = pl.num_programs(0) - 1
    g1 = g1_ref[...]
    slot = i % 2

    @pl.when(i == 0)
    def _init():
        def decode_rows():
            nb = x2s_ref.shape[0]
            x2buf[1, 0:nb, :] = x2s_ref[...]
            x2buf[1, nb:tile, :] = jnp.zeros((tile - nb, D_MODEL), F32)

        def bias_table(h):
            bucket = bucket_ref[...]
            row = lax.broadcasted_iota(jnp.int32, (BLOCK, 2 * BLOCK), 0)
            col = lax.broadcasted_iota(jnp.int32, (BLOCK, 2 * BLOCK), 1)
            dist = row + BLOCK - col
            band = (dist >= 0) & (dist <= WINDOW)
            band_first = band & (col >= PAD)
            b = _bias_from_buckets(bucket, relb_ref, h) * LOG2E
            bias_s[0, h] = jnp.where(band, b, -jnp.inf)
            bias_s[1, h] = jnp.where(band_first, b, -jnp.inf)

        _stream_cast([(win_hbm, win_ref), (wout_hbm, wout_ref), (wup_hbm, wup_ref), (wdn_hbm, wdn_ref)],
                     stage, wsem,
                     fillers=[decode_rows] + [functools.partial(bias_table, h) for h in range(N_HEADS)])
        hm = _rms(meta_ref[...], g1).astype(BF16)
        pm = jnp.dot(hm, win_ref[:, O_K:], preferred_element_type=F32)
        kbuf[0:PAD, :] = jnp.zeros((PAD, KV_W), BF16)
        vbuf[0:PAD, :] = jnp.zeros((PAD, KV_W), BF16)
        kbuf[PAD:BLOCK, :] = pm[:, 0:KV_W].astype(BF16)
        vbuf[PAD:BLOCK, :] = pm[:, KV_W:2 * KV_W].astype(BF16)
        um = pm[:, 2 * KV_W:2 * KV_W + CONV_CH] * _sigmoid(pm[:, 2 * KV_W + CONV_CH:])
        ubuf[0:U_CARRY - N_META, :] = jnp.zeros((U_CARRY - N_META, CONV_CH), F32)
        ubuf[U_CARRY - N_META:U_CARRY, :] = um

    x = x_ref[...]
    h = _rms(x, g1).astype(BF16)
    q = jnp.dot(h, win_ref[:, 0:ATTN_W], preferred_element_type=F32) * (SCALE * LOG2E)
    qbuf[...] = q.astype(BF16)
    kv = jnp.dot(h, win_ref[:, O_K:O_A], preferred_element_type=F32)
    kbuf[BLOCK:BLOCK + tile, :] = kv[:, 0:KV_W].astype(BF16)
    vbuf[BLOCK:BLOCK + tile, :] = kv[:, KV_W:].astype(BF16)
    kvlast[...] = kv[tile - WINDOW:, :]

    a = jnp.dot(h, win_ref[:, O_A:O_B], preferred_element_type=F32)
    b = jnp.dot(h, win_ref[:, O_B:], preferred_element_type=F32)
    ubuf[U_CARRY:U_CARRY + tile, :] = a * _sigmoid(b)

    cb, lng, lnb = cb_ref[...], lng_ref[...], lnb_ref[...]

    def conv_chunk(r0):
        c = _ln_silu(_conv_rows(ubuf, cw_ref.at[0], r0, CONV_ROWS) + cb, lng, lnb)
        mix[r0:r0 + CONV_ROWS, ATTN_W:] = c.astype(BF16)

    def attn_scores(blk, kvh):
        r0 = blk * BLOCK
        c0 = kvh * HEAD_DIM
        qg = jnp.concatenate(
            [qbuf[r0:r0 + BLOCK, (kvh * GROUP + g) * HEAD_DIM:(kvh * GROUP + g + 1) * HEAD_DIM]
             for g in range(GROUP)], axis=0)
        kk = kbuf[r0:r0 + 2 * BLOCK, c0:c0 + HEAD_DIM]
        return lax.dot_general(qg, kk, NT_DIMS, preferred_element_type=F32)

    def attn_finish(blk, kvh, s):
        r0 = blk * BLOCK
        sel = jnp.where(i == 0, 1, 0) if blk == 0 else 0
        c0 = kvh * HEAD_DIM
        vv = vbuf[r0:r0 + 2 * BLOCK, c0:c0 + HEAD_DIM]
        ps, ls = [], []
        for g in range(GROUP):
            hd = kvh * GROUP + g
            sg = s[g * BLOCK:(g + 1) * BLOCK] + bias_s[sel, hd]
            sk = sink_ref[hd] * LOG2E
            m = jnp.maximum(jnp.max(sg, axis=-1, keepdims=True), sk)
            p = jnp.exp2(sg - m)
            ls.append(jnp.sum(p, axis=-1, keepdims=True) + jnp.exp2(sk - m))
            ps.append(p.astype(BF16))
        o = jnp.dot(jnp.concatenate(ps, axis=0), vv, preferred_element_type=F32)
        for g in range(GROUP):
            hd = kvh * GROUP + g
            og = o[g * BLOCK:(g + 1) * BLOCK] / ls[g]
            mix[r0:r0 + BLOCK, hd * HEAD_DIM:(hd + 1) * HEAD_DIM] = og.astype(BF16)

    conv_starts = list(range(0, tile, CONV_ROWS))
    attn_units = [(blk, kvh) for blk in range(tile // BLOCK) for kvh in range(N_KV_HEADS)]
    n_up, n_down = D_FF // UP_CHUNK, D_MODEL // DOWN_CHUNK
    n_slots = n_up + n_down
    unit_iter, conv_iter = iter(attn_units), iter(conv_starts)
    xf = x2buf[1 - slot]
    hfbuf[...] = _rms(xf, g2_ref[...]).astype(BF16)
    x3_cols = []
    for k in range(n_slots):
        units = [next(unit_iter) for _ in range(SLOT_ATTN_UNITS[k])]
        scores = [(blk, kvh, attn_scores(blk, kvh)) for blk, kvh in units]
        if k < n_up:
            c0 = k * UP_CHUNK
            hid = jnp.dot(hfbuf[...], wup_ref[:, c0:c0 + UP_CHUNK], preferred_element_type=F32)
            hidbuf[:, c0:c0 + UP_CHUNK] = _relu_sq_bf16(hid)
        else:
            n0 = (k - n_up) * DOWN_CHUNK
            x3_cols.append(xf[:, n0:n0 + DOWN_CHUNK]
                           + jnp.dot(hidbuf[...], wdn_ref[:, n0:n0 + DOWN_CHUNK],
                                     preferred_element_type=F32))
        zero = None
        for r0 in [next(conv_iter) for _ in range(SLOT_CONV_CHUNKS[k])]:
            conv_chunk(r0)
            zero = _zero_after(mix[r0:r0 + CONV_ROWS, ATTN_W:], zero)
        for blk, kvh, s in scores:
            attn_finish(blk, kvh, s)
        if k + 1 < n_slots and zero is not None:
            _order_after(hfbuf if k + 1 < n_up else hidbuf, zero)
    acc = jnp.concatenate(x3_cols, axis=1)
    y = _rms(acc, gf_ref[...])
    y_ref[...] = y
    ysbuf[...] = y[0:ysbuf.shape[0]]

    x2buf[slot] = x + jnp.dot(mix[...], wout_ref[...], preferred_element_type=F32)

    @pl.when(i == n_tiles - 1)
    def _new_caches():
        nk_ref[...] = kvlast[:, 0:KV_W].T
        nv_ref[...] = kvlast[:, KV_W:].T
        nc_ref[:, 0, :] = ubuf[U_CARRY + tile - (CONV_WIDTH - 1):U_CARRY + tile, :]

    kbuf[0:BLOCK, :] = kbuf[tile:tile + BLOCK, :]
    vbuf[0:BLOCK, :] = vbuf[tile:tile + BLOCK, :]
    ubuf[0:U_CARRY, :] = ubuf[tile:tile + U_CARRY, :]

    @pl.when(i == 0)
    def _sample_out():
        ys_ref[:, 0, :] = ysbuf[...]


def _const_spec(shape):
    return pl.BlockSpec(shape, lambda i: (0,) * len(shape), pipeline_mode=pl.Buffered(1))


def _smem_spec():
    return pl.BlockSpec(memory_space=pltpu.SMEM)


def _prompt_layer(x, meta, rel_bias, sinks, g1, w_in, bucket, cw, cb, lng, lnb, w_out,
                  g2, w_up, w_down, gf, x2_s):
    seq = x.shape[0]
    nb = x2_s.shape[0]
    tile = PROMPT_TILE
    n_tiles = seq // tile
    body = functools.partial(_prompt_layer_body, tile=tile)
    return pl.pallas_call(
        body,
        grid=(n_tiles + 1,),
        in_specs=[
            _smem_spec(), _smem_spec(),
            pl.BlockSpec((tile, D_MODEL), lambda i: (jnp.minimum(i, n_tiles - 1), 0)),
            _const_spec((N_META, D_MODEL)),
            _const_spec((1, D_MODEL)),
            pl.BlockSpec(memory_space=pl.ANY),
            _const_spec((BLOCK, 2 * BLOCK)),
            _const_spec((1, CONV_WIDTH, CONV_CH)),
            _const_spec((1, CONV_CH)),
            _const_spec((1, CONV_CH)),
            _const_spec((1, CONV_CH)),
            pl.BlockSpec(memory_space=pl.ANY),
            _const_spec((1, D_MODEL)),
            pl.BlockSpec(memory_space=pl.ANY),
            pl.BlockSpec(memory_space=pl.ANY),
            _const_spec((1, D_MODEL)),
            _const_spec((nb, D_MODEL)),
        ],
        out_specs=[
            pl.BlockSpec((tile, D_MODEL), lambda i: (jnp.maximum(i - 1, 0), 0)),
            pl.BlockSpec((KV_W, WINDOW), lambda i: (0, 0)),
            pl.BlockSpec((KV_W, WINDOW), lambda i: (0, 0)),
            pl.BlockSpec((CONV_WIDTH - 1, 1, CONV_CH), lambda i: (0, 0, 0)),
            pl.BlockSpec((nb, 1, D_MODEL), lambda i: (0, 0, 0)),
        ],
        out_shape=[
            jax.ShapeDtypeStruct((seq, D_MODEL), F32),
            jax.ShapeDtypeStruct((KV_W, WINDOW), F32),
            jax.ShapeDtypeStruct((KV_W, WINDOW), F32),
            jax.ShapeDtypeStruct((CONV_WIDTH - 1, 1, CONV_CH), F32),
            jax.ShapeDtypeStruct((nb, 1, D_MODEL), F32),
        ],
        scratch_shapes=[
            pltpu.VMEM((BLOCK + tile, KV_W), BF16),
            pltpu.VMEM((BLOCK + tile, KV_W), BF16),
            pltpu.VMEM((U_CARRY + tile, CONV_CH), F32),
            pltpu.VMEM((tile, ATTN_W), BF16),
            pltpu.VMEM((tile, ATTN_W + CONV_CH), BF16),
            pltpu.VMEM((2, N_HEADS, BLOCK, 2 * BLOCK), F32),
            pltpu.VMEM((2, tile, D_MODEL), F32),
            pltpu.VMEM((WINDOW, 2 * KV_W), F32),
            pltpu.VMEM((tile, D_MODEL), BF16),
            pltpu.VMEM((tile, D_FF), BF16),
            pltpu.VMEM((nb, D_MODEL), F32),
            pltpu.VMEM((D_MODEL, D_FF), BF16),
            pltpu.VMEM((D_FF, D_MODEL), BF16),
            pltpu.VMEM((D_MODEL, IN_W), BF16),
            pltpu.VMEM((ATTN_W + CONV_CH, D_MODEL), BF16),
            pltpu.VMEM((WEIGHT_STAGE_DEPTH,) + WEIGHT_STAGE_BLOCK, F32),
            pltpu.SemaphoreType.DMA((WEIGHT_STAGE_DEPTH,)),
        ],
        compiler_params=pltpu.CompilerParams(
            dimension_semantics=("arbitrary",), vmem_limit_bytes=V7X_VMEM_LIMIT_BYTES),
        name="prompt_layer",
    )(rel_bias, sinks, x, meta, g1, w_in, bucket, cw, cb, lng, lnb, w_out, g2, w_up, w_down, gf,
      x2_s)


def _sample_mixer_body(relb_ref, sink_ref, x_ref, ckt_ref, cvt_ref, st_ref, g1_ref, win_ref,
                       bucket_ref, cw_ref, cb_ref, lng_ref, lnb_ref, wout_ref,
                       x2_ref, nkt_ref, nvt_ref, nc_ref,
                       pbuf, mix, bias_c, sink_c, *, chunk):
    i = pl.program_id(0)
    last = pl.num_programs(0) - 1
    rows_h = chunk * HEAD_ROWS

    @pl.when(i == 0)
    def _init():
        h = _rms(x_ref[:, 0, :], g1_ref[...]).astype(BF16)
        pbuf[...] = jnp.dot(h, win_ref[...].astype(BF16), preferred_element_type=F32)
        bucket = bucket_ref[...]
        rid = lax.broadcasted_iota(jnp.int32, (HEAD_ROWS, 1), 0)
        bias = jnp.zeros((HEAD_ROWS, 2 * BLOCK), F32)
        sk = jnp.zeros((HEAD_ROWS, 1), F32)
        for hd in range(N_HEADS):
            bias = jnp.where(rid == hd, _bias_from_buckets(bucket, relb_ref, hd), bias)
            sk = jnp.where(rid == hd, sink_ref[hd], sk)
        bias_c[...] = bias
        sink_c[...] = sk

    r0 = pl.multiple_of(i * chunk, chunk)
    pr = pbuf[pl.ds(r0, chunk), :]
    q = pr[:, 0:ATTN_W] * SCALE
    knew = pr[:, O_K:O_V]
    vnew = pr[:, O_V:O_A]
    unew = pr[:, O_A:O_B] * _sigmoid(pr[:, O_B:])

    def per_head(t):
        n = t.shape[-1]
        return jnp.broadcast_to(t[:, None, :], (chunk, HEAD_ROWS, n)).reshape(rows_h, n)

    hid = lax.broadcasted_iota(jnp.int32, (rows_h, 1), 0) % HEAD_ROWS
    lane = lax.broadcasted_iota(jnp.int32, (1, BLOCK), 1)
    qrep = per_head(q)
    qsum = jnp.zeros((rows_h, BLOCK), F32)
    for c in range(ATTN_W // BLOCK):
        piece = qrep[:, c * BLOCK:(c + 1) * BLOCK]
        in_head = (lane // HEAD_DIM + 2 * c) == hid
        qsum = qsum + jnp.where(in_head, piece, 0.0)
    keep = (hid % 2) == (hid // GROUP)
    qf = jnp.where(keep, qsum, pltpu.roll(qsum, HEAD_DIM, axis=1))
    qf_b = qf.astype(BF16)

    bias = jnp.broadcast_to(bias_c[...][None], (chunk, HEAD_ROWS, 2 * BLOCK)).reshape(rows_h, 2 * BLOCK)
    sk = jnp.broadcast_to(sink_c[...][None], (chunk, HEAD_ROWS, 1)).reshape(rows_h, 1)

    s_rows = []
    for b in range(chunk):
        kt = ckt_ref[b].astype(BF16)
        s_rows.append(jnp.dot(qf_b[b * HEAD_ROWS:(b + 1) * HEAD_ROWS], kt, preferred_element_type=F32))
    s_c = jnp.concatenate(s_rows, axis=0) + bias[:, 0:BLOCK]
    s_n = jnp.sum(qf * per_head(knew), axis=-1, keepdims=True) + bias[:, BLOCK:BLOCK + 1]
    m = jnp.maximum(jnp.maximum(jnp.max(s_c, axis=-1, keepdims=True), s_n), sk)
    p_c = jnp.exp(s_c - m)
    p_n = jnp.exp(s_n - m)
    l = jnp.sum(p_c, axis=-1, keepdims=True) + p_n + jnp.exp(sk - m)
    p_cb = p_c.astype(BF16)
    o_rows = []
    for b in range(chunk):
        vt = cvt_ref[b].astype(BF16)
        o_rows.append(lax.dot_general(p_cb[b * HEAD_ROWS:(b + 1) * HEAD_ROWS], vt, NT_DIMS,
                                      preferred_element_type=F32))
    o = (jnp.concatenate(o_rows, axis=0) + p_n * per_head(vnew)) / l
    o = jnp.where(keep, o, pltpu.roll(o, HEAD_DIM, axis=1))
    o = jnp.where((lane // HEAD_DIM) == (hid % 2), o, 0.0)
    wide = jnp.concatenate([jnp.where(hid // 2 == c, o, 0.0) for c in range(ATTN_W // BLOCK)], axis=1)
    gi = lax.broadcasted_iota(jnp.int32, (chunk, rows_h), 0)
    gj = lax.broadcasted_iota(jnp.int32, (chunk, rows_h), 1)
    gather = jnp.where(gj // HEAD_ROWS == gi, 1.0, 0.0).astype(BF16)
    ao = jnp.dot(gather, wide.astype(BF16), preferred_element_type=F32)
    mix[pl.ds(r0, chunk), 0:ATTN_W] = ao

    pad = jnp.zeros((BLOCK - chunk, KV_W), F32)
    knew_t = jnp.concatenate([knew, pad], axis=0).T
    vnew_t = jnp.concatenate([vnew, pad], axis=0).T
    newest = lane == WINDOW - 1
    for b in range(chunk):
        kcol = jnp.broadcast_to(knew_t[:, b:b + 1], (KV_W, WINDOW))
        vcol = jnp.broadcast_to(vnew_t[:, b:b + 1], (KV_W, WINDOW))
        nkt_ref[b] = jnp.where(newest, kcol, pltpu.roll(ckt_ref[b], WINDOW - 1, axis=1))
        nvt_ref[b] = jnp.where(newest, vcol, pltpu.roll(cvt_ref[b], WINDOW - 1, axis=1))

    acc = cb_ref[...] + cw_ref[0, CONV_WIDTH - 1:CONV_WIDTH, :] * unew
    for w in range(CONV_WIDTH - 1):
        acc = acc + cw_ref[0, w:w + 1, :] * st_ref[w]
    nc_ref[0:CONV_WIDTH - 2] = st_ref[1:CONV_WIDTH - 1]
    nc_ref[CONV_WIDTH - 2] = unew
    mu = jnp.mean(acc, axis=-1, keepdims=True)
    xc = acc - mu
    y = xc * lax.rsqrt(jnp.mean(xc * xc, axis=-1, keepdims=True) + EPS)
    y = y * lng_ref[...] + lnb_ref[...]
    mix[pl.ds(r0, chunk), ATTN_W:] = y * _sigmoid(y)

    @pl.when(i == last)
    def _out():
        x2_ref[...] = x_ref[:, 0, :] + jnp.dot(mix[...].astype(BF16), wout_ref[...].astype(BF16),
                                           preferred_element_type=F32)


def _sample_mixer(x, ckt, cvt, st, rel_bias, sinks, g1, w_in, bucket, cw, cb, lng, lnb, w_out):
    nb = x.shape[0]
    chunk = SAMPLE_CHUNK
    body = functools.partial(_sample_mixer_body, chunk=chunk)
    cache_spec = pl.BlockSpec((chunk, KV_W, WINDOW), lambda i: (i, 0, 0))
    state_spec = pl.BlockSpec((CONV_WIDTH - 1, chunk, CONV_CH), lambda i: (0, i, 0))
    return pl.pallas_call(
        body,
        grid=(nb // chunk,),
        in_specs=[
            _smem_spec(), _smem_spec(),
            _const_spec((nb, 1, D_MODEL)),
            cache_spec, cache_spec, state_spec,
            _const_spec((1, D_MODEL)),
            _const_spec((D_MODEL, IN_W)),
            _const_spec((1, 2 * BLOCK)),
            _const_spec((1, CONV_WIDTH, CONV_CH)),
            _const_spec((1, CONV_CH)),
            _const_spec((1, CONV_CH)),
            _const_spec((1, CONV_CH)),
            _const_spec((ATTN_W + CONV_CH, D_MODEL)),
        ],
        out_specs=[
            pl.BlockSpec((nb, D_MODEL), lambda i: (0, 0)),
            cache_spec, cache_spec, state_spec,
        ],
        out_shape=[
            jax.ShapeDtypeStruct((nb, D_MODEL), F32),
            jax.ShapeDtypeStruct((nb, KV_W, WINDOW), F32),
            jax.ShapeDtypeStruct((nb, KV_W, WINDOW), F32),
            jax.ShapeDtypeStruct((CONV_WIDTH - 1, nb, CONV_CH), F32),
        ],
        scratch_shapes=[
            pltpu.VMEM((nb, IN_W), F32),
            pltpu.VMEM((nb, ATTN_W + CONV_CH), F32),
            pltpu.VMEM((HEAD_ROWS, 2 * BLOCK), F32),
            pltpu.VMEM((HEAD_ROWS, 1), F32),
        ],
        compiler_params=pltpu.CompilerParams(
            dimension_semantics=("arbitrary",), vmem_limit_bytes=V7X_VMEM_LIMIT_BYTES),
        name="sample_mixer",
    )(rel_bias, sinks, x, ckt, cvt, st, g1, w_in, bucket, cw, cb, lng, lnb, w_out)


def kernel(x_prompt, x_sample, cache_k, cache_v, state_conv, meta_tokens, rel_bias, norm1_g, w_in,
           attn_sinks, conv_w, conv_b, conv_ln_g, conv_ln_b, w_out, norm2_g, w_up, w_down, norm_f_g):
    batch, seq, _ = x_prompt.shape
    nb, dec_seq, _ = x_sample.shape
    assert batch == 1 and dec_seq == 1 and w_in.shape[0] == 1
    assert seq % PROMPT_TILE == 0 and nb % SAMPLE_CHUNK == 0 and nb <= PROMPT_TILE
    assert len(SLOT_ATTN_UNITS) == len(SLOT_CONV_CHUNKS) == D_FF // UP_CHUNK + D_MODEL // DOWN_CHUNK
    assert sum(SLOT_ATTN_UNITS) == (PROMPT_TILE // BLOCK) * N_KV_HEADS
    assert sum(SLOT_CONV_CHUNKS) == PROMPT_TILE // CONV_ROWS

    w_in_b, w_out_b = w_in[0], w_out[0]
    g1 = norm1_g[0][None]
    g2 = norm2_g[0][None]
    gf = norm_f_g[None]
    cw, cb = conv_w, conv_b[0][None]
    lng, lnb = conv_ln_g[0][None], conv_ln_b[0][None]
    sinks = attn_sinks[0]

    dist_p = jnp.arange(BLOCK)[:, None] + BLOCK - jnp.arange(2 * BLOCK)[None, :]
    bucket_p = _t5_bucket(jnp.clip(dist_p, 0, WINDOW)).astype(jnp.int32)
    lane = jnp.arange(2 * BLOCK)
    dist_s = jnp.where(lane < WINDOW, WINDOW - lane, 0)
    bucket_s = jnp.where(lane <= WINDOW, _t5_bucket(jnp.clip(dist_s, 0, WINDOW)), -1)
    bucket_s = bucket_s.astype(jnp.int32)[None]

    ckt = jnp.transpose(cache_k[0], (0, 2, 3, 1)).reshape(nb, KV_W, WINDOW)
    cvt = jnp.transpose(cache_v[0], (0, 2, 3, 1)).reshape(nb, KV_W, WINDOW)
    st = jnp.transpose(state_conv[0], (1, 0, 2))
    x2_s, nkt_s, nvt_s, nct_s = _sample_mixer(x_sample, ckt, cvt, st, rel_bias, sinks, g1,
                                              w_in_b, bucket_s, cw, cb, lng, lnb, w_out_b)
    nk_s = jnp.transpose(nkt_s.reshape(nb, N_KV_HEADS, HEAD_DIM, WINDOW), (0, 3, 1, 2))
    nv_s = jnp.transpose(nvt_s.reshape(nb, N_KV_HEADS, HEAD_DIM, WINDOW), (0, 3, 1, 2))
    nc_s = jnp.transpose(nct_s, (1, 0, 2))
    y_p, nkt_p, nvt_p, nc_p, y_s = _prompt_layer(x_prompt[0], meta_tokens, rel_bias, sinks, g1, w_in_b,
                                               bucket_p, cw, cb, lng, lnb, w_out_b, g2, w_up[0],
                                               w_down[0], gf, x2_s)

    def to_cache(t):
        return jnp.transpose(t.reshape(N_KV_HEADS, HEAD_DIM, WINDOW), (2, 0, 1))[None, None]

    return (y_p[None], y_s,
            to_cache(nkt_p), to_cache(nvt_p), jnp.transpose(nc_p, (1, 0, 2))[None],
            nk_s[None], nv_s[None], nc_s[None])
```

```python
import functools
import math

import jax
import jax.numpy as jnp
from jax import lax
from jax.experimental import pallas as pl
from jax.experimental.pallas import tpu as pltpu

D_MODEL = 1024
N_HEADS = 8
N_KV_HEADS = 2
HEAD_DIM = 64
GROUP = N_HEADS // N_KV_HEADS
ATTN_W = N_HEADS * HEAD_DIM
KV_W = N_KV_HEADS * HEAD_DIM
CONV_CH = D_MODEL - ATTN_W
IN_W = ATTN_W + 2 * KV_W + 2 * CONV_CH
CONV_WIDTH = 31
WINDOW = 128
BLOCK = 128
N_BUCKETS = 32
MAX_DISTANCE = WINDOW
N_META = 16
D_FF = 4 * D_MODEL
EPS = 1e-6
SCALE = HEAD_DIM ** -0.5
LOG2E = math.log2(math.e)

O_K = ATTN_W
O_V = ATTN_W + KV_W
O_A = ATTN_W + 2 * KV_W
O_B = O_A + CONV_CH

PAD = (-N_META) % BLOCK
U_CARRY = 32
U_SHIFT = U_CARRY - (CONV_WIDTH - 1)

V7X_VMEM_LIMIT_BYTES = 60 * 1024 * 1024

PROMPT_TILE = 512
UP_CHUNK = 1024
DOWN_CHUNK = 512
SLOT_ATTN_UNITS = (2, 2, 2, 2, 0, 0)
SLOT_CONV_CHUNKS = (0, 0, 0, 0, 4, 4)
CONV_ROWS = 64
WEIGHT_STAGE_BLOCK = (256, 1024)
WEIGHT_STAGE_DEPTH = 4
SAMPLE_CHUNK = 32
HEAD_ROWS = 16

BF16 = jnp.bfloat16
F32 = jnp.float32
NT_DIMS = (((1,), (1,)), ((), ()))


def _t5_bucket(d):
    max_exact = N_BUCKETS // 2
    d_f = jnp.maximum(d, 1).astype(jnp.float32)
    large = max_exact + (jnp.log(d_f / max_exact) / math.log(MAX_DISTANCE / max_exact)
                         * (N_BUCKETS - max_exact)).astype(jnp.int32)
    large = jnp.minimum(large, N_BUCKETS - 1)
    return jnp.where(d < max_exact, d, large)


def _rms(x, g):
    y = x * lax.rsqrt(jnp.mean(x * x, axis=-1, keepdims=True) + EPS)
    return y * g


def _relu_sq_bf16(x):
    r = jnp.maximum(x.astype(BF16), 0.0)
    return r * r


def _sigmoid(x):
    return 1.0 / (1.0 + jnp.exp(-x))


def _bias_from_buckets(bucket, relb_ref, h):
    b = jnp.zeros(bucket.shape, F32)
    for bk in range(N_BUCKETS):
        b = jnp.where(bucket == bk, relb_ref[bk, h], b)
    return b


def _conv_rows(ubuf, cw_ref, r0, rows):
    n = rows + U_CARRY
    strips = []
    for c0 in range(0, CONV_CH, BLOCK):
        win = ubuf[r0:r0 + n, c0:c0 + BLOCK]
        acc = None
        for s in range(8):
            sh = win if s == 0 else pltpu.roll(win, n - s, axis=0)
            for a0 in range(0, U_CARRY + 8, 8):
                w = a0 + s - U_SHIFT
                if 0 <= w < CONV_WIDTH:
                    term = cw_ref[w:w + 1, c0:c0 + BLOCK] * sh[a0:a0 + rows]
                    acc = term if acc is None else acc + term
        strips.append(acc)
    return jnp.concatenate(strips, axis=1)


def _ln_silu(acc, lng, lnb):
    mu = jnp.mean(acc, axis=-1, keepdims=True)
    xc = acc - mu
    y = xc * lax.rsqrt(jnp.mean(xc * xc, axis=-1, keepdims=True) + EPS)
    y = y * lng + lnb
    return y * _sigmoid(y)


def _zero_after(v, prev=None):
    u = pltpu.bitcast(v, jnp.uint32)
    t = prev
    for r0 in range(0, u.shape[0], 8):
        for c0 in range(0, u.shape[1], BLOCK):
            piece = u[r0:r0 + 8, c0:c0 + BLOCK]
            t = piece if t is None else t | piece
    return (t >> 16) >> 16


def _order_after(buf, zero):
    tile = pltpu.bitcast(buf[0:16, 0:BLOCK], jnp.uint32)
    buf[0:16, 0:BLOCK] = pltpu.bitcast(tile | zero, BF16)


def _stream_cast(pairs, stage, sem, fillers=()):
    fillers = list(fillers)
    depth, rows, cols = stage.shape
    blocks = [(src, dst, r, c, min(cols, src.shape[1] - c)) for src, dst in pairs
              for r in range(0, src.shape[0], rows) for c in range(0, src.shape[1], cols)]

    def copy(j):
        src, _, r, c, w = blocks[j]
        return pltpu.make_async_copy(src.at[pl.ds(r, rows), pl.ds(c, w)],
                                     stage.at[j % depth, :, pl.ds(0, w)], sem.at[j % depth])

    for j in range(min(depth - 1, len(blocks))):
        copy(j).start()
    for j, (_, dst, r, c, w) in enumerate(blocks):
        if j + depth - 1 < len(blocks):
            copy(j + depth - 1).start()
        if fillers:
            fillers.pop(0)()
        copy(j).wait()
        dst[r:r + rows, c:c + w] = stage[j % depth, :, 0:w].astype(BF16)
    for filler in fillers:
        filler()


def _prompt_layer_body(*refs, tile):
    i = pl.program_id(0)
    n_tiles = pl.num_programs(0) - 1
    pl.when(i < n_tiles)(functools.partial(_prompt_step, *refs, tile=tile))
    pl.when(i == n_tiles)(functools.partial(_prompt_last_ffn, *refs, tile=tile))


def _prompt_last_ffn(relb_ref, sink_ref, x_ref, meta_ref, g1_ref, win_hbm, bucket_ref,
                     cw_ref, cb_ref, lng_ref, lnb_ref, wout_hbm, g2_ref, wup_hbm, wdn_hbm, gf_ref,
                     x2s_ref, y_ref, nk_ref, nv_ref, nc_ref, ys_ref,
                     kbuf, vbuf, ubuf, qbuf, mix, bias_s, x2buf, kvlast, hfbuf, hidbuf, ysbuf,
                     wup_ref, wdn_ref, win_ref, wout_ref, stage, wsem, *, tile):
    xf = x2buf[1 - pl.program_id(0) % 2]
    hf = _rms(xf, g2_ref[...]).astype(BF16)
    for c0 in range(0, D_FF, UP_CHUNK):
        hid = jnp.dot(hf, wup_ref[:, c0:c0 + UP_CHUNK], preferred_element_type=F32)
        hidbuf[:, c0:c0 + UP_CHUNK] = _relu_sq_bf16(hid)
    cols = [xf[:, n0:n0 + DOWN_CHUNK]
            + jnp.dot(hidbuf[...], wdn_ref[:, n0:n0 + DOWN_CHUNK], preferred_element_type=F32)
            for n0 in range(0, D_MODEL, DOWN_CHUNK)]
    y_ref[...] = _rms(jnp.concatenate(cols, axis=1), gf_ref[...])


def _prompt_step(relb_ref, sink_ref, x_ref, meta_ref, g1_ref, win_hbm, bucket_ref,
                 cw_ref, cb_ref, lng_ref, lnb_ref, wout_hbm, g2_ref, wup_hbm, wdn_hbm, gf_ref,
                 x2s_ref, y_ref, nk_ref, nv_ref, nc_ref, ys_ref,
                 kbuf, vbuf, ubuf, qbuf, mix, bias_s, x2buf, kvlast, hfbuf, hidbuf, ysbuf,
                 wup_ref, wdn_ref, win_ref, wout_ref, stage, wsem, *, tile):
    i = pl.program_id(0)
    n_tiles = pl.num_programs(0) - 1
    g1 = g1_ref[...]
    slot = i % 2

    @pl.when(i == 0)
    def _init():
        def decode_rows():
            nb = x2s_ref.shape[0]
            x2buf[1, 0:nb, :] = x2s_ref[...]
            x2buf[1, nb:tile, :] = jnp.zeros((tile - nb, D_MODEL), F32)

        def bias_table(h):
            bucket = bucket_ref[...]
            row = lax.broadcasted_iota(jnp.int32, (BLOCK, 2 * BLOCK), 0)
            col = lax.broadcasted_iota(jnp.int32, (BLOCK, 2 * BLOCK), 1)
            dist = row + BLOCK - col
            band = (dist >= 0) & (dist <= WINDOW)
            band_first = band & (col >= PAD)
            b = _bias_from_buckets(bucket, relb_ref, h) * LOG2E
            bias_s[0, h] = jnp.where(band, b, -jnp.inf)
            bias_s[1, h] = jnp.where(band_first, b, -jnp.inf)

        _stream_cast([(win_hbm, win_ref), (wout_hbm, wout_ref), (wup_hbm, wup_ref), (wdn_hbm, wdn_ref)],
                     stage, wsem,
                     fillers=[decode_rows] + [functools.partial(bias_table, h) for h in range(N_HEADS)])
        hm = _rms(meta_ref[...], g1).astype(BF16)
        pm = jnp.dot(hm, win_ref[:, O_K:], preferred_element_type=F32)
        kbuf[0:PAD, :] = jnp.zeros((PAD, KV_W), BF16)
        vbuf[0:PAD, :] = jnp.zeros((PAD, KV_W), BF16)
        kbuf[PAD:BLOCK, :] = pm[:, 0:KV_W].astype(BF16)
        vbuf[PAD:BLOCK, :] = pm[:, KV_W:2 * KV_W].astype(BF16)
        um = pm[:, 2 * KV_W:2 * KV_W + CONV_CH] * _sigmoid(pm[:, 2 * KV_W + CONV_CH:])
        ubuf[0:U_CARRY - N_META, :] = jnp.zeros((U_CARRY - N_META, CONV_CH), F32)
        ubuf[U_CARRY - N_META:U_CARRY, :] = um

    x = x_ref[...]
    h = _rms(x, g1).astype(BF16)
    q = jnp.dot(h, win_ref[:, 0:ATTN_W], preferred_element_type=F32) * (SCALE * LOG2E)
    qbuf[...] = q.astype(BF16)
    kv = jnp.dot(h, win_ref[:, O_K:O_A], preferred_element_type=F32)
    kbuf[BLOCK:BLOCK + tile, :] = kv[:, 0:KV_W].astype(BF16)
    vbuf[BLOCK:BLOCK + tile, :] = kv[:, KV_W:].astype(BF16)
    kvlast[...] = kv[tile - WINDOW:, :]

    a = jnp.dot(h, win_ref[:, O_A:O_B], preferred_element_type=F32)
    b = jnp.dot(h, win_ref[:, O_B:], preferred_element_type=F32)
    ubuf[U_CARRY:U_CARRY + tile, :] = a * _sigmoid(b)

    cb, lng, lnb = cb_ref[...], lng_ref[...], lnb_ref[...]

    def conv_chunk(r0):
        c = _ln_silu(_conv_rows(ubuf, cw_ref.at[0], r0, CONV_ROWS) + cb, lng, lnb)
        mix[r0:r0 + CONV_ROWS, ATTN_W:] = c.astype(BF16)

    def attn_scores(blk, kvh):
        r0 = blk * BLOCK
        c0 = kvh * HEAD_DIM
        qg = jnp.concatenate(
            [qbuf[r0:r0 + BLOCK, (kvh * GROUP + g) * HEAD_DIM:(kvh * GROUP + g + 1) * HEAD_DIM]
             for g in range(GROUP)], axis=0)
        kk = kbuf[r0:r0 + 2 * BLOCK, c0:c0 + HEAD_DIM]
        return lax.dot_general(qg, kk, NT_DIMS, preferred_element_type=F32)

    def attn_finish(blk, kvh, s):
        r0 = blk * BLOCK
        sel = jnp.where(i == 0, 1, 0) if blk == 0 else 0
        c0 = kvh * HEAD_DIM
        vv = vbuf[r0:r0 + 2 * BLOCK, c0:c0 + HEAD_DIM]
        ps, ls = [], []
        for g in range(GROUP):
            hd = kvh * GROUP + g
            sg = s[g * BLOCK:(g + 1) * BLOCK] + bias_s[sel, hd]
            sk = sink_ref[hd] * LOG2E
            m = jnp.maximum(jnp.max(sg, axis=-1, keepdims=True), sk)
            p = jnp.exp2(sg - m)
            ls.append(jnp.sum(p, axis=-1, keepdims=True) + jnp.exp2(sk - m))
            ps.append(p.astype(BF16))
        o = jnp.dot(jnp.concatenate(ps, axis=0), vv, preferred_element_type=F32)
        for g in range(GROUP):
            hd = kvh * GROUP + g
            og = o[g * BLOCK:(g + 1) * BLOCK] / ls[g]
            mix[r0:r0 + BLOCK, hd * HEAD_DIM:(hd + 1) * HEAD_DIM] = og.astype(BF16)

    conv_starts = list(range(0, tile, CONV_ROWS))
    attn_units = [(blk, kvh) for blk in range(tile // BLOCK) for kvh in range(N_KV_HEADS)]
    n_up, n_down = D_FF // UP_CHUNK, D_MODEL // DOWN_CHUNK
    n_slots = n_up + n_down
    unit_iter, conv_iter = iter(attn_units), iter(conv_starts)
    xf = x2buf[1 - slot]
    hfbuf[...] = _rms(xf, g2_ref[...]).astype(BF16)
    x3_cols = []
    for k in range(n_slots):
        units = [next(unit_iter) for _ in range(SLOT_ATTN_UNITS[k])]
        scores = [(blk, kvh, attn_scores(blk, kvh)) for blk, kvh in units]
        if k < n_up:
            c0 = k * UP_CHUNK
            hid = jnp.dot(hfbuf[...], wup_ref[:, c0:c0 + UP_CHUNK], preferred_element_type=F32)
            hidbuf[:, c0:c0 + UP_CHUNK] = _relu_sq_bf16(hid)
        else:
            n0 = (k - n_up) * DOWN_CHUNK
            x3_cols.append(xf[:, n0:n0 + DOWN_CHUNK]
                           + jnp.dot(hidbuf[...], wdn_ref[:, n0:n0 + DOWN_CHUNK],
                                     preferred_element_type=F32))
        zero = None
        for r0 in [next(conv_iter) for _ in range(SLOT_CONV_CHUNKS[k])]:
            conv_chunk(r0)
            zero = _zero_after(mix[r0:r0 + CONV_ROWS, ATTN_W:], zero)
        for blk, kvh, s in scores:
            attn_finish(blk, kvh, s)
        if k + 1 < n_slots and zero is not None:
            _order_after(hfbuf if k + 1 < n_up else hidbuf, zero)
    acc = jnp.concatenate(x3_cols, axis=1)
    y = _rms(acc, gf_ref[...])
    y_ref[...] = y
    ysbuf[...] = y[0:ysbuf.shape[0]]

    x2buf[slot] = x + jnp.dot(mix[...], wout_ref[...], preferred_element_type=F32)

    @pl.when(i == n_tiles - 1)
    def _new_caches():
        nk_ref[...] = kvlast[:, 0:KV_W].T
        nv_ref[...] = kvlast[:, KV_W:].T
        nc_ref[:, 0, :] = ubuf[U_CARRY + tile - (CONV_WIDTH - 1):U_CARRY + tile, :]

    kbuf[0:BLOCK, :] = kbuf[tile:tile + BLOCK, :]
    vbuf[0:BLOCK, :] = vbuf[tile:tile + BLOCK, :]
    ubuf[0:U_CARRY, :] = ubuf[tile:tile + U_CARRY, :]

    @pl.when(i == 0)
    def _sample_out():
        ys_ref[:, 0, :] = ysbuf[...]


def _const_spec(shape):
    return pl.BlockSpec(shape, lambda i: (0,) * len(shape), pipeline_mode=pl.Buffered(1))


def _smem_spec():
    return pl.BlockSpec(memory_space=pltpu.SMEM)


def _prompt_layer(x, meta, rel_bias, sinks, g1, w_in, bucket, cw, cb, lng, lnb, w_out,
                  g2, w_up, w_down, gf, x2_s):
    seq = x.shape[0]
    nb = x2_s.shape[0]
    tile = PROMPT_TILE
    n_tiles = seq // tile
    body = functools.partial(_prompt_layer_body, tile=tile)
    return pl.pallas_call(
        body,
        grid=(n_tiles + 1,),
        in_specs=[
            _smem_spec(), _smem_spec(),
            pl.BlockSpec((tile, D_MODEL), lambda i: (jnp.minimum(i, n_tiles - 1), 0)),
            _const_spec((N_META, D_MODEL)),
            _const_spec((1, D_MODEL)),
            pl.BlockSpec(memory_space=pl.ANY),
            _const_spec((BLOCK, 2 * BLOCK)),
            _const_spec((1, CONV_WIDTH, CONV_CH)),
            _const_spec((1, CONV_CH)),
            _const_spec((1, CONV_CH)),
            _const_spec((1, CONV_CH)),
            pl.BlockSpec(memory_space=pl.ANY),
            _const_spec((1, D_MODEL)),
            pl.BlockSpec(memory_space=pl.ANY),
            pl.BlockSpec(memory_space=pl.ANY),
            _const_spec((1, D_MODEL)),
            _const_spec((nb, D_MODEL)),
        ],
        out_specs=[
            pl.BlockSpec((tile, D_MODEL), lambda i: (jnp.maximum(i - 1, 0), 0)),
            pl.BlockSpec((KV_W, WINDOW), lambda i: (0, 0)),
            pl.BlockSpec((KV_W, WINDOW), lambda i: (0, 0)),
            pl.BlockSpec((CONV_WIDTH - 1, 1, CONV_CH), lambda i: (0, 0, 0)),
            pl.BlockSpec((nb, 1, D_MODEL), lambda i: (0, 0, 0)),
        ],
        out_shape=[
            jax.ShapeDtypeStruct((seq, D_MODEL), F32),
            jax.ShapeDtypeStruct((KV_W, WINDOW), F32),
            jax.ShapeDtypeStruct((KV_W, WINDOW), F32),
            jax.ShapeDtypeStruct((CONV_WIDTH - 1, 1, CONV_CH), F32),
            jax.ShapeDtypeStruct((nb, 1, D_MODEL), F32),
        ],
        scratch_shapes=[
            pltpu.VMEM((BLOCK + tile, KV_W), BF16),
            pltpu.VMEM((BLOCK + tile, KV_W), BF16),
            pltpu.VMEM((U_CARRY + tile, CONV_CH), F32),
            pltpu.VMEM((tile, ATTN_W), BF16),
            pltpu.VMEM((tile, ATTN_W + CONV_CH), BF16),
            pltpu.VMEM((2, N_HEADS, BLOCK, 2 * BLOCK), F32),
            pltpu.VMEM((2, tile, D_MODEL), F32),
            pltpu.VMEM((WINDOW, 2 * KV_W), F32),
            pltpu.VMEM((tile, D_MODEL), BF16),
            pltpu.VMEM((tile, D_FF), BF16),
            pltpu.VMEM((nb, D_MODEL), F32),
            pltpu.VMEM((D_MODEL, D_FF), BF16),
            pltpu.VMEM((D_FF, D_MODEL), BF16),
            pltpu.VMEM((D_MODEL, IN_W), BF16),
            pltpu.VMEM((ATTN_W + CONV_CH, D_MODEL), BF16),
            pltpu.VMEM((WEIGHT_STAGE_DEPTH,) + WEIGHT_STAGE_BLOCK, F32),
            pltpu.SemaphoreType.DMA((WEIGHT_STAGE_DEPTH,)),
        ],
        compiler_params=pltpu.CompilerParams(
            dimension_semantics=("arbitrary",), vmem_limit_bytes=V7X_VMEM_LIMIT_BYTES),
        name="prompt_layer",
    )(rel_bias, sinks, x, meta, g1, w_in, bucket, cw, cb, lng, lnb, w_out, g2, w_up, w_down, gf,
      x2_s)


def _sample_mixer_body(relb_ref, sink_ref, x_ref, ckt_ref, cvt_ref, st_ref, g1_ref, win_ref,
                       bucket_ref, cw_ref, cb_ref, lng_ref, lnb_ref, wout_hbm,
                       x2_ref, nkt_ref, nvt_ref, nc_ref,
                       pbuf, mix, bias_c, sink_c, wout_ref, wout_sem, *, chunk):
    i = pl.program_id(0)
    last = pl.num_programs(0) - 1
    rows_h = chunk * HEAD_ROWS
    wout_copy = pltpu.make_async_copy(wout_hbm, wout_ref, wout_sem.at[0])

    @pl.when(i == 0)
    def _init():
        wout_copy.start()
        h = _rms(x_ref[:, 0, :], g1_ref[...]).astype(BF16)
        pbuf[...] = jnp.dot(h, win_ref[...].astype(BF16), preferred_element_type=F32)
        bucket = bucket_ref[...]
        rid = lax.broadcasted_iota(jnp.int32, (HEAD_ROWS, 1), 0)
        bias = jnp.zeros((HEAD_ROWS, 2 * BLOCK), F32)
        sk = jnp.zeros((HEAD_ROWS, 1), F32)
        for hd in range(N_HEADS):
            bias = jnp.where(rid == hd, _bias_from_buckets(bucket, relb_ref, hd), bias)
            sk = jnp.where(rid == hd, sink_ref[hd], sk)
        bias_c[...] = bias
        sink_c[...] = sk

    r0 = pl.multiple_of(i * chunk, chunk)
    pr = pbuf[pl.ds(r0, chunk), :]
    q = pr[:, 0:ATTN_W] * SCALE
    knew = pr[:, O_K:O_V]
    vnew = pr[:, O_V:O_A]
    unew = pr[:, O_A:O_B] * _sigmoid(pr[:, O_B:])

    def per_head(t):
        n = t.shape[-1]
        return jnp.broadcast_to(t[:, None, :], (chunk, HEAD_ROWS, n)).reshape(rows_h, n)

    hid = lax.broadcasted_iota(jnp.int32, (rows_h, 1), 0) % HEAD_ROWS
    lane = lax.broadcasted_iota(jnp.int32, (1, BLOCK), 1)
    qrep = per_head(q)
    qsum = jnp.zeros((rows_h, BLOCK), F32)
    for c in range(ATTN_W // BLOCK):
        piece = qrep[:, c * BLOCK:(c + 1) * BLOCK]
        in_head = (lane // HEAD_DIM + 2 * c) == hid
        qsum = qsum + jnp.where(in_head, piece, 0.0)
    keep = (hid % 2) == (hid // GROUP)
    qf = jnp.where(keep, qsum, pltpu.roll(qsum, HEAD_DIM, axis=1))
    qf_b = qf.astype(BF16)

    bias = jnp.broadcast_to(bias_c[...][None], (chunk, HEAD_ROWS, 2 * BLOCK)).reshape(rows_h, 2 * BLOCK)
    sk = jnp.broadcast_to(sink_c[...][None], (chunk, HEAD_ROWS, 1)).reshape(rows_h, 1)

    s_rows = []
    for b in range(chunk):
        kt = ckt_ref[b].astype(BF16)
        s_rows.append(jnp.dot(qf_b[b * HEAD_ROWS:(b + 1) * HEAD_ROWS], kt, preferred_element_type=F32))
    s_c = jnp.concatenate(s_rows, axis=0) + bias[:, 0:BLOCK]
    s_n = jnp.sum(qf * per_head(knew), axis=-1, keepdims=True) + bias[:, BLOCK:BLOCK + 1]
    m = jnp.maximum(jnp.maximum(jnp.max(s_c, axis=-1, keepdims=True), s_n), sk)
    p_c = jnp.exp(s_c - m)
    p_n = jnp.exp(s_n - m)
    l = jnp.sum(p_c, axis=-1, keepdims=True) + p_n + jnp.exp(sk - m)
    p_cb = p_c.astype(BF16)
    o_rows = []
    for b in range(chunk):
        vt = cvt_ref[b].astype(BF16)
        o_rows.append(lax.dot_general(p_cb[b * HEAD_ROWS:(b + 1) * HEAD_ROWS], vt, NT_DIMS,
                                      preferred_element_type=F32))
    o = (jnp.concatenate(o_rows, axis=0) + p_n * per_head(vnew)) / l
    o = jnp.where(keep, o, pltpu.roll(o, HEAD_DIM, axis=1))
    o = jnp.where((lane // HEAD_DIM) == (hid % 2), o, 0.0)
    wide = jnp.concatenate([jnp.where(hid // 2 == c, o, 0.0) for c in range(ATTN_W // BLOCK)], axis=1)
    gi = lax.broadcasted_iota(jnp.int32, (chunk, rows_h), 0)
    gj = lax.broadcasted_iota(jnp.int32, (chunk, rows_h), 1)
    gather = jnp.where(gj // HEAD_ROWS == gi, 1.0, 0.0).astype(BF16)
    ao = jnp.dot(gather, wide.astype(BF16), preferred_element_type=F32)
    mix[pl.ds(r0, chunk), 0:ATTN_W] = ao

    pad = jnp.zeros((BLOCK - chunk, KV_W), F32)
    knew_t = jnp.concatenate([knew, pad], axis=0).T
    vnew_t = jnp.concatenate([vnew, pad], axis=0).T
    newest = lane == WINDOW - 1
    for b in range(chunk):
        kcol = jnp.broadcast_to(knew_t[:, b:b + 1], (KV_W, WINDOW))
        vcol = jnp.broadcast_to(vnew_t[:, b:b + 1], (KV_W, WINDOW))
        nkt_ref[b] = jnp.where(newest, kcol, pltpu.roll(ckt_ref[b], WINDOW - 1, axis=1))
        nvt_ref[b] = jnp.where(newest, vcol, pltpu.roll(cvt_ref[b], WINDOW - 1, axis=1))

    acc = cb_ref[...] + cw_ref[0, CONV_WIDTH - 1:CONV_WIDTH, :] * unew
    for w in range(CONV_WIDTH - 1):
        acc = acc + cw_ref[0, w:w + 1, :] * st_ref[w]
    nc_ref[0:CONV_WIDTH - 2] = st_ref[1:CONV_WIDTH - 1]
    nc_ref[CONV_WIDTH - 2] = unew
    mu = jnp.mean(acc, axis=-1, keepdims=True)
    xc = acc - mu
    y = xc * lax.rsqrt(jnp.mean(xc * xc, axis=-1, keepdims=True) + EPS)
    y = y * lng_ref[...] + lnb_ref[...]
    mix[pl.ds(r0, chunk), ATTN_W:] = y * _sigmoid(y)

    @pl.when(i == last)
    def _out():
        wout_copy.wait()
        x2_ref[...] = x_ref[:, 0, :] + jnp.dot(mix[...].astype(BF16), wout_ref[...].astype(BF16),
                                           preferred_element_type=F32)


def _sample_mixer(x, ckt, cvt, st, rel_bias, sinks, g1, w_in, bucket, cw, cb, lng, lnb, w_out):
    nb = x.shape[0]
    chunk = SAMPLE_CHUNK
    body = functools.partial(_sample_mixer_body, chunk=chunk)
    cache_spec = pl.BlockSpec((chunk, KV_W, WINDOW), lambda i: (i, 0, 0))
    state_spec = pl.BlockSpec((CONV_WIDTH - 1, chunk, CONV_CH), lambda i: (0, i, 0))
    return pl.pallas_call(
        body,
        grid=(nb // chunk,),
        in_specs=[
            _smem_spec(), _smem_spec(),
            _const_spec((nb, 1, D_MODEL)),
            cache_spec, cache_spec, state_spec,
            _const_spec((1, D_MODEL)),
            _const_spec((D_MODEL, IN_W)),
            _const_spec((1, 2 * BLOCK)),
            _const_spec((1, CONV_WIDTH, CONV_CH)),
            _const_spec((1, CONV_CH)),
            _const_spec((1, CONV_CH)),
            _const_spec((1, CONV_CH)),
            pl.BlockSpec(memory_space=pl.ANY),
        ],
        out_specs=[
            pl.BlockSpec((nb, D_MODEL), lambda i: (0, 0)),
            cache_spec, cache_spec, state_spec,
        ],
        out_shape=[
            jax.ShapeDtypeStruct((nb, D_MODEL), F32),
            jax.ShapeDtypeStruct((nb, KV_W, WINDOW), F32),
            jax.ShapeDtypeStruct((nb, KV_W, WINDOW), F32),
            jax.ShapeDtypeStruct((CONV_WIDTH - 1, nb, CONV_CH), F32),
        ],
        scratch_shapes=[
            pltpu.VMEM((nb, IN_W), F32),
            pltpu.VMEM((nb, ATTN_W + CONV_CH), F32),
            pltpu.VMEM((HEAD_ROWS, 2 * BLOCK), F32),
            pltpu.VMEM((HEAD_ROWS, 1), F32),
            pltpu.VMEM((ATTN_W + CONV_CH, D_MODEL), F32),
            pltpu.SemaphoreType.DMA((1,)),
        ],
        compiler_params=pltpu.CompilerParams(
            dimension_semantics=("arbitrary",), vmem_limit_bytes=V7X_VMEM_LIMIT_BYTES),
        name="sample_mixer",
    )(rel_bias, sinks, x, ckt, cvt, st, g1, w_in, bucket, cw, cb, lng, lnb, w_out)


def kernel(x_prompt, x_sample, cache_k, cache_v, state_conv, meta_tokens, rel_bias, norm1_g, w_in,
           attn_sinks, conv_w, conv_b, conv_ln_g, conv_ln_b, w_out, norm2_g, w_up, w_down, norm_f_g):
    batch, seq, _ = x_prompt.shape
    nb, dec_seq, _ = x_sample.shape
    assert batch == 1 and dec_seq == 1 and w_in.shape[0] == 1
    assert seq % PROMPT_TILE == 0 and nb % SAMPLE_CHUNK == 0 and nb <= PROMPT_TILE
    assert len(SLOT_ATTN_UNITS) == len(SLOT_CONV_CHUNKS) == D_FF // UP_CHUNK + D_MODEL // DOWN_CHUNK
    assert sum(SLOT_ATTN_UNITS) == (PROMPT_TILE // BLOCK) * N_KV_HEADS
    assert sum(SLOT_CONV_CHUNKS) == PROMPT_TILE // CONV_ROWS

    w_in_b, w_out_b = w_in[0], w_out[0]
    g1 = norm1_g[0][None]
    g2 = norm2_g[0][None]
    gf = norm_f_g[None]
    cw, cb = conv_w, conv_b[0][None]
    lng, lnb = conv_ln_g[0][None], conv_ln_b[0][None]
    sinks = attn_sinks[0]

    dist_p = jnp.arange(BLOCK)[:, None] + BLOCK - jnp.arange(2 * BLOCK)[None, :]
    bucket_p = _t5_bucket(jnp.clip(dist_p, 0, WINDOW)).astype(jnp.int32)
    lane = jnp.arange(2 * BLOCK)
    dist_s = jnp.where(lane < WINDOW, WINDOW - lane, 0)
    bucket_s = jnp.where(lane <= WINDOW, _t5_bucket(jnp.clip(dist_s, 0, WINDOW)), -1)
    bucket_s = bucket_s.astype(jnp.int32)[None]

    ckt = jnp.transpose(cache_k[0], (0, 2, 3, 1)).reshape(nb, KV_W, WINDOW)
    cvt = jnp.transpose(cache_v[0], (0, 2, 3, 1)).reshape(nb, KV_W, WINDOW)
    st = jnp.transpose(state_conv[0], (1, 0, 2))
    x2_s, nkt_s, nvt_s, nct_s = _sample_mixer(x_sample, ckt, cvt, st, rel_bias, sinks, g1,
                                              w_in_b, bucket_s, cw, cb, lng, lnb, w_out_b)
    nk_s = jnp.transpose(nkt_s.reshape(nb, N_KV_HEADS, HEAD_DIM, WINDOW), (0, 3, 1, 2))
    nv_s = jnp.transpose(nvt_s.reshape(nb, N_KV_HEADS, HEAD_DIM, WINDOW), (0, 3, 1, 2))
    nc_s = jnp.transpose(nct_s, (1, 0, 2))
    y_p, nkt_p, nvt_p, nc_p, y_s = _prompt_layer(x_prompt[0], meta_tokens, rel_bias, sinks, g1, w_in_b,
                                               bucket_p, cw, cb, lng, lnb, w_out_b, g2, w_up[0],
                                               w_down[0], gf, x2_s)

    def to_cache(t):
        return jnp.transpose(t.reshape(N_KV_HEADS, HEAD_DIM, WINDOW), (2, 0, 1))[None, None]

    return (y_p[None], y_s,
            to_cache(nkt_p), to_cache(nvt_p), jnp.transpose(nc_p, (1, 0, 2))[None],
            nk_s[None], nv_s[None], nc_s[None])
```

```python
import functools
import math

import jax
import jax.numpy as jnp
from jax import lax
from jax.experimental import pallas as pl
from jax.experimental.pallas import tpu as pltpu

D_MODEL = 1024
N_HEADS = 8
N_KV_HEADS = 2
HEAD_DIM = 64
GROUP = N_HEADS // N_KV_HEADS
ATTN_W = N_HEADS * HEAD_DIM
KV_W = N_KV_HEADS * HEAD_DIM
CONV_CH = D_MODEL - ATTN_W
IN_W = ATTN_W + 2 * KV_W + 2 * CONV_CH
CONV_WIDTH = 31
WINDOW = 128
BLOCK = 128
N_BUCKETS = 32
MAX_DISTANCE = WINDOW
N_META = 16
D_FF = 4 * D_MODEL
EPS = 1e-6
SCALE = HEAD_DIM ** -0.5
LOG2E = math.log2(math.e)

O_K = ATTN_W
O_V = ATTN_W + KV_W
O_A = ATTN_W + 2 * KV_W
O_B = O_A + CONV_CH

PAD = (-N_META) % BLOCK
U_CARRY = 32
U_SHIFT = U_CARRY - (CONV_WIDTH - 1)

V7X_VMEM_LIMIT_BYTES = 60 * 1024 * 1024

PROMPT_TILE = 512
UP_CHUNK = 1024
DOWN_CHUNK = 512
SLOT_ATTN_UNITS = (2, 2, 2, 2, 0, 0)
SLOT_CONV_CHUNKS = (0, 0, 0, 0, 4, 4)
CONV_ROWS = 64
WEIGHT_STAGE_BLOCK = (256, 1024)
WEIGHT_STAGE_DEPTH = 4
SAMPLE_CHUNK = 32
HEAD_ROWS = 16

BF16 = jnp.bfloat16
F32 = jnp.float32
NT_DIMS = (((1,), (1,)), ((), ()))


def _t5_bucket(d):
    max_exact = N_BUCKETS // 2
    d_f = jnp.maximum(d, 1).astype(jnp.float32)
    large = max_exact + (jnp.log(d_f / max_exact) / math.log(MAX_DISTANCE / max_exact)
                         * (N_BUCKETS - max_exact)).astype(jnp.int32)
    large = jnp.minimum(large, N_BUCKETS - 1)
    return jnp.where(d < max_exact, d, large)


def _rms(x, g):
    y = x * lax.rsqrt(jnp.mean(x * x, axis=-1, keepdims=True) + EPS)
    return y * g


def _relu_sq_bf16(x):
    r = jnp.maximum(x.astype(BF16), 0.0)
    return r * r


def _sigmoid(x):
    return 1.0 / (1.0 + jnp.exp(-x))


def _bias_from_buckets(bucket, relb_ref, h):
    b = jnp.zeros(bucket.shape, F32)
    for bk in range(N_BUCKETS):
        b = jnp.where(bucket == bk, relb_ref[bk, h], b)
    return b


def _conv_rows(ubuf, cw_ref, r0, rows):
    n = rows + U_CARRY
    strips = []
    for c0 in range(0, CONV_CH, BLOCK):
        win = ubuf[r0:r0 + n, c0:c0 + BLOCK]
        acc = None
        for s in range(8):
            sh = win if s == 0 else pltpu.roll(win, n - s, axis=0)
            for a0 in range(0, U_CARRY + 8, 8):
                w = a0 + s - U_SHIFT
                if 0 <= w < CONV_WIDTH:
                    term = cw_ref[w:w + 1, c0:c0 + BLOCK] * sh[a0:a0 + rows]
                    acc = term if acc is None else acc + term
        strips.append(acc)
    return jnp.concatenate(strips, axis=1)


def _ln_silu(acc, lng, lnb):
    mu = jnp.mean(acc, axis=-1, keepdims=True)
    xc = acc - mu
    y = xc * lax.rsqrt(jnp.mean(xc * xc, axis=-1, keepdims=True) + EPS)
    y = y * lng + lnb
    return y * _sigmoid(y)


def _zero_after(v, prev=None):
    u = pltpu.bitcast(v, jnp.uint32)
    t = prev
    for r0 in range(0, u.shape[0], 8):
        for c0 in range(0, u.shape[1], BLOCK):
            piece = u[r0:r0 + 8, c0:c0 + BLOCK]
            t = piece if t is None else t | piece
    return (t >> 16) >> 16


def _order_after(buf, zero):
    tile = pltpu.bitcast(buf[0:16, 0:BLOCK], jnp.uint32)
    buf[0:16, 0:BLOCK] = pltpu.bitcast(tile | zero, BF16)


def _stream_cast(pairs, stage, sem, fillers=()):
    fillers = list(fillers)
    depth, rows, cols = stage.shape
    blocks = [(src, dst, r, c, min(cols, src.shape[1] - c)) for src, dst in pairs
              for r in range(0, src.shape[0], rows) for c in range(0, src.shape[1], cols)]

    def copy(j):
        src, _, r, c, w = blocks[j]
        return pltpu.make_async_copy(src.at[pl.ds(r, rows), pl.ds(c, w)],
                                     stage.at[j % depth, :, pl.ds(0, w)], sem.at[j % depth])

    for j in range(min(depth - 1, len(blocks))):
        copy(j).start(priority=j % 2)
    for j, (_, dst, r, c, w) in enumerate(blocks):
        if j + depth - 1 < len(blocks):
            copy(j + depth - 1).start(priority=(j + depth - 1) % 2)
        if fillers:
            fillers.pop(0)()
        copy(j).wait()
        dst[r:r + rows, c:c + w] = stage[j % depth, :, 0:w].astype(BF16)
    for filler in fillers:
        filler()


def _prompt_layer_body(*refs, tile):
    i = pl.program_id(0)
    n_tiles = pl.num_programs(0) - 1
    pl.when(i < n_tiles)(functools.partial(_prompt_step, *refs, tile=tile))
    pl.when(i == n_tiles)(functools.partial(_prompt_last_ffn, *refs, tile=tile))


def _prompt_last_ffn(relb_ref, sink_ref, x_ref, meta_ref, g1_ref, win_hbm, bucket_ref,
                     cw_ref, cb_ref, lng_ref, lnb_ref, wout_hbm, g2_ref, wup_hbm, wdn_hbm, gf_ref,
                     x2s_ref, y_ref, nk_ref, nv_ref, nc_ref, ys_ref,
                     kbuf, vbuf, ubuf, qbuf, mix, bias_s, x2buf, kvlast, hfbuf, hidbuf, ysbuf,
                     wup_ref, wdn_ref, win_ref, wout_ref, stage, wsem, *, tile):
    xf = x2buf[1 - pl.program_id(0) % 2]
    hf = _rms(xf, g2_ref[...]).astype(BF16)
    for c0 in range(0, D_FF, UP_CHUNK):
        hid = jnp.dot(hf, wup_ref[:, c0:c0 + UP_CHUNK], preferred_element_type=F32)
        hidbuf[:, c0:c0 + UP_CHUNK] = _relu_sq_bf16(hid)
    cols = [xf[:, n0:n0 + DOWN_CHUNK]
            + jnp.dot(hidbuf[...], wdn_ref[:, n0:n0 + DOWN_CHUNK], preferred_element_type=F32)
            for n0 in range(0, D_MODEL, DOWN_CHUNK)]
    y_ref[...] = _rms(jnp.concatenate(cols, axis=1), gf_ref[...])


def _prompt_step(relb_ref, sink_ref, x_ref, meta_ref, g1_ref, win_hbm, bucket_ref,
                 cw_ref, cb_ref, lng_ref, lnb_ref, wout_hbm, g2_ref, wup_hbm, wdn_hbm, gf_ref,
                 x2s_ref, y_ref, nk_ref, nv_ref, nc_ref, ys_ref,
                 kbuf, vbuf, ubuf, qbuf, mix, bias_s, x2buf, kvlast, hfbuf, hidbuf, ysbuf,
                 wup_ref, wdn_ref, win_ref, wout_ref, stage, wsem, *, tile):
    i = pl.program_id(0)
    n_tiles = pl.num_programs(0) - 1
    g1 = g1_ref[...]
    slot = i % 2

    @pl.when(i == 0)
    def _init():
        def decode_rows():
            nb = x2s_ref.shape[0]
            x2buf[1, 0:nb, :] = x2s_ref[...]
            x2buf[1, nb:tile, :] = jnp.zeros((tile - nb, D_MODEL), F32)

        def bias_table(h):
            bucket = bucket_ref[...]
            row = lax.broadcasted_iota(jnp.int32, (BLOCK, 2 * BLOCK), 0)
            col = lax.broadcasted_iota(jnp.int32, (BLOCK, 2 * BLOCK), 1)
            dist = row + BLOCK - col
            band = (dist >= 0) & (dist <= WINDOW)
            band_first = band & (col >= PAD)
            b = _bias_from_buckets(bucket, relb_ref, h) * LOG2E
            bias_s[0, h] = jnp.where(band, b, -jnp.inf)
            bias_s[1, h] = jnp.where(band_first, b, -jnp.inf)

        _stream_cast([(win_hbm, win_ref), (wout_hbm, wout_ref), (wup_hbm, wup_ref), (wdn_hbm, wdn_ref)],
                     stage, wsem,
                     fillers=[decode_rows] + [functools.partial(bias_table, h) for h in range(N_HEADS)])
        hm = _rms(meta_ref[...], g1).astype(BF16)
        pm = jnp.dot(hm, win_ref[:, O_K:], preferred_element_type=F32)
        kbuf[0:PAD, :] = jnp.zeros((PAD, KV_W), BF16)
        vbuf[0:PAD, :] = jnp.zeros((PAD, KV_W), BF16)
        kbuf[PAD:BLOCK, :] = pm[:, 0:KV_W].astype(BF16)
        vbuf[PAD:BLOCK, :] = pm[:, KV_W:2 * KV_W].astype(BF16)
        um = pm[:, 2 * KV_W:2 * KV_W + CONV_CH] * _sigmoid(pm[:, 2 * KV_W + CONV_CH:])
        ubuf[0:U_CARRY - N_META, :] = jnp.zeros((U_CARRY - N_META, CONV_CH), F32)
        ubuf[U_CARRY - N_META:U_CARRY, :] = um

    x = x_ref[...]
    h = _rms(x, g1).astype(BF16)
    q = jnp.dot(h, win_ref[:, 0:ATTN_W], preferred_element_type=F32) * (SCALE * LOG2E)
    qbuf[...] = q.astype(BF16)
    kv = jnp.dot(h, win_ref[:, O_K:O_A], preferred_element_type=F32)
    kbuf[BLOCK:BLOCK + tile, :] = kv[:, 0:KV_W].astype(BF16)
    vbuf[BLOCK:BLOCK + tile, :] = kv[:, KV_W:].astype(BF16)
    kvlast[...] = kv[tile - WINDOW:, :]

    a = jnp.dot(h, win_ref[:, O_A:O_B], preferred_element_type=F32)
    b = jnp.dot(h, win_ref[:, O_B:], preferred_element_type=F32)
    ubuf[U_CARRY:U_CARRY + tile, :] = a * _sigmoid(b)

    cb, lng, lnb = cb_ref[...], lng_ref[...], lnb_ref[...]

    def conv_chunk(r0):
        c = _ln_silu(_conv_rows(ubuf, cw_ref.at[0], r0, CONV_ROWS) + cb, lng, lnb)
        mix[r0:r0 + CONV_ROWS, ATTN_W:] = c.astype(BF16)

    def attn_scores(blk, kvh):
        r0 = blk * BLOCK
        c0 = kvh * HEAD_DIM
        qg = jnp.concatenate(
            [qbuf[r0:r0 + BLOCK, (kvh * GROUP + g) * HEAD_DIM:(kvh * GROUP + g + 1) * HEAD_DIM]
             for g in range(GROUP)], axis=0)
        kk = kbuf[r0:r0 + 2 * BLOCK, c0:c0 + HEAD_DIM]
        return lax.dot_general(qg, kk, NT_DIMS, preferred_element_type=F32)

    def attn_finish(blk, kvh, s):
        r0 = blk * BLOCK
        sel = jnp.where(i == 0, 1, 0) if blk == 0 else 0
        c0 = kvh * HEAD_DIM
        vv = vbuf[r0:r0 + 2 * BLOCK, c0:c0 + HEAD_DIM]
        ps, ls = [], []
        for g in range(GROUP):
            hd = kvh * GROUP + g
            sg = s[g * BLOCK:(g + 1) * BLOCK] + bias_s[sel, hd]
            sk = sink_ref[hd] * LOG2E
            m = jnp.maximum(jnp.max(sg, axis=-1, keepdims=True), sk)
            p = jnp.exp2(sg - m)
            ls.append(jnp.sum(p, axis=-1, keepdims=True) + jnp.exp2(sk - m))
            ps.append(p.astype(BF16))
        o = jnp.dot(jnp.concatenate(ps, axis=0), vv, preferred_element_type=F32)
        for g in range(GROUP):
            hd = kvh * GROUP + g
            og = o[g * BLOCK:(g + 1) * BLOCK] / ls[g]
            mix[r0:r0 + BLOCK, hd * HEAD_DIM:(hd + 1) * HEAD_DIM] = og.astype(BF16)

    conv_starts = list(range(0, tile, CONV_ROWS))
    attn_units = [(blk, kvh) for blk in range(tile // BLOCK) for kvh in range(N_KV_HEADS)]
    n_up, n_down = D_FF // UP_CHUNK, D_MODEL // DOWN_CHUNK
    n_slots = n_up + n_down
    unit_iter, conv_iter = iter(attn_units), iter(conv_starts)
    xf = x2buf[1 - slot]
    hfbuf[...] = _rms(xf, g2_ref[...]).astype(BF16)
    x3_cols = []
    for k in range(n_slots):
        units = [next(unit_iter) for _ in range(SLOT_ATTN_UNITS[k])]
        scores = [(blk, kvh, attn_scores(blk, kvh)) for blk, kvh in units]
        if k < n_up:
            c0 = k * UP_CHUNK
            hid = jnp.dot(hfbuf[...], wup_ref[:, c0:c0 + UP_CHUNK], preferred_element_type=F32)
            hidbuf[:, c0:c0 + UP_CHUNK] = _relu_sq_bf16(hid)
        else:
            n0 = (k - n_up) * DOWN_CHUNK
            x3_cols.append(xf[:, n0:n0 + DOWN_CHUNK]
                           + jnp.dot(hidbuf[...], wdn_ref[:, n0:n0 + DOWN_CHUNK],
                                     preferred_element_type=F32))
        zero = None
        for r0 in [next(conv_iter) for _ in range(SLOT_CONV_CHUNKS[k])]:
            conv_chunk(r0)
            zero = _zero_after(mix[r0:r0 + CONV_ROWS, ATTN_W:], zero)
        for blk, kvh, s in scores:
            attn_finish(blk, kvh, s)
        if k + 1 < n_slots and zero is not None:
            _order_after(hfbuf if k + 1 < n_up else hidbuf, zero)
    acc = jnp.concatenate(x3_cols, axis=1)
    y = _rms(acc, gf_ref[...])
    y_ref[...] = y
    ysbuf[...] = y[0:ysbuf.shape[0]]

    x2buf[slot] = x + jnp.dot(mix[...], wout_ref[...], preferred_element_type=F32)

    @pl.when(i == n_tiles - 1)
    def _new_caches():
        nk_ref[...] = kvlast[:, 0:KV_W].T
        nv_ref[...] = kvlast[:, KV_W:].T
        nc_ref[:, 0, :] = ubuf[U_CARRY + tile - (CONV_WIDTH - 1):U_CARRY + tile, :]

    kbuf[0:BLOCK, :] = kbuf[tile:tile + BLOCK, :]
    vbuf[0:BLOCK, :] = vbuf[tile:tile + BLOCK, :]
    ubuf[0:U_CARRY, :] = ubuf[tile:tile + U_CARRY, :]

    @pl.when(i == 0)
    def _sample_out():
        ys_ref[:, 0, :] = ysbuf[...]


def _const_spec(shape):
    return pl.BlockSpec(shape, lambda i: (0,) * len(shape), pipeline_mode=pl.Buffered(1))


def _smem_spec():
    return pl.BlockSpec(memory_space=pltpu.SMEM)


def _prompt_layer(x, meta, rel_bias, sinks, g1, w_in, bucket, cw, cb, lng, lnb, w_out,
                  g2, w_up, w_down, gf, x2_s):
    seq = x.shape[0]
    nb = x2_s.shape[0]
    tile = PROMPT_TILE
    n_tiles = seq // tile
    body = functools.partial(_prompt_layer_body, tile=tile)
    return pl.pallas_call(
        body,
        grid=(n_tiles + 1,),
        in_specs=[
            _smem_spec(), _smem_spec(),
            pl.BlockSpec((tile, D_MODEL), lambda i: (jnp.minimum(i, n_tiles - 1), 0)),
            _const_spec((N_META, D_MODEL)),
            _const_spec((1, D_MODEL)),
            pl.BlockSpec(memory_space=pl.ANY),
            _const_spec((BLOCK, 2 * BLOCK)),
            _const_spec((1, CONV_WIDTH, CONV_CH)),
            _const_spec((1, CONV_CH)),
            _const_spec((1, CONV_CH)),
            _const_spec((1, CONV_CH)),
            pl.BlockSpec(memory_space=pl.ANY),
            _const_spec((1, D_MODEL)),
            pl.BlockSpec(memory_space=pl.ANY),
            pl.BlockSpec(memory_space=pl.ANY),
            _const_spec((1, D_MODEL)),
            _const_spec((nb, D_MODEL)),
        ],
        out_specs=[
            pl.BlockSpec((tile, D_MODEL), lambda i: (jnp.maximum(i - 1, 0), 0)),
            pl.BlockSpec((KV_W, WINDOW), lambda i: (0, 0)),
            pl.BlockSpec((KV_W, WINDOW), lambda i: (0, 0)),
            pl.BlockSpec((CONV_WIDTH - 1, 1, CONV_CH), lambda i: (0, 0, 0)),
            pl.BlockSpec((nb, 1, D_MODEL), lambda i: (0, 0, 0)),
        ],
        out_shape=[
            jax.ShapeDtypeStruct((seq, D_MODEL), F32),
            jax.ShapeDtypeStruct((KV_W, WINDOW), F32),
            jax.ShapeDtypeStruct((KV_W, WINDOW), F32),
            jax.ShapeDtypeStruct((CONV_WIDTH - 1, 1, CONV_CH), F32),
            jax.ShapeDtypeStruct((nb, 1, D_MODEL), F32),
        ],
        scratch_shapes=[
            pltpu.VMEM((BLOCK + tile, KV_W), BF16),
            pltpu.VMEM((BLOCK + tile, KV_W), BF16),
            pltpu.VMEM((U_CARRY + tile, CONV_CH), F32),
            pltpu.VMEM((tile, ATTN_W), BF16),
            pltpu.VMEM((tile, ATTN_W + CONV_CH), BF16),
            pltpu.VMEM((2, N_HEADS, BLOCK, 2 * BLOCK), F32),
            pltpu.VMEM((2, tile, D_MODEL), F32),
            pltpu.VMEM((WINDOW, 2 * KV_W), F32),
            pltpu.VMEM((tile, D_MODEL), BF16),
            pltpu.VMEM((tile, D_FF), BF16),
            pltpu.VMEM((nb, D_MODEL), F32),
            pltpu.VMEM((D_MODEL, D_FF), BF16),
            pltpu.VMEM((D_FF, D_MODEL), BF16),
            pltpu.VMEM((D_MODEL, IN_W), BF16),
            pltpu.VMEM((ATTN_W + CONV_CH, D_MODEL), BF16),
            pltpu.VMEM((WEIGHT_STAGE_DEPTH,) + WEIGHT_STAGE_BLOCK, F32),
            pltpu.SemaphoreType.DMA((WEIGHT_STAGE_DEPTH,)),
        ],
        compiler_params=pltpu.CompilerParams(
            dimension_semantics=("arbitrary",), vmem_limit_bytes=V7X_VMEM_LIMIT_BYTES),
        name="prompt_layer",
    )(rel_bias, sinks, x, meta, g1, w_in, bucket, cw, cb, lng, lnb, w_out, g2, w_up, w_down, gf,
      x2_s)


def _sample_mixer_body(relb_ref, sink_ref, x_ref, ckt_ref, cvt_ref, st_ref, g1_ref, win_ref,
                       bucket_ref, cw_ref, cb_ref, lng_ref, lnb_ref, wout_hbm,
                       x2_ref, nkt_ref, nvt_ref, nc_ref,
                       pbuf, mix, bias_c, sink_c, wout_ref, wout_sem, *, chunk):
    i = pl.program_id(0)
    last = pl.num_programs(0) - 1
    rows_h = chunk * HEAD_ROWS
    wout_copy = pltpu.make_async_copy(wout_hbm, wout_ref, wout_sem.at[0])

    @pl.when(i == 0)
    def _init():
        wout_copy.start(priority=1)
        h = _rms(x_ref[:, 0, :], g1_ref[...]).astype(BF16)
        pbuf[...] = jnp.dot(h, win_ref[...].astype(BF16), preferred_element_type=F32)
        bucket = bucket_ref[...]
        rid = lax.broadcasted_iota(jnp.int32, (HEAD_ROWS, 1), 0)
        bias = jnp.zeros((HEAD_ROWS, 2 * BLOCK), F32)
        sk = jnp.zeros((HEAD_ROWS, 1), F32)
        for hd in range(N_HEADS):
            bias = jnp.where(rid == hd, _bias_from_buckets(bucket, relb_ref, hd), bias)
            sk = jnp.where(rid == hd, sink_ref[hd], sk)
        bias_c[...] = bias
        sink_c[...] = sk

    r0 = pl.multiple_of(i * chunk, chunk)
    pr = pbuf[pl.ds(r0, chunk), :]
    q = pr[:, 0:ATTN_W] * SCALE
    knew = pr[:, O_K:O_V]
    vnew = pr[:, O_V:O_A]
    unew = pr[:, O_A:O_B] * _sigmoid(pr[:, O_B:])

    def per_head(t):
        n = t.shape[-1]
        return jnp.broadcast_to(t[:, None, :], (chunk, HEAD_ROWS, n)).reshape(rows_h, n)

    hid = lax.broadcasted_iota(jnp.int32, (rows_h, 1), 0) % HEAD_ROWS
    lane = lax.broadcasted_iota(jnp.int32, (1, BLOCK), 1)
    qrep = per_head(q)
    qsum = jnp.zeros((rows_h, BLOCK), F32)
    for c in range(ATTN_W // BLOCK):
        piece = qrep[:, c * BLOCK:(c + 1) * BLOCK]
        in_head = (lane // HEAD_DIM + 2 * c) == hid
        qsum = qsum + jnp.where(in_head, piece, 0.0)
    keep = (hid % 2) == (hid // GROUP)
    qf = jnp.where(keep, qsum, pltpu.roll(qsum, HEAD_DIM, axis=1))
    qf_b = qf.astype(BF16)

    bias = jnp.broadcast_to(bias_c[...][None], (chunk, HEAD_ROWS, 2 * BLOCK)).reshape(rows_h, 2 * BLOCK)
    sk = jnp.broadcast_to(sink_c[...][None], (chunk, HEAD_ROWS, 1)).reshape(rows_h, 1)

    s_rows = []
    for b in range(chunk):
        kt = ckt_ref[b].astype(BF16)
        s_rows.append(jnp.dot(qf_b[b * HEAD_ROWS:(b + 1) * HEAD_ROWS], kt, preferred_element_type=F32))
    s_c = jnp.concatenate(s_rows, axis=0) + bias[:, 0:BLOCK]
    s_n = jnp.sum(qf * per_head(knew), axis=-1, keepdims=True) + bias[:, BLOCK:BLOCK + 1]
    m = jnp.maximum(jnp.maximum(jnp.max(s_c, axis=-1, keepdims=True), s_n), sk)
    p_c = jnp.exp(s_c - m)
    p_n = jnp.exp(s_n - m)
    l = jnp.sum(p_c, axis=-1, keepdims=True) + p_n + jnp.exp(sk - m)
    p_cb = p_c.astype(BF16)
    o_rows = []
    for b in range(chunk):
        vt = cvt_ref[b].astype(BF16)
        o_rows.append(lax.dot_general(p_cb[b * HEAD_ROWS:(b + 1) * HEAD_ROWS], vt, NT_DIMS,
                                      preferred_element_type=F32))
    o = (jnp.concatenate(o_rows, axis=0) + p_n * per_head(vnew)) / l
    o = jnp.where(keep, o, pltpu.roll(o, HEAD_DIM, axis=1))
    o = jnp.where((lane // HEAD_DIM) == (hid % 2), o, 0.0)
    wide = jnp.concatenate([jnp.where(hid // 2 == c, o, 0.0) for c in range(ATTN_W // BLOCK)], axis=1)
    gi = lax.broadcasted_iota(jnp.int32, (chunk, rows_h), 0)
    gj = lax.broadcasted_iota(jnp.int32, (chunk, rows_h), 1)
    gather = jnp.where(gj // HEAD_ROWS == gi, 1.0, 0.0).astype(BF16)
    ao = jnp.dot(gather, wide.astype(BF16), preferred_element_type=F32)
    mix[pl.ds(r0, chunk), 0:ATTN_W] = ao

    pad = jnp.zeros((BLOCK - chunk, KV_W), F32)
    knew_t = jnp.concatenate([knew, pad], axis=0).T
    vnew_t = jnp.concatenate([vnew, pad], axis=0).T
    newest = lane == WINDOW - 1
    for b in range(chunk):
        kcol = jnp.broadcast_to(knew_t[:, b:b + 1], (KV_W, WINDOW))
        vcol = jnp.broadcast_to(vnew_t[:, b:b + 1], (KV_W, WINDOW))
        nkt_ref[b] = jnp.where(newest, kcol, pltpu.roll(ckt_ref[b], WINDOW - 1, axis=1))
        nvt_ref[b] = jnp.where(newest, vcol, pltpu.roll(cvt_ref[b], WINDOW - 1, axis=1))

    acc = cb_ref[...] + cw_ref[0, CONV_WIDTH - 1:CONV_WIDTH, :] * unew
    for w in range(CONV_WIDTH - 1):
        acc = acc + cw_ref[0, w:w + 1, :] * st_ref[w]
    nc_ref[0:CONV_WIDTH - 2] = st_ref[1:CONV_WIDTH - 1]
    nc_ref[CONV_WIDTH - 2] = unew
    mu = jnp.mean(acc, axis=-1, keepdims=True)
    xc = acc - mu
    y = xc * lax.rsqrt(jnp.mean(xc * xc, axis=-1, keepdims=True) + EPS)
    y = y * lng_ref[...] + lnb_ref[...]
    mix[pl.ds(r0, chunk), ATTN_W:] = y * _sigmoid(y)

    @pl.when(i == last)
    def _out():
        wout_copy.wait()
        x2_ref[...] = x_ref[:, 0, :] + jnp.dot(mix[...].astype(BF16), wout_ref[...].astype(BF16),
                                           preferred_element_type=F32)


def _sample_mixer(x, ckt, cvt, st, rel_bias, sinks, g1, w_in, bucket, cw, cb, lng, lnb, w_out):
    nb = x.shape[0]
    chunk = SAMPLE_CHUNK
    body = functools.partial(_sample_mixer_body, chunk=chunk)
    cache_spec = pl.BlockSpec((chunk, KV_W, WINDOW), lambda i: (i, 0, 0))
    state_spec = pl.BlockSpec((CONV_WIDTH - 1, chunk, CONV_CH), lambda i: (0, i, 0))
    return pl.pallas_call(
        body,
        grid=(nb // chunk,),
        in_specs=[
            _smem_spec(), _smem_spec(),
            _const_spec((nb, 1, D_MODEL)),
            cache_spec, cache_spec, state_spec,
            _const_spec((1, D_MODEL)),
            _const_spec((D_MODEL, IN_W)),
            _const_spec((1, 2 * BLOCK)),
            _const_spec((1, CONV_WIDTH, CONV_CH)),
            _const_spec((1, CONV_CH)),
            _const_spec((1, CONV_CH)),
            _const_spec((1, CONV_CH)),
            pl.BlockSpec(memory_space=pl.ANY),
        ],
        out_specs=[
            pl.BlockSpec((nb, D_MODEL), lambda i: (0, 0)),
            cache_spec, cache_spec, state_spec,
        ],
        out_shape=[
            jax.ShapeDtypeStruct((nb, D_MODEL), F32),
            jax.ShapeDtypeStruct((nb, KV_W, WINDOW), F32),
            jax.ShapeDtypeStruct((nb, KV_W, WINDOW), F32),
            jax.ShapeDtypeStruct((CONV_WIDTH - 1, nb, CONV_CH), F32),
        ],
        scratch_shapes=[
            pltpu.VMEM((nb, IN_W), F32),
            pltpu.VMEM((nb, ATTN_W + CONV_CH), F32),
            pltpu.VMEM((HEAD_ROWS, 2 * BLOCK), F32),
            pltpu.VMEM((HEAD_ROWS, 1), F32),
            pltpu.VMEM((ATTN_W + CONV_CH, D_MODEL), F32),
            pltpu.SemaphoreType.DMA((1,)),
        ],
        compiler_params=pltpu.CompilerParams(
            dimension_semantics=("arbitrary",), vmem_limit_bytes=V7X_VMEM_LIMIT_BYTES),
        name="sample_mixer",
    )(rel_bias, sinks, x, ckt, cvt, st, g1, w_in, bucket, cw, cb, lng, lnb, w_out)


def kernel(x_prompt, x_sample, cache_k, cache_v, state_conv, meta_tokens, rel_bias, norm1_g, w_in,
           attn_sinks, conv_w, conv_b, conv_ln_g, conv_ln_b, w_out, norm2_g, w_up, w_down, norm_f_g):
    batch, seq, _ = x_prompt.shape
    nb, dec_seq, _ = x_sample.shape
    assert batch == 1 and dec_seq == 1 and w_in.shape[0] == 1
    assert seq % PROMPT_TILE == 0 and nb % SAMPLE_CHUNK == 0 and nb <= PROMPT_TILE
    assert len(SLOT_ATTN_UNITS) == len(SLOT_CONV_CHUNKS) == D_FF // UP_CHUNK + D_MODEL // DOWN_CHUNK
    assert sum(SLOT_ATTN_UNITS) == (PROMPT_TILE // BLOCK) * N_KV_HEADS
    assert sum(SLOT_CONV_CHUNKS) == PROMPT_TILE // CONV_ROWS

    w_in_b, w_out_b = w_in[0], w_out[0]
    g1 = norm1_g[0][None]
    g2 = norm2_g[0][None]
    gf = norm_f_g[None]
    cw, cb = conv_w, conv_b[0][None]
    lng, lnb = conv_ln_g[0][None], conv_ln_b[0][None]
    sinks = attn_sinks[0]

    dist_p = jnp.arange(BLOCK)[:, None] + BLOCK - jnp.arange(2 * BLOCK)[None, :]
    bucket_p = _t5_bucket(jnp.clip(dist_p, 0, WINDOW)).astype(jnp.int32)
    lane = jnp.arange(2 * BLOCK)
    dist_s = jnp.where(lane < WINDOW, WINDOW - lane, 0)
    bucket_s = jnp.where(lane <= WINDOW, _t5_bucket(jnp.clip(dist_s, 0, WINDOW)), -1)
    bucket_s = bucket_s.astype(jnp.int32)[None]

    ckt = jnp.transpose(cache_k[0], (0, 2, 3, 1)).reshape(nb, KV_W, WINDOW)
    cvt = jnp.transpose(cache_v[0], (0, 2, 3, 1)).reshape(nb, KV_W, WINDOW)
    st = jnp.transpose(state_conv[0], (1, 0, 2))
    x2_s, nkt_s, nvt_s, nct_s = _sample_mixer(x_sample, ckt, cvt, st, rel_bias, sinks, g1,
                                              w_in_b, bucket_s, cw, cb, lng, lnb, w_out_b)
    nk_s = jnp.transpose(nkt_s.reshape(nb, N_KV_HEADS, HEAD_DIM, WINDOW), (0, 3, 1, 2))
    nv_s = jnp.transpose(nvt_s.reshape(nb, N_KV_HEADS, HEAD_DIM, WINDOW), (0, 3, 1, 2))
    nc_s = jnp.transpose(nct_s, (1, 0, 2))
    y_p, nkt_p, nvt_p, nc_p, y_s = _prompt_layer(x_prompt[0], meta_tokens, rel_bias, sinks, g1, w_in_b,
                                               bucket_p, cw, cb, lng, lnb, w_out_b, g2, w_up[0],
                                               w_down[0], gf, x2_s)

    def to_cache(t):
        return jnp.transpose(t.reshape(N_KV_HEADS, HEAD_DIM, WINDOW), (2, 0, 1))[None, None]

    return (y_p[None], y_s,
            to_cache(nkt_p), to_cache(nvt_p), jnp.transpose(nc_p, (1, 0, 2))[None],
            nk_s[None], nv_s[None], nc_s[None])
```

```python
import functools
import math

import jax
import jax.numpy as jnp
from jax import lax
from jax.experimental import pallas as pl
from jax.experimental.pallas import tpu as pltpu

D_MODEL = 1024
N_HEADS = 8
N_KV_HEADS = 2
HEAD_DIM = 64
GROUP = N_HEADS // N_KV_HEADS
ATTN_W = N_HEADS * HEAD_DIM
KV_W = N_KV_HEADS * HEAD_DIM
CONV_CH = D_MODEL - ATTN_W
IN_W = ATTN_W + 2 * KV_W + 2 * CONV_CH
CONV_WIDTH = 31
WINDOW = 128
BLOCK = 128
N_BUCKETS = 32
MAX_DISTANCE = WINDOW
N_META = 16
D_FF = 4 * D_MODEL
EPS = 1e-6
SCALE = HEAD_DIM ** -0.5
LOG2E = math.log2(math.e)

O_K = ATTN_W
O_V = ATTN_W + KV_W
O_A = ATTN_W + 2 * KV_W
O_B = O_A + CONV_CH

PAD = (-N_META) % BLOCK
U_CARRY = 32
U_SHIFT = U_CARRY - (CONV_WIDTH - 1)

V7X_VMEM_LIMIT_BYTES = 60 * 1024 * 1024

PROMPT_TILE = 512
UP_CHUNK = 1024
DOWN_CHUNK = 512
SLOT_ATTN_UNITS = (2, 2, 2, 2, 0, 0)
SLOT_CONV_CHUNKS = (0, 0, 0, 0, 4, 4)
CONV_ROWS = 64
WEIGHT_STAGE_BLOCK = (256, 1024)
WEIGHT_STAGE_DEPTH = 4
SAMPLE_CHUNK = 16
HEAD_ROWS = 16

BF16 = jnp.bfloat16
F32 = jnp.float32
NT_DIMS = (((1,), (1,)), ((), ()))


def _t5_bucket(d):
    max_exact = N_BUCKETS // 2
    d_f = jnp.maximum(d, 1).astype(jnp.float32)
    large = max_exact + (jnp.log(d_f / max_exact) / math.log(MAX_DISTANCE / max_exact)
                         * (N_BUCKETS - max_exact)).astype(jnp.int32)
    large = jnp.minimum(large, N_BUCKETS - 1)
    return jnp.where(d < max_exact, d, large)


def _rms(x, g):
    y = x * lax.rsqrt(jnp.mean(x * x, axis=-1, keepdims=True) + EPS)
    return y * g


def _relu_sq_bf16(x):
    r = jnp.maximum(x.astype(BF16), 0.0)
    return r * r


def _sigmoid(x):
    return 1.0 / (1.0 + jnp.exp(-x))


def _bias_from_buckets(bucket, relb_ref, h):
    b = jnp.zeros(bucket.shape, F32)
    for bk in range(N_BUCKETS):
        b = jnp.where(bucket == bk, relb_ref[bk, h], b)
    return b


def _conv_rows(ubuf, cw_ref, r0, rows):
    n = rows + U_CARRY
    strips = []
    for c0 in range(0, CONV_CH, BLOCK):
        win = ubuf[r0:r0 + n, c0:c0 + BLOCK]
        acc = None
        for s in range(8):
            sh = win if s == 0 else pltpu.roll(win, n - s, axis=0)
            for a0 in range(0, U_CARRY + 8, 8):
                w = a0 + s - U_SHIFT
                if 0 <= w < CONV_WIDTH:
                    term = cw_ref[w:w + 1, c0:c0 + BLOCK] * sh[a0:a0 + rows]
                    acc = term if acc is None else acc + term
        strips.append(acc)
    return jnp.concatenate(strips, axis=1)


def _ln_silu(acc, lng, lnb):
    mu = jnp.mean(acc, axis=-1, keepdims=True)
    xc = acc - mu
    y = xc * lax.rsqrt(jnp.mean(xc * xc, axis=-1, keepdims=True) + EPS)
    y = y * lng + lnb
    return y * _sigmoid(y)


def _zero_after(v, prev=None):
    u = pltpu.bitcast(v, jnp.uint32)
    t = prev
    for r0 in range(0, u.shape[0], 8):
        for c0 in range(0, u.shape[1], BLOCK):
            piece = u[r0:r0 + 8, c0:c0 + BLOCK]
            t = piece if t is None else t | piece
    return (t >> 16) >> 16


def _order_after(buf, zero):
    tile = pltpu.bitcast(buf[0:16, 0:BLOCK], jnp.uint32)
    buf[0:16, 0:BLOCK] = pltpu.bitcast(tile | zero, BF16)


def _stream_cast(pairs, stage, sem, fillers=()):
    fillers = list(fillers)
    depth, rows, cols = stage.shape
    blocks = [(src, dst, r, c, min(cols, src.shape[1] - c)) for src, dst in pairs
              for r in range(0, src.shape[0], rows) for c in range(0, src.shape[1], cols)]

    def copy(j):
        src, _, r, c, w = blocks[j]
        return pltpu.make_async_copy(src.at[pl.ds(r, rows), pl.ds(c, w)],
                                     stage.at[j % depth, :, pl.ds(0, w)], sem.at[j % depth])

    for j in range(min(depth - 1, len(blocks))):
        copy(j).start()
    for j, (_, dst, r, c, w) in enumerate(blocks):
        if j + depth - 1 < len(blocks):
            copy(j + depth - 1).start()
        if fillers:
            fillers.pop(0)()
        copy(j).wait()
        dst[r:r + rows, c:c + w] = stage[j % depth, :, 0:w].astype(BF16)
    for filler in fillers:
        filler()


def _prompt_layer_body(*refs, tile):
    i = pl.program_id(0)
    n_tiles = pl.num_programs(0) - 1
    pl.when(i < n_tiles)(functools.partial(_prompt_step, *refs, tile=tile))
    pl.when(i == n_tiles)(functools.partial(_prompt_last_ffn, *refs, tile=tile))


def _prompt_last_ffn(relb_ref, sink_ref, x_ref, meta_ref, g1_ref, win_hbm, bucket_ref,
                     cw_ref, cb_ref, lng_ref, lnb_ref, wout_hbm, g2_ref, wup_hbm, wdn_hbm, gf_ref,
                     x2s_ref, y_ref, nk_ref, nv_ref, nc_ref, ys_ref,
                     kbuf, vbuf, ubuf, qbuf, mix, bias_s, x2buf, kvlast, hfbuf, hidbuf, ysbuf,
                     wup_ref, wdn_ref, win_ref, wout_ref, stage, wsem, *, tile):
    xf = x2buf[1 - pl.program_id(0) % 2]
    hf = _rms(xf, g2_ref[...]).astype(BF16)
    for c0 in range(0, D_FF, UP_CHUNK):
        hid = jnp.dot(hf, wup_ref[:, c0:c0 + UP_CHUNK], preferred_element_type=F32)
        hidbuf[:, c0:c0 + UP_CHUNK] = _relu_sq_bf16(hid)
    cols = [xf[:, n0:n0 + DOWN_CHUNK]
            + jnp.dot(hidbuf[...], wdn_ref[:, n0:n0 + DOWN_CHUNK], preferred_element_type=F32)
            for n0 in range(0, D_MODEL, DOWN_CHUNK)]
    y_ref[...] = _rms(jnp.concatenate(cols, axis=1), gf_ref[...])


def _prompt_step(relb_ref, sink_ref, x_ref, meta_ref, g1_ref, win_hbm, bucket_ref,
                 cw_ref, cb_ref, lng_ref, lnb_ref, wout_hbm, g2_ref, wup_hbm, wdn_hbm, gf_ref,
                 x2s_ref, y_ref, nk_ref, nv_ref, nc_ref, ys_ref,
                 kbuf, vbuf, ubuf, qbuf, mix, bias_s, x2buf, kvlast, hfbuf, hidbuf, ysbuf,
                 wup_ref, wdn_ref, win_ref, wout_ref, stage, wsem, *, tile):
    i = pl.program_id(0)
    n_tiles = pl.num_programs(0) - 1
    g1 = g1_ref[...]
    slot = i % 2

    @pl.when(i == 0)
    def _init():
        def decode_rows():
            nb = x2s_ref.shape[0]
            x2buf[1, 0:nb, :] = x2s_ref[...]
            x2buf[1, nb:tile, :] = jnp.zeros((tile - nb, D_MODEL), F32)

        def bias_table(h):
            bucket = bucket_ref[...]
            row = lax.broadcasted_iota(jnp.int32, (BLOCK, 2 * BLOCK), 0)
            col = lax.broadcasted_iota(jnp.int32, (BLOCK, 2 * BLOCK), 1)
            dist = row + BLOCK - col
            band = (dist >= 0) & (dist <= WINDOW)
            band_first = band & (col >= PAD)
            b = _bias_from_buckets(bucket, relb_ref, h) * LOG2E
            bias_s[0, h] = jnp.where(band, b, -jnp.inf)
            bias_s[1, h] = jnp.where(band_first, b, -jnp.inf)

        _stream_cast([(win_hbm, win_ref), (wout_hbm, wout_ref), (wup_hbm, wup_ref), (wdn_hbm, wdn_ref)],
                     stage, wsem,
                     fillers=[decode_rows] + [functools.partial(bias_table, h) for h in range(N_HEADS)])
        hm = _rms(meta_ref[...], g1).astype(BF16)
        pm = jnp.dot(hm, win_ref[:, O_K:], preferred_element_type=F32)
        kbuf[0:PAD, :] = jnp.zeros((PAD, KV_W), BF16)
        vbuf[0:PAD, :] = jnp.zeros((PAD, KV_W), BF16)
        kbuf[PAD:BLOCK, :] = pm[:, 0:KV_W].astype(BF16)
        vbuf[PAD:BLOCK, :] = pm[:, KV_W:2 * KV_W].astype(BF16)
        um = pm[:, 2 * KV_W:2 * KV_W + CONV_CH] * _sigmoid(pm[:, 2 * KV_W + CONV_CH:])
        ubuf[0:U_CARRY - N_META, :] = jnp.zeros((U_CARRY - N_META, CONV_CH), F32)
        ubuf[U_CARRY - N_META:U_CARRY, :] = um

    x = x_ref[...]
    h = _rms(x, g1).astype(BF16)
    q = jnp.dot(h, win_ref[:, 0:ATTN_W], preferred_element_type=F32) * (SCALE * LOG2E)
    qbuf[...] = q.astype(BF16)
    kv = jnp.dot(h, win_ref[:, O_K:O_A], preferred_element_type=F32)
    kbuf[BLOCK:BLOCK + tile, :] = kv[:, 0:KV_W].astype(BF16)
    vbuf[BLOCK:BLOCK + tile, :] = kv[:, KV_W:].astype(BF16)
    kvlast[...] = kv[tile - WINDOW:, :]

    a = jnp.dot(h, win_ref[:, O_A:O_B], preferred_element_type=F32)
    b = jnp.dot(h, win_ref[:, O_B:], preferred_element_type=F32)
    ubuf[U_CARRY:U_CARRY + tile, :] = a * _sigmoid(b)

    cb, lng, lnb = cb_ref[...], lng_ref[...], lnb_ref[...]

    def conv_chunk(r0):
        c = _ln_silu(_conv_rows(ubuf, cw_ref.at[0], r0, CONV_ROWS) + cb, lng, lnb)
        mix[r0:r0 + CONV_ROWS, ATTN_W:] = c.astype(BF16)

    def attn_scores(blk, kvh):
        r0 = blk * BLOCK
        c0 = kvh * HEAD_DIM
        qg = jnp.concatenate(
            [qbuf[r0:r0 + BLOCK, (kvh * GROUP + g) * HEAD_DIM:(kvh * GROUP + g + 1) * HEAD_DIM]
             for g in range(GROUP)], axis=0)
        kk = kbuf[r0:r0 + 2 * BLOCK, c0:c0 + HEAD_DIM]
        return lax.dot_general(qg, kk, NT_DIMS, preferred_element_type=F32)

    def attn_finish(blk, kvh, s):
        r0 = blk * BLOCK
        sel = jnp.where(i == 0, 1, 0) if blk == 0 else 0
        c0 = kvh * HEAD_DIM
        vv = vbuf[r0:r0 + 2 * BLOCK, c0:c0 + HEAD_DIM]
        ps, ls = [], []
        for g in range(GROUP):
            hd = kvh * GROUP + g
            sg = s[g * BLOCK:(g + 1) * BLOCK] + bias_s[sel, hd]
            sk = sink_ref[hd] * LOG2E
            m = jnp.maximum(jnp.max(sg, axis=-1, keepdims=True), sk)
            p = jnp.exp2(sg - m)
            ls.append(jnp.sum(p, axis=-1, keepdims=True) + jnp.exp2(sk - m))
            ps.append(p.astype(BF16))
        o = jnp.dot(jnp.concatenate(ps, axis=0), vv, preferred_element_type=F32)
        for g in range(GROUP):
            hd = kvh * GROUP + g
            og = o[g * BLOCK:(g + 1) * BLOCK] / ls[g]
            mix[r0:r0 + BLOCK, hd * HEAD_DIM:(hd + 1) * HEAD_DIM] = og.astype(BF16)

    conv_starts = list(range(0, tile, CONV_ROWS))
    attn_units = [(blk, kvh) for blk in range(tile // BLOCK) for kvh in range(N_KV_HEADS)]
    n_up, n_down = D_FF // UP_CHUNK, D_MODEL // DOWN_CHUNK
    n_slots = n_up + n_down
    unit_iter, conv_iter = iter(attn_units), iter(conv_starts)
    xf = x2buf[1 - slot]
    hfbuf[...] = _rms(xf, g2_ref[...]).astype(BF16)
    x3_cols = []
    for k in range(n_slots):
        units = [next(unit_iter) for _ in range(SLOT_ATTN_UNITS[k])]
        scores = [(blk, kvh, attn_scores(blk, kvh)) for blk, kvh in units]
        if k < n_up:
            c0 = k * UP_CHUNK
            hid = jnp.dot(hfbuf[...], wup_ref[:, c0:c0 + UP_CHUNK], preferred_element_type=F32)
            hidbuf[:, c0:c0 + UP_CHUNK] = _relu_sq_bf16(hid)
        else:
            n0 = (k - n_up) * DOWN_CHUNK
            x3_cols.append(xf[:, n0:n0 + DOWN_CHUNK]
                           + jnp.dot(hidbuf[...], wdn_ref[:, n0:n0 + DOWN_CHUNK],
                                     preferred_element_type=F32))
        zero = None
        for r0 in [next(conv_iter) for _ in range(SLOT_CONV_CHUNKS[k])]:
            conv_chunk(r0)
            zero = _zero_after(mix[r0:r0 + CONV_ROWS, ATTN_W:], zero)
        for blk, kvh, s in scores:
            attn_finish(blk, kvh, s)
        if k + 1 < n_slots and zero is not None:
            _order_after(hfbuf if k + 1 < n_up else hidbuf, zero)
    acc = jnp.concatenate(x3_cols, axis=1)
    y = _rms(acc, gf_ref[...])
    y_ref[...] = y
    ysbuf[...] = y[0:ysbuf.shape[0]]

    x2buf[slot] = x + jnp.dot(mix[...], wout_ref[...], preferred_element_type=F32)

    @pl.when(i == n_tiles - 1)
    def _new_caches():
        nk_ref[...] = kvlast[:, 0:KV_W].T
        nv_ref[...] = kvlast[:, KV_W:].T
        nc_ref[:, 0, :] = ubuf[U_CARRY + tile - (CONV_WIDTH - 1):U_CARRY + tile, :]

    kbuf[0:BLOCK, :] = kbuf[tile:tile + BLOCK, :]
    vbuf[0:BLOCK, :] = vbuf[tile:tile + BLOCK, :]
    ubuf[0:U_CARRY, :] = ubuf[tile:tile + U_CARRY, :]

    @pl.when(i == 0)
    def _sample_out():
        ys_ref[:, 0, :] = ysbuf[...]


def _const_spec(shape):
    return pl.BlockSpec(shape, lambda i: (0,) * len(shape), pipeline_mode=pl.Buffered(1))


def _smem_spec():
    return pl.BlockSpec(memory_space=pltpu.SMEM)


def _prompt_layer(x, meta, rel_bias, sinks, g1, w_in, bucket, cw, cb, lng, lnb, w_out,
                  g2, w_up, w_down, gf, x2_s):
    seq = x.shape[0]
    nb = x2_s.shape[0]
    tile = PROMPT_TILE
    n_tiles = seq // tile
    body = functools.partial(_prompt_layer_body, tile=tile)
    return pl.pallas_call(
        body,
        grid=(n_tiles + 1,),
        in_specs=[
            _smem_spec(), _smem_spec(),
            pl.BlockSpec((tile, D_MODEL), lambda i: (jnp.minimum(i, n_tiles - 1), 0)),
            _const_spec((N_META, D_MODEL)),
            _const_spec((1, D_MODEL)),
            pl.BlockSpec(memory_space=pl.ANY),
            _const_spec((BLOCK, 2 * BLOCK)),
            _const_spec((1, CONV_WIDTH, CONV_CH)),
            _const_spec((1, CONV_CH)),
            _const_spec((1, CONV_CH)),
            _const_spec((1, CONV_CH)),
            pl.BlockSpec(memory_space=pl.ANY),
            _const_spec((1, D_MODEL)),
            pl.BlockSpec(memory_space=pl.ANY),
            pl.BlockSpec(memory_space=pl.ANY),
            _const_spec((1, D_MODEL)),
            _const_spec((nb, D_MODEL)),
        ],
        out_specs=[
            pl.BlockSpec((tile, D_MODEL), lambda i: (jnp.maximum(i - 1, 0), 0)),
            pl.BlockSpec((KV_W, WINDOW), lambda i: (0, 0)),
            pl.BlockSpec((KV_W, WINDOW), lambda i: (0, 0)),
            pl.BlockSpec((CONV_WIDTH - 1, 1, CONV_CH), lambda i: (0, 0, 0)),
            pl.BlockSpec((nb, 1, D_MODEL), lambda i: (0, 0, 0)),
        ],
        out_shape=[
            jax.ShapeDtypeStruct((seq, D_MODEL), F32),
            jax.ShapeDtypeStruct((KV_W, WINDOW), F32),
            jax.ShapeDtypeStruct((KV_W, WINDOW), F32),
            jax.ShapeDtypeStruct((CONV_WIDTH - 1, 1, CONV_CH), F32),
            jax.ShapeDtypeStruct((nb, 1, D_MODEL), F32),
        ],
        scratch_shapes=[
            pltpu.VMEM((BLOCK + tile, KV_W), BF16),
            pltpu.VMEM((BLOCK + tile, KV_W), BF16),
            pltpu.VMEM((U_CARRY + tile, CONV_CH), F32),
            pltpu.VMEM((tile, ATTN_W), BF16),
            pltpu.VMEM((tile, ATTN_W + CONV_CH), BF16),
            pltpu.VMEM((2, N_HEADS, BLOCK, 2 * BLOCK), F32),
            pltpu.VMEM((2, tile, D_MODEL), F32),
            pltpu.VMEM((WINDOW, 2 * KV_W), F32),
            pltpu.VMEM((tile, D_MODEL), BF16),
            pltpu.VMEM((tile, D_FF), BF16),
            pltpu.VMEM((nb, D_MODEL), F32),
            pltpu.VMEM((D_MODEL, D_FF), BF16),
            pltpu.VMEM((D_FF, D_MODEL), BF16),
            pltpu.VMEM((D_MODEL, IN_W), BF16),
            pltpu.VMEM((ATTN_W + CONV_CH, D_MODEL), BF16),
            pltpu.VMEM((WEIGHT_STAGE_DEPTH,) + WEIGHT_STAGE_BLOCK, F32),
            pltpu.SemaphoreType.DMA((WEIGHT_STAGE_DEPTH,)),
        ],
        compiler_params=pltpu.CompilerParams(
            dimension_semantics=("arbitrary",), vmem_limit_bytes=V7X_VMEM_LIMIT_BYTES),
        name="prompt_layer",
    )(rel_bias, sinks, x, meta, g1, w_in, bucket, cw, cb, lng, lnb, w_out, g2, w_up, w_down, gf,
      x2_s)


def _sample_mixer_body(relb_ref, sink_ref, x_ref, ckt_ref, cvt_ref, st_ref, g1_ref, win_ref,
                       bucket_ref, cw_ref, cb_ref, lng_ref, lnb_ref, wout_hbm,
                       x2_ref, nkt_ref, nvt_ref, nc_ref,
                       pbuf, mix, bias_c, sink_c, wout_ref, wout_sem, *, chunk):
    i = pl.program_id(0)
    last = pl.num_programs(0) - 1
    rows_h = chunk * HEAD_ROWS
    wout_copy = pltpu.make_async_copy(wout_hbm, wout_ref, wout_sem.at[0])

    @pl.when(i == 0)
    def _init():
        wout_copy.start()
        h = _rms(x_ref[:, 0, :], g1_ref[...]).astype(BF16)
        pbuf[...] = jnp.dot(h, win_ref[...].astype(BF16), preferred_element_type=F32)
        bucket = bucket_ref[...]
        rid = lax.broadcasted_iota(jnp.int32, (HEAD_ROWS, 1), 0)
        bias = jnp.zeros((HEAD_ROWS, 2 * BLOCK), F32)
        sk = jnp.zeros((HEAD_ROWS, 1), F32)
        for hd in range(N_HEADS):
            bias = jnp.where(rid == hd, _bias_from_buckets(bucket, relb_ref, hd), bias)
            sk = jnp.where(rid == hd, sink_ref[hd], sk)
        bias_c[...] = bias
        sink_c[...] = sk

    r0 = pl.multiple_of(i * chunk, chunk)
    pr = pbuf[pl.ds(r0, chunk), :]
    q = pr[:, 0:ATTN_W] * SCALE
    knew = pr[:, O_K:O_V]
    vnew = pr[:, O_V:O_A]
    unew = pr[:, O_A:O_B] * _sigmoid(pr[:, O_B:])

    def per_head(t):
        n = t.shape[-1]
        return jnp.broadcast_to(t[:, None, :], (chunk, HEAD_ROWS, n)).reshape(rows_h, n)

    hid = lax.broadcasted_iota(jnp.int32, (rows_h, 1), 0) % HEAD_ROWS
    lane = lax.broadcasted_iota(jnp.int32, (1, BLOCK), 1)
    qrep = per_head(q)
    qsum = jnp.zeros((rows_h, BLOCK), F32)
    for c in range(ATTN_W // BLOCK):
        piece = qrep[:, c * BLOCK:(c + 1) * BLOCK]
        in_head = (lane // HEAD_DIM + 2 * c) == hid
        qsum = qsum + jnp.where(in_head, piece, 0.0)
    keep = (hid % 2) == (hid // GROUP)
    qf = jnp.where(keep, qsum, pltpu.roll(qsum, HEAD_DIM, axis=1))
    qf_b = qf.astype(BF16)

    bias = jnp.broadcast_to(bias_c[...][None], (chunk, HEAD_ROWS, 2 * BLOCK)).reshape(rows_h, 2 * BLOCK)
    sk = jnp.broadcast_to(sink_c[...][None], (chunk, HEAD_ROWS, 1)).reshape(rows_h, 1)

    s_rows = []
    for b in range(chunk):
        kt = ckt_ref[b].astype(BF16)
        s_rows.append(jnp.dot(qf_b[b * HEAD_ROWS:(b + 1) * HEAD_ROWS], kt, preferred_element_type=F32))
    s_c = jnp.concatenate(s_rows, axis=0) + bias[:, 0:BLOCK]
    s_n = jnp.sum(qf * per_head(knew), axis=-1, keepdims=True) + bias[:, BLOCK:BLOCK + 1]
    m = jnp.maximum(jnp.maximum(jnp.max(s_c, axis=-1, keepdims=True), s_n), sk)
    p_c = jnp.exp(s_c - m)
    p_n = jnp.exp(s_n - m)
    l = jnp.sum(p_c, axis=-1, keepdims=True) + p_n + jnp.exp(sk - m)
    p_cb = p_c.astype(BF16)
    o_rows = []
    for b in range(chunk):
        vt = cvt_ref[b].astype(BF16)
        o_rows.append(lax.dot_general(p_cb[b * HEAD_ROWS:(b + 1) * HEAD_ROWS], vt, NT_DIMS,
                                      preferred_element_type=F32))
    o = (jnp.concatenate(o_rows, axis=0) + p_n * per_head(vnew)) / l
    o = jnp.where(keep, o, pltpu.roll(o, HEAD_DIM, axis=1))
    o = jnp.where((lane // HEAD_DIM) == (hid % 2), o, 0.0)
    wide = jnp.concatenate([jnp.where(hid // 2 == c, o, 0.0) for c in range(ATTN_W // BLOCK)], axis=1)
    gi = lax.broadcasted_iota(jnp.int32, (chunk, rows_h), 0)
    gj = lax.broadcasted_iota(jnp.int32, (chunk, rows_h), 1)
    gather = jnp.where(gj // HEAD_ROWS == gi, 1.0, 0.0).astype(BF16)
    ao = jnp.dot(gather, wide.astype(BF16), preferred_element_type=F32)
    mix[pl.ds(r0, chunk), 0:ATTN_W] = ao

    pad = jnp.zeros((BLOCK - chunk, KV_W), F32)
    knew_t = jnp.concatenate([knew, pad], axis=0).T
    vnew_t = jnp.concatenate([vnew, pad], axis=0).T
    newest = lane == WINDOW - 1
    for b in range(chunk):
        kcol = jnp.broadcast_to(knew_t[:, b:b + 1], (KV_W, WINDOW))
        vcol = jnp.broadcast_to(vnew_t[:, b:b + 1], (KV_W, WINDOW))
        nkt_ref[b] = jnp.where(newest, kcol, pltpu.roll(ckt_ref[b], WINDOW - 1, axis=1))
        nvt_ref[b] = jnp.where(newest, vcol, pltpu.roll(cvt_ref[b], WINDOW - 1, axis=1))

    acc = cb_ref[...] + cw_ref[0, CONV_WIDTH - 1:CONV_WIDTH, :] * unew
    for w in range(CONV_WIDTH - 1):
        acc = acc + cw_ref[0, w:w + 1, :] * st_ref[w]
    nc_ref[0:CONV_WIDTH - 2] = st_ref[1:CONV_WIDTH - 1]
    nc_ref[CONV_WIDTH - 2] = unew
    mu = jnp.mean(acc, axis=-1, keepdims=True)
    xc = acc - mu
    y = xc * lax.rsqrt(jnp.mean(xc * xc, axis=-1, keepdims=True) + EPS)
    y = y * lng_ref[...] + lnb_ref[...]
    mix[pl.ds(r0, chunk), ATTN_W:] = y * _sigmoid(y)

    @pl.when(i == last)
    def _out():
        wout_copy.wait()
        x2_ref[...] = x_ref[:, 0, :] + jnp.dot(mix[...].astype(BF16), wout_ref[...].astype(BF16),
                                           preferred_element_type=F32)


def _sample_mixer(x, ckt, cvt, st, rel_bias, sinks, g1, w_in, bucket, cw, cb, lng, lnb, w_out):
    nb = x.shape[0]
    chunk = SAMPLE_CHUNK
    body = functools.partial(_sample_mixer_body, chunk=chunk)
    cache_spec = pl.BlockSpec((chunk, KV_W, WINDOW), lambda i: (i, 0, 0))
    state_spec = pl.BlockSpec((CONV_WIDTH - 1, chunk, CONV_CH), lambda i: (0, i, 0))
    return pl.pallas_call(
        body,
        grid=(nb // chunk,),
        in_specs=[
            _smem_spec(), _smem_spec(),
            _const_spec((nb, 1, D_MODEL)),
            cache_spec, cache_spec, state_spec,
            _const_spec((1, D_MODEL)),
            _const_spec((D_MODEL, IN_W)),
            _const_spec((1, 2 * BLOCK)),
            _const_spec((1, CONV_WIDTH, CONV_CH)),
            _const_spec((1, CONV_CH)),
            _const_spec((1, CONV_CH)),
            _const_spec((1, CONV_CH)),
            pl.BlockSpec(memory_space=pl.ANY),
        ],
        out_specs=[
            pl.BlockSpec((nb, D_MODEL), lambda i: (0, 0)),
            cache_spec, cache_spec, state_spec,
        ],
        out_shape=[
            jax.ShapeDtypeStruct((nb, D_MODEL), F32),
            jax.ShapeDtypeStruct((nb, KV_W, WINDOW), F32),
            jax.ShapeDtypeStruct((nb, KV_W, WINDOW), F32),
            jax.ShapeDtypeStruct((CONV_WIDTH - 1, nb, CONV_CH), F32),
        ],
        scratch_shapes=[
            pltpu.VMEM((nb, IN_W), F32),
            pltpu.VMEM((nb, ATTN_W + CONV_CH), F32),
            pltpu.VMEM((HEAD_ROWS, 2 * BLOCK), F32),
            pltpu.VMEM((HEAD_ROWS, 1), F32),
            pltpu.VMEM((ATTN_W + CONV_CH, D_MODEL), F32),
            pltpu.SemaphoreType.DMA((1,)),
        ],
        compiler_params=pltpu.CompilerParams(
            dimension_semantics=("arbitrary",), vmem_limit_bytes=V7X_VMEM_LIMIT_BYTES),
        name="sample_mixer",
    )(rel_bias, sinks, x, ckt, cvt, st, g1, w_in, bucket, cw, cb, lng, lnb, w_out)


def kernel(x_prompt, x_sample, cache_k, cache_v, state_conv, meta_tokens, rel_bias, norm1_g, w_in,
           attn_sinks, conv_w, conv_b, conv_ln_g, conv_ln_b, w_out, norm2_g, w_up, w_down, norm_f_g):
    batch, seq, _ = x_prompt.shape
    nb, dec_seq, _ = x_sample.shape
    assert batch == 1 and dec_seq == 1 and w_in.shape[0] == 1
    assert seq % PROMPT_TILE == 0 and nb % SAMPLE_CHUNK == 0 and nb <= PROMPT_TILE
    assert len(SLOT_ATTN_UNITS) == len(SLOT_CONV_CHUNKS) == D_FF // UP_CHUNK + D_MODEL // DOWN_CHUNK
    assert sum(SLOT_ATTN_UNITS) == (PROMPT_TILE // BLOCK) * N_KV_HEADS
    assert sum(SLOT_CONV_CHUNKS) == PROMPT_TILE // CONV_ROWS

    w_in_b, w_out_b = w_in[0], w_out[0]
    g1 = norm1_g[0][None]
    g2 = norm2_g[0][None]
    gf = norm_f_g[None]
    cw, cb = conv_w, conv_b[0][None]
    lng, lnb = conv_ln_g[0][None], conv_ln_b[0][None]
    sinks = attn_sinks[0]

    dist_p = jnp.arange(BLOCK)[:, None] + BLOCK - jnp.arange(2 * BLOCK)[None, :]
    bucket_p = _t5_bucket(jnp.clip(dist_p, 0, WINDOW)).astype(jnp.int32)
    lane = jnp.arange(2 * BLOCK)
    dist_s = jnp.where(lane < WINDOW, WINDOW - lane, 0)
    bucket_s = jnp.where(lane <= WINDOW, _t5_bucket(jnp.clip(dist_s, 0, WINDOW)), -1)
    bucket_s = bucket_s.astype(jnp.int32)[None]

    ckt = jnp.transpose(cache_k[0], (0, 2, 3, 1)).reshape(nb, KV_W, WINDOW)
    cvt = jnp.transpose(cache_v[0], (0, 2, 3, 1)).reshape(nb, KV_W, WINDOW)
    st = jnp.transpose(state_conv[0], (1, 0, 2))
    x2_s, nkt_s, nvt_s, nct_s = _sample_mixer(x_sample, ckt, cvt, st, rel_bias, sinks, g1,
                                              w_in_b, bucket_s, cw, cb, lng, lnb, w_out_b)
    nk_s = jnp.transpose(nkt_s.reshape(nb, N_KV_HEADS, HEAD_DIM, WINDOW), (0, 3, 1, 2))
    nv_s = jnp.transpose(nvt_s.reshape(nb, N_KV_HEADS, HEAD_DIM, WINDOW), (0, 3, 1, 2))
    nc_s = jnp.transpose(nct_s, (1, 0, 2))
    y_p, nkt_p, nvt_p, nc_p, y_s = _prompt_layer(x_prompt[0], meta_tokens, rel_bias, sinks, g1, w_in_b,
                                               bucket_p, cw, cb, lng, lnb, w_out_b, g2, w_up[0],
                                               w_down[0], gf, x2_s)

    def to_cache(t):
        return jnp.transpose(t.reshape(N_KV_HEADS, HEAD_DIM, WINDOW), (2, 0, 1))[None, None]

    return (y_p[None], y_s,
            to_cache(nkt_p), to_cache(nvt_p), jnp.transpose(nc_p, (1, 0, 2))[None],
            nk_s[None], nv_s[None], nc_s[None])
```
